```python
import math
import jax, jax.numpy as jnp
from jax import lax
import numpy as np

D_MODEL = 1024
BATCH = 32
SEQ = 256
DEPTH = 2
DEC_BATCH = 2
DEC_SEQ = 4096
PAST_LEN = 256

GRID_W = 64
HEAD_DIM = 64
GQA_WIDTH = D_MODEL // 2
DIFF_WIDTH = D_MODEL // 4
HGRN_WIDTH = D_MODEL // 4
GQA_Q_HEADS = GQA_WIDTH // HEAD_DIM
GQA_KV_HEADS = 2
GQA_GROUP = GQA_Q_HEADS // GQA_KV_HEADS
DIFF_HEADS = DIFF_WIDTH // HEAD_DIM
DIFF_QK_DIM = HEAD_DIM // 2
HGRN_HEADS = HGRN_WIDTH // HEAD_DIM
HGRN_KEY_DIM = HEAD_DIM
HGRN_VAL_DIM = HEAD_DIM
MIX_WIDTH = GQA_WIDTH + DIFF_WIDTH + HGRN_WIDTH
SPLIT_SIZES = (GQA_WIDTH, GQA_KV_HEADS * HEAD_DIM, GQA_KV_HEADS * HEAD_DIM, GQA_WIDTH,
               DIFF_WIDTH, DIFF_WIDTH, DIFF_WIDTH, DIFF_WIDTH,
               HGRN_WIDTH, HGRN_WIDTH, HGRN_WIDTH, HGRN_WIDTH, HGRN_WIDTH)
IN_WIDTH = sum(SPLIT_SIZES)
SPLIT_POINTS = tuple(sum(SPLIT_SIZES[:i + 1]) for i in range(len(SPLIT_SIZES) - 1))
Q_BLOCK = 128
SCAN_CHUNK = 64
ROPE_THETA = 10000.0
RMS_EPS = 1e-6
LN_EPS = 1e-5
FORGET_MIN = 1e-6
DEEPNORM_ALPHA = (2 * DEPTH) ** 0.25
DEEPNORM_BETA = (8 * DEPTH) ** -0.25

kernel_name = "hybrid_gqa_diffattn_hgrn2_diffusion_step"


def rms_norm(x, gain):
    xf = x.astype(jnp.float32)
    y = xf * lax.rsqrt(jnp.mean(xf * xf, axis=-1, keepdims=True) + RMS_EPS)
    return (y * gain.astype(jnp.float32)).astype(x.dtype)


def layer_norm(x, g, b):
    xf = x.astype(jnp.float32)
    mu = jnp.mean(xf, axis=-1, keepdims=True)
    xc = xf - mu
    var = jnp.mean(xc * xc, axis=-1, keepdims=True)
    return (xc * lax.rsqrt(var + LN_EPS) * g.astype(jnp.float32) + b.astype(jnp.float32)).astype(x.dtype)


def axial_rope_tables(n_tokens, dim):
    rows = n_tokens // GRID_W
    row = jnp.repeat(jnp.arange(rows, dtype=jnp.float32), GRID_W)
    col = jnp.tile(jnp.arange(GRID_W, dtype=jnp.float32), rows)
    quarter = dim // 4
    inv_freq = ROPE_THETA ** (-jnp.arange(quarter, dtype=jnp.float32) / quarter)
    ar = row[:, None] * inv_freq[None, :]
    ac = col[:, None] * inv_freq[None, :]
    ang = jnp.concatenate([ar, ar, ac, ac], axis=-1)
    return jnp.cos(ang), jnp.sin(ang)


def _rotate_half(u):
    u1, u2 = jnp.split(u, 2, axis=-1)
    return jnp.concatenate([-u2, u1], axis=-1)


def apply_axial_rope(x, cos, sin):
    shape = (1, cos.shape[0]) + (1,) * (x.ndim - 3) + (cos.shape[1],)
    c = cos.reshape(shape).astype(x.dtype)
    s = sin.reshape(shape).astype(x.dtype)
    xa, xb = jnp.split(x, 2, axis=-1)
    xr = jnp.concatenate([_rotate_half(xa), _rotate_half(xb)], axis=-1)
    return x * c + xr * s


def _query_blocks(q):
    b, n = q.shape[:2]
    q = q.reshape((b, n // Q_BLOCK, Q_BLOCK) + q.shape[2:])
    return jnp.moveaxis(q, 1, 0)


def _merge_blocks(o):
    o = jnp.moveaxis(o, 0, 1)
    return o.reshape((o.shape[0], o.shape[1] * o.shape[2]) + o.shape[3:])


def gqa_attention(q, k, v):
    b, n = q.shape[:2]
    qg = q.reshape(b, n, GQA_KV_HEADS, GQA_GROUP, HEAD_DIM)
    scale = HEAD_DIM ** -0.5

    def block(qb):
        s = jnp.einsum('bqgrd,bkgd->bgrqk', qb, k).astype(jnp.float32) * scale
        p = jax.nn.softmax(s, axis=-1).astype(v.dtype)
        return jnp.einsum('bgrqk,bkgd->bqgrd', p, v)

    o = _merge_blocks(lax.map(block, _query_blocks(qg)))
    return o.reshape(b, n, GQA_WIDTH)


def diff_attention(q, k, v, lam):
    scale = DIFF_QK_DIM ** -0.5

    def block(qb):
        s = jnp.einsum('bqhmd,bkhmd->bhmqk', qb, k).astype(jnp.float32) * scale
        p = jax.nn.softmax(s, axis=-1)
        a = p[:, :, 0] - lam * p[:, :, 1]
        return jnp.einsum('bhqk,bkhd->bqhd', a.astype(v.dtype), v)

    return _merge_blocks(lax.map(block, _query_blocks(q)))


def hgrn_scan(q, k, g, v, s0):
    b, n, h, _ = q.shape
    nc = n // SCAN_CHUNK

    def chunks(a):
        a = a.reshape(b, nc, SCAN_CHUNK, h, a.shape[-1])
        return jnp.transpose(a, (1, 0, 3, 2, 4))

    causal = jnp.tril(jnp.ones((SCAN_CHUNK, SCAN_CHUNK), dtype=bool))[:, :, None]

    def step(S, xs):
        qc, kc, gc, vc = xs
        G = jnp.cumsum(gc, axis=2)
        inter = jnp.einsum('bhck,bhkv->bhcv', qc * jnp.exp(G), S)
        diff = G[:, :, :, None, :] - G[:, :, None, :, :]
        decay = jnp.where(causal, jnp.exp(jnp.where(causal, diff, 0.0)), 0.0)
        A = jnp.einsum('bhtk,bhsk,bhtsk->bhts', qc, kc, decay)
        intra = jnp.einsum('bhts,bhsv->bhtv', A, vc)
        G_last = G[:, :, -1:, :]
        S_new = (jnp.exp(G_last[:, :, 0, :])[..., None] * S
                 + jnp.einsum('bhsk,bhsv->bhkv', kc * jnp.exp(G_last - G), vc))
        return S_new, inter + intra

    S_fin, o = lax.scan(step, s0, (chunks(q), chunks(k), chunks(g), chunks(v)))
    o = jnp.transpose(o, (1, 0, 3, 2, 4)).reshape(b, n, h, v.shape[-1])
    return o, S_fin


def hgrn_gates(z, lb):
    b, n, _ = z.shape
    z = z.reshape(b, n, HGRN_HEADS, HGRN_KEY_DIM).astype(jnp.float32)
    lb = lb.reshape(HGRN_HEADS, HGRN_KEY_DIM)
    k = (1.0 - lb) * jax.nn.sigmoid(-z)
    f = lb + (1.0 - lb) * jax.nn.sigmoid(z)
    g = jnp.log(jnp.maximum(f, FORGET_MIN))
    return k, g


def modulation(cond, w_ada_l, b_ada_l):
    m = jnp.einsum('bd,de->be', jax.nn.silu(cond), w_ada_l) + b_ada_l
    shift, scale, gate = jnp.split(m, 3, axis=-1)
    return shift[:, None, :], scale[:, None, :], gate[:, None, :]


def mixer(h, ctx, rope, lam_init, lb, w_in, q_norm, k_norm, lam_p, subln, hg_norm, w_out):
    b, n, _ = h.shape
    parts = jnp.split(jnp.einsum('bnd,de->bne', h, w_in), SPLIT_POINTS, axis=-1)
    a_q, a_k, a_v, a_g, d_q, d_k, d_v, d_g, r_q, r_ff, r_fb, r_i, r_g = parts

    a_q = rms_norm(a_q.reshape(b, n, GQA_Q_HEADS, HEAD_DIM), q_norm)
    a_k = rms_norm(a_k.reshape(b, n, GQA_KV_HEADS, HEAD_DIM), k_norm)
    a_v = a_v.reshape(b, n, GQA_KV_HEADS, HEAD_DIM)
    d_q = d_q.reshape(b, n, DIFF_HEADS, 2, DIFF_QK_DIM)
    d_k = d_k.reshape(b, n, DIFF_HEADS, 2, DIFF_QK_DIM)
    d_v = d_v.reshape(b, n, DIFF_HEADS, HEAD_DIM)

    if ctx is None:
        ka, va, kd, vd = a_k, a_v, d_k, d_v
        s0_f = jnp.zeros((b, HGRN_HEADS, HGRN_KEY_DIM, HGRN_VAL_DIM), jnp.float32)
        s0_b = s0_f
    else:
        cg_k, cg_v, cd_k, cd_v, st = ctx
        (cos_a, sin_a), (cos_d, sin_d) = rope
        a_q = apply_axial_rope(a_q, cos_a, sin_a)
        a_k = apply_axial_rope(a_k, cos_a, sin_a)
        d_q = apply_axial_rope(d_q, cos_d, sin_d)
        d_k = apply_axial_rope(d_k, cos_d, sin_d)
        ka = jnp.concatenate([cg_k.astype(a_k.dtype), a_k], axis=1)
        va = jnp.concatenate([cg_v.astype(a_v.dtype), a_v], axis=1)
        cd_k = cd_k.reshape(cd_k.shape[:2] + (DIFF_HEADS, 2, DIFF_QK_DIM))
        kd = jnp.concatenate([cd_k.astype(d_k.dtype), d_k], axis=1)
        vd = jnp.concatenate([cd_v.astype(d_v.dtype), d_v], axis=1)
        s0_f = st[:, 0].astype(jnp.float32)
        s0_b = st[:, 1].astype(jnp.float32)

    out_a = gqa_attention(a_q, ka, va) * jax.nn.silu(a_g)

    lp = lam_p.astype(jnp.float32)
    lam = jnp.exp(jnp.sum(lp[0] * lp[1])) - jnp.exp(jnp.sum(lp[2] * lp[3])) + lam_init
    o_d = diff_attention(d_q, kd, vd, lam)
    o_d = rms_norm(o_d, subln) * (1.0 - lam_init)
    out_d = o_d.reshape(b, n, DIFF_WIDTH) * jax.nn.silu(d_g)

    q_r = jax.nn.silu(r_q).reshape(b, n, HGRN_HEADS, HGRN_KEY_DIM).astype(jnp.float32)
    v_r = r_i.reshape(b, n, HGRN_HEADS, HGRN_VAL_DIM).astype(jnp.float32)
    k_f, g_f = hgrn_gates(r_ff, lb[0])
    k_b, g_b = hgrn_gates(r_fb, lb[1])
    o_f, sf = hgrn_scan(q_r, k_f, g_f, v_r, s0_f)
    flip = lambda a: jnp.flip(a, axis=1)
    o_b, sb = hgrn_scan(flip(q_r), flip(k_b), flip(g_b), flip(v_r), s0_b)
    o_r = rms_norm(o_f + flip(o_b), hg_norm).astype(h.dtype)
    out_r = o_r.reshape(b, n, HGRN_WIDTH) * jax.nn.silu(r_g)

    y = jnp.einsum('bne,ed->bnd', jnp.concatenate([out_a, out_d, out_r], axis=-1), w_out)
    if ctx is None:
        ctx_out = (a_k, a_v, d_k.reshape(b, n, DIFF_HEADS, HEAD_DIM), d_v,
                   jnp.stack([sf, sb], axis=1).astype(h.dtype))
        return y, ctx_out
    return y, None


def setup_inputs(seed: int = 0) -> dict:
    key = jax.random.key(seed)
    ks = jax.random.split(key, 24)
    f32 = jnp.float32
    nrm = lambda k, shape: jax.random.normal(k, shape, f32)
    s_in = D_MODEL ** -0.5
    return {
        "x_prompt": nrm(ks[0], (BATCH, SEQ, D_MODEL)),
        "x_sample": nrm(ks[1], (DEC_BATCH, DEC_SEQ, D_MODEL)),
        "cache_gqa_k": nrm(ks[2], (DEC_BATCH, DEPTH, PAST_LEN, GQA_KV_HEADS, HEAD_DIM)),
        "cache_gqa_v": nrm(ks[3], (DEC_BATCH, DEPTH, PAST_LEN, GQA_KV_HEADS, HEAD_DIM)),
        "cache_diff_k": nrm(ks[4], (DEC_BATCH, DEPTH, PAST_LEN, DIFF_HEADS, HEAD_DIM)),
        "cache_diff_v": nrm(ks[5], (DEC_BATCH, DEPTH, PAST_LEN, DIFF_HEADS, HEAD_DIM)),
        "state_hgrn": 0.5 * nrm(ks[6], (DEC_BATCH, DEPTH, 2, HGRN_HEADS, HGRN_KEY_DIM, HGRN_VAL_DIM)),
        "c": nrm(ks[7], (DEC_BATCH, D_MODEL)),
        "c_ctx": nrm(ks[8], (D_MODEL,)),
        "w_ada": 0.5 * s_in * nrm(ks[9], (DEPTH, D_MODEL, 3 * D_MODEL)),
        "b_ada": 0.02 * nrm(ks[10], (DEPTH, 3 * D_MODEL)),
        "w_in": s_in * nrm(ks[11], (DEPTH, D_MODEL, IN_WIDTH)),
        "gqa_q_norm": 1.0 + 0.02 * nrm(ks[12], (DEPTH, HEAD_DIM)),
        "gqa_k_norm": 1.0 + 0.02 * nrm(ks[13], (DEPTH, HEAD_DIM)),
        "diff_lambda": 0.1 * nrm(ks[14], (DEPTH, 4, DIFF_QK_DIM)),
        "diff_subln": 1.0 + 0.02 * nrm(ks[15], (DEPTH, HEAD_DIM)),
        "hgrn_lower_bounds": 0.1 * nrm(ks[16], (DEPTH, 2, HGRN_WIDTH)),
        "hgrn_norm": 1.0 + 0.02 * nrm(ks[17], (DEPTH, HEAD_DIM)),
        "w_out": DEEPNORM_BETA * MIX_WIDTH ** -0.5 * nrm(ks[18], (DEPTH, MIX_WIDTH, D_MODEL)),
        "ln_g": 1.0 + 0.02 * nrm(ks[19], (DEPTH, D_MODEL)),
        "ln_b": 0.02 * nrm(ks[20], (DEPTH, D_MODEL)),
    }


def reference(x_prompt, x_sample, cache_gqa_k, cache_gqa_v, cache_diff_k, cache_diff_v, state_hgrn,
              c, c_ctx, w_ada, b_ada, w_in, gqa_q_norm, gqa_k_norm, diff_lambda, diff_subln,
              hgrn_lower_bounds, hgrn_norm, w_out, ln_g, ln_b):
    lbs = jax.nn.softmax(hgrn_lower_bounds.astype(jnp.float32), axis=0)
    lbs = jnp.cumsum(lbs, axis=0) - lbs[0:1]

    x = x_prompt
    gk_l, gv_l, dk_l, dv_l, st_l = [], [], [], [], []
    for l in range(DEPTH):
        lam_init = 0.8 - 0.6 * math.exp(-0.3 * l)
        shift, scale, gate = modulation(c_ctx[None, :], w_ada[l], b_ada[l])
        h = x * (1.0 + scale) + shift
        y, (gk, gv, dk, dv, st) = mixer(h, None, None, lam_init, lbs[l], w_in[l], gqa_q_norm[l],
                                        gqa_k_norm[l], diff_lambda[l], diff_subln[l], hgrn_norm[l], w_out[l])
        x = layer_norm(DEEPNORM_ALPHA * x + gate * y, ln_g[l], ln_b[l])
        gk_l.append(gk); gv_l.append(gv); dk_l.append(dk); dv_l.append(dv); st_l.append(st)
    y_prompt = x
    new_gqa_k = jnp.stack(gk_l, axis=1)
    new_gqa_v = jnp.stack(gv_l, axis=1)
    new_diff_k = jnp.stack(dk_l, axis=1)
    new_diff_v = jnp.stack(dv_l, axis=1)
    new_state_hgrn = jnp.stack(st_l, axis=1)

    n_lat = x_sample.shape[1]
    rope = (axial_rope_tables(n_lat, HEAD_DIM), axial_rope_tables(n_lat, DIFF_QK_DIM))
    x = x_sample
    for l in range(DEPTH):
        lam_init = 0.8 - 0.6 * math.exp(-0.3 * l)
        shift, scale, gate = modulation(c, w_ada[l], b_ada[l])
        h = x * (1.0 + scale) + shift
        ctx = (cache_gqa_k[:, l], cache_gqa_v[:, l], cache_diff_k[:, l], cache_diff_v[:, l], state_hgrn[:, l])
        y, _ = mixer(h, ctx, rope, lam_init, lbs[l], w_in[l], gqa_q_norm[l], gqa_k_norm[l],
                     diff_lambda[l], diff_subln[l], hgrn_norm[l], w_out[l])
        x = layer_norm(DEEPNORM_ALPHA * x + gate * y, ln_g[l], ln_b[l])
    y_sample = x
    return (y_prompt, y_sample, new_gqa_k, new_gqa_v, new_diff_k, new_diff_v, new_state_hgrn)
```

```python
import functools
import math

import numpy as np
import jax
import jax.numpy as jnp
from jax import lax
from jax.experimental import pallas as pl
from jax.experimental.pallas import tpu as pltpu

F32 = jnp.float32
BF16 = jnp.bfloat16

D_MODEL = 1024
DEPTH = 2
GRID_W = 64
HEAD_DIM = 64
GQA_WIDTH = 512
GQA_KV_HEADS = 2
DIFF_WIDTH = 256
DIFF_HEADS = 4
DIFF_QK_DIM = 32
HGRN_WIDTH = 256
HGRN_HEADS = 4
IN_WIDTH = 3584
SCAN_CHUNK = 64
SUB_CHUNK = 16
ROPE_THETA = 10000.0
RMS_EPS = 1e-6
LN_EPS = 1e-5
FORGET_MIN = 1e-6
DEEPNORM_ALPHA = (2 * DEPTH) ** 0.25
LANES = 128
MOD_ROWS = 8
VMEM_LIMIT = 48 * 1024 * 1024

_A_Q, _A_K, _A_V, _A_G = 0, 512, 640, 768
_D_Q, _D_K, _D_V, _D_G = 1280, 1536, 1792, 2048
_R_Z, _R_G = 2304, 3328

_NT = (((1,), (1,)), ((), ()))
_TN = (((0,), (0,)), ((), ()))


def _silu(x):
    return x * jax.nn.sigmoid(x)


def _params(sem):
    return pltpu.CompilerParams(dimension_semantics=sem, vmem_limit_bytes=VMEM_LIMIT)


def _mod_kernel(c_ref, w_ref, b_ref, o_ref):
    c = c_ref[...]
    o_ref[0] = jnp.dot(_silu(c), w_ref[0], preferred_element_type=F32) + b_ref[0]


def _modulation(cond, w_ada, b_ada):
    tn = 1024
    return pl.pallas_call(
        _mod_kernel,
        grid=(DEPTH, 3 * D_MODEL // tn),
        in_specs=[pl.BlockSpec((MOD_ROWS, D_MODEL), lambda l, j: (0, 0)),
                  pl.BlockSpec((1, D_MODEL, tn), lambda l, j: (l, 0, j)),
                  pl.BlockSpec((1, 1, tn), lambda l, j: (l, 0, j))],
        out_specs=pl.BlockSpec((1, MOD_ROWS, tn), lambda l, j: (l, 0, j)),
        out_shape=jax.ShapeDtypeStruct((DEPTH, MOD_ROWS, 3 * D_MODEL), F32),
        compiler_params=_params(("arbitrary", "arbitrary")),
        name="modulation",
    )(cond, w_ada, b_ada.reshape(DEPTH, 1, 3 * D_MODEL))


def _rope(x, c, s1, s2, shift):
    return x * c + pltpu.roll(x, LANES - shift, 1) * s1 + pltpu.roll(x, shift, 1) * s2


def _inproj_kernel(*refs, rope, ctx_out):
    it = iter(refs)
    x_ref, sc_ref, sh_ref, w_ref, qg_ref, kg_ref, bdq_ref, bdk_ref = [next(it) for _ in range(8)]
    if rope:
        ca, s1a, s2a, cd, s1d, s2d = [next(it)[...] for _ in range(6)]
    qa_o, kdup_o, vaug_o, qd_o, kd_o, vaugd_o, gates_o, zr_o = [next(it) for _ in range(8)]
    if ctx_out:
        kn_o, av_o, dk_o, dv_o = [next(it) for _ in range(4)]

    h = (x_ref[...] * (1.0 + sc_ref[0]) + sh_ref[0]).astype(BF16)

    def proj(lo, hi):
        return jnp.dot(h, w_ref[:, lo:hi], preferred_element_type=F32)

    tm = h.shape[0]
    low = lax.broadcasted_iota(jnp.int32, (tm, LANES), 1) < HEAD_DIM
    inv_d = 1.0 / HEAD_DIM

    aq = proj(_A_Q, _A_K)
    ssq = jnp.dot((aq * aq).astype(BF16), bdq_ref[...], preferred_element_type=F32)
    qn = aq * lax.rsqrt(ssq * inv_d + RMS_EPS) * qg_ref[...]
    for j in range(GQA_WIDTH // LANES):
        slab = qn[:, j * LANES:(j + 1) * LANES]
        if rope:
            slab = _rope(slab, ca, s1a, s2a, HEAD_DIM // 4)
        qa_o[:, j * LANES:(j + 1) * LANES] = slab.astype(BF16)

    ak = proj(_A_K, _A_V)
    ssqk = jnp.dot((ak * ak).astype(BF16), bdk_ref[...], preferred_element_type=F32)
    kn = ak * lax.rsqrt(ssqk * inv_d + RMS_EPS) * kg_ref[...]
    if ctx_out:
        kn_o[...] = kn
    if rope:
        kn = _rope(kn, ca, s1a, s2a, HEAD_DIM // 4)
    kr = pltpu.roll(kn, HEAD_DIM, 1)
    kdup_o[0] = jnp.where(low, kn, kr).astype(BF16)
    kdup_o[1] = jnp.where(low, kr, kn).astype(BF16)

    av = proj(_A_V, _A_G)
    if ctx_out:
        av_o[...] = av
    vr = pltpu.roll(av, HEAD_DIM, 1)
    vaug_o[0] = jnp.where(low, av, 1.0).astype(BF16)
    vaug_o[1] = jnp.where(low, vr, 1.0).astype(BF16)

    gates_o[:, 0:GQA_WIDTH] = _silu(proj(_A_G, _D_Q))

    dq = proj(_D_Q, _D_K)
    dk = proj(_D_K, _D_V)
    if ctx_out:
        dk_o[...] = dk
    for j in range(DIFF_WIDTH // LANES):
        sq = dq[:, j * LANES:(j + 1) * LANES]
        sk = dk[:, j * LANES:(j + 1) * LANES]
        if rope:
            sq = _rope(sq, cd, s1d, s2d, DIFF_QK_DIM // 4)
            sk = _rope(sk, cd, s1d, s2d, DIFF_QK_DIM // 4)
        qd_o[:, j * LANES:(j + 1) * LANES] = sq.astype(BF16)
        kd_o[:, j * LANES:(j + 1) * LANES] = sk.astype(BF16)
    dv = proj(_D_V, _D_G)
    if ctx_out:
        dv_o[...] = dv
    for j in range(DIFF_WIDTH // LANES):
        sv = dv[:, j * LANES:(j + 1) * LANES]
        vaugd_o[2 * j] = jnp.where(low, sv, 1.0).astype(BF16)
        vaugd_o[2 * j + 1] = jnp.where(low, 1.0, sv).astype(BF16)
    gates_o[:, GQA_WIDTH:GQA_WIDTH + DIFF_WIDTH] = _silu(proj(_D_G, _R_Z))

    zr_o[...] = proj(_R_Z, _R_G)
    gates_o[:, GQA_WIDTH + DIFF_WIDTH:] = _silu(proj(_R_G, IN_WIDTH))


def _in_projection(x, scale, shift, w_in_b, qgain, kgain, bdq, bdk, rope_tabs, tokens_per_batch, ctx_out):
    t_total = x.shape[0]
    tm = 256
    nb_rows = tokens_per_batch // tm
    mod_rows = t_total // scale.shape[0] // tm
    rope = rope_tabs is not None
    full = lambda shape: pl.BlockSpec(shape, lambda i: (0,) * len(shape))
    in_specs = [pl.BlockSpec((tm, D_MODEL), lambda i: (i, 0)),
                pl.BlockSpec((1, 1, D_MODEL), lambda i: (i // mod_rows, 0, 0)),
                pl.BlockSpec((1, 1, D_MODEL), lambda i: (i // mod_rows, 0, 0)),
                full((D_MODEL, IN_WIDTH)),
                full((1, GQA_WIDTH)), full((1, LANES)),
                full((GQA_WIDTH, GQA_WIDTH)), full((LANES, LANES))]
    args = [x, scale, shift, w_in_b, qgain, kgain, bdq, bdk]
    if rope:
        in_specs += [pl.BlockSpec((tm, LANES), lambda i: (i % nb_rows, 0))] * 6
        args += list(rope_tabs)
    row = lambda w: pl.BlockSpec((tm, w), lambda i: (i, 0))
    stk = lambda n: pl.BlockSpec((n, tm, LANES), lambda i: (0, i, 0))
    out_specs = [row(GQA_WIDTH), stk(2), stk(2), row(DIFF_WIDTH), row(DIFF_WIDTH), stk(4),
                 row(D_MODEL), row(4 * HGRN_WIDTH)]
    out_shape = [jax.ShapeDtypeStruct((t_total, GQA_WIDTH), BF16),
                 jax.ShapeDtypeStruct((2, t_total, LANES), BF16),
                 jax.ShapeDtypeStruct((2, t_total, LANES), BF16),
                 jax.ShapeDtypeStruct((t_total, DIFF_WIDTH), BF16),
                 jax.ShapeDtypeStruct((t_total, DIFF_WIDTH), BF16),
                 jax.ShapeDtypeStruct((4, t_total, LANES), BF16),
                 jax.ShapeDtypeStruct((t_total, D_MODEL), F32),
                 jax.ShapeDtypeStruct((t_total, 4 * HGRN_WIDTH), F32)]
    if ctx_out:
        out_specs += [row(LANES), row(LANES), row(DIFF_WIDTH), row(DIFF_WIDTH)]
        out_shape += [jax.ShapeDtypeStruct((t_total, LANES), F32),
                      jax.ShapeDtypeStruct((t_total, LANES), F32),
                      jax.ShapeDtypeStruct((t_total, DIFF_WIDTH), F32),
                      jax.ShapeDtypeStruct((t_total, DIFF_WIDTH), F32)]
    return pl.pallas_call(
        functools.partial(_inproj_kernel, rope=rope, ctx_out=ctx_out),
        grid=(t_total // tm,),
        in_specs=in_specs, out_specs=out_specs, out_shape=out_shape,
        compiler_params=_params(("parallel",)),
        name="in_projection",
    )(*args)


def _softmax_step(s, m_sc):
    m_prev = m_sc[...]
    m_new = jnp.maximum(m_prev, jnp.max(s, axis=1, keepdims=True))
    m_sc[...] = m_new
    return jnp.exp(s - m_new).astype(BF16), jnp.exp(m_prev - m_new)


def _gqa_kernel(*refs, has_cache, tq, tk, nk):
    if has_cache:
        q_ref, kc_ref, vc_ref, k_ref, v_ref, o_ref, m_sc, acc_sc = refs
    else:
        q_ref, k_ref, v_ref, o_ref, m_sc, acc_sc = refs
    q = q_ref[...]
    low = lax.broadcasted_iota(jnp.int32, (tq, LANES), 1) < HEAD_DIM
    zero = jnp.zeros_like(q)
    qs = jnp.concatenate([jnp.where(low, q, zero), jnp.where(low, zero, q)], axis=0)
    m_sc[...] = jnp.full(m_sc.shape, -jnp.inf, F32)
    acc_sc[...] = jnp.zeros(acc_sc.shape, F32)

    def update(k, v):
        s = lax.dot_general(qs, k, _NT, preferred_element_type=F32)
        p, alpha = _softmax_step(s, m_sc)
        acc_sc[...] = alpha * acc_sc[...] + jnp.dot(p, v, preferred_element_type=F32)

    if has_cache:
        update(kc_ref[0, 0], vc_ref[0, 0])

    def body(j, carry):
        off = pl.multiple_of(j * tk, tk)
        update(k_ref[0, pl.ds(off, tk), :], v_ref[0, pl.ds(off, tk), :])
        return carry

    lax.fori_loop(0, nk, body, 0)
    acc = acc_sc[...]
    on = acc / pltpu.roll(acc, HEAD_DIM, 1)
    o_ref[...] = jnp.where(low, on[:tq], pltpu.roll(on[tq:], HEAD_DIM, 1))


def _gqa_attention(qa, kdup, vaug, cache, batch, tokens_per_batch, tq, tk):
    t_total = qa.shape[0]
    nq = tokens_per_batch // tq
    nk = tokens_per_batch // tk
    has_cache = cache is not None
    in_specs = [pl.BlockSpec((tq, LANES), lambda b, p, i: (b * nq + i, p))]
    args = [qa]
    if has_cache:
        past = cache[0].shape[2]
        in_specs += [pl.BlockSpec((1, 1, past, LANES), lambda b, p, i: (b, p // 2, 0, 0))] * 2
        args += list(cache)
    in_specs += [pl.BlockSpec((1, tokens_per_batch, LANES), lambda b, p, i: (p // 2, b, 0))] * 2
    args += [kdup, vaug]
    return pl.pallas_call(
        functools.partial(_gqa_kernel, has_cache=has_cache, tq=tq, tk=tk, nk=nk),
        grid=(batch, GQA_WIDTH // LANES, nq),
        in_specs=in_specs,
        out_specs=pl.BlockSpec((tq, LANES), lambda b, p, i: (b * nq + i, p)),
        out_shape=jax.ShapeDtypeStruct((t_total, GQA_WIDTH), F32),
        scratch_shapes=[pltpu.VMEM((2 * tq, 1), F32), pltpu.VMEM((2 * tq, LANES), F32)],
        compiler_params=_params(("parallel", "parallel", "parallel")),
        name="gqa_attention",
    )(*args)


def _diff_kernel(*refs, has_cache, tq, tk, nk, lam_init):
    if has_cache:
        q_ref, lp_ref, kc_ref, vc0_ref, vc1_ref, k_ref, v0_ref, v1_ref, o_ref, m_sc, acc_sc = refs
    else:
        q_ref, lp_ref, k_ref, v0_ref, v1_ref, o_ref, m_sc, acc_sc = refs
    q = q_ref[...]
    lane = lax.broadcasted_iota(jnp.int32, (tq, LANES), 1)
    zero = jnp.zeros_like(q)
    qs = jnp.concatenate([jnp.where((lane >> 5) == j, q, zero) for j in range(4)], axis=0)
    scale = DIFF_QK_DIM ** -0.5
    m_sc[...] = jnp.full(m_sc.shape, -jnp.inf, F32)
    acc_sc[...] = jnp.zeros(acc_sc.shape, F32)
    h2 = 2 * tq

    def update(k, v0, v1):
        s = lax.dot_general(qs, k, _NT, preferred_element_type=F32) * scale
        p, alpha = _softmax_step(s, m_sc)
        pv = jnp.concatenate([jnp.dot(p[:h2], v0, preferred_element_type=F32),
                              jnp.dot(p[h2:], v1, preferred_element_type=F32)], axis=0)
        acc_sc[...] = alpha * acc_sc[...] + pv

    if has_cache:
        update(kc_ref[0], vc0_ref[0, 0], vc1_ref[0, 0])

    def body(j, carry):
        off = pl.multiple_of(j * tk, tk)
        update(k_ref[pl.ds(off, tk), :], v0_ref[0, pl.ds(off, tk), :], v1_ref[0, pl.ds(off, tk), :])
        return carry

    lax.fori_loop(0, nk, body, 0)

    lp = lp_ref[...]
    lam = (jnp.exp(jnp.sum(lp[0:1] * lp[1:2], axis=1, keepdims=True))
           - jnp.exp(jnp.sum(lp[2:3] * lp[3:4], axis=1, keepdims=True)) + lam_init)
    acc = acc_sc[...]
    on = acc / pltpu.roll(acc, HEAD_DIM, 1)
    o0 = on[0:tq] - lam * on[tq:h2]
    o1 = on[h2:h2 + tq] - lam * on[h2 + tq:]
    o_ref[...] = jnp.where(lane < HEAD_DIM, o0, o1)


def _diff_attention(qd, kd, vaugd, lam_p, cache, batch, tokens_per_batch, tq, tk, lam_init):
    t_total = qd.shape[0]
    nq = tokens_per_batch // tq
    nk = tokens_per_batch // tk
    has_cache = cache is not None
    in_specs = [pl.BlockSpec((tq, LANES), lambda b, p, i: (b * nq + i, p)),
                pl.BlockSpec((4, DIFF_QK_DIM), lambda b, p, i: (0, 0))]
    args = [qd, lam_p]
    if has_cache:
        ckd, cvd = cache
        past = ckd.shape[1]
        in_specs += [pl.BlockSpec((1, past, LANES), lambda b, p, i: (b, 0, p)),
                     pl.BlockSpec((1, 1, past, LANES), lambda b, p, i: (b, 2 * p, 0, 0)),
                     pl.BlockSpec((1, 1, past, LANES), lambda b, p, i: (b, 2 * p + 1, 0, 0))]
        args += [ckd, cvd, cvd]
    in_specs += [pl.BlockSpec((tokens_per_batch, LANES), lambda b, p, i: (b, p)),
                 pl.BlockSpec((1, tokens_per_batch, LANES), lambda b, p, i: (2 * p, b, 0)),
                 pl.BlockSpec((1, tokens_per_batch, LANES), lambda b, p, i: (2 * p + 1, b, 0))]
    args += [kd, vaugd, vaugd]
    return pl.pallas_call(
        functools.partial(_diff_kernel, has_cache=has_cache, tq=tq, tk=tk, nk=nk, lam_init=lam_init),
        grid=(batch, DIFF_WIDTH // LANES, nq),
        in_specs=in_specs,
        out_specs=pl.BlockSpec((tq, LANES), lambda b, p, i: (b * nq + i, p)),
        out_shape=jax.ShapeDtypeStruct((t_total, DIFF_WIDTH), F32),
        scratch_shapes=[pltpu.VMEM((4 * tq, 1), F32), pltpu.VMEM((4 * tq, LANES), F32)],
        compiler_params=_params(("parallel", "parallel", "parallel")),
        name="diff_attention",
    )(*args)


def _hgrn_chunk(q, z, v, lb, st, e_ref, k_sc, g_sc, lhs_sc, rev):
    C, W, c = SCAN_CHUNK, HGRN_WIDTH, SUB_CHUNK
    n_sub = C // c
    row = lax.broadcasted_iota(jnp.int32, (C, W), 0)
    lane = lax.broadcasted_iota(jnp.int32, (C, W), 1)
    qs = _silu(q)
    sig = jax.nn.sigmoid(z)
    om = 1.0 - lb
    k = om * (1.0 - sig)
    g = jnp.log(jnp.maximum(lb + om * sig, FORGET_MIN))

    ri = lax.broadcasted_iota(jnp.int32, (C, C), 0)
    ci = lax.broadcasted_iota(jnp.int32, (C, C), 1)
    tri = jnp.where((ci >= ri) if rev else (ci <= ri), 1.0, 0.0).astype(BF16)
    g1 = g.astype(BF16)
    e1 = g - g1.astype(F32)
    g2 = e1.astype(BF16)
    g3 = (e1 - g2.astype(F32)).astype(BF16)
    G = (jnp.dot(tri, g1, preferred_element_type=F32) + jnp.dot(tri, g2, preferred_element_type=F32)
         + jnp.dot(tri, g3, preferred_element_type=F32))
    k_sc[...] = k
    g_sc[...] = G
    last = 0 if rev else C - 1
    g_last = g_sc[pl.ds(last, 1), :]

    out = lax.dot_general((qs * jnp.exp(G)).astype(BF16), st.astype(BF16), _NT, preferred_element_type=F32)
    k_end = (k * jnp.exp(g_last - G)).astype(BF16)
    vb = v.astype(BF16)
    ut = lax.dot_general(vb, k_end, _TN, preferred_element_type=F32)
    bd = (lax.broadcasted_iota(jnp.int32, (W, W), 0) >> 6) == (lax.broadcasted_iota(jnp.int32, (W, W), 1) >> 6)
    st_new = st * jnp.exp(g_last) + jnp.where(bd, ut, 0.0)

    head_masks = [(lane >> 6) == hh for hh in range(HGRN_HEADS)]

    def block_diag(a):
        return jnp.concatenate([jnp.where(hm, a, jnp.zeros_like(a)) for hm in head_masks], axis=0)

    sub = row >> 4
    if rev:
        bounds = {i: c * (i + 1) for i in range(n_sub - 1)}
    else:
        bounds = {i: c * i - 1 for i in range(1, n_sub)}
    rq = G
    for i, r in bounds.items():
        rq = jnp.where(sub == i, g_sc[pl.ds(r, 1), :], rq)
    qt = (qs * jnp.exp(jnp.minimum(G - rq, 0.0))).astype(BF16)
    a = jnp.zeros((C, W), F32)
    for i, r in bounds.items():
        valid = (row >= c * (i + 1)) if rev else (row < c * i)
        kp = jnp.where(valid, k * jnp.exp(jnp.minimum(g_sc[pl.ds(r, 1), :] - G, 0.0)), 0.0)
        ai = lax.dot_general(qt, block_diag(kp.astype(BF16)), _NT, preferred_element_type=F32)
        a = jnp.where(sub == i, ai, a)

    for sl in range(c):
        kb = jnp.concatenate([jnp.broadcast_to(k_sc[pl.ds(c * i + sl, 1), :], (c, W)) for i in range(n_sub)], axis=0)
        gb = jnp.concatenate([jnp.broadcast_to(g_sc[pl.ds(c * i + sl, 1), :], (c, W)) for i in range(n_sub)], axis=0)
        p = (qs * kb) * jnp.exp(jnp.minimum(G - gb, 0.0))
        lhs_sc[:, sl * W:(sl + 1) * W] = p.astype(BF16)
    rd = jnp.dot(lhs_sc[...], e_ref[...], preferred_element_type=F32)
    scol = lane & (C - 1)
    keep = ((scol >> 4) == sub) & ((scol >= row) if rev else (scol <= row))
    a = a + jnp.where(keep, rd, 0.0)

    out = out + jnp.dot(a.astype(BF16), block_diag(vb), preferred_element_type=F32)
    return out, st_new


def _hgrn_kernel(*refs, layer, n_chunks, has_s0):
    it = iter(refs)
    qf, zf, vf, qb, zb, vb, lbraw, e_ref = [next(it) for _ in range(8)]
    s0 = next(it) if has_s0 else None
    of_ref, ob_ref, st_ref, k_sc, g_sc, lhs_sc = [next(it) for _ in range(6)]

    @pl.when(pl.program_id(1) == 0)
    def _():
        if has_s0:
            st_ref[...] = s0[...]
        else:
            st_ref[...] = jnp.zeros(st_ref.shape, F32)

    lbr = lbraw[...]
    ex = jnp.exp(lbr - jnp.max(lbr, axis=0, keepdims=True))
    sm = ex / jnp.sum(ex, axis=0, keepdims=True)
    lbs = jnp.zeros(sm.shape[1:], F32)
    for j in range(1, layer + 1):
        lbs = lbs + sm[j]
    lb_f, lb_b = lbs[0:1], lbs[1:2]

    def fwd(cidx, carry):
        off = pl.multiple_of(cidx * SCAN_CHUNK, SCAN_CHUNK)
        sl = pl.ds(off, SCAN_CHUNK)
        o, s_new = _hgrn_chunk(qf[sl, :], zf[sl, :], vf[sl, :], lb_f, st_ref[0, 0], e_ref, k_sc, g_sc, lhs_sc, False)
        of_ref[sl, :] = o
        st_ref[0, 0] = s_new
        return carry

    def bwd(cidx, carry):
        off = pl.multiple_of((n_chunks - 1 - cidx) * SCAN_CHUNK, SCAN_CHUNK)
        sl = pl.ds(off, SCAN_CHUNK)
        o, s_new = _hgrn_chunk(qb[sl, :], zb[sl, :], vb[sl, :], lb_b, st_ref[0, 1], e_ref, k_sc, g_sc, lhs_sc, True)
        ob_ref[sl, :] = o
        st_ref[0, 1] = s_new
        return carry

    lax.fori_loop(0, n_chunks, fwd, 0)
    lax.fori_loop(0, n_chunks, bwd, 0)


def _hgrn_scan(zr, lb_raw, e_mat, s0, layer, batch, tokens_per_batch):
    t_total = zr.shape[0]
    tb = 256
    nblk = tokens_per_batch // tb
    W = HGRN_WIDTH
    has_s0 = s0 is not None
    fblk = lambda col: pl.BlockSpec((tb, W), lambda b, j: (b * nblk + j, col))
    bblk = lambda col: pl.BlockSpec((tb, W), lambda b, j: (b * nblk + nblk - 1 - j, col))
    in_specs = [fblk(0), fblk(1), fblk(3), bblk(0), bblk(2), bblk(3),
                pl.BlockSpec((DEPTH, 2, W), lambda b, j: (0, 0, 0)),
                pl.BlockSpec((SUB_CHUNK * W, W), lambda b, j: (0, 0))]
    args = [zr] * 6 + [lb_raw, e_mat]
    if has_s0:
        in_specs.append(pl.BlockSpec((1, 2, W, W), lambda b, j: (b, 0, 0, 0)))
        args.append(s0)
    return pl.pallas_call(
        functools.partial(_hgrn_kernel, layer=layer, n_chunks=tb // SCAN_CHUNK, has_s0=has_s0),
        grid=(batch, nblk),
        in_specs=in_specs,
        out_specs=[pl.BlockSpec((tb, W), lambda b, j: (b * nblk + j, 0)),
                   pl.BlockSpec((tb, W), lambda b, j: (b * nblk + nblk - 1 - j, 0)),
                   pl.BlockSpec((1, 2, W, W), lambda b, j: (b, 0, 0, 0))],
        out_shape=[jax.ShapeDtypeStruct((t_total, W), F32),
                   jax.ShapeDtypeStruct((t_total, W), F32),
                   jax.ShapeDtypeStruct((batch, 2, W, W), F32)],
        scratch_shapes=[pltpu.VMEM((SCAN_CHUNK, W), F32), pltpu.VMEM((SCAN_CHUNK, W), F32),
                        pltpu.VMEM((SCAN_CHUNK, SUB_CHUNK * W), BF16)],
        compiler_params=_params(("parallel", "arbitrary")),
        name="hgrn_scan",
    )(*args)


def _outproj_kernel(oa_ref, od_ref, of_ref, ob_ref, g_ref, x_ref, gate_ref, w_ref, sub_ref, hgn_ref,
                    bd_ref, lng_ref, lnb_ref, o_ref, *, lam_init):
    inv_d = 1.0 / HEAD_DIM
    g = g_ref[...]
    bd = bd_ref[...]

    def head_rms(t, gain):
        ssq = jnp.dot((t * t).astype(BF16), bd, preferred_element_type=F32)
        return t * lax.rsqrt(ssq * inv_d + RMS_EPS) * gain

    ma = (oa_ref[...] * g[:, 0:GQA_WIDTH]).astype(BF16)
    md = (head_rms(od_ref[...], sub_ref[...]) * (1.0 - lam_init)
          * g[:, GQA_WIDTH:GQA_WIDTH + DIFF_WIDTH]).astype(BF16)
    mr = (head_rms(of_ref[...] + ob_ref[...], hgn_ref[...]) * g[:, GQA_WIDTH + DIFF_WIDTH:]).astype(BF16)
    y = (jnp.dot(ma, w_ref[0:GQA_WIDTH, :], preferred_element_type=F32)
         + jnp.dot(md, w_ref[GQA_WIDTH:GQA_WIDTH + DIFF_WIDTH, :], preferred_element_type=F32)
         + jnp.dot(mr, w_ref[GQA_WIDTH + DIFF_WIDTH:, :], preferred_element_type=F32))
    xn = DEEPNORM_ALPHA * x_ref[...] + gate_ref[0] * y
    mu = jnp.mean(xn, axis=-1, keepdims=True)
    xc = xn - mu
    var = jnp.mean(xc * xc, axis=-1, keepdims=True)
    o_ref[...] = xc * lax.rsqrt(var + LN_EPS) * lng_ref[...] + lnb_ref[...]


def _out_projection(out_a, out_d, o_f, o_b, gates, x, gate, w_out_b, subln, hgn, bd, ln_g, ln_b,
                    tokens_per_batch, lam_init):
    t_total = x.shape[0]
    tm = 256
    mod_rows = t_total // gate.shape[0] // tm
    row = lambda w: pl.BlockSpec((tm, w), lambda i: (i, 0))
    full = lambda shape: pl.BlockSpec(shape, lambda i: (0,) * len(shape))
    return pl.pallas_call(
        functools.partial(_outproj_kernel, lam_init=lam_init),
        grid=(t_total // tm,),
        in_specs=[row(GQA_WIDTH), row(DIFF_WIDTH), row(HGRN_WIDTH), row(HGRN_WIDTH), row(D_MODEL), row(D_MODEL),
                  pl.BlockSpec((1, 1, D_MODEL), lambda i: (i // mod_rows, 0, 0)),
                  full((D_MODEL, D_MODEL)), full((1, DIFF_WIDTH)), full((1, HGRN_WIDTH)),
                  full((DIFF_WIDTH, DIFF_WIDTH)), full((1, D_MODEL)), full((1, D_MODEL))],
        out_specs=row(D_MODEL),
        out_shape=jax.ShapeDtypeStruct((t_total, D_MODEL), F32),
        compiler_params=_params(("parallel",)),
        name="out_projection",
    )(out_a, out_d, o_f, o_b, gates, x, gate, w_out_b, subln, hgn, bd, ln_g, ln_b)


def _block_diag_ones(n):
    idx = np.arange(n) // HEAD_DIM
    return jnp.asarray((idx[:, None] == idx[None, :]).astype(np.float32), dtype=BF16)


def _channel_sum_matrix():
    sl = np.arange(SUB_CHUNK)[:, None, None]
    ln = np.arange(HGRN_WIDTH)[None, :, None]
    col = np.arange(HGRN_WIDTH)[None, None, :]
    e = ((col // HEAD_DIM) == (ln // HEAD_DIM)) & (((col % SCAN_CHUNK) % SUB_CHUNK) == sl)
    return jnp.asarray(e.reshape(SUB_CHUNK * HGRN_WIDTH, HGRN_WIDTH).astype(np.float32), dtype=BF16)


def _rope_tables(n_tokens, dim):
    rows = n_tokens // GRID_W
    row = jnp.repeat(jnp.arange(rows, dtype=F32), GRID_W)
    col = jnp.tile(jnp.arange(GRID_W, dtype=F32), rows)
    quarter = dim // 4
    inv_freq = ROPE_THETA ** (-jnp.arange(quarter, dtype=F32) / quarter)
    ar = row[:, None] * inv_freq[None, :]
    ac = col[:, None] * inv_freq[None, :]
    ang = jnp.concatenate([ar, ar, ac, ac], axis=-1)
    cos = jnp.tile(jnp.cos(ang), (1, LANES // dim))
    sin = jnp.tile(jnp.sin(ang), (1, LANES // dim))
    first = (jnp.arange(LANES) % (dim // 2)) < quarter
    return cos, jnp.where(first, -sin, 0.0), jnp.where(first, 0.0, sin)


def _mixer_layer(x, scale, shift, gate, layer, batch, tokens_per_batch, rope_tabs, cache, consts, weights, ctx_out):
    (w_in_b, w_out_b, qgain, kgain, lam_p, subln, hgn, lb_raw, ln_g, ln_b) = weights
    bdq, bdk, bdd, e_mat = consts
    lam_init = 0.8 - 0.6 * math.exp(-0.3 * layer)
    outs = _in_projection(x, scale, shift, w_in_b, qgain, kgain, bdq, bdk, rope_tabs, tokens_per_batch, ctx_out)
    qa, kdup, vaug, qd, kd, vaugd, gates, zr = outs[:8]
    if cache is None:
        gqa_cache = diff_cache = s0 = None
        tq_a = tq_d = tk = tokens_per_batch
    else:
        gqa_cache, diff_cache, s0 = cache
        tq_a, tq_d, tk = 256, 128, 512
    out_a = _gqa_attention(qa, kdup, vaug, gqa_cache, batch, tokens_per_batch, tq_a, tk)
    out_d = _diff_attention(qd, kd, vaugd, lam_p, diff_cache, batch, tokens_per_batch, tq_d, tk, lam_init)
    o_f, o_b, st = _hgrn_scan(zr, lb_raw, e_mat, s0, layer, batch, tokens_per_batch)
    x_new = _out_projection(out_a, out_d, o_f, o_b, gates, x, gate, w_out_b, subln, hgn, bdd, ln_g, ln_b,
                            tokens_per_batch, lam_init)
    return x_new, outs[8:], st


def kernel(x_prompt, x_sample, cache_gqa_k, cache_gqa_v, cache_diff_k, cache_diff_v, state_hgrn, c, c_ctx,
           w_ada, b_ada, w_in, gqa_q_norm, gqa_k_norm, diff_lambda, diff_subln, hgrn_lower_bounds, hgrn_norm,
           w_out, ln_g, ln_b):
    batch, seq, _ = x_prompt.shape
    dec_batch, dec_seq, _ = x_sample.shape
    past = cache_gqa_k.shape[2]
    H, HD = HGRN_HEADS, HEAD_DIM

    cond = jnp.zeros((MOD_ROWS, D_MODEL), F32).at[0].set(c_ctx).at[1:1 + dec_batch].set(c)
    mod = _modulation(cond, w_ada, b_ada)
    consts = (_block_diag_ones(GQA_WIDTH), _block_diag_ones(LANES), _block_diag_ones(DIFF_WIDTH),
              _channel_sum_matrix())
    rope_tabs = _rope_tables(dec_seq, HEAD_DIM) + _rope_tables(dec_seq, DIFF_QK_DIM)
    w_in_b = w_in.astype(BF16)
    w_out_b = w_out.astype(BF16)

    def layer_weights(l):
        return (w_in_b[l], w_out_b[l],
                jnp.tile(gqa_q_norm[l], GQA_WIDTH // HD)[None, :] * (HD ** -0.5),
                jnp.tile(gqa_k_norm[l], LANES // HD)[None, :],
                diff_lambda[l],
                jnp.tile(diff_subln[l], DIFF_WIDTH // HD)[None, :],
                jnp.tile(hgrn_norm[l], HGRN_WIDTH // HD)[None, :],
                hgrn_lower_bounds, ln_g[l][None, :], ln_b[l][None, :])

    def mod_rows(l, lo, n):
        m = mod[l, lo:lo + n]
        return (m[:, None, 0:D_MODEL], m[:, None, D_MODEL:2 * D_MODEL], m[:, None, 2 * D_MODEL:])

    def states_out(st):
        s6 = st.reshape(st.shape[0], 2, H, HD, H, HD)
        return jnp.swapaxes(jnp.stack([s6[:, :, h, :, h, :] for h in range(H)], axis=2), -1, -2)

    x = x_prompt.reshape(batch * seq, D_MODEL)
    ctx_leaves = []
    for l in range(DEPTH):
        shift, scale, gate = mod_rows(l, 0, 1)
        x, extra, st = _mixer_layer(x, scale, shift, gate, l, batch, seq, None, None, consts,
                                    layer_weights(l), True)
        kn, av, dk, dv = extra
        ctx_leaves.append((kn.reshape(batch, seq, GQA_KV_HEADS, HD), av.reshape(batch, seq, GQA_KV_HEADS, HD),
                           dk.reshape(batch, seq, DIFF_HEADS, HD), dv.reshape(batch, seq, DIFF_HEADS, HD),
                           states_out(st)))
    y_prompt = x.reshape(batch, seq, D_MODEL)
    new_leaves = [jnp.stack([lv[i] for lv in ctx_leaves], axis=1) for i in range(5)]

    ones = jnp.ones((dec_batch, past, HD), F32)
    eye = jnp.eye(H, dtype=F32)
    x = x_sample.reshape(dec_batch * dec_seq, D_MODEL)
    for l in range(DEPTH):
        shift, scale, gate = mod_rows(l, 1, dec_batch)
        ck, cv = cache_gqa_k[:, l], cache_gqa_v[:, l]
        ckd = jnp.stack([jnp.concatenate([ck[:, :, g], ck[:, :, g]], axis=-1) for g in range(GQA_KV_HEADS)], axis=1)
        cvd = jnp.stack([jnp.concatenate([cv[:, :, g], ones], axis=-1) for g in range(GQA_KV_HEADS)], axis=1)
        dkc = cache_diff_k[:, l].reshape(dec_batch, past, DIFF_WIDTH)
        dvc = cache_diff_v[:, l]
        dvd = jnp.stack([jnp.concatenate([dvc[:, :, h], ones] if h % 2 == 0 else [ones, dvc[:, :, h]], axis=-1)
                         for h in range(DIFF_HEADS)], axis=1)
        s0t = jnp.swapaxes(state_hgrn[:, l].astype(F32), -1, -2)
        s0 = (s0t[:, :, :, :, None, :] * eye[None, None, :, None, :, None]).reshape(dec_batch, 2, H * HD, H * HD)
        cache = ((ckd.astype(BF16), cvd.astype(BF16)), (dkc.astype(BF16), dvd.astype(BF16)), s0)
        x, _, _ = _mixer_layer(x, scale, shift, gate, l, dec_batch, dec_seq, rope_tabs, cache, consts,
                               layer_weights(l), False)
    y_sample = x.reshape(dec_batch, dec_seq, D_MODEL)
    return (y_prompt, y_sample) + tuple(new_leaves)
```

```python
import functools
import math

import numpy as np
import jax
import jax.numpy as jnp
from jax import lax
from jax.experimental import pallas as pl
from jax.experimental.pallas import tpu as pltpu

F32 = jnp.float32
BF16 = jnp.bfloat16

D_MODEL = 1024
DEPTH = 2
GRID_W = 64
HEAD_DIM = 64
GQA_WIDTH = 512
GQA_KV_HEADS = 2
DIFF_WIDTH = 256
DIFF_HEADS = 4
DIFF_QK_DIM = 32
HGRN_WIDTH = 256
HGRN_HEADS = 4
IN_WIDTH = 3584
SCAN_CHUNK = 64
SUB_CHUNK = 16
ROPE_THETA = 10000.0
RMS_EPS = 1e-6
LN_EPS = 1e-5
FORGET_MIN = 1e-6
DEEPNORM_ALPHA = (2 * DEPTH) ** 0.25
LANES = 128
MOD_ROWS = 8
VMEM_LIMIT = 48 * 1024 * 1024
SM_ROWS = 32
LOG2E = math.log2(math.e)

_A_Q, _A_K, _A_V, _A_G = 0, 512, 640, 768
_D_Q, _D_K, _D_V, _D_G = 1280, 1536, 1792, 2048
_R_Z, _R_G = 2304, 3328

_NT = (((1,), (1,)), ((), ()))
_TN = (((0,), (0,)), ((), ()))


def _silu(x):
    return x * jax.nn.sigmoid(x)


def _params(sem):
    return pltpu.CompilerParams(dimension_semantics=sem, vmem_limit_bytes=VMEM_LIMIT)


def _mod_kernel(c_ref, w_ref, b_ref, o_ref):
    c = c_ref[...]
    o_ref[0] = jnp.dot(_silu(c), w_ref[0], preferred_element_type=F32) + b_ref[0]


def _modulation(cond, w_ada, b_ada):
    tn = 1024
    return pl.pallas_call(
        _mod_kernel,
        grid=(DEPTH, 3 * D_MODEL // tn),
        in_specs=[pl.BlockSpec((MOD_ROWS, D_MODEL), lambda l, j: (0, 0)),
                  pl.BlockSpec((1, D_MODEL, tn), lambda l, j: (l, 0, j)),
                  pl.BlockSpec((1, 1, tn), lambda l, j: (l, 0, j))],
        out_specs=pl.BlockSpec((1, MOD_ROWS, tn), lambda l, j: (l, 0, j)),
        out_shape=jax.ShapeDtypeStruct((DEPTH, MOD_ROWS, 3 * D_MODEL), F32),
        compiler_params=_params(("arbitrary", "arbitrary")),
        name="modulation",
    )(cond, w_ada, b_ada.reshape(DEPTH, 1, 3 * D_MODEL))


def _rope(x, c, s1, s2, shift):
    return x * c + pltpu.roll(x, LANES - shift, 1) * s1 + pltpu.roll(x, shift, 1) * s2


def _inproj_kernel(*refs, rope, ctx_out):
    it = iter(refs)
    x_ref, sc_ref, sh_ref, w_ref, qg_ref, kg_ref, bdq_ref, bdk_ref = [next(it) for _ in range(8)]
    if rope:
        ca, s1a, s2a, cd, s1d, s2d = [next(it)[...] for _ in range(6)]
    qa_o, kdup_o, vaug_o, qd_o, kd_o, vaugd_o, gates_o, zr_o = [next(it) for _ in range(8)]
    if ctx_out:
        kn_o, av_o, dk_o, dv_o = [next(it) for _ in range(4)]

    h = (x_ref[...] * (1.0 + sc_ref[0]) + sh_ref[0]).astype(BF16)

    def proj(lo, hi):
        return jnp.dot(h, w_ref[:, lo:hi], preferred_element_type=F32)

    tm = h.shape[0]
    low = lax.broadcasted_iota(jnp.int32, (tm, LANES), 1) < HEAD_DIM
    inv_d = 1.0 / HEAD_DIM

    aq = proj(_A_Q, _A_K)
    ssq = jnp.dot((aq * aq).astype(BF16), bdq_ref[...], preferred_element_type=F32)
    qn = aq * lax.rsqrt(ssq * inv_d + RMS_EPS) * qg_ref[...]
    for j in range(GQA_WIDTH // LANES):
        slab = qn[:, j * LANES:(j + 1) * LANES]
        if rope:
            slab = _rope(slab, ca, s1a, s2a, HEAD_DIM // 4)
        qa_o[:, j * LANES:(j + 1) * LANES] = slab.astype(BF16)

    ak = proj(_A_K, _A_V)
    ssqk = jnp.dot((ak * ak).astype(BF16), bdk_ref[...], preferred_element_type=F32)
    kn = ak * lax.rsqrt(ssqk * inv_d + RMS_EPS) * kg_ref[...]
    if ctx_out:
        kn_o[...] = kn
    if rope:
        kn = _rope(kn, ca, s1a, s2a, HEAD_DIM // 4)
    kr = pltpu.roll(kn, HEAD_DIM, 1)
    kdup_o[0] = jnp.where(low, kn, kr).astype(BF16)
    kdup_o[1] = jnp.where(low, kr, kn).astype(BF16)

    av = proj(_A_V, _A_G)
    if ctx_out:
        av_o[...] = av
    vr = pltpu.roll(av, HEAD_DIM, 1)
    vaug_o[0] = jnp.where(low, av, 1.0).astype(BF16)
    vaug_o[1] = jnp.where(low, vr, 1.0).astype(BF16)

    gates_o[:, 0:GQA_WIDTH] = _silu(proj(_A_G, _D_Q))

    dq = proj(_D_Q, _D_K) * (DIFF_QK_DIM ** -0.5 * LOG2E)
    dk = proj(_D_K, _D_V)
    if ctx_out:
        dk_o[...] = dk
    for j in range(DIFF_WIDTH // LANES):
        sq = dq[:, j * LANES:(j + 1) * LANES]
        sk = dk[:, j * LANES:(j + 1) * LANES]
        if rope:
            sq = _rope(sq, cd, s1d, s2d, DIFF_QK_DIM // 4)
            sk = _rope(sk, cd, s1d, s2d, DIFF_QK_DIM // 4)
        qd_o[:, j * LANES:(j + 1) * LANES] = sq.astype(BF16)
        kd_o[:, j * LANES:(j + 1) * LANES] = sk.astype(BF16)
    dv = proj(_D_V, _D_G)
    if ctx_out:
        dv_o[...] = dv
    for j in range(DIFF_WIDTH // LANES):
        sv = dv[:, j * LANES:(j + 1) * LANES]
        vaugd_o[2 * j] = jnp.where(low, sv, 1.0).astype(BF16)
        vaugd_o[2 * j + 1] = jnp.where(low, 1.0, sv).astype(BF16)
    gates_o[:, GQA_WIDTH:GQA_WIDTH + DIFF_WIDTH] = _silu(proj(_D_G, _R_Z))

    zr_o[...] = proj(_R_Z, _R_G)
    gates_o[:, GQA_WIDTH + DIFF_WIDTH:] = _silu(proj(_R_G, IN_WIDTH))


def _in_projection(x, scale, shift, w_in_b, qgain, kgain, bdq, bdk, rope_tabs, tokens_per_batch, ctx_out):
    t_total = x.shape[0]
    tm = 256
    nb_rows = tokens_per_batch // tm
    mod_rows = t_total // scale.shape[0] // tm
    rope = rope_tabs is not None
    full = lambda shape: pl.BlockSpec(shape, lambda i: (0,) * len(shape))
    in_specs = [pl.BlockSpec((tm, D_MODEL), lambda i: (i, 0)),
                pl.BlockSpec((1, 1, D_MODEL), lambda i: (i // mod_rows, 0, 0)),
                pl.BlockSpec((1, 1, D_MODEL), lambda i: (i // mod_rows, 0, 0)),
                full((D_MODEL, IN_WIDTH)),
                full((1, GQA_WIDTH)), full((1, LANES)),
                full((GQA_WIDTH, GQA_WIDTH)), full((LANES, LANES))]
    args = [x, scale, shift, w_in_b, qgain, kgain, bdq, bdk]
    if rope:
        in_specs += [pl.BlockSpec((tm, LANES), lambda i: (i % nb_rows, 0))] * 6
        args += list(rope_tabs)
    row = lambda w: pl.BlockSpec((tm, w), lambda i: (i, 0))
    stk = lambda n: pl.BlockSpec((n, tm, LANES), lambda i: (0, i, 0))
    out_specs = [row(GQA_WIDTH), stk(2), stk(2), row(DIFF_WIDTH), row(DIFF_WIDTH), stk(4),
                 row(D_MODEL), row(4 * HGRN_WIDTH)]
    out_shape = [jax.ShapeDtypeStruct((t_total, GQA_WIDTH), BF16),
                 jax.ShapeDtypeStruct((2, t_total, LANES), BF16),
                 jax.ShapeDtypeStruct((2, t_total, LANES), BF16),
                 jax.ShapeDtypeStruct((t_total, DIFF_WIDTH), BF16),
                 jax.ShapeDtypeStruct((t_total, DIFF_WIDTH), BF16),
                 jax.ShapeDtypeStruct((4, t_total, LANES), BF16),
                 jax.ShapeDtypeStruct((t_total, D_MODEL), F32),
                 jax.ShapeDtypeStruct((t_total, 4 * HGRN_WIDTH), F32)]
    if ctx_out:
        out_specs += [row(LANES), row(LANES), row(DIFF_WIDTH), row(DIFF_WIDTH)]
        out_shape += [jax.ShapeDtypeStruct((t_total, LANES), F32),
                      jax.ShapeDtypeStruct((t_total, LANES), F32),
                      jax.ShapeDtypeStruct((t_total, DIFF_WIDTH), F32),
                      jax.ShapeDtypeStruct((t_total, DIFF_WIDTH), F32)]
    return pl.pallas_call(
        functools.partial(_inproj_kernel, rope=rope, ctx_out=ctx_out),
        grid=(t_total // tm,),
        in_specs=in_specs, out_specs=out_specs, out_shape=out_shape,
        compiler_params=_params(("parallel",)),
        name="in_projection",
    )(*args)


def _chunk_rows(j, tk):
    return pl.ds(j * tk, tk) if isinstance(j, int) else pl.ds(pl.multiple_of(j * tk, tk), tk)


def _attention_scratch(rows, tk):
    return ([pltpu.VMEM((rows, LANES), BF16)]
            + [pltpu.VMEM((rows, tk), F32)] * 2 + [pltpu.VMEM((rows, tk), BF16)] * 2
            + [pltpu.VMEM((rows, LANES), F32)] * 4)


def _softmax_stage(s_sc, p_sc, a_sc, m_sc):
    tk = s_sc.shape[1]
    for r in range(s_sc.shape[0] // SM_ROWS):
        rows = pl.ds(r * SM_ROWS, SM_ROWS)
        s = s_sc[rows, :]
        m_prev = m_sc[rows, :]
        m_new = jnp.maximum(m_prev, jnp.max(s, axis=1, keepdims=True))
        m_sc[rows, :] = m_new
        p_sc[rows, :] = jnp.exp2(s - jnp.concatenate([m_new] * (tk // LANES), axis=1)).astype(BF16)
        a_sc[rows, :] = jnp.exp2(m_prev - m_new)


def _flash_pipeline(n_new, cache, k_at, v_at, qk, sm, av):
    o = 0 if cache is None else 1
    n = n_new + o
    k_of = lambda c: cache[0]() if (o and c == 0) else k_at(c - o)
    v_of = lambda c: cache[1]() if (o and c == 0) else v_at(c - o)
    qk(0, k_of(0))
    if n == 1:
        sm(0)
        av(0, v_of(0))
        return
    qk(1, k_of(1))
    sm(0)
    t0 = 2
    if o:
        assert n > 2
        qk(0, k_of(2))
        sm(1)
        av(0, v_of(0))
        t0 = 3
    assert (n - t0) % 2 == 0

    def pair(i, carry):
        t = t0 + 2 * i
        for d in (0, 1):
            par = (t0 + d) % 2
            qk(par, k_at(t + d - o))
            sm(1 - par)
            av(par, v_at(t + d - 2 - o))
        return carry

    lax.fori_loop(0, (n - t0) // 2, pair, 0)
    sm((n - 1) % 2)
    av(n % 2, v_of(n - 2))
    av((n - 1) % 2, v_of(n - 1))


def _gqa_kernel(*refs, has_cache, tq, tk, nk):
    if has_cache:
        q_ref, kc_ref, vc_ref = refs[:3]
        refs = refs[3:]
    else:
        q_ref = refs[0]
        refs = refs[1:]
    k_ref, v_ref, o_ref, qs_sc, s0, s1, p0, p1, a0, a1, m_sc, acc_sc = refs
    s_b, p_b, a_b = (s0, s1), (p0, p1), (a0, a1)
    q = q_ref[...]
    low = lax.broadcasted_iota(jnp.int32, (tq, LANES), 1) < HEAD_DIM
    zero = jnp.zeros_like(q)
    qs_sc[0:tq, :] = jnp.where(low, q, zero)
    qs_sc[tq:, :] = jnp.where(low, zero, q)
    m_sc[...] = jnp.full(m_sc.shape, -jnp.inf, F32)
    acc_sc[...] = jnp.zeros(acc_sc.shape, F32)

    def qk(slot, k):
        s_b[slot][...] = lax.dot_general(qs_sc[...], k, _NT, preferred_element_type=F32)

    def sm(slot):
        _softmax_stage(s_b[slot], p_b[slot], a_b[slot], m_sc)

    def av(slot, v):
        acc_sc[...] = a_b[slot][...] * acc_sc[...] + jnp.dot(p_b[slot][...], v, preferred_element_type=F32)

    chunk = lambda ref: (lambda j: ref[0, _chunk_rows(j, tk), :])
    cache = (lambda: kc_ref[0, 0], lambda: vc_ref[0, 0]) if has_cache else None
    _flash_pipeline(nk, cache, chunk(k_ref), chunk(v_ref), qk, sm, av)

    acc = acc_sc[...]
    on = acc / pltpu.roll(acc, HEAD_DIM, 1)
    o_ref[...] = jnp.where(low, on[:tq], pltpu.roll(on[tq:], HEAD_DIM, 1))


def _gqa_attention(qa, kdup, vaug, cache, batch, tokens_per_batch, tq, tk):
    t_total = qa.shape[0]
    nq = tokens_per_batch // tq
    nk = tokens_per_batch // tk
    has_cache = cache is not None
    in_specs = [pl.BlockSpec((tq, LANES), lambda b, p, i: (b * nq + i, p))]
    args = [qa]
    if has_cache:
        past = cache[0].shape[2]
        in_specs += [pl.BlockSpec((1, 1, past, LANES), lambda b, p, i: (b, p // 2, 0, 0))] * 2
        args += list(cache)
    in_specs += [pl.BlockSpec((1, tokens_per_batch, LANES), lambda b, p, i: (p // 2, b, 0))] * 2
    args += [kdup, vaug]
    return pl.pallas_call(
        functools.partial(_gqa_kernel, has_cache=has_cache, tq=tq, tk=tk, nk=nk),
        grid=(batch, GQA_WIDTH // LANES, nq),
        in_specs=in_specs,
        out_specs=pl.BlockSpec((tq, LANES), lambda b, p, i: (b * nq + i, p)),
        out_shape=jax.ShapeDtypeStruct((t_total, GQA_WIDTH), F32),
        scratch_shapes=_attention_scratch(2 * tq, tk),
        compiler_params=_params(("parallel", "parallel", "parallel")),
        name="gqa_attention",
    )(*args)


def _diff_kernel(*refs, has_cache, tq, tk, nk, lam_init):
    q_ref, lp_ref = refs[:2]
    if has_cache:
        kc_ref, vc0_ref, vc1_ref = refs[2:5]
        refs = refs[5:]
    else:
        refs = refs[2:]
    k_ref, v0_ref, v1_ref, o_ref, qs_sc, s0, s1, p0, p1, a0, a1, m_sc, acc_sc = refs
    s_b, p_b, a_b = (s0, s1), (p0, p1), (a0, a1)
    q = q_ref[...]
    lane = lax.broadcasted_iota(jnp.int32, (tq, LANES), 1)
    zero = jnp.zeros_like(q)
    for j in range(4):
        qs_sc[j * tq:(j + 1) * tq, :] = jnp.where((lane >> 5) == j, q, zero)
    m_sc[...] = jnp.full(m_sc.shape, -jnp.inf, F32)
    acc_sc[...] = jnp.zeros(acc_sc.shape, F32)
    h2 = 2 * tq

    def qk(slot, k):
        s_b[slot][...] = lax.dot_general(qs_sc[...], k, _NT, preferred_element_type=F32)

    def sm(slot):
        _softmax_stage(s_b[slot], p_b[slot], a_b[slot], m_sc)

    def av(slot, v):
        v0, v1 = v
        acc_sc[0:h2, :] = (a_b[slot][0:h2, :] * acc_sc[0:h2, :]
                           + jnp.dot(p_b[slot][0:h2, :], v0, preferred_element_type=F32))
        acc_sc[h2:, :] = (a_b[slot][h2:, :] * acc_sc[h2:, :]
                          + jnp.dot(p_b[slot][h2:, :], v1, preferred_element_type=F32))

    k_at = lambda j: k_ref[_chunk_rows(j, tk), :]
    v_at = lambda j: (v0_ref[0, _chunk_rows(j, tk), :], v1_ref[0, _chunk_rows(j, tk), :])
    cache = (lambda: kc_ref[0], lambda: (vc0_ref[0, 0], vc1_ref[0, 0])) if has_cache else None
    _flash_pipeline(nk, cache, k_at, v_at, qk, sm, av)

    lp = lp_ref[...]
    lam = (jnp.exp(jnp.sum(lp[0:1] * lp[1:2], axis=1, keepdims=True))
           - jnp.exp(jnp.sum(lp[2:3] * lp[3:4], axis=1, keepdims=True)) + lam_init)
    acc = acc_sc[...]
    on = acc / pltpu.roll(acc, HEAD_DIM, 1)
    o0 = on[0:tq] - lam * on[tq:h2]
    o1 = on[h2:h2 + tq] - lam * on[h2 + tq:]
    o_ref[...] = jnp.where(lane < HEAD_DIM, o0, o1)


def _diff_attention(qd, kd, vaugd, lam_p, cache, batch, tokens_per_batch, tq, tk, lam_init):
    t_total = qd.shape[0]
    nq = tokens_per_batch // tq
    nk = tokens_per_batch // tk
    has_cache = cache is not None
    in_specs = [pl.BlockSpec((tq, LANES), lambda b, p, i: (b * nq + i, p)),
                pl.BlockSpec((4, DIFF_QK_DIM), lambda b, p, i: (0, 0))]
    args = [qd, lam_p]
    if has_cache:
        ckd, cvd = cache
        past = ckd.shape[1]
        in_specs += [pl.BlockSpec((1, past, LANES), lambda b, p, i: (b, 0, p)),
                     pl.BlockSpec((1, 1, past, LANES), lambda b, p, i: (b, 2 * p, 0, 0)),
                     pl.BlockSpec((1, 1, past, LANES), lambda b, p, i: (b, 2 * p + 1, 0, 0))]
        args += [ckd, cvd, cvd]
    in_specs += [pl.BlockSpec((tokens_per_batch, LANES), lambda b, p, i: (b, p)),
                 pl.BlockSpec((1, tokens_per_batch, LANES), lambda b, p, i: (2 * p, b, 0)),
                 pl.BlockSpec((1, tokens_per_batch, LANES), lambda b, p, i: (2 * p + 1, b, 0))]
    args += [kd, vaugd, vaugd]
    return pl.pallas_call(
        functools.partial(_diff_kernel, has_cache=has_cache, tq=tq, tk=tk, nk=nk, lam_init=lam_init),
        grid=(batch, DIFF_WIDTH // LANES, nq),
        in_specs=in_specs,
        out_specs=pl.BlockSpec((tq, LANES), lambda b, p, i: (b * nq + i, p)),
        out_shape=jax.ShapeDtypeStruct((t_total, DIFF_WIDTH), F32),
        scratch_shapes=_attention_scratch(4 * tq, tk),
        compiler_params=_params(("parallel", "parallel", "parallel")),
        name="diff_attention",
    )(*args)


def _hgrn_chunk(q, z, v, lb, st, e_ref, k_sc, g_sc, lhs_sc, rev):
    C, W, c = SCAN_CHUNK, HGRN_WIDTH, SUB_CHUNK
    n_sub = C // c
    row = lax.broadcasted_iota(jnp.int32, (C, W), 0)
    lane = lax.broadcasted_iota(jnp.int32, (C, W), 1)
    qs = _silu(q)
    sig = jax.nn.sigmoid(z)
    om = 1.0 - lb
    k = om * (1.0 - sig)
    g = jnp.log(jnp.maximum(lb + om * sig, FORGET_MIN))

    ri = lax.broadcasted_iota(jnp.int32, (C, C), 0)
    ci = lax.broadcasted_iota(jnp.int32, (C, C), 1)
    tri = jnp.where((ci >= ri) if rev else (ci <= ri), 1.0, 0.0).astype(BF16)
    g1 = g.astype(BF16)
    e1 = g - g1.astype(F32)
    g2 = e1.astype(BF16)
    g3 = (e1 - g2.astype(F32)).astype(BF16)
    G = (jnp.dot(tri, g1, preferred_element_type=F32) + jnp.dot(tri, g2, preferred_element_type=F32)
         + jnp.dot(tri, g3, preferred_element_type=F32))
    k_sc[...] = k
    g_sc[...] = G
    last = 0 if rev else C - 1
    g_last = g_sc[pl.ds(last, 1), :]

    out = lax.dot_general((qs * jnp.exp(G)).astype(BF16), st.astype(BF16), _NT, preferred_element_type=F32)
    k_end = (k * jnp.exp(g_last - G)).astype(BF16)
    vb = v.astype(BF16)
    ut = lax.dot_general(vb, k_end, _TN, preferred_element_type=F32)
    bd = (lax.broadcasted_iota(jnp.int32, (W, W), 0) >> 6) == (lax.broadcasted_iota(jnp.int32, (W, W), 1) >> 6)
    st_new = st * jnp.exp(g_last) + jnp.where(bd, ut, 0.0)

    head_masks = [(lane >> 6) == hh for hh in range(HGRN_HEADS)]

    def block_diag(a):
        return jnp.concatenate([jnp.where(hm, a, jnp.zeros_like(a)) for hm in head_masks], axis=0)

    sub = row >> 4
    if rev:
        bounds = {i: c * (i + 1) for i in range(n_sub - 1)}
    else:
        bounds = {i: c * i - 1 for i in range(1, n_sub)}
    rq = G
    for i, r in bounds.items():
        rq = jnp.where(sub == i, g_sc[pl.ds(r, 1), :], rq)
    qt = (qs * jnp.exp(jnp.minimum(G - rq, 0.0))).astype(BF16)
    a = jnp.zeros((C, W), F32)
    for i, r in bounds.items():
        valid = (row >= c * (i + 1)) if rev else (row < c * i)
        kp = jnp.where(valid, k * jnp.exp(jnp.minimum(g_sc[pl.ds(r, 1), :] - G, 0.0)), 0.0)
        ai = lax.dot_general(qt, block_diag(kp.astype(BF16)), _NT, preferred_element_type=F32)
        a = jnp.where(sub == i, ai, a)

    for sl in range(c):
        kb = jnp.concatenate([jnp.broadcast_to(k_sc[pl.ds(c * i + sl, 1), :], (c, W)) for i in range(n_sub)], axis=0)
        gb = jnp.concatenate([jnp.broadcast_to(g_sc[pl.ds(c * i + sl, 1), :], (c, W)) for i in range(n_sub)], axis=0)
        p = (qs * kb) * jnp.exp(jnp.minimum(G - gb, 0.0))
        lhs_sc[:, sl * W:(sl + 1) * W] = p.astype(BF16)
    rd = jnp.dot(lhs_sc[...], e_ref[...], preferred_element_type=F32)
    scol = lane & (C - 1)
    keep = ((scol >> 4) == sub) & ((scol >= row) if rev else (scol <= row))
    a = a + jnp.where(keep, rd, 0.0)

    out = out + jnp.dot(a.astype(BF16), block_diag(vb), preferred_element_type=F32)
    return out, st_new


def _hgrn_kernel(*refs, layer, n_chunks, has_s0):
    it = iter(refs)
    qf, zf, vf, qb, zb, vb, lbraw, e_ref = [next(it) for _ in range(8)]
    s0 = next(it) if has_s0 else None
    of_ref, ob_ref, st_ref, k_sc, g_sc, lhs_sc = [next(it) for _ in range(6)]

    @pl.when(pl.program_id(1) == 0)
    def _():
        if has_s0:
            st_ref[...] = s0[...]
        else:
            st_ref[...] = jnp.zeros(st_ref.shape, F32)

    lbr = lbraw[...]
    ex = jnp.exp(lbr - jnp.max(lbr, axis=0, keepdims=True))
    sm = ex / jnp.sum(ex, axis=0, keepdims=True)
    lbs = jnp.zeros(sm.shape[1:], F32)
    for j in range(1, layer + 1):
        lbs = lbs + sm[j]
    lb_f, lb_b = lbs[0:1], lbs[1:2]

    def fwd(cidx, carry):
        off = pl.multiple_of(cidx * SCAN_CHUNK, SCAN_CHUNK)
        sl = pl.ds(off, SCAN_CHUNK)
        o, s_new = _hgrn_chunk(qf[sl, :], zf[sl, :], vf[sl, :], lb_f, st_ref[0, 0], e_ref, k_sc, g_sc, lhs_sc, False)
        of_ref[sl, :] = o
        st_ref[0, 0] = s_new
        return carry

    def bwd(cidx, carry):
        off = pl.multiple_of((n_chunks - 1 - cidx) * SCAN_CHUNK, SCAN_CHUNK)
        sl = pl.ds(off, SCAN_CHUNK)
        o, s_new = _hgrn_chunk(qb[sl, :], zb[sl, :], vb[sl, :], lb_b, st_ref[0, 1], e_ref, k_sc, g_sc, lhs_sc, True)
        ob_ref[sl, :] = o
        st_ref[0, 1] = s_new
        return carry

    lax.fori_loop(0, n_chunks, fwd, 0)
    lax.fori_loop(0, n_chunks, bwd, 0)


def _hgrn_scan(zr, lb_raw, e_mat, s0, layer, batch, tokens_per_batch):
    t_total = zr.shape[0]
    tb = 256
    nblk = tokens_per_batch // tb
    W = HGRN_WIDTH
    has_s0 = s0 is not None
    fblk = lambda col: pl.BlockSpec((tb, W), lambda b, j: (b * nblk + j, col))
    bblk = lambda col: pl.BlockSpec((tb, W), lambda b, j: (b * nblk + nblk - 1 - j, col))
    in_specs = [fblk(0), fblk(1), fblk(3), bblk(0), bblk(2), bblk(3),
                pl.BlockSpec((DEPTH, 2, W), lambda b, j: (0, 0, 0)),
                pl.BlockSpec((SUB_CHUNK * W, W), lambda b, j: (0, 0))]
    args = [zr] * 6 + [lb_raw, e_mat]
    if has_s0:
        in_specs.append(pl.BlockSpec((1, 2, W, W), lambda b, j: (b, 0, 0, 0)))
        args.append(s0)
    return pl.pallas_call(
        functools.partial(_hgrn_kernel, layer=layer, n_chunks=tb // SCAN_CHUNK, has_s0=has_s0),
        grid=(batch, nblk),
        in_specs=in_specs,
        out_specs=[pl.BlockSpec((tb, W), lambda b, j: (b * nblk + j, 0)),
                   pl.BlockSpec((tb, W), lambda b, j: (b * nblk + nblk - 1 - j, 0)),
                   pl.BlockSpec((1, 2, W, W), lambda b, j: (b, 0, 0, 0))],
        out_shape=[jax.ShapeDtypeStruct((t_total, W), F32),
                   jax.ShapeDtypeStruct((t_total, W), F32),
                   jax.ShapeDtypeStruct((batch, 2, W, W), F32)],
        scratch_shapes=[pltpu.VMEM((SCAN_CHUNK, W), F32), pltpu.VMEM((SCAN_CHUNK, W), F32),
                        pltpu.VMEM((SCAN_CHUNK, SUB_CHUNK * W), BF16)],
        compiler_params=_params(("parallel", "arbitrary")),
        name="hgrn_scan",
    )(*args)


def _outproj_kernel(oa_ref, od_ref, of_ref, ob_ref, g_ref, x_ref, gate_ref, w_ref, sub_ref, hgn_ref,
                    bd_ref, lng_ref, lnb_ref, o_ref, *, lam_init):
    inv_d = 1.0 / HEAD_DIM
    g = g_ref[...]
    bd = bd_ref[...]

    def head_rms(t, gain):
        ssq = jnp.dot((t * t).astype(BF16), bd, preferred_element_type=F32)
        return t * lax.rsqrt(ssq * inv_d + RMS_EPS) * gain

    ma = (oa_ref[...] * g[:, 0:GQA_WIDTH]).astype(BF16)
    md = (head_rms(od_ref[...], sub_ref[...]) * (1.0 - lam_init)
          * g[:, GQA_WIDTH:GQA_WIDTH + DIFF_WIDTH]).astype(BF16)
    mr = (head_rms(of_ref[...] + ob_ref[...], hgn_ref[...]) * g[:, GQA_WIDTH + DIFF_WIDTH:]).astype(BF16)
    y = (jnp.dot(ma, w_ref[0:GQA_WIDTH, :], preferred_element_type=F32)
         + jnp.dot(md, w_ref[GQA_WIDTH:GQA_WIDTH + DIFF_WIDTH, :], preferred_element_type=F32)
         + jnp.dot(mr, w_ref[GQA_WIDTH + DIFF_WIDTH:, :], preferred_element_type=F32))
    xn = DEEPNORM_ALPHA * x_ref[...] + gate_ref[0] * y
    mu = jnp.mean(xn, axis=-1, keepdims=True)
    xc = xn - mu
    var = jnp.mean(xc * xc, axis=-1, keepdims=True)
    o_ref[...] = xc * lax.rsqrt(var + LN_EPS) * lng_ref[...] + lnb_ref[...]


def _out_projection(out_a, out_d, o_f, o_b, gates, x, gate, w_out_b, subln, hgn, bd, ln_g, ln_b,
                    tokens_per_batch, lam_init):
    t_total = x.shape[0]
    tm = 256
    mod_rows = t_total // gate.shape[0] // tm
    row = lambda w: pl.BlockSpec((tm, w), lambda i: (i, 0))
    full = lambda shape: pl.BlockSpec(shape, lambda i: (0,) * len(shape))
    return pl.pallas_call(
        functools.partial(_outproj_kernel, lam_init=lam_init),
        grid=(t_total // tm,),
        in_specs=[row(GQA_WIDTH), row(DIFF_WIDTH), row(HGRN_WIDTH), row(HGRN_WIDTH), row(D_MODEL), row(D_MODEL),
                  pl.BlockSpec((1, 1, D_MODEL), lambda i: (i // mod_rows, 0, 0)),
                  full((D_MODEL, D_MODEL)), full((1, DIFF_WIDTH)), full((1, HGRN_WIDTH)),
                  full((DIFF_WIDTH, DIFF_WIDTH)), full((1, D_MODEL)), full((1, D_MODEL))],
        out_specs=row(D_MODEL),
        out_shape=jax.ShapeDtypeStruct((t_total, D_MODEL), F32),
        compiler_params=_params(("parallel",)),
        name="out_projection",
    )(out_a, out_d, o_f, o_b, gates, x, gate, w_out_b, subln, hgn, bd, ln_g, ln_b)


def _block_diag_ones(n):
    idx = np.arange(n) // HEAD_DIM
    return jnp.asarray((idx[:, None] == idx[None, :]).astype(np.float32), dtype=BF16)


def _channel_sum_matrix():
    sl = np.arange(SUB_CHUNK)[:, None, None]
    ln = np.arange(HGRN_WIDTH)[None, :, None]
    col = np.arange(HGRN_WIDTH)[None, None, :]
    e = ((col // HEAD_DIM) == (ln // HEAD_DIM)) & (((col % SCAN_CHUNK) % SUB_CHUNK) == sl)
    return jnp.asarray(e.reshape(SUB_CHUNK * HGRN_WIDTH, HGRN_WIDTH).astype(np.float32), dtype=BF16)


def _rope_tables(n_tokens, dim):
    rows = n_tokens // GRID_W
    row = jnp.repeat(jnp.arange(rows, dtype=F32), GRID_W)
    col = jnp.tile(jnp.arange(GRID_W, dtype=F32), rows)
    quarter = dim // 4
    inv_freq = ROPE_THETA ** (-jnp.arange(quarter, dtype=F32) / quarter)
    ar = row[:, None] * inv_freq[None, :]
    ac = col[:, None] * inv_freq[None, :]
    ang = jnp.concatenate([ar, ar, ac, ac], axis=-1)
    cos = jnp.tile(jnp.cos(ang), (1, LANES // dim))
    sin = jnp.tile(jnp.sin(ang), (1, LANES // dim))
    first = (jnp.arange(LANES) % (dim // 2)) < quarter
    return cos, jnp.where(first, -sin, 0.0), jnp.where(first, 0.0, sin)


def _mixer_layer(x, scale, shift, gate, layer, batch, tokens_per_batch, rope_tabs, cache, consts, weights, ctx_out):
    (w_in_b, w_out_b, qgain, kgain, lam_p, subln, hgn, lb_raw, ln_g, ln_b) = weights
    bdq, bdk, bdd, e_mat = consts
    lam_init = 0.8 - 0.6 * math.exp(-0.3 * layer)
    outs = _in_projection(x, scale, shift, w_in_b, qgain, kgain, bdq, bdk, rope_tabs, tokens_per_batch, ctx_out)
    qa, kdup, vaug, qd, kd, vaugd, gates, zr = outs[:8]
    if cache is None:
        gqa_cache = diff_cache = s0 = None
        tq_a = tq_d = tk = tokens_per_batch
    else:
        gqa_cache, diff_cache, s0 = cache
        tq_a, tq_d, tk = 256, 128, 256
    out_a = _gqa_attention(qa, kdup, vaug, gqa_cache, batch, tokens_per_batch, tq_a, tk)
    out_d = _diff_attention(qd, kd, vaugd, lam_p, diff_cache, batch, tokens_per_batch, tq_d, tk, lam_init)
    o_f, o_b, st = _hgrn_scan(zr, lb_raw, e_mat, s0, layer, batch, tokens_per_batch)
    x_new = _out_projection(out_a, out_d, o_f, o_b, gates, x, gate, w_out_b, subln, hgn, bdd, ln_g, ln_b,
                            tokens_per_batch, lam_init)
    return x_new, outs[8:], st


def kernel(x_prompt, x_sample, cache_gqa_k, cache_gqa_v, cache_diff_k, cache_diff_v, state_hgrn, c, c_ctx,
           w_ada, b_ada, w_in, gqa_q_norm, gqa_k_norm, diff_lambda, diff_subln, hgrn_lower_bounds, hgrn_norm,
           w_out, ln_g, ln_b):
    batch, seq, _ = x_prompt.shape
    dec_batch, dec_seq, _ = x_sample.shape
    past = cache_gqa_k.shape[2]
    H, HD = HGRN_HEADS, HEAD_DIM

    cond = jnp.zeros((MOD_ROWS, D_MODEL), F32).at[0].set(c_ctx).at[1:1 + dec_batch].set(c)
    mod = _modulation(cond, w_ada, b_ada)
    consts = (_block_diag_ones(GQA_WIDTH), _block_diag_ones(LANES), _block_diag_ones(DIFF_WIDTH),
              _channel_sum_matrix())
    rope_tabs = _rope_tables(dec_seq, HEAD_DIM) + _rope_tables(dec_seq, DIFF_QK_DIM)
    w_in_b = w_in.astype(BF16)
    w_out_b = w_out.astype(BF16)

    def layer_weights(l):
        return (w_in_b[l], w_out_b[l],
                jnp.tile(gqa_q_norm[l], GQA_WIDTH // HD)[None, :] * (HD ** -0.5 * LOG2E),
                jnp.tile(gqa_k_norm[l], LANES // HD)[None, :],
                diff_lambda[l],
                jnp.tile(diff_subln[l], DIFF_WIDTH // HD)[None, :],
                jnp.tile(hgrn_norm[l], HGRN_WIDTH // HD)[None, :],
                hgrn_lower_bounds, ln_g[l][None, :], ln_b[l][None, :])

    def mod_rows(l, lo, n):
        m = mod[l, lo:lo + n]
        return (m[:, None, 0:D_MODEL], m[:, None, D_MODEL:2 * D_MODEL], m[:, None, 2 * D_MODEL:])

    def states_out(st):
        s6 = st.reshape(st.shape[0], 2, H, HD, H, HD)
        return jnp.swapaxes(jnp.stack([s6[:, :, h, :, h, :] for h in range(H)], axis=2), -1, -2)

    x = x_prompt.reshape(batch * seq, D_MODEL)
    ctx_leaves = []
    for l in range(DEPTH):
        shift, scale, gate = mod_rows(l, 0, 1)
        x, extra, st = _mixer_layer(x, scale, shift, gate, l, batch, seq, None, None, consts,
                                    layer_weights(l), True)
        kn, av, dk, dv = extra
        ctx_leaves.append((kn.reshape(batch, seq, GQA_KV_HEADS, HD), av.reshape(batch, seq, GQA_KV_HEADS, HD),
                           dk.reshape(batch, seq, DIFF_HEADS, HD), dv.reshape(batch, seq, DIFF_HEADS, HD),
                           states_out(st)))
    y_prompt = x.reshape(batch, seq, D_MODEL)
    new_leaves = [jnp.stack([lv[i] for lv in ctx_leaves], axis=1) for i in range(5)]

    ones = jnp.ones((dec_batch, past, HD), F32)
    eye = jnp.eye(H, dtype=F32)
    x = x_sample.reshape(dec_batch * dec_seq, D_MODEL)
    for l in range(DEPTH):
        shift, scale, gate = mod_rows(l, 1, dec_batch)
        ck, cv = cache_gqa_k[:, l], cache_gqa_v[:, l]
        ckd = jnp.stack([jnp.concatenate([ck[:, :, g], ck[:, :, g]], axis=-1) for g in range(GQA_KV_HEADS)], axis=1)
        cvd = jnp.stack([jnp.concatenate([cv[:, :, g], ones], axis=-1) for g in range(GQA_KV_HEADS)], axis=1)
        dkc = cache_diff_k[:, l].reshape(dec_batch, past, DIFF_WIDTH)
        dvc = cache_diff_v[:, l]
        dvd = jnp.stack([jnp.concatenate([dvc[:, :, h], ones] if h % 2 == 0 else [ones, dvc[:, :, h]], axis=-1)
                         for h in range(DIFF_HEADS)], axis=1)
        s0t = jnp.swapaxes(state_hgrn[:, l].astype(F32), -1, -2)
        s0 = (s0t[:, :, :, :, None, :] * eye[None, None, :, None, :, None]).reshape(dec_batch, 2, H * HD, H * HD)
        cache = ((ckd.astype(BF16), cvd.astype(BF16)), (dkc.astype(BF16), dvd.astype(BF16)), s0)
        x, _, _ = _mixer_layer(x, scale, shift, gate, l, dec_batch, dec_seq, rope_tabs, cache, consts,
                               layer_weights(l), False)
    y_sample = x.reshape(dec_batch, dec_seq, D_MODEL)
    return (y_prompt, y_sample) + tuple(new_leaves)
```

```python
import functools
import math

import numpy as np
import jax
import jax.numpy as jnp
from jax import lax
from jax.experimental import pallas as pl
from jax.experimental.pallas import tpu as pltpu

F32 = jnp.float32
BF16 = jnp.bfloat16

D_MODEL = 1024
DEPTH = 2
GRID_W = 64
HEAD_DIM = 64
GQA_WIDTH = 512
GQA_KV_HEADS = 2
DIFF_WIDTH = 256
DIFF_HEADS = 4
DIFF_QK_DIM = 32
HGRN_WIDTH = 256
HGRN_HEADS = 4
IN_WIDTH = 3584
SCAN_CHUNK = 64
HGRN_LEVELS = 4
ROPE_THETA = 10000.0
RMS_EPS = 1e-6
LN_EPS = 1e-5
FORGET_MIN = 1e-6
DEEPNORM_ALPHA = (2 * DEPTH) ** 0.25
LANES = 128
MOD_ROWS = 8
VMEM_LIMIT = 48 * 1024 * 1024
SM_ROWS = 32
LOG2E = math.log2(math.e)

_A_Q, _A_K, _A_V, _A_G = 0, 512, 640, 768
_D_Q, _D_K, _D_V, _D_G = 1280, 1536, 1792, 2048
_R_Z, _R_G = 2304, 3328

_NT = (((1,), (1,)), ((), ()))
_TN = (((0,), (0,)), ((), ()))


def _silu(x):
    return x * jax.nn.sigmoid(x)


def _params(sem):
    return pltpu.CompilerParams(dimension_semantics=sem, vmem_limit_bytes=VMEM_LIMIT)


def _mod_kernel(c_ref, w_ref, b_ref, o_ref):
    c = c_ref[...]
    o_ref[0] = jnp.dot(_silu(c), w_ref[0], preferred_element_type=F32) + b_ref[0]


def _modulation(cond, w_ada, b_ada):
    tn = 1024
    return pl.pallas_call(
        _mod_kernel,
        grid=(DEPTH, 3 * D_MODEL // tn),
        in_specs=[pl.BlockSpec((MOD_ROWS, D_MODEL), lambda l, j: (0, 0)),
                  pl.BlockSpec((1, D_MODEL, tn), lambda l, j: (l, 0, j)),
                  pl.BlockSpec((1, 1, tn), lambda l, j: (l, 0, j))],
        out_specs=pl.BlockSpec((1, MOD_ROWS, tn), lambda l, j: (l, 0, j)),
        out_shape=jax.ShapeDtypeStruct((DEPTH, MOD_ROWS, 3 * D_MODEL), F32),
        compiler_params=_params(("arbitrary", "arbitrary")),
        name="modulation",
    )(cond, w_ada, b_ada.reshape(DEPTH, 1, 3 * D_MODEL))


def _rope(x, c, s1, s2, shift):
    return x * c + pltpu.roll(x, LANES - shift, 1) * s1 + pltpu.roll(x, shift, 1) * s2


def _inproj_kernel(*refs, rope, ctx_out, ctx_alias):
    it = iter(refs)
    x_ref, sc_ref, sh_ref, w_ref, qg_ref, kg_ref, bdq_ref, bdk_ref = [next(it) for _ in range(8)]
    if rope:
        ca, s1a, s2a, cd, s1d, s2d = [next(it)[...] for _ in range(6)]
    if ctx_alias:
        for _ in range(4):
            next(it)
    qa_o, kdup_o, vaug_o, qd_o, kd_o, vaugd_o, gates_o, zr_o = [next(it) for _ in range(8)]
    if ctx_out:
        kn_o, av_o, dk_o, dv_o = [next(it).at[0, 0] for _ in range(4)]

    h = (x_ref[...] * (1.0 + sc_ref[0]) + sh_ref[0]).astype(BF16)

    def proj(lo, hi):
        return jnp.dot(h, w_ref[:, lo:hi], preferred_element_type=F32)

    tm = h.shape[0]
    low = lax.broadcasted_iota(jnp.int32, (tm, LANES), 1) < HEAD_DIM
    inv_d = 1.0 / HEAD_DIM

    aq = proj(_A_Q, _A_K)
    ssq = jnp.dot((aq * aq).astype(BF16), bdq_ref[...], preferred_element_type=F32)
    qn = aq * lax.rsqrt(ssq * inv_d + RMS_EPS) * qg_ref[...]
    for j in range(GQA_WIDTH // LANES):
        slab = qn[:, j * LANES:(j + 1) * LANES]
        if rope:
            slab = _rope(slab, ca, s1a, s2a, HEAD_DIM // 4)
        qa_o[:, j * LANES:(j + 1) * LANES] = slab.astype(BF16)

    ak = proj(_A_K, _A_V)
    ssqk = jnp.dot((ak * ak).astype(BF16), bdk_ref[...], preferred_element_type=F32)
    kn = ak * lax.rsqrt(ssqk * inv_d + RMS_EPS) * kg_ref[...]
    if ctx_out:
        kn_o[...] = kn
    if rope:
        kn = _rope(kn, ca, s1a, s2a, HEAD_DIM // 4)
    kr = pltpu.roll(kn, HEAD_DIM, 1)
    kdup_o[0] = jnp.where(low, kn, kr).astype(BF16)
    kdup_o[1] = jnp.where(low, kr, kn).astype(BF16)

    av = proj(_A_V, _A_G)
    if ctx_out:
        av_o[...] = av
    vr = pltpu.roll(av, HEAD_DIM, 1)
    vaug_o[0] = jnp.where(low, av, 1.0).astype(BF16)
    vaug_o[1] = jnp.where(low, vr, 1.0).astype(BF16)

    gates_o[:, 0:GQA_WIDTH] = _silu(proj(_A_G, _D_Q))

    dq = proj(_D_Q, _D_K) * (DIFF_QK_DIM ** -0.5 * LOG2E)
    dk = proj(_D_K, _D_V)
    if ctx_out:
        dk_o[...] = dk
    for j in range(DIFF_WIDTH // LANES):
        sq = dq[:, j * LANES:(j + 1) * LANES]
        sk = dk[:, j * LANES:(j + 1) * LANES]
        if rope:
            sq = _rope(sq, cd, s1d, s2d, DIFF_QK_DIM // 4)
            sk = _rope(sk, cd, s1d, s2d, DIFF_QK_DIM // 4)
        qd_o[:, j * LANES:(j + 1) * LANES] = sq.astype(BF16)
        kd_o[:, j * LANES:(j + 1) * LANES] = sk.astype(BF16)
    dv = proj(_D_V, _D_G)
    if ctx_out:
        dv_o[...] = dv
    for j in range(DIFF_WIDTH // LANES):
        sv = dv[:, j * LANES:(j + 1) * LANES]
        vaugd_o[2 * j] = jnp.where(low, sv, 1.0).astype(BF16)
        vaugd_o[2 * j + 1] = jnp.where(low, 1.0, sv).astype(BF16)
    gates_o[:, GQA_WIDTH:GQA_WIDTH + DIFF_WIDTH] = _silu(proj(_D_G, _R_Z))

    zr = proj(_R_Z, _R_G)
    zr_o[:, 0:HGRN_WIDTH] = _silu(zr[:, 0:HGRN_WIDTH])
    zr_o[:, HGRN_WIDTH:] = zr[:, HGRN_WIDTH:]
    gates_o[:, GQA_WIDTH + DIFF_WIDTH:] = _silu(proj(_R_G, IN_WIDTH))


def _in_projection(x, scale, shift, w_in_b, qgain, kgain, bdq, bdk, rope_tabs, tokens_per_batch, ctx_out,
                   layer, ctx_prev):
    t_total = x.shape[0]
    tm = 256
    nb_rows = tokens_per_batch // tm
    mod_rows = t_total // scale.shape[0] // tm
    rope = rope_tabs is not None
    full = lambda shape: pl.BlockSpec(shape, lambda i: (0,) * len(shape))
    in_specs = [pl.BlockSpec((tm, D_MODEL), lambda i: (i, 0)),
                pl.BlockSpec((1, 1, D_MODEL), lambda i: (i // mod_rows, 0, 0)),
                pl.BlockSpec((1, 1, D_MODEL), lambda i: (i // mod_rows, 0, 0)),
                full((D_MODEL, IN_WIDTH)),
                full((1, GQA_WIDTH)), full((1, LANES)),
                full((GQA_WIDTH, GQA_WIDTH)), full((LANES, LANES))]
    args = [x, scale, shift, w_in_b, qgain, kgain, bdq, bdk]
    if rope:
        in_specs += [pl.BlockSpec((tm, LANES), lambda i: (i % nb_rows, 0))] * 6
        args += list(rope_tabs)
    row = lambda w: pl.BlockSpec((tm, w), lambda i: (i, 0))
    stk = lambda n: pl.BlockSpec((n, tm, LANES), lambda i: (0, i, 0))
    out_specs = [row(GQA_WIDTH), stk(2), stk(2), row(DIFF_WIDTH), row(DIFF_WIDTH), stk(4),
                 row(D_MODEL), row(4 * HGRN_WIDTH)]
    out_shape = [jax.ShapeDtypeStruct((t_total, GQA_WIDTH), BF16),
                 jax.ShapeDtypeStruct((2, t_total, LANES), BF16),
                 jax.ShapeDtypeStruct((2, t_total, LANES), BF16),
                 jax.ShapeDtypeStruct((t_total, DIFF_WIDTH), BF16),
                 jax.ShapeDtypeStruct((t_total, DIFF_WIDTH), BF16),
                 jax.ShapeDtypeStruct((4, t_total, LANES), BF16),
                 jax.ShapeDtypeStruct((t_total, D_MODEL), F32),
                 jax.ShapeDtypeStruct((t_total, 4 * HGRN_WIDTH), F32)]
    aliases = {}
    if ctx_out:
        assert tokens_per_batch == tm
        widths = (LANES, LANES, DIFF_WIDTH, DIFF_WIDTH)
        if ctx_prev is not None:
            aliases = {len(args) + n: len(out_specs) + n for n in range(4)}
            in_specs += [pl.BlockSpec(memory_space=pl.ANY)] * 4
            args += list(ctx_prev)
        out_specs += [pl.BlockSpec((1, 1, tm, w), lambda i: (i, layer, 0, 0)) for w in widths]
        out_shape += [jax.ShapeDtypeStruct((t_total // tm, DEPTH, tm, w), F32) for w in widths]
    return pl.pallas_call(
        functools.partial(_inproj_kernel, rope=rope, ctx_out=ctx_out, ctx_alias=bool(aliases)),
        grid=(t_total // tm,),
        in_specs=in_specs, out_specs=out_specs, out_shape=out_shape,
        input_output_aliases=aliases,
        compiler_params=_params(("parallel",)),
        name="in_projection",
    )(*args)


def _chunk_rows(j, tk):
    return pl.ds(j * tk, tk) if isinstance(j, int) else pl.ds(pl.multiple_of(j * tk, tk), tk)


def _attention_scratch(rows, tk):
    return ([pltpu.VMEM((rows, LANES), BF16)]
            + [pltpu.VMEM((rows, tk), F32)] * 2 + [pltpu.VMEM((rows, tk), BF16)] * 2
            + [pltpu.VMEM((rows, LANES), F32)] * 4)


def _softmax_stage(s_sc, p_sc, a_sc, m_sc):
    tk = s_sc.shape[1]
    for r in range(s_sc.shape[0] // SM_ROWS):
        rows = pl.ds(r * SM_ROWS, SM_ROWS)
        s = s_sc[rows, :]
        m_prev = m_sc[rows, :]
        m_new = jnp.maximum(m_prev, jnp.max(s, axis=1, keepdims=True))
        m_sc[rows, :] = m_new
        p_sc[rows, :] = jnp.exp2(s - jnp.concatenate([m_new] * (tk // LANES), axis=1)).astype(BF16)
        a_sc[rows, :] = jnp.exp2(m_prev - m_new)


def _flash_pipeline(n_new, cache, k_at, v_at, qk, sm, av):
    o = 0 if cache is None else 1
    n = n_new + o
    k_of = lambda c: cache[0]() if (o and c == 0) else k_at(c - o)
    v_of = lambda c: cache[1]() if (o and c == 0) else v_at(c - o)
    qk(0, k_of(0))
    if n == 1:
        sm(0)
        av(0, v_of(0))
        return
    qk(1, k_of(1))
    sm(0)
    t0 = 2
    if o:
        assert n > 2
        qk(0, k_of(2))
        sm(1)
        av(0, v_of(0))
        t0 = 3
    assert (n - t0) % 2 == 0

    def pair(i, carry):
        t = t0 + 2 * i
        for d in (0, 1):
            par = (t0 + d) % 2
            qk(par, k_at(t + d - o))
            sm(1 - par)
            av(par, v_at(t + d - 2 - o))
        return carry

    lax.fori_loop(0, (n - t0) // 2, pair, 0)
    sm((n - 1) % 2)
    av(n % 2, v_of(n - 2))
    av((n - 1) % 2, v_of(n - 1))


def _gqa_kernel(*refs, has_cache, tq, tk, nk):
    if has_cache:
        q_ref, kc_ref, vc_ref = refs[:3]
        refs = refs[3:]
    else:
        q_ref = refs[0]
        refs = refs[1:]
    k_ref, v_ref, o_ref, qs_sc, s0, s1, p0, p1, a0, a1, m_sc, acc_sc = refs
    s_b, p_b, a_b = (s0, s1), (p0, p1), (a0, a1)
    q = q_ref[...]
    low = lax.broadcasted_iota(jnp.int32, (tq, LANES), 1) < HEAD_DIM
    zero = jnp.zeros_like(q)
    qs_sc[0:tq, :] = jnp.where(low, q, zero)
    qs_sc[tq:, :] = jnp.where(low, zero, q)
    m_sc[...] = jnp.full(m_sc.shape, -jnp.inf, F32)
    acc_sc[...] = jnp.zeros(acc_sc.shape, F32)

    def qk(slot, k):
        s_b[slot][...] = lax.dot_general(qs_sc[...], k, _NT, preferred_element_type=F32)

    def sm(slot):
        _softmax_stage(s_b[slot], p_b[slot], a_b[slot], m_sc)

    def av(slot, v):
        acc_sc[...] = a_b[slot][...] * acc_sc[...] + jnp.dot(p_b[slot][...], v, preferred_element_type=F32)

    chunk = lambda ref: (lambda j: ref[0, _chunk_rows(j, tk), :])
    cache = (lambda: kc_ref[0, 0], lambda: vc_ref[0, 0]) if has_cache else None
    _flash_pipeline(nk, cache, chunk(k_ref), chunk(v_ref), qk, sm, av)

    acc = acc_sc[...]
    on = acc / pltpu.roll(acc, HEAD_DIM, 1)
    o_ref[...] = jnp.where(low, on[:tq], pltpu.roll(on[tq:], HEAD_DIM, 1))


def _gqa_attention(qa, kdup, vaug, cache, batch, tokens_per_batch, tq, tk):
    t_total = qa.shape[0]
    nq = tokens_per_batch // tq
    nk = tokens_per_batch // tk
    has_cache = cache is not None
    in_specs = [pl.BlockSpec((tq, LANES), lambda b, p, i: (b * nq + i, p))]
    args = [qa]
    if has_cache:
        past = cache[0].shape[2]
        in_specs += [pl.BlockSpec((1, 1, past, LANES), lambda b, p, i: (b, p // 2, 0, 0))] * 2
        args += list(cache)
    in_specs += [pl.BlockSpec((1, tokens_per_batch, LANES), lambda b, p, i: (p // 2, b, 0))] * 2
    args += [kdup, vaug]
    return pl.pallas_call(
        functools.partial(_gqa_kernel, has_cache=has_cache, tq=tq, tk=tk, nk=nk),
        grid=(batch, GQA_WIDTH // LANES, nq),
        in_specs=in_specs,
        out_specs=pl.BlockSpec((tq, LANES), lambda b, p, i: (b * nq + i, p)),
        out_shape=jax.ShapeDtypeStruct((t_total, GQA_WIDTH), F32),
        scratch_shapes=_attention_scratch(2 * tq, tk),
        compiler_params=_params(("parallel", "parallel", "parallel")),
        name="gqa_attention",
    )(*args)


def _diff_kernel(*refs, has_cache, tq, tk, nk, lam_init):
    q_ref, lp_ref = refs[:2]
    if has_cache:
        kc_ref, vc0_ref, vc1_ref = refs[2:5]
        refs = refs[5:]
    else:
        refs = refs[2:]
    k_ref, v0_ref, v1_ref, o_ref, qs_sc, s0, s1, p0, p1, a0, a1, m_sc, acc_sc = refs
    s_b, p_b, a_b = (s0, s1), (p0, p1), (a0, a1)
    q = q_ref[...]
    lane = lax.broadcasted_iota(jnp.int32, (tq, LANES), 1)
    zero = jnp.zeros_like(q)
    for j in range(4):
        qs_sc[j * tq:(j + 1) * tq, :] = jnp.where((lane >> 5) == j, q, zero)
    m_sc[...] = jnp.full(m_sc.shape, -jnp.inf, F32)
    acc_sc[...] = jnp.zeros(acc_sc.shape, F32)
    h2 = 2 * tq

    def qk(slot, k):
        s_b[slot][...] = lax.dot_general(qs_sc[...], k, _NT, preferred_element_type=F32)

    def sm(slot):
        _softmax_stage(s_b[slot], p_b[slot], a_b[slot], m_sc)

    def av(slot, v):
        v0, v1 = v
        acc_sc[0:h2, :] = (a_b[slot][0:h2, :] * acc_sc[0:h2, :]
                           + jnp.dot(p_b[slot][0:h2, :], v0, preferred_element_type=F32))
        acc_sc[h2:, :] = (a_b[slot][h2:, :] * acc_sc[h2:, :]
                          + jnp.dot(p_b[slot][h2:, :], v1, preferred_element_type=F32))

    k_at = lambda j: k_ref[_chunk_rows(j, tk), :]
    v_at = lambda j: (v0_ref[0, _chunk_rows(j, tk), :], v1_ref[0, _chunk_rows(j, tk), :])
    cache = (lambda: kc_ref[0], lambda: (vc0_ref[0, 0], vc1_ref[0, 0])) if has_cache else None
    _flash_pipeline(nk, cache, k_at, v_at, qk, sm, av)

    lp = lp_ref[...]
    lam = (jnp.exp(jnp.sum(lp[0:1] * lp[1:2], axis=1, keepdims=True))
           - jnp.exp(jnp.sum(lp[2:3] * lp[3:4], axis=1, keepdims=True)) + lam_init)
    acc = acc_sc[...]
    on = acc / pltpu.roll(acc, HEAD_DIM, 1)
    o0 = on[0:tq] - lam * on[tq:h2]
    o1 = on[h2:h2 + tq] - lam * on[h2 + tq:]
    o_ref[...] = jnp.where(lane < HEAD_DIM, o0, o1)


def _diff_attention(qd, kd, vaugd, lam_p, cache, batch, tokens_per_batch, tq, tk, lam_init):
    t_total = qd.shape[0]
    nq = tokens_per_batch // tq
    nk = tokens_per_batch // tk
    has_cache = cache is not None
    in_specs = [pl.BlockSpec((tq, LANES), lambda b, p, i: (b * nq + i, p)),
                pl.BlockSpec((4, DIFF_QK_DIM), lambda b, p, i: (0, 0))]
    args = [qd, lam_p]
    if has_cache:
        ckd, cvd = cache
        past = ckd.shape[1]
        in_specs += [pl.BlockSpec((1, past, LANES), lambda b, p, i: (b, 0, p)),
                     pl.BlockSpec((1, 1, past, LANES), lambda b, p, i: (b, 2 * p, 0, 0)),
                     pl.BlockSpec((1, 1, past, LANES), lambda b, p, i: (b, 2 * p + 1, 0, 0))]
        args += [ckd, cvd, cvd]
    in_specs += [pl.BlockSpec((tokens_per_batch, LANES), lambda b, p, i: (b, p)),
                 pl.BlockSpec((1, tokens_per_batch, LANES), lambda b, p, i: (2 * p, b, 0)),
                 pl.BlockSpec((1, tokens_per_batch, LANES), lambda b, p, i: (2 * p + 1, b, 0))]
    args += [kd, vaugd, vaugd]
    return pl.pallas_call(
        functools.partial(_diff_kernel, has_cache=has_cache, tq=tq, tk=tk, nk=nk, lam_init=lam_init),
        grid=(batch, DIFF_WIDTH // LANES, nq),
        in_specs=in_specs,
        out_specs=pl.BlockSpec((tq, LANES), lambda b, p, i: (b * nq + i, p)),
        out_shape=jax.ShapeDtypeStruct((t_total, DIFF_WIDTH), F32),
        scratch_shapes=_attention_scratch(4 * tq, tk),
        compiler_params=_params(("parallel", "parallel", "parallel")),
        name="diff_attention",
    )(*args)


def _hgrn_chunk(qs, z, v, lb, st, g_sc, tri, lmask_ref, hmask, bdmask, rev):
    C, W = SCAN_CHUNK, HGRN_WIDTH
    sig = jax.nn.sigmoid(z)
    om = 1.0 - lb
    k = om * (1.0 - sig)
    g = jnp.log2(jnp.maximum(lb + om * sig, FORGET_MIN))

    g1 = g.astype(BF16)
    e1 = g - g1.astype(F32)
    g2 = e1.astype(BF16)
    g3 = (e1 - g2.astype(F32)).astype(BF16)
    G = (jnp.dot(tri, g1, preferred_element_type=F32) + jnp.dot(tri, g2, preferred_element_type=F32)
         + jnp.dot(tri, g3, preferred_element_type=F32))
    g_sc[...] = G
    g_last = g_sc[pl.ds(0 if rev else C - 1, 1), :]

    out = lax.dot_general((qs * jnp.exp2(G)).astype(BF16), st.astype(BF16), _NT, preferred_element_type=F32)
    k_end = (k * jnp.exp2(g_last - G)).astype(BF16)
    vb = v.astype(BF16)
    st_new = st * jnp.exp2(g_last) + lax.dot_general(vb, k_end, _TN, preferred_element_type=F32) * bdmask

    def block_diag(a):
        return jnp.concatenate([a] * HGRN_HEADS, axis=0) * hmask

    def ref_rows(block, off):
        return jnp.concatenate([jnp.broadcast_to(g_sc[pl.ds(b0 + off, 1), :], (block, W))
                                for b0 in range(0, C, block)], axis=0)

    a = None
    for level in range(HGRN_LEVELS):
        block = 8 << level
        half = block // 2
        if level == 0:
            r = ref_rows(block, half)
            qe, ke = G - r, r - G
        else:
            r = ref_rows(block, half if rev else half - 1)
            qe, ke = jnp.minimum(G - r, 0.0), jnp.minimum(r - G, 0.0)
        al = lax.dot_general((qs * jnp.exp2(qe)).astype(BF16), block_diag((k * jnp.exp2(ke)).astype(BF16)),
                             _NT, preferred_element_type=F32)
        m = lmask_ref[rev, level]
        a = jnp.where(m > 0.5, al, 0.0) if level == 0 else a + al * m

    out = out + jnp.dot(a.astype(BF16), block_diag(vb), preferred_element_type=F32)
    return out, st_new


def _hgrn_kernel(*refs, layer, n_chunks, has_s0, state_out, state_alias):
    it = iter(refs)
    qf, zf, vf, qb, zb, vb, lbraw, tri_ref, lmask_ref, hmask_ref, bdmask_ref = [next(it) for _ in range(11)]
    s0 = next(it) if has_s0 else None
    if state_alias:
        next(it)
    of_ref, ob_ref = next(it), next(it)
    sto_ref = next(it) if state_out else None
    st_ref, g_sc = next(it), next(it)

    @pl.when(pl.program_id(1) == 0)
    def _():
        if has_s0:
            st_ref[...] = s0[0]
        else:
            st_ref[...] = jnp.zeros(st_ref.shape, F32)

    lbr = lbraw[...]
    ex = jnp.exp(lbr - jnp.max(lbr, axis=0, keepdims=True))
    sm = ex / jnp.sum(ex, axis=0, keepdims=True)
    lbs = jnp.zeros(sm.shape[1:], F32)
    for j in range(1, layer + 1):
        lbs = lbs + sm[j]
    lb_f, lb_b = lbs[0:1], lbs[1:2]

    hmask = hmask_ref[...]
    bdmask = bdmask_ref[...]

    s_f, s_b = st_ref[0], st_ref[1]
    for cidx in range(n_chunks):
        sf = pl.ds(cidx * SCAN_CHUNK, SCAN_CHUNK)
        sb = pl.ds((n_chunks - 1 - cidx) * SCAN_CHUNK, SCAN_CHUNK)
        o_f, s_f = _hgrn_chunk(qf[sf, :], zf[sf, :], vf[sf, :], lb_f, s_f, g_sc.at[2 * cidx], tri_ref[0],
                               lmask_ref, hmask, bdmask, 0)
        o_b, s_b = _hgrn_chunk(qb[sb, :], zb[sb, :], vb[sb, :], lb_b, s_b, g_sc.at[2 * cidx + 1], tri_ref[1],
                               lmask_ref, hmask, bdmask, 1)
        of_ref[sf, :] = o_f
        ob_ref[sb, :] = o_b
    st_ref[0] = s_f
    st_ref[1] = s_b

    if state_out:
        @pl.when(pl.program_id(1) == pl.num_programs(1) - 1)
        def _():
            for d, s in enumerate((s_f, s_b)):
                for hh in range(HGRN_HEADS):
                    slab = s[hh * HEAD_DIM:(hh + 1) * HEAD_DIM, (hh // 2) * LANES:(hh // 2 + 1) * LANES]
                    sto_ref[0, 0, d, hh] = pltpu.roll(slab, HEAD_DIM, 1) if hh % 2 else slab


def _hgrn_scan(zr, lb_raw, scan_consts, s0, state_prev, state_out, layer, batch, tokens_per_batch):
    t_total = zr.shape[0]
    tb = 256
    nblk = tokens_per_batch // tb
    W = HGRN_WIDTH
    has_s0 = s0 is not None
    state_alias = state_prev is not None
    fblk = lambda col: pl.BlockSpec((tb, W), lambda b, j: (b * nblk + j, col))
    bblk = lambda col: pl.BlockSpec((tb, W), lambda b, j: (b * nblk + nblk - 1 - j, col))
    full = lambda a: pl.BlockSpec(a.shape, lambda b, j: (0,) * a.ndim)
    in_specs = [fblk(0), fblk(1), fblk(3), bblk(0), bblk(2), bblk(3),
                pl.BlockSpec((DEPTH, 2, W), lambda b, j: (0, 0, 0))] + [full(a) for a in scan_consts]
    args = [zr] * 6 + [lb_raw] + list(scan_consts)
    if has_s0:
        in_specs.append(pl.BlockSpec((1, 2, W, W), lambda b, j: (b, 0, 0, 0)))
        args.append(s0)
    aliases = {}
    if state_alias:
        aliases = {len(args): 2}
        in_specs.append(pl.BlockSpec(memory_space=pl.ANY))
        args.append(state_prev)
    out_specs = [pl.BlockSpec((tb, W), lambda b, j: (b * nblk + j, 0)),
                 pl.BlockSpec((tb, W), lambda b, j: (b * nblk + nblk - 1 - j, 0))]
    out_shape = [jax.ShapeDtypeStruct((t_total, W), F32), jax.ShapeDtypeStruct((t_total, W), F32)]
    if state_out:
        out_specs.append(pl.BlockSpec((1, 1, 2, HGRN_HEADS, HEAD_DIM, LANES), lambda b, j: (b, layer, 0, 0, 0, 0)))
        out_shape.append(jax.ShapeDtypeStruct((batch, DEPTH, 2, HGRN_HEADS, HEAD_DIM, LANES), F32))
    return pl.pallas_call(
        functools.partial(_hgrn_kernel, layer=layer, n_chunks=tb // SCAN_CHUNK, has_s0=has_s0,
                          state_out=state_out, state_alias=state_alias),
        grid=(batch, nblk),
        in_specs=in_specs, out_specs=out_specs, out_shape=out_shape,
        input_output_aliases=aliases,
        scratch_shapes=[pltpu.VMEM((2, W, W), F32), pltpu.VMEM((2 * tb // SCAN_CHUNK, SCAN_CHUNK, W), F32)],
        compiler_params=_params(("parallel", "arbitrary")),
        name="hgrn_scan",
    )(*args)


def _outproj_kernel(oa_ref, od_ref, of_ref, ob_ref, g_ref, x_ref, gate_ref, w_ref, sub_ref, hgn_ref,
                    bd_ref, lng_ref, lnb_ref, o_ref, *, lam_init):
    inv_d = 1.0 / HEAD_DIM
    g = g_ref[...]
    bd = bd_ref[...]

    def head_rms(t, gain):
        ssq = jnp.dot((t * t).astype(BF16), bd, preferred_element_type=F32)
        return t * lax.rsqrt(ssq * inv_d + RMS_EPS) * gain

    ma = (oa_ref[...] * g[:, 0:GQA_WIDTH]).astype(BF16)
    md = (head_rms(od_ref[...], sub_ref[...]) * (1.0 - lam_init)
          * g[:, GQA_WIDTH:GQA_WIDTH + DIFF_WIDTH]).astype(BF16)
    mr = (head_rms(of_ref[...] + ob_ref[...], hgn_ref[...]) * g[:, GQA_WIDTH + DIFF_WIDTH:]).astype(BF16)
    y = (jnp.dot(ma, w_ref[0:GQA_WIDTH, :], preferred_element_type=F32)
         + jnp.dot(md, w_ref[GQA_WIDTH:GQA_WIDTH + DIFF_WIDTH, :], preferred_element_type=F32)
         + jnp.dot(mr, w_ref[GQA_WIDTH + DIFF_WIDTH:, :], preferred_element_type=F32))
    xn = DEEPNORM_ALPHA * x_ref[...] + gate_ref[0] * y
    mu = jnp.mean(xn, axis=-1, keepdims=True)
    xc = xn - mu
    var = jnp.mean(xc * xc, axis=-1, keepdims=True)
    o_ref[...] = xc * lax.rsqrt(var + LN_EPS) * lng_ref[...] + lnb_ref[...]


def _out_projection(out_a, out_d, o_f, o_b, gates, x, gate, w_out_b, subln, hgn, bd, ln_g, ln_b,
                    tokens_per_batch, lam_init):
    t_total = x.shape[0]
    tm = 256
    mod_rows = t_total // gate.shape[0] // tm
    row = lambda w: pl.BlockSpec((tm, w), lambda i: (i, 0))
    full = lambda shape: pl.BlockSpec(shape, lambda i: (0,) * len(shape))
    return pl.pallas_call(
        functools.partial(_outproj_kernel, lam_init=lam_init),
        grid=(t_total // tm,),
        in_specs=[row(GQA_WIDTH), row(DIFF_WIDTH), row(HGRN_WIDTH), row(HGRN_WIDTH), row(D_MODEL), row(D_MODEL),
                  pl.BlockSpec((1, 1, D_MODEL), lambda i: (i // mod_rows, 0, 0)),
                  full((D_MODEL, D_MODEL)), full((1, DIFF_WIDTH)), full((1, HGRN_WIDTH)),
                  full((DIFF_WIDTH, DIFF_WIDTH)), full((1, D_MODEL)), full((1, D_MODEL))],
        out_specs=row(D_MODEL),
        out_shape=jax.ShapeDtypeStruct((t_total, D_MODEL), F32),
        compiler_params=_params(("parallel",)),
        name="out_projection",
    )(out_a, out_d, o_f, o_b, gates, x, gate, w_out_b, subln, hgn, bd, ln_g, ln_b)


def _block_diag_ones(n):
    idx = np.arange(n) // HEAD_DIM
    return jnp.asarray((idx[:, None] == idx[None, :]).astype(np.float32), dtype=BF16)


def _scan_constants():
    C, W, H = SCAN_CHUNK, HGRN_WIDTH, HGRN_HEADS
    t = np.arange(C)[:, None]
    u = np.arange(C)[None, :]
    tri = np.stack([u <= t, u >= t])
    s = (np.arange(W) % C)[None, :]
    lmask = np.zeros((2, HGRN_LEVELS, C, W), np.float32)
    for level in range(HGRN_LEVELS):
        block = 8 << level
        same = (t // block) == (s // block)
        if level == 0:
            lmask[0, level] = same & (s <= t)
            lmask[1, level] = same & (s >= t)
        else:
            t_hi, s_hi = (t % block) >= block // 2, (s % block) >= block // 2
            lmask[0, level] = same & t_hi & ~s_hi
            lmask[1, level] = same & ~t_hi & s_hi
    rows = np.arange(H * C)[:, None]
    hmask = (rows // C) == (np.arange(W)[None, :] // HEAD_DIM)
    bdmask = (np.arange(W)[:, None] // HEAD_DIM) == (np.arange(W)[None, :] // HEAD_DIM)
    return (jnp.asarray(tri.astype(np.float32), dtype=BF16), jnp.asarray(lmask),
            jnp.asarray(hmask.astype(np.float32), dtype=BF16), jnp.asarray(bdmask.astype(np.float32)))


def _rope_tables(n_tokens, dim):
    rows = n_tokens // GRID_W
    row = jnp.repeat(jnp.arange(rows, dtype=F32), GRID_W)
    col = jnp.tile(jnp.arange(GRID_W, dtype=F32), rows)
    quarter = dim // 4
    inv_freq = ROPE_THETA ** (-jnp.arange(quarter, dtype=F32) / quarter)
    ar = row[:, None] * inv_freq[None, :]
    ac = col[:, None] * inv_freq[None, :]
    ang = jnp.concatenate([ar, ar, ac, ac], axis=-1)
    cos = jnp.tile(jnp.cos(ang), (1, LANES // dim))
    sin = jnp.tile(jnp.sin(ang), (1, LANES // dim))
    first = (jnp.arange(LANES) % (dim // 2)) < quarter
    return cos, jnp.where(first, -sin, 0.0), jnp.where(first, 0.0, sin)


def _mixer_layer(x, scale, shift, gate, layer, batch, tokens_per_batch, rope_tabs, cache, consts, weights,
                 ctx_out, ctx_prev):
    (w_in_b, w_out_b, qgain, kgain, lam_p, subln, hgn, lb_raw, ln_g, ln_b) = weights
    bdq, bdk, bdd, scan_consts = consts
    lam_init = 0.8 - 0.6 * math.exp(-0.3 * layer)
    outs = _in_projection(x, scale, shift, w_in_b, qgain, kgain, bdq, bdk, rope_tabs, tokens_per_batch, ctx_out,
                          layer, None if ctx_prev is None else ctx_prev[0])
    qa, kdup, vaug, qd, kd, vaugd, gates, zr = outs[:8]
    if cache is None:
        gqa_cache = diff_cache = s0 = None
        tq_a = tq_d = tk = tokens_per_batch
    else:
        gqa_cache, diff_cache, s0 = cache
        tq_a, tq_d, tk = 256, 128, 256
    out_a = _gqa_attention(qa, kdup, vaug, gqa_cache, batch, tokens_per_batch, tq_a, tk)
    out_d = _diff_attention(qd, kd, vaugd, lam_p, diff_cache, batch, tokens_per_batch, tq_d, tk, lam_init)
    scan = _hgrn_scan(zr, lb_raw, scan_consts, s0, None if ctx_prev is None else ctx_prev[1], ctx_out, layer,
                      batch, tokens_per_batch)
    x_new = _out_projection(out_a, out_d, scan[0], scan[1], gates, x, gate, w_out_b, subln, hgn, bdd, ln_g, ln_b,
                            tokens_per_batch, lam_init)
    return x_new, ((outs[8:], scan[2]) if ctx_out else None)


def kernel(x_prompt, x_sample, cache_gqa_k, cache_gqa_v, cache_diff_k, cache_diff_v, state_hgrn, c, c_ctx,
           w_ada, b_ada, w_in, gqa_q_norm, gqa_k_norm, diff_lambda, diff_subln, hgrn_lower_bounds, hgrn_norm,
           w_out, ln_g, ln_b):
    batch, seq, _ = x_prompt.shape
    dec_batch, dec_seq, _ = x_sample.shape
    past = cache_gqa_k.shape[2]
    H, HD = HGRN_HEADS, HEAD_DIM

    cond = jnp.zeros((MOD_ROWS, D_MODEL), F32).at[0].set(c_ctx).at[1:1 + dec_batch].set(c)
    mod = _modulation(cond, w_ada, b_ada)
    consts = (_block_diag_ones(GQA_WIDTH), _block_diag_ones(LANES), _block_diag_ones(DIFF_WIDTH),
              _scan_constants())
    rope_tabs = _rope_tables(dec_seq, HEAD_DIM) + _rope_tables(dec_seq, DIFF_QK_DIM)
    w_in_b = w_in.astype(BF16)
    w_out_b = w_out.astype(BF16)

    def layer_weights(l):
        return (w_in_b[l], w_out_b[l],
                jnp.tile(gqa_q_norm[l], GQA_WIDTH // HD)[None, :] * (HD ** -0.5 * LOG2E),
                jnp.tile(gqa_k_norm[l], LANES // HD)[None, :],
                diff_lambda[l],
                jnp.tile(diff_subln[l], DIFF_WIDTH // HD)[None, :],
                jnp.tile(hgrn_norm[l], HGRN_WIDTH // HD)[None, :],
                hgrn_lower_bounds, ln_g[l][None, :], ln_b[l][None, :])

    def mod_rows(l, lo, n):
        m = mod[l, lo:lo + n]
        return (m[:, None, 0:D_MODEL], m[:, None, D_MODEL:2 * D_MODEL], m[:, None, 2 * D_MODEL:])

    x = x_prompt.reshape(batch * seq, D_MODEL)
    ctx = None
    for l in range(DEPTH):
        shift, scale, gate = mod_rows(l, 0, 1)
        x, ctx = _mixer_layer(x, scale, shift, gate, l, batch, seq, None, None, consts, layer_weights(l), True, ctx)
    y_prompt = x.reshape(batch, seq, D_MODEL)
    (kn, av, dk, dv), st = ctx
    new_leaves = [kn.reshape(batch, DEPTH, seq, GQA_KV_HEADS, HD), av.reshape(batch, DEPTH, seq, GQA_KV_HEADS, HD),
                  dk.reshape(batch, DEPTH, seq, DIFF_HEADS, HD), dv.reshape(batch, DEPTH, seq, DIFF_HEADS, HD),
                  jnp.swapaxes(st[..., :HD], -1, -2)]

    ones = jnp.ones((dec_batch, past, HD), F32)
    eye = jnp.eye(H, dtype=F32)
    x = x_sample.reshape(dec_batch * dec_seq, D_MODEL)
    for l in range(DEPTH):
        shift, scale, gate = mod_rows(l, 1, dec_batch)
        ck, cv = cache_gqa_k[:, l], cache_gqa_v[:, l]
        ckd = jnp.stack([jnp.concatenate([ck[:, :, g], ck[:, :, g]], axis=-1) for g in range(GQA_KV_HEADS)], axis=1)
        cvd = jnp.stack([jnp.concatenate([cv[:, :, g], ones], axis=-1) for g in range(GQA_KV_HEADS)], axis=1)
        dkc = cache_diff_k[:, l].reshape(dec_batch, past, DIFF_WIDTH)
        dvc = cache_diff_v[:, l]
        dvd = jnp.stack([jnp.concatenate([dvc[:, :, h], ones] if h % 2 == 0 else [ones, dvc[:, :, h]], axis=-1)
                         for h in range(DIFF_HEADS)], axis=1)
        s0t = jnp.swapaxes(state_hgrn[:, l].astype(F32), -1, -2)
        s0 = (s0t[:, :, :, :, None, :] * eye[None, None, :, None, :, None]).reshape(dec_batch, 2, H * HD, H * HD)
        cache = ((ckd.astype(BF16), cvd.astype(BF16)), (dkc.astype(BF16), dvd.astype(BF16)), s0)
        x, _ = _mixer_layer(x, scale, shift, gate, l, dec_batch, dec_seq, rope_tabs, cache, consts,
                            layer_weights(l), False, None)
    y_sample = x.reshape(dec_batch, dec_seq, D_MODEL)
    return (y_prompt, y_sample) + tuple(new_leaves)
```

```python
import functools
import math

import numpy as np
import jax
import jax.numpy as jnp
from jax import lax
from jax.experimental import pallas as pl
from jax.experimental.pallas import tpu as pltpu

F32 = jnp.float32
BF16 = jnp.bfloat16

D_MODEL = 1024
DEPTH = 2
GRID_W = 64
HEAD_DIM = 64
GQA_WIDTH = 512
GQA_KV_HEADS = 2
DIFF_WIDTH = 256
DIFF_HEADS = 4
DIFF_QK_DIM = 32
HGRN_WIDTH = 256
HGRN_HEADS = 4
IN_WIDTH = 3584
SCAN_CHUNK = 64
HGRN_LEVELS = 4
ROPE_THETA = 10000.0
RMS_EPS = 1e-6
LN_EPS = 1e-5
FORGET_MIN = 1e-6
DEEPNORM_ALPHA = (2 * DEPTH) ** 0.25
LANES = 128
MOD_ROWS = 8
VMEM_LIMIT = 48 * 1024 * 1024
SM_ROWS = 32
LOG2E = math.log2(math.e)

_A_Q, _A_K, _A_V, _A_G = 0, 512, 640, 768
_D_Q, _D_K, _D_V, _D_G = 1280, 1536, 1792, 2048
_R_Z, _R_G = 2304, 3328

_NT = (((1,), (1,)), ((), ()))
_TN = (((0,), (0,)), ((), ()))


def _silu(x):
    return x * jax.nn.sigmoid(x)


def _params(sem):
    return pltpu.CompilerParams(dimension_semantics=sem, vmem_limit_bytes=VMEM_LIMIT)


def _mod_kernel(c_ref, w_ref, b_ref, o_ref):
    c = c_ref[...]
    o_ref[0] = jnp.dot(_silu(c), w_ref[0], preferred_element_type=F32) + b_ref[0]


def _modulation(cond, w_ada, b_ada):
    tn = 1024
    return pl.pallas_call(
        _mod_kernel,
        grid=(DEPTH, 3 * D_MODEL // tn),
        in_specs=[pl.BlockSpec((MOD_ROWS, D_MODEL), lambda l, j: (0, 0)),
                  pl.BlockSpec((1, D_MODEL, tn), lambda l, j: (l, 0, j)),
                  pl.BlockSpec((1, 1, tn), lambda l, j: (l, 0, j))],
        out_specs=pl.BlockSpec((1, MOD_ROWS, tn), lambda l, j: (l, 0, j)),
        out_shape=jax.ShapeDtypeStruct((DEPTH, MOD_ROWS, 3 * D_MODEL), F32),
        compiler_params=_params(("arbitrary", "arbitrary")),
        name="modulation",
    )(cond, w_ada, b_ada.reshape(DEPTH, 1, 3 * D_MODEL))


def _rope(x, c, s1, s2, shift):
    return x * c + pltpu.roll(x, LANES - shift, 1) * s1 + pltpu.roll(x, shift, 1) * s2


def _inproj_kernel(*refs, rope, ctx_out, ctx_alias):
    it = iter(refs)
    x_ref, sc_ref, sh_ref, w_ref, qg_ref, kg_ref, bdq_ref, bdk_ref = [next(it) for _ in range(8)]
    if rope:
        ca, s1a, s2a, cd, s1d, s2d = [next(it)[...] for _ in range(6)]
    if ctx_alias:
        for _ in range(4):
            next(it)
    qa_o, kdup_o, vaug_o, qd_o, kd_o, vaugd_o, gates_o, zr_o = [next(it) for _ in range(8)]
    if ctx_out:
        kn_o, av_o, dk_o, dv_o = [next(it).at[0, 0] for _ in range(4)]

    h = (x_ref[...] * (1.0 + sc_ref[0]) + sh_ref[0]).astype(BF16)

    def proj(lo, hi):
        return jnp.dot(h, w_ref[:, lo:hi], preferred_element_type=F32)

    tm = h.shape[0]
    low = lax.broadcasted_iota(jnp.int32, (tm, LANES), 1) < HEAD_DIM
    inv_d = 1.0 / HEAD_DIM

    aq = proj(_A_Q, _A_K)
    ssq = jnp.dot((aq * aq).astype(BF16), bdq_ref[...], preferred_element_type=F32)
    qn = aq * lax.rsqrt(ssq * inv_d + RMS_EPS) * qg_ref[...]
    for j in range(GQA_WIDTH // LANES):
        slab = qn[:, j * LANES:(j + 1) * LANES]
        if rope:
            slab = _rope(slab, ca, s1a, s2a, HEAD_DIM // 4)
        qa_o[:, j * LANES:(j + 1) * LANES] = slab.astype(BF16)

    ak = proj(_A_K, _A_V)
    ssqk = jnp.dot((ak * ak).astype(BF16), bdk_ref[...], preferred_element_type=F32)
    kn = ak * lax.rsqrt(ssqk * inv_d + RMS_EPS) * kg_ref[...]
    if ctx_out:
        kn_o[...] = kn
    if rope:
        kn = _rope(kn, ca, s1a, s2a, HEAD_DIM // 4)
    kr = pltpu.roll(kn, HEAD_DIM, 1)
    kdup_o[0] = jnp.where(low, kn, kr).astype(BF16)
    kdup_o[1] = jnp.where(low, kr, kn).astype(BF16)

    av = proj(_A_V, _A_G)
    if ctx_out:
        av_o[...] = av
    vr = pltpu.roll(av, HEAD_DIM, 1)
    vaug_o[0] = jnp.where(low, av, 1.0).astype(BF16)
    vaug_o[1] = jnp.where(low, vr, 1.0).astype(BF16)

    gates_o[:, 0:GQA_WIDTH] = _silu(proj(_A_G, _D_Q))

    dq = proj(_D_Q, _D_K) * (DIFF_QK_DIM ** -0.5 * LOG2E)
    dk = proj(_D_K, _D_V)
    if ctx_out:
        dk_o[...] = dk
    for j in range(DIFF_WIDTH // LANES):
        sq = dq[:, j * LANES:(j + 1) * LANES]
        sk = dk[:, j * LANES:(j + 1) * LANES]
        if rope:
            sq = _rope(sq, cd, s1d, s2d, DIFF_QK_DIM // 4)
            sk = _rope(sk, cd, s1d, s2d, DIFF_QK_DIM // 4)
        qd_o[:, j * LANES:(j + 1) * LANES] = sq.astype(BF16)
        kd_o[:, j * LANES:(j + 1) * LANES] = sk.astype(BF16)
    dv = proj(_D_V, _D_G)
    if ctx_out:
        dv_o[...] = dv
    for j in range(DIFF_WIDTH // LANES):
        sv = dv[:, j * LANES:(j + 1) * LANES]
        vaugd_o[2 * j] = jnp.where(low, sv, 1.0).astype(BF16)
        vaugd_o[2 * j + 1] = jnp.where(low, 1.0, sv).astype(BF16)
    gates_o[:, GQA_WIDTH:GQA_WIDTH + DIFF_WIDTH] = _silu(proj(_D_G, _R_Z))

    zr = proj(_R_Z, _R_G)
    zr_o[:, 0:HGRN_WIDTH] = _silu(zr[:, 0:HGRN_WIDTH])
    zr_o[:, HGRN_WIDTH:] = zr[:, HGRN_WIDTH:]
    gates_o[:, GQA_WIDTH + DIFF_WIDTH:] = _silu(proj(_R_G, IN_WIDTH))


def _in_projection(x, scale, shift, w_in_b, qgain, kgain, bdq, bdk, rope_tabs, tokens_per_batch, ctx_out,
                   layer, ctx_prev):
    t_total = x.shape[0]
    tm = 256
    nb_rows = tokens_per_batch // tm
    mod_rows = t_total // scale.shape[0] // tm
    rope = rope_tabs is not None
    full = lambda shape: pl.BlockSpec(shape, lambda i: (0,) * len(shape))
    in_specs = [pl.BlockSpec((tm, D_MODEL), lambda i: (i, 0)),
                pl.BlockSpec((1, 1, D_MODEL), lambda i: (i // mod_rows, 0, 0)),
                pl.BlockSpec((1, 1, D_MODEL), lambda i: (i // mod_rows, 0, 0)),
                full((D_MODEL, IN_WIDTH)),
                full((1, GQA_WIDTH)), full((1, LANES)),
                full((GQA_WIDTH, GQA_WIDTH)), full((LANES, LANES))]
    args = [x, scale, shift, w_in_b, qgain, kgain, bdq, bdk]
    if rope:
        in_specs += [pl.BlockSpec((tm, LANES), lambda i: (i % nb_rows, 0))] * 6
        args += list(rope_tabs)
    row = lambda w: pl.BlockSpec((tm, w), lambda i: (i, 0))
    stk = lambda n: pl.BlockSpec((n, tm, LANES), lambda i: (0, i, 0))
    out_specs = [row(GQA_WIDTH), stk(2), stk(2), row(DIFF_WIDTH), row(DIFF_WIDTH), stk(4),
                 row(D_MODEL), row(4 * HGRN_WIDTH)]
    out_shape = [jax.ShapeDtypeStruct((t_total, GQA_WIDTH), BF16),
                 jax.ShapeDtypeStruct((2, t_total, LANES), BF16),
                 jax.ShapeDtypeStruct((2, t_total, LANES), BF16),
                 jax.ShapeDtypeStruct((t_total, DIFF_WIDTH), BF16),
                 jax.ShapeDtypeStruct((t_total, DIFF_WIDTH), BF16),
                 jax.ShapeDtypeStruct((4, t_total, LANES), BF16),
                 jax.ShapeDtypeStruct((t_total, D_MODEL), F32),
                 jax.ShapeDtypeStruct((t_total, 4 * HGRN_WIDTH), F32)]
    aliases = {}
    if ctx_out:
        assert tokens_per_batch == tm
        widths = (LANES, LANES, DIFF_WIDTH, DIFF_WIDTH)
        if ctx_prev is not None:
            aliases = {len(args) + n: len(out_specs) + n for n in range(4)}
            in_specs += [pl.BlockSpec(memory_space=pl.ANY)] * 4
            args += list(ctx_prev)
        out_specs += [pl.BlockSpec((1, 1, tm, w), lambda i: (i, layer, 0, 0)) for w in widths]
        out_shape += [jax.ShapeDtypeStruct((t_total // tm, DEPTH, tm, w), F32) for w in widths]
    return pl.pallas_call(
        functools.partial(_inproj_kernel, rope=rope, ctx_out=ctx_out, ctx_alias=bool(aliases)),
        grid=(t_total // tm,),
        in_specs=in_specs, out_specs=out_specs, out_shape=out_shape,
        input_output_aliases=aliases,
        compiler_params=_params(("parallel",)),
        name="in_projection",
    )(*args)


def _chunk_rows(j, tk):
    return pl.ds(j * tk, tk) if isinstance(j, int) else pl.ds(pl.multiple_of(j * tk, tk), tk)


def _attention_scratch(rows, tk):
    return ([pltpu.VMEM((rows, LANES), BF16)]
            + [pltpu.VMEM((rows, tk), F32)] * 2 + [pltpu.VMEM((rows, tk), BF16)] * 2
            + [pltpu.VMEM((rows, LANES), F32)] * 4)


def _softmax_stage(s_sc, p_sc, a_sc, m_sc):
    tk = s_sc.shape[1]
    for r in range(s_sc.shape[0] // SM_ROWS):
        rows = pl.ds(r * SM_ROWS, SM_ROWS)
        s = s_sc[rows, :]
        m_prev = m_sc[rows, :]
        m_new = jnp.maximum(m_prev, jnp.max(s, axis=1, keepdims=True))
        m_sc[rows, :] = m_new
        p_sc[rows, :] = jnp.exp2(s - jnp.concatenate([m_new] * (tk // LANES), axis=1)).astype(BF16)
        a_sc[rows, :] = jnp.exp2(m_prev - m_new)


def _flash_pipeline(n_new, cache, k_at, v_at, qk, sm, av, unroll=True):
    o = 0 if cache is None else 1
    n = n_new + o
    k_of = lambda c: cache[0]() if (o and c == 0) else k_at(c - o)
    v_of = lambda c: cache[1]() if (o and c == 0) else v_at(c - o)
    qk(0, k_of(0))
    if n == 1:
        sm(0)
        av(0, v_of(0))
        return
    qk(1, k_of(1))
    sm(0)
    t0 = 2
    if o:
        assert n > 2
        qk(0, k_of(2))
        sm(1)
        av(0, v_of(0))
        t0 = 3
    assert (n - t0) % 2 == 0

    def pair(i, carry):
        t = t0 + 2 * i
        for d in (0, 1):
            par = (t0 + d) % 2
            qk(par, k_at(t + d - o))
            sm(1 - par)
            av(par, v_at(t + d - 2 - o))
        return carry

    if unroll:
        for i in range((n - t0) // 2):
            pair(i, 0)
    else:
        lax.fori_loop(0, (n - t0) // 2, pair, 0)
    sm((n - 1) % 2)
    av(n % 2, v_of(n - 2))
    av((n - 1) % 2, v_of(n - 1))


def _item_schedule(nk, cache, k_at, v_at, start, qk, sm, av, finish):
    if nk == 1 and cache is None:
        return (lambda: (start(), qk(0, k_at(0))), lambda: sm(0), lambda: (av(0, v_at(0)), finish()))
    return (start, lambda: _flash_pipeline(nk, cache, k_at, v_at, qk, sm, av), finish)


def _run_items(stages):
    n = len(stages)
    for t in range(n + 2):
        for d in range(3):
            if 0 <= t - d < n:
                stages[t - d][d]()


def _gqa_kernel(*refs, has_cache, tq, tk, nk, n_items):
    if has_cache:
        q_ref, kc_ref, vc_ref = refs[:3]
        refs = refs[3:]
    else:
        q_ref = refs[0]
        refs = refs[1:]
    k_ref, v_ref, o_ref = refs[:3]
    low = lax.broadcasted_iota(jnp.int32, (tq, LANES), 1) < HEAD_DIM
    rows = 2 * tq

    def item_stages(item):
        qs_sc, s0, s1, p0, p1, a0, a1, m_sc, acc_sc = [r.at[pl.ds(item * rows, rows)] for r in refs[3:]]
        s_b, p_b, a_b = (s0, s1), (p0, p1), (a0, a1)

        def start():
            q = q_ref[pl.ds(item * tq, tq), :]
            zero = jnp.zeros_like(q)
            qs_sc[0:tq, :] = jnp.where(low, q, zero)
            qs_sc[tq:, :] = jnp.where(low, zero, q)
            m_sc[...] = jnp.full(m_sc.shape, -jnp.inf, F32)
            acc_sc[...] = jnp.zeros(acc_sc.shape, F32)

        def qk(slot, k):
            s_b[slot][...] = lax.dot_general(qs_sc[...], k, _NT, preferred_element_type=F32)

        def sm(slot):
            _softmax_stage(s_b[slot], p_b[slot], a_b[slot], m_sc)

        def av(slot, v):
            acc_sc[...] = a_b[slot][...] * acc_sc[...] + jnp.dot(p_b[slot][...], v, preferred_element_type=F32)

        def finish():
            acc = acc_sc[...]
            on = acc / pltpu.roll(acc, HEAD_DIM, 1)
            o_ref[pl.ds(item * tq, tq), :] = jnp.where(low, on[:tq], pltpu.roll(on[tq:], HEAD_DIM, 1))

        chunk = lambda ref: (lambda j: ref[0, _chunk_rows(item * nk + j, tk), :])
        cache = (lambda: kc_ref[0, 0], lambda: vc_ref[0, 0]) if has_cache else None
        return _item_schedule(nk, cache, chunk(k_ref), chunk(v_ref), start, qk, sm, av, finish)

    _run_items([item_stages(item) for item in range(n_items)])


def _gqa_attention(qa, kdup, vaug, cache, batch, tokens_per_batch, tq, tk, n_items):
    t_total = qa.shape[0]
    nq = tokens_per_batch // tq
    nk = tokens_per_batch // tk
    has_cache = cache is not None
    assert n_items == 1 or (nq == 1 and not has_cache)
    in_specs = [pl.BlockSpec((n_items * tq, LANES), lambda b, p, i: (b * nq + i, p))]
    args = [qa]
    if has_cache:
        past = cache[0].shape[2]
        in_specs += [pl.BlockSpec((1, 1, past, LANES), lambda b, p, i: (b, p // 2, 0, 0))] * 2
        args += list(cache)
    in_specs += [pl.BlockSpec((1, n_items * tokens_per_batch, LANES), lambda b, p, i: (p // 2, b, 0))] * 2
    args += [kdup, vaug]
    return pl.pallas_call(
        functools.partial(_gqa_kernel, has_cache=has_cache, tq=tq, tk=tk, nk=nk, n_items=n_items),
        grid=(batch // n_items, GQA_WIDTH // LANES, nq),
        in_specs=in_specs,
        out_specs=pl.BlockSpec((n_items * tq, LANES), lambda b, p, i: (b * nq + i, p)),
        out_shape=jax.ShapeDtypeStruct((t_total, GQA_WIDTH), F32),
        scratch_shapes=_attention_scratch(n_items * 2 * tq, tk),
        compiler_params=_params(("parallel", "parallel", "parallel")),
        name="gqa_attention",
    )(*args)


def _diff_kernel(*refs, has_cache, tq, tk, nk, n_items, lam_init):
    q_ref, lp_ref = refs[:2]
    if has_cache:
        kc_ref, vc0_ref, vc1_ref = refs[2:5]
        refs = refs[5:]
    else:
        refs = refs[2:]
    k_ref, v0_ref, v1_ref, o_ref = refs[:4]
    lane = lax.broadcasted_iota(jnp.int32, (tq, LANES), 1)
    lp = lp_ref[...]
    lam = (jnp.exp(jnp.sum(lp[0:1] * lp[1:2], axis=1, keepdims=True))
           - jnp.exp(jnp.sum(lp[2:3] * lp[3:4], axis=1, keepdims=True)) + lam_init)
    h2 = 2 * tq
    rows = 4 * tq
    def item_stages(item):
        qs_sc, s0, s1, p0, p1, a0, a1, m_sc, acc_sc = [r.at[pl.ds(item * rows, rows)] for r in refs[4:]]
        s_b, p_b, a_b = (s0, s1), (p0, p1), (a0, a1)

        def start():
            q = q_ref[pl.ds(item * tq, tq), :]
            zero = jnp.zeros_like(q)
            for j in range(4):
                qs_sc[j * tq:(j + 1) * tq, :] = jnp.where((lane >> 5) == j, q, zero)
            m_sc[...] = jnp.full(m_sc.shape, -jnp.inf, F32)
            acc_sc[...] = jnp.zeros(acc_sc.shape, F32)

        def qk(slot, k):
            s_b[slot][...] = lax.dot_general(qs_sc[...], k, _NT, preferred_element_type=F32)

        def sm(slot):
            _softmax_stage(s_b[slot], p_b[slot], a_b[slot], m_sc)

        def av(slot, v):
            v0, v1 = v
            acc_sc[0:h2, :] = (a_b[slot][0:h2, :] * acc_sc[0:h2, :]
                               + jnp.dot(p_b[slot][0:h2, :], v0, preferred_element_type=F32))
            acc_sc[h2:, :] = (a_b[slot][h2:, :] * acc_sc[h2:, :]
                              + jnp.dot(p_b[slot][h2:, :], v1, preferred_element_type=F32))

        def finish():
            acc = acc_sc[...]
            on = acc / pltpu.roll(acc, HEAD_DIM, 1)
            o0 = on[0:tq] - lam * on[tq:h2]
            o1 = on[h2:h2 + tq] - lam * on[h2 + tq:]
            o_ref[pl.ds(item * tq, tq), :] = jnp.where(lane < HEAD_DIM, o0, o1)

        k_at = lambda j: k_ref[_chunk_rows(item * nk + j, tk), :]
        v_at = lambda j: (v0_ref[0, _chunk_rows(item * nk + j, tk), :], v1_ref[0, _chunk_rows(item * nk + j, tk), :])
        cache = (lambda: kc_ref[0], lambda: (vc0_ref[0, 0], vc1_ref[0, 0])) if has_cache else None
        return _item_schedule(nk, cache, k_at, v_at, start, qk, sm, av, finish)

    _run_items([item_stages(item) for item in range(n_items)])


def _diff_attention(qd, kd, vaugd, lam_p, cache, batch, tokens_per_batch, tq, tk, n_items, lam_init):
    t_total = qd.shape[0]
    nq = tokens_per_batch // tq
    nk = tokens_per_batch // tk
    has_cache = cache is not None
    assert n_items == 1 or (nq == 1 and not has_cache)
    in_specs = [pl.BlockSpec((n_items * tq, LANES), lambda b, p, i: (b * nq + i, p)),
                pl.BlockSpec((4, DIFF_QK_DIM), lambda b, p, i: (0, 0))]
    args = [qd, lam_p]
    if has_cache:
        ckd, cvd = cache
        past = ckd.shape[1]
        in_specs += [pl.BlockSpec((1, past, LANES), lambda b, p, i: (b, 0, p)),
                     pl.BlockSpec((1, 1, past, LANES), lambda b, p, i: (b, 2 * p, 0, 0)),
                     pl.BlockSpec((1, 1, past, LANES), lambda b, p, i: (b, 2 * p + 1, 0, 0))]
        args += [ckd, cvd, cvd]
    kv_rows = n_items * tokens_per_batch
    in_specs += [pl.BlockSpec((kv_rows, LANES), lambda b, p, i: (b, p)),
                 pl.BlockSpec((1, kv_rows, LANES), lambda b, p, i: (2 * p, b, 0)),
                 pl.BlockSpec((1, kv_rows, LANES), lambda b, p, i: (2 * p + 1, b, 0))]
    args += [kd, vaugd, vaugd]
    return pl.pallas_call(
        functools.partial(_diff_kernel, has_cache=has_cache, tq=tq, tk=tk, nk=nk, n_items=n_items,
                          lam_init=lam_init),
        grid=(batch // n_items, DIFF_WIDTH // LANES, nq),
        in_specs=in_specs,
        out_specs=pl.BlockSpec((n_items * tq, LANES), lambda b, p, i: (b * nq + i, p)),
        out_shape=jax.ShapeDtypeStruct((t_total, DIFF_WIDTH), F32),
        scratch_shapes=_attention_scratch(n_items * 4 * tq, tk),
        compiler_params=_params(("parallel", "parallel", "parallel")),
        name="diff_attention",
    )(*args)


def _hgrn_chunk(qs, z, v, lb, st, g_sc, tri, lmask_ref, hmask, bdmask, rev):
    C, W = SCAN_CHUNK, HGRN_WIDTH
    sig = jax.nn.sigmoid(z)
    om = 1.0 - lb
    k = om * (1.0 - sig)
    g = jnp.log2(jnp.maximum(lb + om * sig, FORGET_MIN))

    g1 = g.astype(BF16)
    e1 = g - g1.astype(F32)
    g2 = e1.astype(BF16)
    g3 = (e1 - g2.astype(F32)).astype(BF16)
    G = (jnp.dot(tri, g1, preferred_element_type=F32) + jnp.dot(tri, g2, preferred_element_type=F32)
         + jnp.dot(tri, g3, preferred_element_type=F32))
    g_sc[...] = G
    g_last = g_sc[pl.ds(0 if rev else C - 1, 1), :]

    out = lax.dot_general((qs * jnp.exp2(G)).astype(BF16), st.astype(BF16), _NT, preferred_element_type=F32)
    k_end = (k * jnp.exp2(g_last - G)).astype(BF16)
    vb = v.astype(BF16)
    st_new = st * jnp.exp2(g_last) + lax.dot_general(vb, k_end, _TN, preferred_element_type=F32) * bdmask

    def block_diag(a):
        return jnp.concatenate([a] * HGRN_HEADS, axis=0) * hmask

    def ref_rows(block, off):
        return jnp.concatenate([jnp.broadcast_to(g_sc[pl.ds(b0 + off, 1), :], (block, W))
                                for b0 in range(0, C, block)], axis=0)

    a = None
    for level in range(HGRN_LEVELS):
        block = 8 << level
        half = block // 2
        if level == 0:
            r = ref_rows(block, half)
            qe, ke = G - r, r - G
        else:
            r = ref_rows(block, half if rev else half - 1)
            qe, ke = jnp.minimum(G - r, 0.0), jnp.minimum(r - G, 0.0)
        al = lax.dot_general((qs * jnp.exp2(qe)).astype(BF16), block_diag((k * jnp.exp2(ke)).astype(BF16)),
                             _NT, preferred_element_type=F32)
        m = lmask_ref[rev, level]
        a = jnp.where(m > 0.5, al, 0.0) if level == 0 else a + al * m

    out = out + jnp.dot(a.astype(BF16), block_diag(vb), preferred_element_type=F32)
    return out, st_new


def _hgrn_kernel(*refs, layer, n_chunks, has_s0, state_out, state_alias):
    it = iter(refs)
    qf, zf, vf, qb, zb, vb, lbraw, tri_ref, lmask_ref, hmask_ref, bdmask_ref = [next(it) for _ in range(11)]
    s0 = next(it) if has_s0 else None
    if state_alias:
        next(it)
    of_ref, ob_ref = next(it), next(it)
    sto_ref = next(it) if state_out else None
    st_ref, g_sc = next(it), next(it)

    @pl.when(pl.program_id(1) == 0)
    def _():
        if has_s0:
            st_ref[...] = s0[0]
        else:
            st_ref[...] = jnp.zeros(st_ref.shape, F32)

    lbr = lbraw[...]
    ex = jnp.exp(lbr - jnp.max(lbr, axis=0, keepdims=True))
    sm = ex / jnp.sum(ex, axis=0, keepdims=True)
    lbs = jnp.zeros(sm.shape[1:], F32)
    for j in range(1, layer + 1):
        lbs = lbs + sm[j]
    lb_f, lb_b = lbs[0:1], lbs[1:2]

    hmask = hmask_ref[...]
    bdmask = bdmask_ref[...]

    s_f, s_b = st_ref[0], st_ref[1]
    for cidx in range(n_chunks):
        sf = pl.ds(cidx * SCAN_CHUNK, SCAN_CHUNK)
        sb = pl.ds((n_chunks - 1 - cidx) * SCAN_CHUNK, SCAN_CHUNK)
        o_f, s_f = _hgrn_chunk(qf[sf, :], zf[sf, :], vf[sf, :], lb_f, s_f, g_sc.at[2 * cidx], tri_ref[0],
                               lmask_ref, hmask, bdmask, 0)
        o_b, s_b = _hgrn_chunk(qb[sb, :], zb[sb, :], vb[sb, :], lb_b, s_b, g_sc.at[2 * cidx + 1], tri_ref[1],
                               lmask_ref, hmask, bdmask, 1)
        of_ref[sf, :] = o_f
        ob_ref[sb, :] = o_b
    st_ref[0] = s_f
    st_ref[1] = s_b

    if state_out:
        @pl.when(pl.program_id(1) == pl.num_programs(1) - 1)
        def _():
            for d, s in enumerate((s_f, s_b)):
                for hh in range(HGRN_HEADS):
                    slab = s[hh * HEAD_DIM:(hh + 1) * HEAD_DIM, (hh // 2) * LANES:(hh // 2 + 1) * LANES]
                    sto_ref[0, 0, d, hh] = pltpu.roll(slab, HEAD_DIM, 1) if hh % 2 else slab


def _hgrn_scan(zr, lb_raw, scan_consts, s0, state_prev, state_out, layer, batch, tokens_per_batch):
    t_total = zr.shape[0]
    tb = 256
    nblk = tokens_per_batch // tb
    W = HGRN_WIDTH
    has_s0 = s0 is not None
    state_alias = state_prev is not None
    fblk = lambda col: pl.BlockSpec((tb, W), lambda b, j: (b * nblk + j, col))
    bblk = lambda col: pl.BlockSpec((tb, W), lambda b, j: (b * nblk + nblk - 1 - j, col))
    full = lambda a: pl.BlockSpec(a.shape, lambda b, j: (0,) * a.ndim)
    in_specs = [fblk(0), fblk(1), fblk(3), bblk(0), bblk(2), bblk(3),
                pl.BlockSpec((DEPTH, 2, W), lambda b, j: (0, 0, 0))] + [full(a) for a in scan_consts]
    args = [zr] * 6 + [lb_raw] + list(scan_consts)
    if has_s0:
        in_specs.append(pl.BlockSpec((1, 2, W, W), lambda b, j: (b, 0, 0, 0)))
        args.append(s0)
    aliases = {}
    if state_alias:
        aliases = {len(args): 2}
        in_specs.append(pl.BlockSpec(memory_space=pl.ANY))
        args.append(state_prev)
    out_specs = [pl.BlockSpec((tb, W), lambda b, j: (b * nblk + j, 0)),
                 pl.BlockSpec((tb, W), lambda b, j: (b * nblk + nblk - 1 - j, 0))]
    out_shape = [jax.ShapeDtypeStruct((t_total, W), F32), jax.ShapeDtypeStruct((t_total, W), F32)]
    if state_out:
        out_specs.append(pl.BlockSpec((1, 1, 2, HGRN_HEADS, HEAD_DIM, LANES), lambda b, j: (b, layer, 0, 0, 0, 0)))
        out_shape.append(jax.ShapeDtypeStruct((batch, DEPTH, 2, HGRN_HEADS, HEAD_DIM, LANES), F32))
    return pl.pallas_call(
        functools.partial(_hgrn_kernel, layer=layer, n_chunks=tb // SCAN_CHUNK, has_s0=has_s0,
                          state_out=state_out, state_alias=state_alias),
        grid=(batch, nblk),
        in_specs=in_specs, out_specs=out_specs, out_shape=out_shape,
        input_output_aliases=aliases,
        scratch_shapes=[pltpu.VMEM((2, W, W), F32), pltpu.VMEM((2 * tb // SCAN_CHUNK, SCAN_CHUNK, W), F32)],
        compiler_params=_params(("parallel", "arbitrary")),
        name="hgrn_scan",
    )(*args)


def _outproj_kernel(oa_ref, od_ref, of_ref, ob_ref, g_ref, x_ref, gate_ref, w_ref, sub_ref, hgn_ref,
                    bd_ref, lng_ref, lnb_ref, o_ref, *, lam_init):
    inv_d = 1.0 / HEAD_DIM
    g = g_ref[...]
    bd = bd_ref[...]

    def head_rms(t, gain):
        ssq = jnp.dot((t * t).astype(BF16), bd, preferred_element_type=F32)
        return t * lax.rsqrt(ssq * inv_d + RMS_EPS) * gain

    ma = (oa_ref[...] * g[:, 0:GQA_WIDTH]).astype(BF16)
    md = (head_rms(od_ref[...], sub_ref[...]) * (1.0 - lam_init)
          * g[:, GQA_WIDTH:GQA_WIDTH + DIFF_WIDTH]).astype(BF16)
    mr = (head_rms(of_ref[...] + ob_ref[...], hgn_ref[...]) * g[:, GQA_WIDTH + DIFF_WIDTH:]).astype(BF16)
    y = (jnp.dot(ma, w_ref[0:GQA_WIDTH, :], preferred_element_type=F32)
         + jnp.dot(md, w_ref[GQA_WIDTH:GQA_WIDTH + DIFF_WIDTH, :], preferred_element_type=F32)
         + jnp.dot(mr, w_ref[GQA_WIDTH + DIFF_WIDTH:, :], preferred_element_type=F32))
    xn = DEEPNORM_ALPHA * x_ref[...] + gate_ref[0] * y
    mu = jnp.mean(xn, axis=-1, keepdims=True)
    xc = xn - mu
    var = jnp.mean(xc * xc, axis=-1, keepdims=True)
    o_ref[...] = xc * lax.rsqrt(var + LN_EPS) * lng_ref[...] + lnb_ref[...]


def _out_projection(out_a, out_d, o_f, o_b, gates, x, gate, w_out_b, subln, hgn, bd, ln_g, ln_b,
                    tokens_per_batch, lam_init):
    t_total = x.shape[0]
    tm = 256
    mod_rows = t_total // gate.shape[0] // tm
    row = lambda w: pl.BlockSpec((tm, w), lambda i: (i, 0))
    full = lambda shape: pl.BlockSpec(shape, lambda i: (0,) * len(shape))
    return pl.pallas_call(
        functools.partial(_outproj_kernel, lam_init=lam_init),
        grid=(t_total // tm,),
        in_specs=[row(GQA_WIDTH), row(DIFF_WIDTH), row(HGRN_WIDTH), row(HGRN_WIDTH), row(D_MODEL), row(D_MODEL),
                  pl.BlockSpec((1, 1, D_MODEL), lambda i: (i // mod_rows, 0, 0)),
                  full((D_MODEL, D_MODEL)), full((1, DIFF_WIDTH)), full((1, HGRN_WIDTH)),
                  full((DIFF_WIDTH, DIFF_WIDTH)), full((1, D_MODEL)), full((1, D_MODEL))],
        out_specs=row(D_MODEL),
        out_shape=jax.ShapeDtypeStruct((t_total, D_MODEL), F32),
        compiler_params=_params(("parallel",)),
        name="out_projection",
    )(out_a, out_d, o_f, o_b, gates, x, gate, w_out_b, subln, hgn, bd, ln_g, ln_b)


def _block_diag_ones(n):
    idx = np.arange(n) // HEAD_DIM
    return jnp.asarray((idx[:, None] == idx[None, :]).astype(np.float32), dtype=BF16)


def _scan_constants():
    C, W, H = SCAN_CHUNK, HGRN_WIDTH, HGRN_HEADS
    t = np.arange(C)[:, None]
    u = np.arange(C)[None, :]
    tri = np.stack([u <= t, u >= t])
    s = (np.arange(W) % C)[None, :]
    lmask = np.zeros((2, HGRN_LEVELS, C, W), np.float32)
    for level in range(HGRN_LEVELS):
        block = 8 << level
        same = (t // block) == (s // block)
        if level == 0:
            lmask[0, level] = same & (s <= t)
            lmask[1, level] = same & (s >= t)
        else:
            t_hi, s_hi = (t % block) >= block // 2, (s % block) >= block // 2
            lmask[0, level] = same & t_hi & ~s_hi
            lmask[1, level] = same & ~t_hi & s_hi
    rows = np.arange(H * C)[:, None]
    hmask = (rows // C) == (np.arange(W)[None, :] // HEAD_DIM)
    bdmask = (np.arange(W)[:, None] // HEAD_DIM) == (np.arange(W)[None, :] // HEAD_DIM)
    return (jnp.asarray(tri.astype(np.float32), dtype=BF16), jnp.asarray(lmask),
            jnp.asarray(hmask.astype(np.float32), dtype=BF16), jnp.asarray(bdmask.astype(np.float32)))


def _rope_tables(n_tokens, dim):
    rows = n_tokens // GRID_W
    row = jnp.repeat(jnp.arange(rows, dtype=F32), GRID_W)
    col = jnp.tile(jnp.arange(GRID_W, dtype=F32), rows)
    quarter = dim // 4
    inv_freq = ROPE_THETA ** (-jnp.arange(quarter, dtype=F32) / quarter)
    ar = row[:, None] * inv_freq[None, :]
    ac = col[:, None] * inv_freq[None, :]
    ang = jnp.concatenate([ar, ar, ac, ac], axis=-1)
    cos = jnp.tile(jnp.cos(ang), (1, LANES // dim))
    sin = jnp.tile(jnp.sin(ang), (1, LANES // dim))
    first = (jnp.arange(LANES) % (dim // 2)) < quarter
    return cos, jnp.where(first, -sin, 0.0), jnp.where(first, 0.0, sin)


def _mixer_layer(x, scale, shift, gate, layer, batch, tokens_per_batch, rope_tabs, cache, consts, weights,
                 ctx_out, ctx_prev):
    (w_in_b, w_out_b, qgain, kgain, lam_p, subln, hgn, lb_raw, ln_g, ln_b) = weights
    bdq, bdk, bdd, scan_consts = consts
    lam_init = 0.8 - 0.6 * math.exp(-0.3 * layer)
    outs = _in_projection(x, scale, shift, w_in_b, qgain, kgain, bdq, bdk, rope_tabs, tokens_per_batch, ctx_out,
                          layer, None if ctx_prev is None else ctx_prev[0])
    qa, kdup, vaug, qd, kd, vaugd, gates, zr = outs[:8]
    if cache is None:
        gqa_cache = diff_cache = s0 = None
        tq_a = tq_d = tk = tokens_per_batch
        items_a, items_d = 2, 2
    else:
        gqa_cache, diff_cache, s0 = cache
        tq_a, tq_d, tk = 512, 256, 256
        items_a = items_d = 1
    out_a = _gqa_attention(qa, kdup, vaug, gqa_cache, batch, tokens_per_batch, tq_a, tk, items_a)
    out_d = _diff_attention(qd, kd, vaugd, lam_p, diff_cache, batch, tokens_per_batch, tq_d, tk, items_d, lam_init)
    scan = _hgrn_scan(zr, lb_raw, scan_consts, s0, None if ctx_prev is None else ctx_prev[1], ctx_out, layer,
                      batch, tokens_per_batch)
    x_new = _out_projection(out_a, out_d, scan[0], scan[1], gates, x, gate, w_out_b, subln, hgn, bdd, ln_g, ln_b,
                            tokens_per_batch, lam_init)
    return x_new, ((outs[8:], scan[2]) if ctx_out else None)


def kernel(x_prompt, x_sample, cache_gqa_k, cache_gqa_v, cache_diff_k, cache_diff_v, state_hgrn, c, c_ctx,
           w_ada, b_ada, w_in, gqa_q_norm, gqa_k_norm, diff_lambda, diff_subln, hgrn_lower_bounds, hgrn_norm,
           w_out, ln_g, ln_b):
    batch, seq, _ = x_prompt.shape
    dec_batch, dec_seq, _ = x_sample.shape
    past = cache_gqa_k.shape[2]
    H, HD = HGRN_HEADS, HEAD_DIM

    cond = jnp.zeros((MOD_ROWS, D_MODEL), F32).at[0].set(c_ctx).at[1:1 + dec_batch].set(c)
    mod = _modulation(cond, w_ada, b_ada)
    consts = (_block_diag_ones(GQA_WIDTH), _block_diag_ones(LANES), _block_diag_ones(DIFF_WIDTH),
              _scan_constants())
    rope_tabs = _rope_tables(dec_seq, HEAD_DIM) + _rope_tables(dec_seq, DIFF_QK_DIM)
    w_in_b = w_in.astype(BF16)
    w_out_b = w_out.astype(BF16)

    def layer_weights(l):
        return (w_in_b[l], w_out_b[l],
                jnp.tile(gqa_q_norm[l], GQA_WIDTH // HD)[None, :] * (HD ** -0.5 * LOG2E),
                jnp.tile(gqa_k_norm[l], LANES // HD)[None, :],
                diff_lambda[l],
                jnp.tile(diff_subln[l], DIFF_WIDTH // HD)[None, :],
                jnp.tile(hgrn_norm[l], HGRN_WIDTH // HD)[None, :],
                hgrn_lower_bounds, ln_g[l][None, :], ln_b[l][None, :])

    def mod_rows(l, lo, n):
        m = mod[l, lo:lo + n]
        return (m[:, None, 0:D_MODEL], m[:, None, D_MODEL:2 * D_MODEL], m[:, None, 2 * D_MODEL:])

    x = x_prompt.reshape(batch * seq, D_MODEL)
    ctx = None
    for l in range(DEPTH):
        shift, scale, gate = mod_rows(l, 0, 1)
        x, ctx = _mixer_layer(x, scale, shift, gate, l, batch, seq, None, None, consts, layer_weights(l), True, ctx)
    y_prompt = x.reshape(batch, seq, D_MODEL)
    (kn, av, dk, dv), st = ctx
    new_leaves = [kn.reshape(batch, DEPTH, seq, GQA_KV_HEADS, HD), av.reshape(batch, DEPTH, seq, GQA_KV_HEADS, HD),
                  dk.reshape(batch, DEPTH, seq, DIFF_HEADS, HD), dv.reshape(batch, DEPTH, seq, DIFF_HEADS, HD),
                  jnp.swapaxes(st[..., :HD], -1, -2)]

    ones = jnp.ones((dec_batch, past, HD), F32)
    eye = jnp.eye(H, dtype=F32)
    x = x_sample.reshape(dec_batch * dec_seq, D_MODEL)
    for l in range(DEPTH):
        shift, scale, gate = mod_rows(l, 1, dec_batch)
        ck, cv = cache_gqa_k[:, l], cache_gqa_v[:, l]
        ckd = jnp.stack([jnp.concatenate([ck[:, :, g], ck[:, :, g]], axis=-1) for g in range(GQA_KV_HEADS)], axis=1)
        cvd = jnp.stack([jnp.concatenate([cv[:, :, g], ones], axis=-1) for g in range(GQA_KV_HEADS)], axis=1)
        dkc = cache_diff_k[:, l].reshape(dec_batch, past, DIFF_WIDTH)
        dvc = cache_diff_v[:, l]
        dvd = jnp.stack([jnp.concatenate([dvc[:, :, h], ones] if h % 2 == 0 else [ones, dvc[:, :, h]], axis=-1)
                         for h in range(DIFF_HEADS)], axis=1)
        s0t = jnp.swapaxes(state_hgrn[:, l].astype(F32), -1, -2)
        s0 = (s0t[:, :, :, :, None, :] * eye[None, None, :, None, :, None]).reshape(dec_batch, 2, H * HD, H * HD)
        cache = ((ckd.astype(BF16), cvd.astype(BF16)), (dkc.astype(BF16), dvd.astype(BF16)), s0)
        x, _ = _mixer_layer(x, scale, shift, gate, l, dec_batch, dec_seq, rope_tabs, cache, consts,
                            layer_weights(l), False, None)
    y_sample = x.reshape(dec_batch, dec_seq, D_MODEL)
    return (y_prompt, y_sample) + tuple(new_leaves)
```

```python
import functools
import math

import numpy as np
import jax
import jax.numpy as jnp
from jax import lax
from jax.experimental import pallas as pl
from jax.experimental.pallas import tpu as pltpu

F32 = jnp.float32
BF16 = jnp.bfloat16

D_MODEL = 1024
DEPTH = 2
GRID_W = 64
HEAD_DIM = 64
GQA_WIDTH = 512
GQA_KV_HEADS = 2
DIFF_WIDTH = 256
DIFF_HEADS = 4
DIFF_QK_DIM = 32
HGRN_WIDTH = 256
HGRN_HEADS = 4
IN_WIDTH = 3584
SCAN_CHUNK = 64
HGRN_LEVELS = 4
ROPE_THETA = 10000.0
RMS_EPS = 1e-6
LN_EPS = 1e-5
FORGET_MIN = 1e-6
DEEPNORM_ALPHA = (2 * DEPTH) ** 0.25
LANES = 128
SUBLANES = 8
MOD_ROWS = SUBLANES
VMEM_LIMIT = 48 * 1024 * 1024
SM_ROWS = 32
LOG2E = math.log2(math.e)

_A_Q, _A_K, _A_V, _A_G = 0, 512, 640, 768
_D_Q, _D_K, _D_V, _D_G = 1280, 1536, 1792, 2048
_R_Z, _R_G = 2304, 3328

_NT = (((1,), (1,)), ((), ()))
_TN = (((0,), (0,)), ((), ()))


def _silu(x):
    return x * jax.nn.sigmoid(x)


def _params(sem):
    return pltpu.CompilerParams(dimension_semantics=sem, vmem_limit_bytes=VMEM_LIMIT)


def _mod_kernel(c_ref, w_ref, b_ref, o_ref):
    c = c_ref[...]
    o_ref[0] = jnp.dot(_silu(c), w_ref[0], preferred_element_type=F32) + b_ref[0]


def _modulation(cond, w_ada, b_ada):
    tn = 1024
    return pl.pallas_call(
        _mod_kernel,
        grid=(DEPTH, 3 * D_MODEL // tn),
        in_specs=[pl.BlockSpec((MOD_ROWS, D_MODEL), lambda l, j: (0, 0)),
                  pl.BlockSpec((1, D_MODEL, tn), lambda l, j: (l, 0, j)),
                  pl.BlockSpec((1, 1, tn), lambda l, j: (l, 0, j))],
        out_specs=pl.BlockSpec((1, MOD_ROWS, tn), lambda l, j: (l, 0, j)),
        out_shape=jax.ShapeDtypeStruct((DEPTH, MOD_ROWS, 3 * D_MODEL), F32),
        compiler_params=_params(("arbitrary", "arbitrary")),
        name="modulation",
    )(cond, w_ada, b_ada.reshape(DEPTH, 1, 3 * D_MODEL))


def _rope(x, c, s1, s2, shift):
    return x * c + pltpu.roll(x, LANES - shift, 1) * s1 + pltpu.roll(x, shift, 1) * s2


def _inproj_kernel(*refs, rope, ctx_out, ctx_alias):
    it = iter(refs)
    x_ref, sc_ref, sh_ref, w_ref, qg_ref, kg_ref, bdq_ref, bdk_ref = [next(it) for _ in range(8)]
    if rope:
        ca, s1a, s2a, cd, s1d, s2d = [next(it)[...] for _ in range(6)]
    if ctx_alias:
        for _ in range(4):
            next(it)
    qa_o, kdup_o, vaug_o, qd_o, kd_o, vaugd_o, gates_o, zr_o = [next(it) for _ in range(8)]
    if ctx_out:
        ctx_refs = [next(it) for _ in range(4)]
        if not ctx_alias:
            for r in ctx_refs:
                r[0, 1:] = jnp.zeros((r.shape[1] - 1,) + r.shape[2:], F32)
        kn_o, av_o, dk_o, dv_o = [r.at[0, 0] for r in ctx_refs]

    h = (x_ref[...] * (1.0 + sc_ref[0]) + sh_ref[0]).astype(BF16)

    def proj(lo, hi):
        return jnp.dot(h, w_ref[:, lo:hi], preferred_element_type=F32)

    tm = h.shape[0]
    low = lax.broadcasted_iota(jnp.int32, (tm, LANES), 1) < HEAD_DIM
    inv_d = 1.0 / HEAD_DIM

    aq = proj(_A_Q, _A_K)
    ssq = jnp.dot((aq * aq).astype(BF16), bdq_ref[...], preferred_element_type=F32)
    qn = aq * lax.rsqrt(ssq * inv_d + RMS_EPS) * qg_ref[...]
    for j in range(GQA_WIDTH // LANES):
        slab = qn[:, j * LANES:(j + 1) * LANES]
        if rope:
            slab = _rope(slab, ca, s1a, s2a, HEAD_DIM // 4)
        qa_o[:, j * LANES:(j + 1) * LANES] = slab.astype(BF16)

    ak = proj(_A_K, _A_V)
    ssqk = jnp.dot((ak * ak).astype(BF16), bdk_ref[...], preferred_element_type=F32)
    kn = ak * lax.rsqrt(ssqk * inv_d + RMS_EPS) * kg_ref[...]
    if ctx_out:
        kn_o[...] = kn
    if rope:
        kn = _rope(kn, ca, s1a, s2a, HEAD_DIM // 4)
    kr = pltpu.roll(kn, HEAD_DIM, 1)
    kdup_o[0] = jnp.where(low, kn, kr).astype(BF16)
    kdup_o[1] = jnp.where(low, kr, kn).astype(BF16)

    av = proj(_A_V, _A_G)
    if ctx_out:
        av_o[...] = av
    vr = pltpu.roll(av, HEAD_DIM, 1)
    vaug_o[0] = jnp.where(low, av, 1.0).astype(BF16)
    vaug_o[1] = jnp.where(low, vr, 1.0).astype(BF16)

    gates_o[:, 0:GQA_WIDTH] = _silu(proj(_A_G, _D_Q))

    dq = proj(_D_Q, _D_K) * (DIFF_QK_DIM ** -0.5 * LOG2E)
    dk = proj(_D_K, _D_V)
    if ctx_out:
        dk_o[...] = dk
    for j in range(DIFF_WIDTH // LANES):
        sq = dq[:, j * LANES:(j + 1) * LANES]
        sk = dk[:, j * LANES:(j + 1) * LANES]
        if rope:
            sq = _rope(sq, cd, s1d, s2d, DIFF_QK_DIM // 4)
            sk = _rope(sk, cd, s1d, s2d, DIFF_QK_DIM // 4)
        qd_o[:, j * LANES:(j + 1) * LANES] = sq.astype(BF16)
        kd_o[:, j * LANES:(j + 1) * LANES] = sk.astype(BF16)
    dv = proj(_D_V, _D_G)
    if ctx_out:
        dv_o[...] = dv
    for j in range(DIFF_WIDTH // LANES):
        sv = dv[:, j * LANES:(j + 1) * LANES]
        vaugd_o[2 * j] = jnp.where(low, sv, 1.0).astype(BF16)
        vaugd_o[2 * j + 1] = jnp.where(low, 1.0, sv).astype(BF16)
    gates_o[:, GQA_WIDTH:GQA_WIDTH + DIFF_WIDTH] = _silu(proj(_D_G, _R_Z))

    zr = proj(_R_Z, _R_G)
    zr_o[:, 0:HGRN_WIDTH] = _silu(zr[:, 0:HGRN_WIDTH])
    zr_o[:, HGRN_WIDTH:] = zr[:, HGRN_WIDTH:]
    gates_o[:, GQA_WIDTH + DIFF_WIDTH:] = _silu(proj(_R_G, IN_WIDTH))


def _in_projection(x, scale, shift, w_in_b, qgain, kgain, bdq, bdk, rope_tabs, tokens_per_batch, ctx_out,
                   layer, ctx_prev):
    t_total = x.shape[0]
    tm = 256
    nb_rows = tokens_per_batch // tm
    mod_rows = t_total // scale.shape[0] // tm
    rope = rope_tabs is not None
    full = lambda shape: pl.BlockSpec(shape, lambda i: (0,) * len(shape))
    in_specs = [pl.BlockSpec((tm, D_MODEL), lambda i: (i, 0)),
                pl.BlockSpec((1, 1, D_MODEL), lambda i: (i // mod_rows, 0, 0)),
                pl.BlockSpec((1, 1, D_MODEL), lambda i: (i // mod_rows, 0, 0)),
                full((D_MODEL, IN_WIDTH)),
                full((1, GQA_WIDTH)), full((1, LANES)),
                full((GQA_WIDTH, GQA_WIDTH)), full((LANES, LANES))]
    args = [x, scale, shift, w_in_b, qgain, kgain, bdq, bdk]
    if rope:
        in_specs += [pl.BlockSpec((tm, LANES), lambda i: (i % nb_rows, 0))] * 6
        args += list(rope_tabs)
    row = lambda w: pl.BlockSpec((tm, w), lambda i: (i, 0))
    stk = lambda n: pl.BlockSpec((n, tm, LANES), lambda i: (0, i, 0))
    out_specs = [row(GQA_WIDTH), stk(2), stk(2), row(DIFF_WIDTH), row(DIFF_WIDTH), stk(4),
                 row(D_MODEL), row(4 * HGRN_WIDTH)]
    out_shape = [jax.ShapeDtypeStruct((t_total, GQA_WIDTH), BF16),
                 jax.ShapeDtypeStruct((2, t_total, LANES), BF16),
                 jax.ShapeDtypeStruct((2, t_total, LANES), BF16),
                 jax.ShapeDtypeStruct((t_total, DIFF_WIDTH), BF16),
                 jax.ShapeDtypeStruct((t_total, DIFF_WIDTH), BF16),
                 jax.ShapeDtypeStruct((4, t_total, LANES), BF16),
                 jax.ShapeDtypeStruct((t_total, D_MODEL), F32),
                 jax.ShapeDtypeStruct((t_total, 4 * HGRN_WIDTH), F32)]
    aliases = {}
    if ctx_out:
        assert tokens_per_batch == tm
        widths = (LANES, LANES, DIFF_WIDTH, DIFF_WIDTH)
        if ctx_prev is not None:
            aliases = {len(args) + n: len(out_specs) + n for n in range(4)}
            in_specs += [pl.BlockSpec(memory_space=pl.ANY)] * 4
            args += list(ctx_prev)
        if ctx_prev is None:
            assert layer == 0
            out_specs += [pl.BlockSpec((1, DEPTH, tm, w), lambda i: (i, 0, 0, 0)) for w in widths]
        else:
            out_specs += [pl.BlockSpec((1, 1, tm, w), lambda i: (i, layer, 0, 0)) for w in widths]
        out_shape += [jax.ShapeDtypeStruct((t_total // tm, DEPTH, tm, w), F32) for w in widths]
    return pl.pallas_call(
        functools.partial(_inproj_kernel, rope=rope, ctx_out=ctx_out, ctx_alias=bool(aliases)),
        grid=(t_total // tm,),
        in_specs=in_specs, out_specs=out_specs, out_shape=out_shape,
        input_output_aliases=aliases,
        compiler_params=_params(("parallel",)),
        name="in_projection",
    )(*args)


def _chunk_rows(j, tk):
    return pl.ds(j * tk, tk) if isinstance(j, int) else pl.ds(pl.multiple_of(j * tk, tk), tk)


def _attention_scratch(n_items, rows, tk):
    cols = n_items * rows
    return ([pltpu.VMEM((cols, LANES), BF16)]
            + [pltpu.VMEM((tk, cols), F32)] * 2 + [pltpu.VMEM((tk, cols), BF16)] * 2
            + [pltpu.VMEM((SUBLANES, cols), F32)] * 3
            + [pltpu.VMEM((LANES, cols), F32)])


def _item_scratch(refs, item, rows):
    qs = refs[0].at[pl.ds(item * rows, rows)]
    return [qs] + [r.at[:, pl.ds(item * rows, rows)] for r in refs[1:]]


def _softmax_stage(s_sc, p_sc, a_sc, m_sc):
    for c in range(s_sc.shape[1] // LANES):
        cols = pl.ds(c * LANES, LANES)
        s = s_sc[:, cols]
        m_prev = m_sc[:, cols]
        m_new = jnp.maximum(m_prev, jnp.max(s, axis=0, keepdims=True))
        m_sc[:, cols] = m_new
        p_sc[:, cols] = jnp.exp2(s - m_new[0:1, :]).astype(BF16)
        a_sc[:, cols] = jnp.exp2(m_prev - m_new)


def _rescale(a_sc, acc):
    return jnp.concatenate([a_sc[...]] * (acc.shape[0] // SUBLANES), axis=0) * acc


def _flash_pipeline(n_new, cache, k_at, v_at, qk, sm, av):
    o = 0 if cache is None else 1
    n = n_new + o
    k_of = lambda c: cache[0]() if (o and c == 0) else k_at(c - o)
    v_of = lambda c: cache[1]() if (o and c == 0) else v_at(c - o)
    qk(0, k_of(0))
    if n == 1:
        sm(0)
        av(0, v_of(0))
        return
    qk(1, k_of(1))
    sm(0)
    t0 = 2
    if o:
        assert n > 2
        qk(0, k_of(2))
        sm(1)
        av(0, v_of(0))
        t0 = 3
    for t in range(t0, n):
        qk(t % 2, k_at(t - o))
        sm(1 - t % 2)
        av(t % 2, v_at(t - 2 - o))
    sm((n - 1) % 2)
    av(n % 2, v_of(n - 2))
    av((n - 1) % 2, v_of(n - 1))


def _item_schedule(nk, cache, k_at, v_at, start, qk, sm, av, finish):
    if nk == 1 and cache is None:
        return (lambda: (start(), qk(0, k_at(0))), lambda: sm(0), lambda: (av(0, v_at(0)), finish()))
    return (start, lambda: _flash_pipeline(nk, cache, k_at, v_at, qk, sm, av), finish)


def _run_items(stages):
    n = len(stages)
    for t in range(n + 2):
        for d in range(3):
            if 0 <= t - d < n:
                stages[t - d][d]()


def _gqa_kernel(*refs, has_cache, tq, tk, nk, n_items):
    if has_cache:
        q_ref, kc_ref, vc_ref = refs[:3]
        refs = refs[3:]
    else:
        q_ref = refs[0]
        refs = refs[1:]
    k_ref, v_ref, o_ref = refs[:3]
    low = lax.broadcasted_iota(jnp.int32, (tq, LANES), 1) < HEAD_DIM
    rows = 2 * tq

    def item_stages(item):
        qs_sc, s0, s1, p0, p1, a0, a1, m_sc, acc_sc = _item_scratch(refs[3:], item, rows)
        s_b, p_b, a_b = (s0, s1), (p0, p1), (a0, a1)

        def start():
            q = q_ref[pl.ds(item * tq, tq), :]
            zero = jnp.zeros_like(q)
            qs_sc[0:tq, :] = jnp.where(low, q, zero)
            qs_sc[tq:, :] = jnp.where(low, zero, q)
            m_sc[...] = jnp.full(m_sc.shape, -jnp.inf, F32)
            acc_sc[...] = jnp.zeros(acc_sc.shape, F32)

        keys = {}

        def qk(slot, k):
            keys[slot] = k.shape[0]
            s_b[slot][0:k.shape[0], :] = lax.dot_general(k, qs_sc[...], _NT, preferred_element_type=F32)

        def sm(slot):
            w = keys[slot]
            _softmax_stage(s_b[slot].at[0:w], p_b[slot].at[0:w], a_b[slot], m_sc)

        def av(slot, v):
            acc_sc[...] = (_rescale(a_b[slot], acc_sc[...])
                           + lax.dot_general(v, p_b[slot][0:v.shape[0], :], _TN, preferred_element_type=F32))

        def finish():
            acc = acc_sc[...]
            o_t = acc[0:HEAD_DIM, :] / acc[HEAD_DIM:, :]
            o_ref[pl.ds(item * tq, tq), :] = jnp.concatenate([o_t[:, 0:tq], o_t[:, tq:]], axis=0).T

        chunk = lambda ref: (lambda j: ref[0, _chunk_rows(item * nk + j, tk), :])
        cache = (lambda: kc_ref[0, 0], lambda: vc_ref[0, 0]) if has_cache else None
        return _item_schedule(nk, cache, chunk(k_ref), chunk(v_ref), start, qk, sm, av, finish)

    _run_items([item_stages(item) for item in range(n_items)])


def _gqa_attention(qa, kdup, vaug, cache, batch, tokens_per_batch, tq, tk, n_items):
    t_total = qa.shape[0]
    nq = tokens_per_batch // tq
    nk = tokens_per_batch // tk
    has_cache = cache is not None
    assert n_items == 1 or (nq == 1 and not has_cache)
    in_specs = [pl.BlockSpec((n_items * tq, LANES), lambda b, p, i: (b * nq + i, p))]
    args = [qa]
    if has_cache:
        past = cache[0].shape[2]
        in_specs += [pl.BlockSpec((1, 1, past, LANES), lambda b, p, i: (b, p // 2, 0, 0))] * 2
        args += list(cache)
    in_specs += [pl.BlockSpec((1, n_items * tokens_per_batch, LANES), lambda b, p, i: (p // 2, b, 0))] * 2
    args += [kdup, vaug]
    return pl.pallas_call(
        functools.partial(_gqa_kernel, has_cache=has_cache, tq=tq, tk=tk, nk=nk, n_items=n_items),
        grid=(batch // n_items, GQA_WIDTH // LANES, nq),
        in_specs=in_specs,
        out_specs=pl.BlockSpec((n_items * tq, LANES), lambda b, p, i: (b * nq + i, p)),
        out_shape=jax.ShapeDtypeStruct((t_total, GQA_WIDTH), F32),
        scratch_shapes=_attention_scratch(n_items, 2 * tq, tk),
        compiler_params=_params(("parallel", "parallel", "parallel")),
        name="gqa_attention",
    )(*args)


def _diff_kernel(*refs, has_cache, tq, tk, nk, n_items, lam_init):
    q_ref, lp_ref = refs[:2]
    if has_cache:
        kc_ref, vc0_ref, vc1_ref = refs[2:5]
        refs = refs[5:]
    else:
        refs = refs[2:]
    k_ref, v0_ref, v1_ref, o_ref = refs[:4]
    lane = lax.broadcasted_iota(jnp.int32, (tq, LANES), 1)
    lp = lp_ref[...]
    lam = (jnp.exp(jnp.sum(lp[0:1] * lp[1:2], axis=1, keepdims=True))
           - jnp.exp(jnp.sum(lp[2:3] * lp[3:4], axis=1, keepdims=True)) + lam_init)
    h2 = 2 * tq
    rows = 4 * tq

    def item_stages(item):
        qs_sc, s0, s1, p0, p1, a0, a1, m_sc, acc_sc = _item_scratch(refs[4:], item, rows)
        s_b, p_b, a_b = (s0, s1), (p0, p1), (a0, a1)

        def start():
            q = q_ref[pl.ds(item * tq, tq), :]
            zero = jnp.zeros_like(q)
            for j in range(4):
                qs_sc[j * tq:(j + 1) * tq, :] = jnp.where((lane >> 5) == j, q, zero)
            m_sc[...] = jnp.full(m_sc.shape, -jnp.inf, F32)
            acc_sc[...] = jnp.zeros(acc_sc.shape, F32)

        keys = {}

        def qk(slot, k):
            keys[slot] = k.shape[0]
            s_b[slot][0:k.shape[0], :] = lax.dot_general(k, qs_sc[...], _NT, preferred_element_type=F32)

        def sm(slot):
            w = keys[slot]
            _softmax_stage(s_b[slot].at[0:w], p_b[slot].at[0:w], a_b[slot], m_sc)

        def av(slot, v):
            for half, vh in zip((pl.ds(0, h2), pl.ds(h2, h2)), v):
                acc_sc[:, half] = (_rescale(a_b[slot].at[:, half], acc_sc[:, half])
                                   + lax.dot_general(vh, p_b[slot][0:vh.shape[0], half], _TN,
                                                     preferred_element_type=F32))

        def finish():
            n0 = acc_sc[0:HEAD_DIM, 0:h2] / acc_sc[HEAD_DIM:, 0:h2]
            n1 = acc_sc[HEAD_DIM:, h2:] / acc_sc[0:HEAD_DIM, h2:]
            o_t = jnp.concatenate([n0[:, 0:tq] - lam * n0[:, tq:], n1[:, 0:tq] - lam * n1[:, tq:]], axis=0)
            o_ref[pl.ds(item * tq, tq), :] = o_t.T

        k_at = lambda j: k_ref[_chunk_rows(item * nk + j, tk), :]
        v_at = lambda j: (v0_ref[0, _chunk_rows(item * nk + j, tk), :], v1_ref[0, _chunk_rows(item * nk + j, tk), :])
        cache = (lambda: kc_ref[0], lambda: (vc0_ref[0, 0], vc1_ref[0, 0])) if has_cache else None
        return _item_schedule(nk, cache, k_at, v_at, start, qk, sm, av, finish)

    _run_items([item_stages(item) for item in range(n_items)])


def _diff_attention(qd, kd, vaugd, lam_p, cache, batch, tokens_per_batch, tq, tk, n_items, lam_init):
    t_total = qd.shape[0]
    nq = tokens_per_batch // tq
    nk = tokens_per_batch // tk
    has_cache = cache is not None
    assert n_items == 1 or (nq == 1 and not has_cache)
    in_specs = [pl.BlockSpec((n_items * tq, LANES), lambda b, p, i: (b * nq + i, p)),
                pl.BlockSpec((4, DIFF_QK_DIM), lambda b, p, i: (0, 0))]
    args = [qd, lam_p]
    if has_cache:
        ckd, cvd = cache
        past = ckd.shape[1]
        in_specs += [pl.BlockSpec((1, past, LANES), lambda b, p, i: (b, 0, p)),
                     pl.BlockSpec((1, 1, past, LANES), lambda b, p, i: (b, 2 * p, 0, 0)),
                     pl.BlockSpec((1, 1, past, LANES), lambda b, p, i: (b, 2 * p + 1, 0, 0))]
        args += [ckd, cvd, cvd]
    kv_rows = n_items * tokens_per_batch
    in_specs += [pl.BlockSpec((kv_rows, LANES), lambda b, p, i: (b, p)),
                 pl.BlockSpec((1, kv_rows, LANES), lambda b, p, i: (2 * p, b, 0)),
                 pl.BlockSpec((1, kv_rows, LANES), lambda b, p, i: (2 * p + 1, b, 0))]
    args += [kd, vaugd, vaugd]
    return pl.pallas_call(
        functools.partial(_diff_kernel, has_cache=has_cache, tq=tq, tk=tk, nk=nk, n_items=n_items,
                          lam_init=lam_init),
        grid=(batch // n_items, DIFF_WIDTH // LANES, nq),
        in_specs=in_specs,
        out_specs=pl.BlockSpec((n_items * tq, LANES), lambda b, p, i: (b * nq + i, p)),
        out_shape=jax.ShapeDtypeStruct((t_total, DIFF_WIDTH), F32),
        scratch_shapes=_attention_scratch(n_items, 4 * tq, tk),
        compiler_params=_params(("parallel", "parallel", "parallel")),
        name="diff_attention",
    )(*args)


def _hgrn_gates(qs, z, v, lb, g_sc, tri):
    sig = jax.nn.sigmoid(z)
    om = 1.0 - lb
    k = om * (1.0 - sig)
    g = jnp.log2(jnp.maximum(lb + om * sig, FORGET_MIN))
    g1 = g.astype(BF16)
    e1 = g - g1.astype(F32)
    g2 = e1.astype(BF16)
    g3 = (e1 - g2.astype(F32)).astype(BF16)
    G = (jnp.dot(tri, g1, preferred_element_type=F32) + jnp.dot(tri, g2, preferred_element_type=F32)
         + jnp.dot(tri, g3, preferred_element_type=F32))
    g_sc[...] = G
    return qs, k, G, v.astype(BF16)


def _hgrn_intra(gates, g_sc, lmask_ref, hmask, bdmask, rev):
    C, W = SCAN_CHUNK, HGRN_WIDTH
    qs, k, G, vb = gates
    g_last = g_sc[pl.ds(0 if rev else C - 1, 1), :]
    q_in = (qs * jnp.exp2(G)).astype(BF16)
    k_end = (k * jnp.exp2(g_last - G)).astype(BF16)
    ut = lax.dot_general(vb, k_end, _TN, preferred_element_type=F32) * bdmask

    def block_diag(a):
        return jnp.concatenate([a] * HGRN_HEADS, axis=0) * hmask

    def ref_rows(block, off):
        return jnp.concatenate([jnp.broadcast_to(g_sc[pl.ds(b0 + off, 1), :], (block, W))
                                for b0 in range(0, C, block)], axis=0)

    a = None
    for level in range(HGRN_LEVELS):
        block = 8 << level
        half = block // 2
        if level == 0:
            r = ref_rows(block, half)
            qe, ke = G - r, r - G
        else:
            r = ref_rows(block, half if rev else half - 1)
            qe, ke = jnp.minimum(G - r, 0.0), jnp.minimum(r - G, 0.0)
        al = lax.dot_general((qs * jnp.exp2(qe)).astype(BF16), block_diag((k * jnp.exp2(ke)).astype(BF16)),
                             _NT, preferred_element_type=F32)
        m = lmask_ref[rev, level]
        a = jnp.where(m > 0.5, al, 0.0) if level == 0 else a + al * m

    intra = jnp.dot(a.astype(BF16), block_diag(vb), preferred_element_type=F32)
    return q_in, jnp.exp2(g_last), ut, intra


def _hgrn_state(parts, st):
    q_in, decay, ut, intra = parts
    out = lax.dot_general(q_in, st.astype(BF16), _NT, preferred_element_type=F32) + intra
    return out, st * decay + ut


def _hgrn_kernel(*refs, layer, n_chunks, has_s0, state_out, state_alias):
    it = iter(refs)
    qf, zf, vf, qb, zb, vb, lbraw, tri_ref, lmask_ref, hmask_ref, bdmask_ref = [next(it) for _ in range(11)]
    s0 = next(it) if has_s0 else None
    if state_alias:
        next(it)
    of_ref, ob_ref = next(it), next(it)
    sto_ref = next(it) if state_out else None
    st_ref, g_sc = next(it), next(it)

    @pl.when(pl.program_id(1) == 0)
    def _():
        if has_s0:
            st_ref[...] = s0[0]
        else:
            st_ref[...] = jnp.zeros(st_ref.shape, F32)

    lbr = lbraw[...]
    ex = jnp.exp(lbr - jnp.max(lbr, axis=0, keepdims=True))
    sm = ex / jnp.sum(ex, axis=0, keepdims=True)
    lbs = jnp.zeros(sm.shape[1:], F32)
    for j in range(1, layer + 1):
        lbs = lbs + sm[j]
    lb_f, lb_b = lbs[0:1], lbs[1:2]

    hmask = hmask_ref[...]
    bdmask = bdmask_ref[...]

    chunks = []
    for cidx in range(n_chunks):
        chunks.append((0, pl.ds(cidx * SCAN_CHUNK, SCAN_CHUNK), qf, zf, vf, lb_f, of_ref))
        chunks.append((1, pl.ds((n_chunks - 1 - cidx) * SCAN_CHUNK, SCAN_CHUNK), qb, zb, vb, lb_b, ob_ref))
    state = [st_ref[0], st_ref[1]]
    gates, parts = {}, {}
    for t in range(len(chunks) + 2):
        if t < len(chunks):
            d, rows, q_r, z_r, v_r, lb, _ = chunks[t]
            gates[t] = _hgrn_gates(q_r[rows, :], z_r[rows, :], v_r[rows, :], lb, g_sc.at[t], tri_ref[d])
        if 0 <= t - 1 < len(chunks):
            parts[t - 1] = _hgrn_intra(gates.pop(t - 1), g_sc.at[t - 1], lmask_ref, hmask, bdmask, chunks[t - 1][0])
        if 0 <= t - 2 < len(chunks):
            d, rows, _, _, _, _, o_r = chunks[t - 2]
            o_r[rows, :], state[d] = _hgrn_state(parts.pop(t - 2), state[d])
    st_ref[0], st_ref[1] = state
    s_f, s_b = state

    if state_out:
        @pl.when(pl.program_id(1) == pl.num_programs(1) - 1)
        def _():
            if not state_alias:
                sto_ref[0, 1:] = jnp.zeros((sto_ref.shape[1] - 1,) + sto_ref.shape[2:], F32)
            for d, s in enumerate((s_f, s_b)):
                for hh in range(HGRN_HEADS):
                    slab = s[hh * HEAD_DIM:(hh + 1) * HEAD_DIM, (hh // 2) * LANES:(hh // 2 + 1) * LANES]
                    sto_ref[0, 0, d, hh] = pltpu.roll(slab, HEAD_DIM, 1) if hh % 2 else slab


def _hgrn_scan(zr, lb_raw, scan_consts, s0, state_prev, state_out, layer, batch, tokens_per_batch):
    t_total = zr.shape[0]
    tb = 256
    nblk = tokens_per_batch // tb
    W = HGRN_WIDTH
    has_s0 = s0 is not None
    state_alias = state_prev is not None
    fblk = lambda col: pl.BlockSpec((tb, W), lambda b, j: (b * nblk + j, col))
    bblk = lambda col: pl.BlockSpec((tb, W), lambda b, j: (b * nblk + nblk - 1 - j, col))
    full = lambda a: pl.BlockSpec(a.shape, lambda b, j: (0,) * a.ndim)
    in_specs = [fblk(0), fblk(1), fblk(3), bblk(0), bblk(2), bblk(3),
                pl.BlockSpec((DEPTH, 2, W), lambda b, j: (0, 0, 0))] + [full(a) for a in scan_consts]
    args = [zr] * 6 + [lb_raw] + list(scan_consts)
    if has_s0:
        in_specs.append(pl.BlockSpec((1, 2, W, W), lambda b, j: (b, 0, 0, 0)))
        args.append(s0)
    aliases = {}
    if state_alias:
        aliases = {len(args): 2}
        in_specs.append(pl.BlockSpec(memory_space=pl.ANY))
        args.append(state_prev)
    out_specs = [pl.BlockSpec((tb, W), lambda b, j: (b * nblk + j, 0)),
                 pl.BlockSpec((tb, W), lambda b, j: (b * nblk + nblk - 1 - j, 0))]
    out_shape = [jax.ShapeDtypeStruct((t_total, W), F32), jax.ShapeDtypeStruct((t_total, W), F32)]
    if state_out and not state_alias:
        assert layer == 0
        out_specs.append(pl.BlockSpec((1, DEPTH, 2, HGRN_HEADS, HEAD_DIM, LANES), lambda b, j: (b, 0, 0, 0, 0, 0)))
    elif state_out:
        out_specs.append(pl.BlockSpec((1, 1, 2, HGRN_HEADS, HEAD_DIM, LANES), lambda b, j: (b, layer, 0, 0, 0, 0)))
    if state_out:
        out_shape.append(jax.ShapeDtypeStruct((batch, DEPTH, 2, HGRN_HEADS, HEAD_DIM, LANES), F32))
    return pl.pallas_call(
        functools.partial(_hgrn_kernel, layer=layer, n_chunks=tb // SCAN_CHUNK, has_s0=has_s0,
                          state_out=state_out, state_alias=state_alias),
        grid=(batch, nblk),
        in_specs=in_specs, out_specs=out_specs, out_shape=out_shape,
        input_output_aliases=aliases,
        scratch_shapes=[pltpu.VMEM((2, W, W), F32), pltpu.VMEM((2 * tb // SCAN_CHUNK, SCAN_CHUNK, W), F32)],
        compiler_params=_params(("parallel", "arbitrary")),
        name="hgrn_scan",
    )(*args)


def _outproj_kernel(oa_ref, od_ref, of_ref, ob_ref, g_ref, x_ref, gate_ref, w_ref, sub_ref, hgn_ref,
                    bd_ref, lng_ref, lnb_ref, o_ref, *, lam_init):
    inv_d = 1.0 / HEAD_DIM
    g = g_ref[...]
    bd = bd_ref[...]

    def head_rms(t, gain):
        ssq = jnp.dot((t * t).astype(BF16), bd, preferred_element_type=F32)
        return t * lax.rsqrt(ssq * inv_d + RMS_EPS) * gain

    ma = (oa_ref[...] * g[:, 0:GQA_WIDTH]).astype(BF16)
    md = (head_rms(od_ref[...], sub_ref[...]) * (1.0 - lam_init)
          * g[:, GQA_WIDTH:GQA_WIDTH + DIFF_WIDTH]).astype(BF16)
    mr = (head_rms(of_ref[...] + ob_ref[...], hgn_ref[...]) * g[:, GQA_WIDTH + DIFF_WIDTH:]).astype(BF16)
    y = (jnp.dot(ma, w_ref[0:GQA_WIDTH, :], preferred_element_type=F32)
         + jnp.dot(md, w_ref[GQA_WIDTH:GQA_WIDTH + DIFF_WIDTH, :], preferred_element_type=F32)
         + jnp.dot(mr, w_ref[GQA_WIDTH + DIFF_WIDTH:, :], preferred_element_type=F32))
    xn = DEEPNORM_ALPHA * x_ref[...] + gate_ref[0] * y
    mu = jnp.mean(xn, axis=-1, keepdims=True)
    xc = xn - mu
    var = jnp.mean(xc * xc, axis=-1, keepdims=True)
    o_ref[...] = xc * lax.rsqrt(var + LN_EPS) * lng_ref[...] + lnb_ref[...]


def _out_projection(out_a, out_d, o_f, o_b, gates, x, gate, w_out_b, subln, hgn, bd, ln_g, ln_b,
                    tokens_per_batch, lam_init):
    t_total = x.shape[0]
    tm = 256
    mod_rows = t_total // gate.shape[0] // tm
    row = lambda w: pl.BlockSpec((tm, w), lambda i: (i, 0))
    full = lambda shape: pl.BlockSpec(shape, lambda i: (0,) * len(shape))
    return pl.pallas_call(
        functools.partial(_outproj_kernel, lam_init=lam_init),
        grid=(t_total // tm,),
        in_specs=[row(GQA_WIDTH), row(DIFF_WIDTH), row(HGRN_WIDTH), row(HGRN_WIDTH), row(D_MODEL), row(D_MODEL),
                  pl.BlockSpec((1, 1, D_MODEL), lambda i: (i // mod_rows, 0, 0)),
                  full((D_MODEL, D_MODEL)), full((1, DIFF_WIDTH)), full((1, HGRN_WIDTH)),
                  full((DIFF_WIDTH, DIFF_WIDTH)), full((1, D_MODEL)), full((1, D_MODEL))],
        out_specs=row(D_MODEL),
        out_shape=jax.ShapeDtypeStruct((t_total, D_MODEL), F32),
        compiler_params=_params(("parallel",)),
        name="out_projection",
    )(out_a, out_d, o_f, o_b, gates, x, gate, w_out_b, subln, hgn, bd, ln_g, ln_b)


def _block_diag_ones(n):
    idx = np.arange(n) // HEAD_DIM
    return jnp.asarray((idx[:, None] == idx[None, :]).astype(np.float32), dtype=BF16)


def _scan_constants():
    C, W, H = SCAN_CHUNK, HGRN_WIDTH, HGRN_HEADS
    t = np.arange(C)[:, None]
    u = np.arange(C)[None, :]
    tri = np.stack([u <= t, u >= t])
    s = (np.arange(W) % C)[None, :]
    lmask = np.zeros((2, HGRN_LEVELS, C, W), np.float32)
    for level in range(HGRN_LEVELS):
        block = 8 << level
        same = (t // block) == (s // block)
        if level == 0:
            lmask[0, level] = same & (s <= t)
            lmask[1, level] = same & (s >= t)
        else:
            t_hi, s_hi = (t % block) >= block // 2, (s % block) >= block // 2
            lmask[0, level] = same & t_hi & ~s_hi
            lmask[1, level] = same & ~t_hi & s_hi
    rows = np.arange(H * C)[:, None]
    hmask = (rows // C) == (np.arange(W)[None, :] // HEAD_DIM)
    bdmask = (np.arange(W)[:, None] // HEAD_DIM) == (np.arange(W)[None, :] // HEAD_DIM)
    return (jnp.asarray(tri.astype(np.float32), dtype=BF16), jnp.asarray(lmask),
            jnp.asarray(hmask.astype(np.float32), dtype=BF16), jnp.asarray(bdmask.astype(np.float32)))


def _rope_tables(n_tokens, dim):
    rows = n_tokens // GRID_W
    row = jnp.repeat(jnp.arange(rows, dtype=F32), GRID_W)
    col = jnp.tile(jnp.arange(GRID_W, dtype=F32), rows)
    quarter = dim // 4
    inv_freq = ROPE_THETA ** (-jnp.arange(quarter, dtype=F32) / quarter)
    ar = row[:, None] * inv_freq[None, :]
    ac = col[:, None] * inv_freq[None, :]
    ang = jnp.concatenate([ar, ar, ac, ac], axis=-1)
    cos = jnp.tile(jnp.cos(ang), (1, LANES // dim))
    sin = jnp.tile(jnp.sin(ang), (1, LANES // dim))
    first = (jnp.arange(LANES) % (dim // 2)) < quarter
    return cos, jnp.where(first, -sin, 0.0), jnp.where(first, 0.0, sin)


def _mixer_layer(x, scale, shift, gate, layer, batch, tokens_per_batch, rope_tabs, cache, consts, weights,
                 ctx_out, ctx_prev):
    (w_in_b, w_out_b, qgain, kgain, lam_p, subln, hgn, lb_raw, ln_g, ln_b) = weights
    bdq, bdk, bdd, scan_consts = consts
    lam_init = 0.8 - 0.6 * math.exp(-0.3 * layer)
    outs = _in_projection(x, scale, shift, w_in_b, qgain, kgain, bdq, bdk, rope_tabs, tokens_per_batch, ctx_out,
                          layer, None if ctx_prev is None else ctx_prev[0])
    qa, kdup, vaug, qd, kd, vaugd, gates, zr = outs[:8]
    if cache is None:
        gqa_cache = diff_cache = s0 = None
        tq_a = tq_d = tk = tokens_per_batch
        items_a, items_d = 2, 2
    else:
        gqa_cache, diff_cache, s0 = cache
        tq_a, tq_d, tk = 512, 256, 512
        items_a = items_d = 1
    out_a = _gqa_attention(qa, kdup, vaug, gqa_cache, batch, tokens_per_batch, tq_a, tk, items_a)
    out_d = _diff_attention(qd, kd, vaugd, lam_p, diff_cache, batch, tokens_per_batch, tq_d, tk, items_d, lam_init)
    scan = _hgrn_scan(zr, lb_raw, scan_consts, s0, None if ctx_prev is None else ctx_prev[1], ctx_out, layer,
                      batch, tokens_per_batch)
    x_new = _out_projection(out_a, out_d, scan[0], scan[1], gates, x, gate, w_out_b, subln, hgn, bdd, ln_g, ln_b,
                            tokens_per_batch, lam_init)
    return x_new, ((outs[8:], scan[2]) if ctx_out else None)


def kernel(x_prompt, x_sample, cache_gqa_k, cache_gqa_v, cache_diff_k, cache_diff_v, state_hgrn, c, c_ctx,
           w_ada, b_ada, w_in, gqa_q_norm, gqa_k_norm, diff_lambda, diff_subln, hgrn_lower_bounds, hgrn_norm,
           w_out, ln_g, ln_b):
    batch, seq, _ = x_prompt.shape
    dec_batch, dec_seq, _ = x_sample.shape
    past = cache_gqa_k.shape[2]
    H, HD = HGRN_HEADS, HEAD_DIM

    cond = jnp.zeros((MOD_ROWS, D_MODEL), F32).at[0].set(c_ctx).at[1:1 + dec_batch].set(c)
    mod = _modulation(cond, w_ada, b_ada)
    consts = (_block_diag_ones(GQA_WIDTH), _block_diag_ones(LANES), _block_diag_ones(DIFF_WIDTH),
              _scan_constants())
    rope_tabs = _rope_tables(dec_seq, HEAD_DIM) + _rope_tables(dec_seq, DIFF_QK_DIM)
    w_in_b = w_in.astype(BF16)
    w_out_b = w_out.astype(BF16)

    def layer_weights(l):
        return (w_in_b[l], w_out_b[l],
                jnp.tile(gqa_q_norm[l], GQA_WIDTH // HD)[None, :] * (HD ** -0.5 * LOG2E),
                jnp.tile(gqa_k_norm[l], LANES // HD)[None, :],
                diff_lambda[l],
                jnp.tile(diff_subln[l], DIFF_WIDTH // HD)[None, :],
                jnp.tile(hgrn_norm[l], HGRN_WIDTH // HD)[None, :],
                hgrn_lower_bounds, ln_g[l][None, :], ln_b[l][None, :])

    def mod_rows(l, lo, n):
        m = mod[l, lo:lo + n]
        return (m[:, None, 0:D_MODEL], m[:, None, D_MODEL:2 * D_MODEL], m[:, None, 2 * D_MODEL:])

    x = x_prompt.reshape(batch * seq, D_MODEL)
    ctx = None
    for l in range(DEPTH):
        shift, scale, gate = mod_rows(l, 0, 1)
        x, ctx = _mixer_layer(x, scale, shift, gate, l, batch, seq, None, None, consts, layer_weights(l), True, ctx)
    y_prompt = x.reshape(batch, seq, D_MODEL)
    (kn, av, dk, dv), st = ctx
    new_leaves = [kn.reshape(batch, DEPTH, seq, GQA_KV_HEADS, HD), av.reshape(batch, DEPTH, seq, GQA_KV_HEADS, HD),
                  dk.reshape(batch, DEPTH, seq, DIFF_HEADS, HD), dv.reshape(batch, DEPTH, seq, DIFF_HEADS, HD),
                  jnp.swapaxes(st[..., :HD], -1, -2)]

    ones = jnp.ones((dec_batch, past, HD), F32)
    eye = jnp.eye(H, dtype=F32)
    x = x_sample.reshape(dec_batch * dec_seq, D_MODEL)
    for l in range(DEPTH):
        shift, scale, gate = mod_rows(l, 1, dec_batch)
        ck, cv = cache_gqa_k[:, l], cache_gqa_v[:, l]
        ckd = jnp.stack([jnp.concatenate([ck[:, :, g], ck[:, :, g]], axis=-1) for g in range(GQA_KV_HEADS)], axis=1)
        cvd = jnp.stack([jnp.concatenate([cv[:, :, g], ones], axis=-1) for g in range(GQA_KV_HEADS)], axis=1)
        dkc = cache_diff_k[:, l].reshape(dec_batch, past, DIFF_WIDTH)
        dvc = cache_diff_v[:, l]
        dvd = jnp.stack([jnp.concatenate([dvc[:, :, h], ones] if h % 2 == 0 else [ones, dvc[:, :, h]], axis=-1)
                         for h in range(DIFF_HEADS)], axis=1)
        s0t = jnp.swapaxes(state_hgrn[:, l].astype(F32), -1, -2)
        s0 = (s0t[:, :, :, :, None, :] * eye[None, None, :, None, :, None]).reshape(dec_batch, 2, H * HD, H * HD)
        cache = ((ckd.astype(BF16), cvd.astype(BF16)), (dkc.astype(BF16), dvd.astype(BF16)), s0)
        x, _ = _mixer_layer(x, scale, shift, gate, l, dec_batch, dec_seq, rope_tabs, cache, consts,
                            layer_weights(l), False, None)
    y_sample = x.reshape(dec_batch, dec_seq, D_MODEL)
    return (y_prompt, y_sample) + tuple(new_leaves)
```

```python
import functools
import math

import numpy as np
import jax
import jax.numpy as jnp
from jax import lax
from jax.experimental import pallas as pl
from jax.experimental.pallas import tpu as pltpu

F32 = jnp.float32
BF16 = jnp.bfloat16

D_MODEL = 1024
DEPTH = 2
GRID_W = 64
HEAD_DIM = 64
GQA_WIDTH = 512
GQA_KV_HEADS = 2
DIFF_WIDTH = 256
DIFF_HEADS = 4
DIFF_QK_DIM = 32
HGRN_WIDTH = 256
HGRN_HEADS = 4
IN_WIDTH = 3584
SCAN_CHUNK = 64
HGRN_LEVELS = 4
ROPE_THETA = 10000.0
RMS_EPS = 1e-6
LN_EPS = 1e-5
FORGET_MIN = 1e-6
DEEPNORM_ALPHA = (2 * DEPTH) ** 0.25
LANES = 128
SUBLANES = 8
MOD_ROWS = SUBLANES
VMEM_LIMIT = 48 * 1024 * 1024
SM_ROWS = 32
PROJ_ROWS = 512
LOG2E = math.log2(math.e)

_A_Q, _A_K, _A_V, _A_G = 0, 512, 640, 768
_D_Q, _D_K, _D_V, _D_G = 1280, 1536, 1792, 2048
_R_Z, _R_G = 2304, 3328

_NT = (((1,), (1,)), ((), ()))
_TN = (((0,), (0,)), ((), ()))


def _silu(x):
    return x * jax.nn.sigmoid(x)


def _params(sem):
    return pltpu.CompilerParams(dimension_semantics=sem, vmem_limit_bytes=VMEM_LIMIT)


def _mod_kernel(c_ref, w_ref, b_ref, o_ref):
    c = c_ref[...]
    o_ref[0] = jnp.dot(_silu(c), w_ref[0], preferred_element_type=F32) + b_ref[0]


def _modulation(cond, w_ada, b_ada):
    tn = 1024
    return pl.pallas_call(
        _mod_kernel,
        grid=(DEPTH, 3 * D_MODEL // tn),
        in_specs=[pl.BlockSpec((MOD_ROWS, D_MODEL), lambda l, j: (0, 0)),
                  pl.BlockSpec((1, D_MODEL, tn), lambda l, j: (l, 0, j)),
                  pl.BlockSpec((1, 1, tn), lambda l, j: (l, 0, j))],
        out_specs=pl.BlockSpec((1, MOD_ROWS, tn), lambda l, j: (l, 0, j)),
        out_shape=jax.ShapeDtypeStruct((DEPTH, MOD_ROWS, 3 * D_MODEL), F32),
        compiler_params=_params(("arbitrary", "arbitrary")),
        name="modulation",
    )(cond, w_ada, b_ada.reshape(DEPTH, 1, 3 * D_MODEL))


def _rope(x, c, s1, s2, shift):
    return x * c + pltpu.roll(x, LANES - shift, 1) * s1 + pltpu.roll(x, shift, 1) * s2


def _inproj_kernel(*refs, rope, ctx_out, ctx_alias):
    it = iter(refs)
    x_ref, sc_ref, sh_ref, w_ref, qg_ref, kg_ref, bdq_ref, bdk_ref = [next(it) for _ in range(8)]
    if rope:
        ca, s1a, s2a, cd, s1d, s2d = [next(it)[...] for _ in range(6)]
    if ctx_alias:
        for _ in range(4):
            next(it)
    qa_o, kdup_o, vaug_o, qd_o, kd_o, vaugd_o, gates_o, zr_o = [next(it) for _ in range(8)]
    if ctx_out:
        ctx_refs = [next(it) for _ in range(4)]
        if not ctx_alias:
            for r in ctx_refs:
                r[:, 1:] = jnp.zeros((r.shape[0], r.shape[1] - 1) + r.shape[2:], F32)

        def put_ctx(n, val):
            r = ctx_refs[n]
            r[:, 0] = val.reshape(r.shape[0], r.shape[2], r.shape[3])

    h = (x_ref[...] * (1.0 + sc_ref[0]) + sh_ref[0]).astype(BF16)

    def proj(lo, hi):
        return jnp.dot(h, w_ref[:, lo:hi], preferred_element_type=F32)

    tm = h.shape[0]
    low = lax.broadcasted_iota(jnp.int32, (tm, LANES), 1) < HEAD_DIM
    inv_d = 1.0 / HEAD_DIM

    aq = proj(_A_Q, _A_K)
    ssq = jnp.dot((aq * aq).astype(BF16), bdq_ref[...], preferred_element_type=F32)
    qn = aq * lax.rsqrt(ssq * inv_d + RMS_EPS) * qg_ref[...]
    for j in range(GQA_WIDTH // LANES):
        slab = qn[:, j * LANES:(j + 1) * LANES]
        if rope:
            slab = _rope(slab, ca, s1a, s2a, HEAD_DIM // 4)
        qa_o[:, j * LANES:(j + 1) * LANES] = slab.astype(BF16)

    ak = proj(_A_K, _A_V)
    ssqk = jnp.dot((ak * ak).astype(BF16), bdk_ref[...], preferred_element_type=F32)
    kn = ak * lax.rsqrt(ssqk * inv_d + RMS_EPS) * kg_ref[...]
    if ctx_out:
        put_ctx(0, kn)
    if rope:
        kn = _rope(kn, ca, s1a, s2a, HEAD_DIM // 4)
    kr = pltpu.roll(kn, HEAD_DIM, 1)
    kdup_o[0] = jnp.where(low, kn, kr).astype(BF16)
    kdup_o[1] = jnp.where(low, kr, kn).astype(BF16)

    av = proj(_A_V, _A_G)
    if ctx_out:
        put_ctx(1, av)
    vr = pltpu.roll(av, HEAD_DIM, 1)
    vaug_o[0] = jnp.where(low, av, 1.0).astype(BF16)
    vaug_o[1] = jnp.where(low, vr, 1.0).astype(BF16)

    gates_o[:, 0:GQA_WIDTH] = _silu(proj(_A_G, _D_Q)).astype(BF16)

    dq = proj(_D_Q, _D_K) * (DIFF_QK_DIM ** -0.5 * LOG2E)
    dk = proj(_D_K, _D_V)
    if ctx_out:
        put_ctx(2, dk)
    for j in range(DIFF_WIDTH // LANES):
        sq = dq[:, j * LANES:(j + 1) * LANES]
        sk = dk[:, j * LANES:(j + 1) * LANES]
        if rope:
            sq = _rope(sq, cd, s1d, s2d, DIFF_QK_DIM // 4)
            sk = _rope(sk, cd, s1d, s2d, DIFF_QK_DIM // 4)
        qd_o[:, j * LANES:(j + 1) * LANES] = sq.astype(BF16)
        kd_o[:, j * LANES:(j + 1) * LANES] = sk.astype(BF16)
    dv = proj(_D_V, _D_G)
    if ctx_out:
        put_ctx(3, dv)
    for j in range(DIFF_WIDTH // LANES):
        sv = dv[:, j * LANES:(j + 1) * LANES]
        vaugd_o[2 * j] = jnp.where(low, sv, 1.0).astype(BF16)
        vaugd_o[2 * j + 1] = jnp.where(low, 1.0, sv).astype(BF16)
    gates_o[:, GQA_WIDTH:GQA_WIDTH + DIFF_WIDTH] = _silu(proj(_D_G, _R_Z)).astype(BF16)

    zr = proj(_R_Z, _R_G)
    zr_o[:, 0:HGRN_WIDTH] = _silu(zr[:, 0:HGRN_WIDTH])
    zr_o[:, HGRN_WIDTH:] = zr[:, HGRN_WIDTH:]
    gates_o[:, GQA_WIDTH + DIFF_WIDTH:] = _silu(proj(_R_G, IN_WIDTH)).astype(BF16)


def _in_projection(x, scale, shift, w_in_b, qgain, kgain, bdq, bdk, rope_tabs, tokens_per_batch, ctx_out,
                   layer, ctx_prev):
    t_total = x.shape[0]
    tm = PROJ_ROWS
    nb_rows = tokens_per_batch // tm
    mod_rows = t_total // scale.shape[0] // tm
    rope = rope_tabs is not None
    assert not rope or nb_rows >= 1
    full = lambda shape: pl.BlockSpec(shape, lambda i: (0,) * len(shape))
    in_specs = [pl.BlockSpec((tm, D_MODEL), lambda i: (i, 0)),
                pl.BlockSpec((1, 1, D_MODEL), lambda i: (i // mod_rows, 0, 0)),
                pl.BlockSpec((1, 1, D_MODEL), lambda i: (i // mod_rows, 0, 0)),
                full((D_MODEL, IN_WIDTH)),
                full((1, GQA_WIDTH)), full((1, LANES)),
                full((GQA_WIDTH, GQA_WIDTH)), full((LANES, LANES))]
    args = [x, scale, shift, w_in_b, qgain, kgain, bdq, bdk]
    if rope:
        in_specs += [pl.BlockSpec((tm, LANES), lambda i: (i % nb_rows, 0))] * 6
        args += list(rope_tabs)
    row = lambda w: pl.BlockSpec((tm, w), lambda i: (i, 0))
    stk = lambda n: pl.BlockSpec((n, tm, LANES), lambda i: (0, i, 0))
    out_specs = [row(GQA_WIDTH), stk(2), stk(2), row(DIFF_WIDTH), row(DIFF_WIDTH), stk(4),
                 row(D_MODEL), row(4 * HGRN_WIDTH)]
    out_shape = [jax.ShapeDtypeStruct((t_total, GQA_WIDTH), BF16),
                 jax.ShapeDtypeStruct((2, t_total, LANES), BF16),
                 jax.ShapeDtypeStruct((2, t_total, LANES), BF16),
                 jax.ShapeDtypeStruct((t_total, DIFF_WIDTH), BF16),
                 jax.ShapeDtypeStruct((t_total, DIFF_WIDTH), BF16),
                 jax.ShapeDtypeStruct((4, t_total, LANES), BF16),
                 jax.ShapeDtypeStruct((t_total, D_MODEL), BF16),
                 jax.ShapeDtypeStruct((t_total, 4 * HGRN_WIDTH), F32)]
    aliases = {}
    if ctx_out:
        seq = tokens_per_batch
        nseq = tm // seq
        assert nseq * seq == tm
        widths = (LANES, LANES, DIFF_WIDTH, DIFF_WIDTH)
        if ctx_prev is not None:
            aliases = {len(args) + n: len(out_specs) + n for n in range(4)}
            in_specs += [pl.BlockSpec(memory_space=pl.ANY)] * 4
            args += list(ctx_prev)
        if ctx_prev is None:
            assert layer == 0
            out_specs += [pl.BlockSpec((nseq, DEPTH, seq, w), lambda i: (i, 0, 0, 0)) for w in widths]
        else:
            out_specs += [pl.BlockSpec((nseq, 1, seq, w), lambda i: (i, layer, 0, 0)) for w in widths]
        out_shape += [jax.ShapeDtypeStruct((t_total // seq, DEPTH, seq, w), F32) for w in widths]
    return pl.pallas_call(
        functools.partial(_inproj_kernel, rope=rope, ctx_out=ctx_out, ctx_alias=bool(aliases)),
        grid=(t_total // tm,),
        in_specs=in_specs, out_specs=out_specs, out_shape=out_shape,
        input_output_aliases=aliases,
        compiler_params=_params(("parallel",)),
        name="in_projection",
    )(*args)


def _chunk_rows(j, tk):
    return pl.ds(j * tk, tk) if isinstance(j, int) else pl.ds(pl.multiple_of(j * tk, tk), tk)


def _attention_scratch(n_items, rows, tk):
    rows = n_items * rows
    return ([pltpu.VMEM((rows, LANES), BF16)]
            + [pltpu.VMEM((rows, tk), F32)] * 2 + [pltpu.VMEM((rows, tk), BF16)] * 2
            + [pltpu.VMEM((rows, LANES), F32)] * 4)


def _item_scratch(refs, item, rows):
    return [r.at[pl.ds(item * rows, rows)] for r in refs]


def _softmax_stage(s_sc, p_sc, a_sc, m_sc, keys):
    for r in range(s_sc.shape[0] // SM_ROWS):
        rows = pl.ds(r * SM_ROWS, SM_ROWS)
        s = s_sc[rows, 0:keys]
        m_prev = m_sc[rows, :]
        m_new = jnp.maximum(m_prev, jnp.max(s, axis=1, keepdims=True))
        m_sc[rows, :] = m_new
        p_sc[rows, 0:keys] = jnp.exp2(s - jnp.concatenate([m_new] * (keys // LANES), axis=1)).astype(BF16)
        a_sc[rows, :] = jnp.exp2(m_prev - m_new)


def _flash_pipeline(n_new, cache, k_at, v_at, qk, sm, av):
    o = 0 if cache is None else 1
    n = n_new + o
    k_of = lambda c: cache[0]() if (o and c == 0) else k_at(c - o)
    v_of = lambda c: cache[1]() if (o and c == 0) else v_at(c - o)
    qk(0, k_of(0))
    if n == 1:
        sm(0)
        av(0, v_of(0))
        return
    qk(1, k_of(1))
    sm(0)
    t0 = 2
    if o:
        assert n > 2
        qk(0, k_of(2))
        sm(1)
        av(0, v_of(0))
        t0 = 3
    for t in range(t0, n):
        qk(t % 2, k_at(t - o))
        sm(1 - t % 2)
        av(t % 2, v_at(t - 2 - o))
    sm((n - 1) % 2)
    av(n % 2, v_of(n - 2))
    av((n - 1) % 2, v_of(n - 1))


def _item_schedule(nk, cache, k_at, v_at, start, qk, sm, av, finish):
    if nk == 1 and cache is None:
        return (lambda: (start(), qk(0, k_at(0))), lambda: sm(0), lambda: (av(0, v_at(0)), finish()))
    return (start, lambda: _flash_pipeline(nk, cache, k_at, v_at, qk, sm, av), finish)


def _run_items(stages):
    n = len(stages)
    for t in range(n + 2):
        for d in range(3):
            if 0 <= t - d < n:
                stages[t - d][d]()


def _gqa_kernel(*refs, has_cache, tq, tk, nk, n_items):
    if has_cache:
        q_ref, kc_ref, vc_ref = refs[:3]
        refs = refs[3:]
    else:
        q_ref = refs[0]
        refs = refs[1:]
    k_ref, v_ref, o_ref = refs[:3]
    low = lax.broadcasted_iota(jnp.int32, (tq, LANES), 1) < HEAD_DIM
    rows = 2 * tq

    def item_stages(item):
        qs_sc, s0, s1, p0, p1, a0, a1, m_sc, acc_sc = _item_scratch(refs[3:], item, rows)
        s_b, p_b, a_b = (s0, s1), (p0, p1), (a0, a1)

        def start():
            q = q_ref[pl.ds(item * tq, tq), :]
            zero = jnp.zeros_like(q)
            qs_sc[0:tq, :] = jnp.where(low, q, zero)
            qs_sc[tq:, :] = jnp.where(low, zero, q)
            m_sc[...] = jnp.full(m_sc.shape, -jnp.inf, F32)
            acc_sc[...] = jnp.zeros(acc_sc.shape, F32)

        keys = {}

        def qk(slot, k):
            keys[slot] = k.shape[0]
            s_b[slot][:, 0:k.shape[0]] = lax.dot_general(qs_sc[...], k, _NT, preferred_element_type=F32)

        def sm(slot):
            _softmax_stage(s_b[slot], p_b[slot], a_b[slot], m_sc, keys[slot])

        def av(slot, v):
            acc_sc[...] = (a_b[slot][...] * acc_sc[...]
                           + jnp.dot(p_b[slot][:, 0:v.shape[0]], v, preferred_element_type=F32))

        def finish():
            acc = acc_sc[...]
            on = acc / pltpu.roll(acc, HEAD_DIM, 1)
            o_ref[pl.ds(item * tq, tq), :] = jnp.where(low, on[:tq], pltpu.roll(on[tq:], HEAD_DIM, 1)).astype(BF16)

        chunk = lambda ref: (lambda j: ref[0, _chunk_rows(item * nk + j, tk), :])
        cache = (lambda: kc_ref[0, 0], lambda: vc_ref[0, 0]) if has_cache else None
        return _item_schedule(nk, cache, chunk(k_ref), chunk(v_ref), start, qk, sm, av, finish)

    _run_items([item_stages(item) for item in range(n_items)])


def _gqa_attention(qa, kdup, vaug, cache, batch, tokens_per_batch, tq, tk, n_items):
    t_total = qa.shape[0]
    nq = tokens_per_batch // tq
    nk = tokens_per_batch // tk
    has_cache = cache is not None
    assert n_items == 1 or (nq == 1 and not has_cache)
    in_specs = [pl.BlockSpec((n_items * tq, LANES), lambda b, p, i: (b * nq + i, p))]
    args = [qa]
    if has_cache:
        past = cache[0].shape[2]
        in_specs += [pl.BlockSpec((1, 1, past, LANES), lambda b, p, i: (b, p // 2, 0, 0))] * 2
        args += list(cache)
    in_specs += [pl.BlockSpec((1, n_items * tokens_per_batch, LANES), lambda b, p, i: (p // 2, b, 0))] * 2
    args += [kdup, vaug]
    return pl.pallas_call(
        functools.partial(_gqa_kernel, has_cache=has_cache, tq=tq, tk=tk, nk=nk, n_items=n_items),
        grid=(batch // n_items, GQA_WIDTH // LANES, nq),
        in_specs=in_specs,
        out_specs=pl.BlockSpec((n_items * tq, LANES), lambda b, p, i: (b * nq + i, p)),
        out_shape=jax.ShapeDtypeStruct((t_total, GQA_WIDTH), BF16),
        scratch_shapes=_attention_scratch(n_items, 2 * tq, tk),
        compiler_params=_params(("parallel", "parallel", "parallel")),
        name="gqa_attention",
    )(*args)


def _diff_kernel(*refs, has_cache, tq, tk, nk, n_items, lam_init):
    q_ref, lp_ref = refs[:2]
    if has_cache:
        kc_ref, vc0_ref, vc1_ref = refs[2:5]
        refs = refs[5:]
    else:
        refs = refs[2:]
    k_ref, v0_ref, v1_ref, o_ref = refs[:4]
    lane = lax.broadcasted_iota(jnp.int32, (tq, LANES), 1)
    lp = lp_ref[...]
    lam = (jnp.exp(jnp.sum(lp[0:1] * lp[1:2], axis=1, keepdims=True))
           - jnp.exp(jnp.sum(lp[2:3] * lp[3:4], axis=1, keepdims=True)) + lam_init)
    h2 = 2 * tq
    rows = 4 * tq

    def item_stages(item):
        qs_sc, s0, s1, p0, p1, a0, a1, m_sc, acc_sc = _item_scratch(refs[4:], item, rows)
        s_b, p_b, a_b = (s0, s1), (p0, p1), (a0, a1)

        def start():
            q = q_ref[pl.ds(item * tq, tq), :]
            zero = jnp.zeros_like(q)
            for j in range(4):
                qs_sc[j * tq:(j + 1) * tq, :] = jnp.where((lane >> 5) == j, q, zero)
            m_sc[...] = jnp.full(m_sc.shape, -jnp.inf, F32)
            acc_sc[...] = jnp.zeros(acc_sc.shape, F32)

        keys = {}

        def qk(slot, k):
            keys[slot] = k.shape[0]
            s_b[slot][:, 0:k.shape[0]] = lax.dot_general(qs_sc[...], k, _NT, preferred_element_type=F32)

        def sm(slot):
            _softmax_stage(s_b[slot], p_b[slot], a_b[slot], m_sc, keys[slot])

        def av(slot, v):
            for half, vh in zip((pl.ds(0, h2), pl.ds(h2, h2)), v):
                acc_sc[half, :] = (a_b[slot][half, :] * acc_sc[half, :]
                                   + jnp.dot(p_b[slot][half, 0:vh.shape[0]], vh, preferred_element_type=F32))

        def finish():
            acc = acc_sc[...]
            on = acc / pltpu.roll(acc, HEAD_DIM, 1)
            o0 = on[0:tq] - lam * on[tq:h2]
            o1 = on[h2:h2 + tq] - lam * on[h2 + tq:]
            o_ref[pl.ds(item * tq, tq), :] = jnp.where(lane < HEAD_DIM, o0, o1).astype(BF16)

        k_at = lambda j: k_ref[_chunk_rows(item * nk + j, tk), :]
        v_at = lambda j: (v0_ref[0, _chunk_rows(item * nk + j, tk), :], v1_ref[0, _chunk_rows(item * nk + j, tk), :])
        cache = (lambda: kc_ref[0], lambda: (vc0_ref[0, 0], vc1_ref[0, 0])) if has_cache else None
        return _item_schedule(nk, cache, k_at, v_at, start, qk, sm, av, finish)

    _run_items([item_stages(item) for item in range(n_items)])


def _diff_attention(qd, kd, vaugd, lam_p, cache, batch, tokens_per_batch, tq, tk, n_items, lam_init):
    t_total = qd.shape[0]
    nq = tokens_per_batch // tq
    nk = tokens_per_batch // tk
    has_cache = cache is not None
    assert n_items == 1 or (nq == 1 and not has_cache)
    in_specs = [pl.BlockSpec((n_items * tq, LANES), lambda b, p, i: (b * nq + i, p)),
                pl.BlockSpec((4, DIFF_QK_DIM), lambda b, p, i: (0, 0))]
    args = [qd, lam_p]
    if has_cache:
        ckd, cvd = cache
        past = ckd.shape[1]
        in_specs += [pl.BlockSpec((1, past, LANES), lambda b, p, i: (b, 0, p)),
                     pl.BlockSpec((1, 1, past, LANES), lambda b, p, i: (b, 2 * p, 0, 0)),
                     pl.BlockSpec((1, 1, past, LANES), lambda b, p, i: (b, 2 * p + 1, 0, 0))]
        args += [ckd, cvd, cvd]
    kv_rows = n_items * tokens_per_batch
    in_specs += [pl.BlockSpec((kv_rows, LANES), lambda b, p, i: (b, p)),
                 pl.BlockSpec((1, kv_rows, LANES), lambda b, p, i: (2 * p, b, 0)),
                 pl.BlockSpec((1, kv_rows, LANES), lambda b, p, i: (2 * p + 1, b, 0))]
    args += [kd, vaugd, vaugd]
    return pl.pallas_call(
        functools.partial(_diff_kernel, has_cache=has_cache, tq=tq, tk=tk, nk=nk, n_items=n_items,
                          lam_init=lam_init),
        grid=(batch // n_items, DIFF_WIDTH // LANES, nq),
        in_specs=in_specs,
        out_specs=pl.BlockSpec((n_items * tq, LANES), lambda b, p, i: (b * nq + i, p)),
        out_shape=jax.ShapeDtypeStruct((t_total, DIFF_WIDTH), BF16),
        scratch_shapes=_attention_scratch(n_items, 4 * tq, tk),
        compiler_params=_params(("parallel", "parallel", "parallel")),
        name="diff_attention",
    )(*args)


def _hgrn_gates(qs, z, v, lb, g_sc, tri):
    sig = jax.nn.sigmoid(z)
    om = 1.0 - lb
    k = om * (1.0 - sig)
    g = jnp.log2(jnp.maximum(lb + om * sig, FORGET_MIN))
    g1 = g.astype(BF16)
    e1 = g - g1.astype(F32)
    g2 = e1.astype(BF16)
    g3 = (e1 - g2.astype(F32)).astype(BF16)
    G = (jnp.dot(tri, g1, preferred_element_type=F32) + jnp.dot(tri, g2, preferred_element_type=F32)
         + jnp.dot(tri, g3, preferred_element_type=F32))
    g_sc[...] = G
    return qs, k, G, v.astype(BF16)


def _hgrn_intra(gates, g_sc, lmask_ref, hmask, bdmask, rev):
    C, W = SCAN_CHUNK, HGRN_WIDTH
    qs, k, G, vb = gates
    g_last = g_sc[pl.ds(0 if rev else C - 1, 1), :]
    q_in = (qs * jnp.exp2(G)).astype(BF16)
    k_end = (k * jnp.exp2(g_last - G)).astype(BF16)
    ut = lax.dot_general(vb, k_end, _TN, preferred_element_type=F32) * bdmask

    def block_diag(a):
        return jnp.concatenate([a] * HGRN_HEADS, axis=0) * hmask

    def ref_rows(block, off):
        return jnp.concatenate([jnp.broadcast_to(g_sc[pl.ds(b0 + off, 1), :], (block, W))
                                for b0 in range(0, C, block)], axis=0)

    a = None
    for level in range(HGRN_LEVELS):
        block = 8 << level
        half = block // 2
        if level == 0:
            r = ref_rows(block, half)
            qe, ke = G - r, r - G
        else:
            r = ref_rows(block, half if rev else half - 1)
            qe, ke = jnp.minimum(G - r, 0.0), jnp.minimum(r - G, 0.0)
        al = lax.dot_general((qs * jnp.exp2(qe)).astype(BF16), block_diag((k * jnp.exp2(ke)).astype(BF16)),
                             _NT, preferred_element_type=F32)
        m = lmask_ref[rev, level]
        a = jnp.where(m > 0.5, al, 0.0) if level == 0 else a + al * m

    intra = jnp.dot(a.astype(BF16), block_diag(vb), preferred_element_type=F32)
    return q_in, jnp.exp2(g_last), ut, intra


def _hgrn_state(parts, st):
    q_in, decay, ut, intra = parts
    out = lax.dot_general(q_in, st.astype(BF16), _NT, preferred_element_type=F32) + intra
    return out, st * decay + ut


def _hgrn_kernel(*refs, layer, n_chunks, has_s0, state_out, state_alias):
    it = iter(refs)
    qf, zf, vf, qb, zb, vb, lbraw, tri_ref, lmask_ref, hmask_ref, bdmask_ref = [next(it) for _ in range(11)]
    s0 = next(it) if has_s0 else None
    if state_alias:
        next(it)
    of_ref, ob_ref = next(it), next(it)
    sto_ref = next(it) if state_out else None
    st_ref, g_sc = next(it), next(it)

    @pl.when(pl.program_id(1) == 0)
    def _():
        if has_s0:
            st_ref[...] = s0[0]
        else:
            st_ref[...] = jnp.zeros(st_ref.shape, F32)

    lbr = lbraw[...]
    ex = jnp.exp(lbr - jnp.max(lbr, axis=0, keepdims=True))
    sm = ex / jnp.sum(ex, axis=0, keepdims=True)
    lbs = jnp.zeros(sm.shape[1:], F32)
    for j in range(1, layer + 1):
        lbs = lbs + sm[j]
    lb_f, lb_b = lbs[0:1], lbs[1:2]

    hmask = hmask_ref[...]
    bdmask = bdmask_ref[...]

    chunks = []
    for cidx in range(n_chunks):
        chunks.append((0, pl.ds(cidx * SCAN_CHUNK, SCAN_CHUNK), qf, zf, vf, lb_f, of_ref))
        chunks.append((1, pl.ds((n_chunks - 1 - cidx) * SCAN_CHUNK, SCAN_CHUNK), qb, zb, vb, lb_b, ob_ref))
    state = [st_ref[0], st_ref[1]]
    gates, parts = {}, {}
    for t in range(len(chunks) + 2):
        if t < len(chunks):
            d, rows, q_r, z_r, v_r, lb, _ = chunks[t]
            gates[t] = _hgrn_gates(q_r[rows, :], z_r[rows, :], v_r[rows, :], lb, g_sc.at[t], tri_ref[d])
        if 0 <= t - 1 < len(chunks):
            parts[t - 1] = _hgrn_intra(gates.pop(t - 1), g_sc.at[t - 1], lmask_ref, hmask, bdmask, chunks[t - 1][0])
        if 0 <= t - 2 < len(chunks):
            d, rows, _, _, _, _, o_r = chunks[t - 2]
            out, state[d] = _hgrn_state(parts.pop(t - 2), state[d])
            o_r[rows, :] = out.astype(BF16)
    st_ref[0], st_ref[1] = state
    s_f, s_b = state

    if state_out:
        @pl.when(pl.program_id(1) == pl.num_programs(1) - 1)
        def _():
            if not state_alias:
                sto_ref[0, 1:] = jnp.zeros((sto_ref.shape[1] - 1,) + sto_ref.shape[2:], F32)
            for d, s in enumerate((s_f, s_b)):
                for hh in range(HGRN_HEADS):
                    slab = s[hh * HEAD_DIM:(hh + 1) * HEAD_DIM, (hh // 2) * LANES:(hh // 2 + 1) * LANES]
                    sto_ref[0, 0, d, hh] = pltpu.roll(slab, HEAD_DIM, 1) if hh % 2 else slab


def _hgrn_scan(zr, lb_raw, scan_consts, s0, state_prev, state_out, layer, batch, tokens_per_batch):
    t_total = zr.shape[0]
    tb = 256
    nblk = tokens_per_batch // tb
    W = HGRN_WIDTH
    has_s0 = s0 is not None
    state_alias = state_prev is not None
    fblk = lambda col: pl.BlockSpec((tb, W), lambda b, j: (b * nblk + j, col))
    bblk = lambda col: pl.BlockSpec((tb, W), lambda b, j: (b * nblk + nblk - 1 - j, col))
    full = lambda a: pl.BlockSpec(a.shape, lambda b, j: (0,) * a.ndim)
    in_specs = [fblk(0), fblk(1), fblk(3), bblk(0), bblk(2), bblk(3),
                pl.BlockSpec((DEPTH, 2, W), lambda b, j: (0, 0, 0))] + [full(a) for a in scan_consts]
    args = [zr] * 6 + [lb_raw] + list(scan_consts)
    if has_s0:
        in_specs.append(pl.BlockSpec((1, 2, W, W), lambda b, j: (b, 0, 0, 0)))
        args.append(s0)
    aliases = {}
    if state_alias:
        aliases = {len(args): 2}
        in_specs.append(pl.BlockSpec(memory_space=pl.ANY))
        args.append(state_prev)
    out_specs = [pl.BlockSpec((tb, W), lambda b, j: (b * nblk + j, 0)),
                 pl.BlockSpec((tb, W), lambda b, j: (b * nblk + nblk - 1 - j, 0))]
    out_shape = [jax.ShapeDtypeStruct((t_total, W), BF16), jax.ShapeDtypeStruct((t_total, W), BF16)]
    if state_out and not state_alias:
        assert layer == 0
        out_specs.append(pl.BlockSpec((1, DEPTH, 2, HGRN_HEADS, HEAD_DIM, LANES), lambda b, j: (b, 0, 0, 0, 0, 0)))
    elif state_out:
        out_specs.append(pl.BlockSpec((1, 1, 2, HGRN_HEADS, HEAD_DIM, LANES), lambda b, j: (b, layer, 0, 0, 0, 0)))
    if state_out:
        out_shape.append(jax.ShapeDtypeStruct((batch, DEPTH, 2, HGRN_HEADS, HEAD_DIM, LANES), F32))
    return pl.pallas_call(
        functools.partial(_hgrn_kernel, layer=layer, n_chunks=tb // SCAN_CHUNK, has_s0=has_s0,
                          state_out=state_out, state_alias=state_alias),
        grid=(batch, nblk),
        in_specs=in_specs, out_specs=out_specs, out_shape=out_shape,
        input_output_aliases=aliases,
        scratch_shapes=[pltpu.VMEM((2, W, W), F32), pltpu.VMEM((2 * tb // SCAN_CHUNK, SCAN_CHUNK, W), F32)],
        compiler_params=_params(("parallel", "arbitrary")),
        name="hgrn_scan",
    )(*args)


def _outproj_kernel(oa_ref, od_ref, of_ref, ob_ref, g_ref, x_ref, gate_ref, w_ref, sub_ref, hgn_ref,
                    bd_ref, lng_ref, lnb_ref, o_ref, *, lam_init):
    inv_d = 1.0 / HEAD_DIM
    g = g_ref[...].astype(F32)
    bd = bd_ref[...]

    def head_rms(t, gain):
        ssq = jnp.dot((t * t).astype(BF16), bd, preferred_element_type=F32)
        return t * lax.rsqrt(ssq * inv_d + RMS_EPS) * gain

    ma = (oa_ref[...].astype(F32) * g[:, 0:GQA_WIDTH]).astype(BF16)
    md = (head_rms(od_ref[...].astype(F32), sub_ref[...]) * (1.0 - lam_init)
          * g[:, GQA_WIDTH:GQA_WIDTH + DIFF_WIDTH]).astype(BF16)
    mr = (head_rms(of_ref[...].astype(F32) + ob_ref[...].astype(F32), hgn_ref[...])
          * g[:, GQA_WIDTH + DIFF_WIDTH:]).astype(BF16)
    y = (jnp.dot(ma, w_ref[0:GQA_WIDTH, :], preferred_element_type=F32)
         + jnp.dot(md, w_ref[GQA_WIDTH:GQA_WIDTH + DIFF_WIDTH, :], preferred_element_type=F32)
         + jnp.dot(mr, w_ref[GQA_WIDTH + DIFF_WIDTH:, :], preferred_element_type=F32))
    xn = DEEPNORM_ALPHA * x_ref[...] + gate_ref[0] * y
    mu = jnp.mean(xn, axis=-1, keepdims=True)
    xc = xn - mu
    var = jnp.mean(xc * xc, axis=-1, keepdims=True)
    o_ref[...] = xc * lax.rsqrt(var + LN_EPS) * lng_ref[...] + lnb_ref[...]


def _out_projection(out_a, out_d, o_f, o_b, gates, x, gate, w_out_b, subln, hgn, bd, ln_g, ln_b,
                    tokens_per_batch, lam_init):
    t_total = x.shape[0]
    tm = 256
    mod_rows = t_total // gate.shape[0] // tm
    row = lambda w: pl.BlockSpec((tm, w), lambda i: (i, 0))
    full = lambda shape: pl.BlockSpec(shape, lambda i: (0,) * len(shape))
    return pl.pallas_call(
        functools.partial(_outproj_kernel, lam_init=lam_init),
        grid=(t_total // tm,),
        in_specs=[row(GQA_WIDTH), row(DIFF_WIDTH), row(HGRN_WIDTH), row(HGRN_WIDTH), row(D_MODEL), row(D_MODEL),
                  pl.BlockSpec((1, 1, D_MODEL), lambda i: (i // mod_rows, 0, 0)),
                  full((D_MODEL, D_MODEL)), full((1, DIFF_WIDTH)), full((1, HGRN_WIDTH)),
                  full((DIFF_WIDTH, DIFF_WIDTH)), full((1, D_MODEL)), full((1, D_MODEL))],
        out_specs=row(D_MODEL),
        out_shape=jax.ShapeDtypeStruct((t_total, D_MODEL), F32),
        compiler_params=_params(("parallel",)),
        name="out_projection",
    )(out_a, out_d, o_f, o_b, gates, x, gate, w_out_b, subln, hgn, bd, ln_g, ln_b)


def _block_diag_ones(n):
    idx = np.arange(n) // HEAD_DIM
    return jnp.asarray((idx[:, None] == idx[None, :]).astype(np.float32), dtype=BF16)


def _scan_constants():
    C, W, H = SCAN_CHUNK, HGRN_WIDTH, HGRN_HEADS
    t = np.arange(C)[:, None]
    u = np.arange(C)[None, :]
    tri = np.stack([u <= t, u >= t])
    s = (np.arange(W) % C)[None, :]
    lmask = np.zeros((2, HGRN_LEVELS, C, W), np.float32)
    for level in range(HGRN_LEVELS):
        block = 8 << level
        same = (t // block) == (s // block)
        if level == 0:
            lmask[0, level] = same & (s <= t)
            lmask[1, level] = same & (s >= t)
        else:
            t_hi, s_hi = (t % block) >= block // 2, (s % block) >= block // 2
            lmask[0, level] = same & t_hi & ~s_hi
            lmask[1, level] = same & ~t_hi & s_hi
    rows = np.arange(H * C)[:, None]
    hmask = (rows // C) == (np.arange(W)[None, :] // HEAD_DIM)
    bdmask = (np.arange(W)[:, None] // HEAD_DIM) == (np.arange(W)[None, :] // HEAD_DIM)
    return (jnp.asarray(tri.astype(np.float32), dtype=BF16), jnp.asarray(lmask),
            jnp.asarray(hmask.astype(np.float32), dtype=BF16), jnp.asarray(bdmask.astype(np.float32)))


def _rope_tables(n_tokens, dim):
    rows = n_tokens // GRID_W
    row = jnp.repeat(jnp.arange(rows, dtype=F32), GRID_W)
    col = jnp.tile(jnp.arange(GRID_W, dtype=F32), rows)
    quarter = dim // 4
    inv_freq = ROPE_THETA ** (-jnp.arange(quarter, dtype=F32) / quarter)
    ar = row[:, None] * inv_freq[None, :]
    ac = col[:, None] * inv_freq[None, :]
    ang = jnp.concatenate([ar, ar, ac, ac], axis=-1)
    cos = jnp.tile(jnp.cos(ang), (1, LANES // dim))
    sin = jnp.tile(jnp.sin(ang), (1, LANES // dim))
    first = (jnp.arange(LANES) % (dim // 2)) < quarter
    return cos, jnp.where(first, -sin, 0.0), jnp.where(first, 0.0, sin)


def _mixer_layer(x, scale, shift, gate, layer, batch, tokens_per_batch, rope_tabs, cache, consts, weights,
                 ctx_out, ctx_prev):
    (w_in_b, w_out_b, qgain, kgain, lam_p, subln, hgn, lb_raw, ln_g, ln_b) = weights
    bdq, bdk, bdd, scan_consts = consts
    lam_init = 0.8 - 0.6 * math.exp(-0.3 * layer)
    outs = _in_projection(x, scale, shift, w_in_b, qgain, kgain, bdq, bdk, rope_tabs, tokens_per_batch, ctx_out,
                          layer, None if ctx_prev is None else ctx_prev[0])
    qa, kdup, vaug, qd, kd, vaugd, gates, zr = outs[:8]
    if cache is None:
        gqa_cache = diff_cache = s0 = None
        tq_a = tq_d = tk = tokens_per_batch
        items_a, items_d = 2, 2
    else:
        gqa_cache, diff_cache, s0 = cache
        tq_a, tq_d, tk = 512, 256, 512
        items_a = items_d = 1
    out_a = _gqa_attention(qa, kdup, vaug, gqa_cache, batch, tokens_per_batch, tq_a, tk, items_a)
    out_d = _diff_attention(qd, kd, vaugd, lam_p, diff_cache, batch, tokens_per_batch, tq_d, tk, items_d, lam_init)
    scan = _hgrn_scan(zr, lb_raw, scan_consts, s0, None if ctx_prev is None else ctx_prev[1], ctx_out, layer,
                      batch, tokens_per_batch)
    x_new = _out_projection(out_a, out_d, scan[0], scan[1], gates, x, gate, w_out_b, subln, hgn, bdd, ln_g, ln_b,
                            tokens_per_batch, lam_init)
    return x_new, ((outs[8:], scan[2]) if ctx_out else None)


def kernel(x_prompt, x_sample, cache_gqa_k, cache_gqa_v, cache_diff_k, cache_diff_v, state_hgrn, c, c_ctx,
           w_ada, b_ada, w_in, gqa_q_norm, gqa_k_norm, diff_lambda, diff_subln, hgrn_lower_bounds, hgrn_norm,
           w_out, ln_g, ln_b):
    batch, seq, _ = x_prompt.shape
    dec_batch, dec_seq, _ = x_sample.shape
    past = cache_gqa_k.shape[2]
    H, HD = HGRN_HEADS, HEAD_DIM

    cond = jnp.zeros((MOD_ROWS, D_MODEL), F32).at[0].set(c_ctx).at[1:1 + dec_batch].set(c)
    mod = _modulation(cond, w_ada, b_ada)
    consts = (_block_diag_ones(GQA_WIDTH), _block_diag_ones(LANES), _block_diag_ones(DIFF_WIDTH),
              _scan_constants())
    rope_tabs = _rope_tables(dec_seq, HEAD_DIM) + _rope_tables(dec_seq, DIFF_QK_DIM)
    w_in_b = w_in.astype(BF16)
    w_out_b = w_out.astype(BF16)

    def layer_weights(l):
        return (w_in_b[l], w_out_b[l],
                jnp.tile(gqa_q_norm[l], GQA_WIDTH // HD)[None, :] * (HD ** -0.5 * LOG2E),
                jnp.tile(gqa_k_norm[l], LANES // HD)[None, :],
                diff_lambda[l],
                jnp.tile(diff_subln[l], DIFF_WIDTH // HD)[None, :],
                jnp.tile(hgrn_norm[l], HGRN_WIDTH // HD)[None, :],
                hgrn_lower_bounds, ln_g[l][None, :], ln_b[l][None, :])

    def mod_rows(l, lo, n):
        m = mod[l, lo:lo + n]
        return (m[:, None, 0:D_MODEL], m[:, None, D_MODEL:2 * D_MODEL], m[:, None, 2 * D_MODEL:])

    x = x_prompt.reshape(batch * seq, D_MODEL)
    ctx = None
    for l in range(DEPTH):
        shift, scale, gate = mod_rows(l, 0, 1)
        x, ctx = _mixer_layer(x, scale, shift, gate, l, batch, seq, None, None, consts, layer_weights(l), True, ctx)
    y_prompt = x.reshape(batch, seq, D_MODEL)
    (kn, av, dk, dv), st = ctx
    new_leaves = [kn.reshape(batch, DEPTH, seq, GQA_KV_HEADS, HD), av.reshape(batch, DEPTH, seq, GQA_KV_HEADS, HD),
                  dk.reshape(batch, DEPTH, seq, DIFF_HEADS, HD), dv.reshape(batch, DEPTH, seq, DIFF_HEADS, HD),
                  jnp.swapaxes(st[..., :HD], -1, -2)]

    ones = jnp.ones((dec_batch, past, HD), F32)
    eye = jnp.eye(H, dtype=F32)
    x = x_sample.reshape(dec_batch * dec_seq, D_MODEL)
    for l in range(DEPTH):
        shift, scale, gate = mod_rows(l, 1, dec_batch)
        ck, cv = cache_gqa_k[:, l], cache_gqa_v[:, l]
        ckd = jnp.stack([jnp.concatenate([ck[:, :, g], ck[:, :, g]], axis=-1) for g in range(GQA_KV_HEADS)], axis=1)
        cvd = jnp.stack([jnp.concatenate([cv[:, :, g], ones], axis=-1) for g in range(GQA_KV_HEADS)], axis=1)
        dkc = cache_diff_k[:, l].reshape(dec_batch, past, DIFF_WIDTH)
        dvc = cache_diff_v[:, l]
        dvd = jnp.stack([jnp.concatenate([dvc[:, :, h], ones] if h % 2 == 0 else [ones, dvc[:, :, h]], axis=-1)
                         for h in range(DIFF_HEADS)], axis=1)
        s0t = jnp.swapaxes(state_hgrn[:, l].astype(F32), -1, -2)
        s0 = (s0t[:, :, :, :, None, :] * eye[None, None, :, None, :, None]).reshape(dec_batch, 2, H * HD, H * HD)
        cache = ((ckd.astype(BF16), cvd.astype(BF16)), (dkc.astype(BF16), dvd.astype(BF16)), s0)
        x, _ = _mixer_layer(x, scale, shift, gate, l, dec_batch, dec_seq, rope_tabs, cache, consts,
                            layer_weights(l), False, None)
    y_sample = x.reshape(dec_batch, dec_seq, D_MODEL)
    return (y_prompt, y_sample) + tuple(new_leaves)
```

```python
import functools
import math

import numpy as np
import jax
import jax.numpy as jnp
from jax import lax
from jax.experimental import pallas as pl
from jax.experimental.pallas import tpu as pltpu

F32 = jnp.float32
BF16 = jnp.bfloat16

D_MODEL = 1024
DEPTH = 2
GRID_W = 64
HEAD_DIM = 64
GQA_WIDTH = 512
GQA_KV_HEADS = 2
DIFF_WIDTH = 256
DIFF_HEADS = 4
DIFF_QK_DIM = 32
HGRN_WIDTH = 256
HGRN_HEADS = 4
IN_WIDTH = 3584
SCAN_CHUNK = 64
HGRN_LEVELS = 4
ROPE_THETA = 10000.0
RMS_EPS = 1e-6
LN_EPS = 1e-5
FORGET_MIN = 1e-6
DEEPNORM_ALPHA = (2 * DEPTH) ** 0.25
LANES = 128
SUBLANES = 8
MOD_ROWS = SUBLANES
VMEM_LIMIT = 48 * 1024 * 1024
SM_ROWS = 32
PROJ_ROWS = 512
SCAN_BLOCK = 512
LOG2E = math.log2(math.e)

_A_Q, _A_K, _A_V, _A_G = 0, 512, 640, 768
_D_Q, _D_K, _D_V, _D_G = 1280, 1536, 1792, 2048
_R_Z, _R_G = 2304, 3328

_NT = (((1,), (1,)), ((), ()))
_TN = (((0,), (0,)), ((), ()))


def _silu(x):
    return x * jax.nn.sigmoid(x)


def _params(sem):
    return pltpu.CompilerParams(dimension_semantics=sem, vmem_limit_bytes=VMEM_LIMIT)


def _mod_kernel(c_ref, w_ref, b_ref, o_ref):
    c = c_ref[...]
    o_ref[0] = jnp.dot(_silu(c), w_ref[0], preferred_element_type=F32) + b_ref[0]


def _modulation(cond, w_ada, b_ada):
    tn = 1024
    return pl.pallas_call(
        _mod_kernel,
        grid=(DEPTH, 3 * D_MODEL // tn),
        in_specs=[pl.BlockSpec((MOD_ROWS, D_MODEL), lambda l, j: (0, 0)),
                  pl.BlockSpec((1, D_MODEL, tn), lambda l, j: (l, 0, j)),
                  pl.BlockSpec((1, 1, tn), lambda l, j: (l, 0, j))],
        out_specs=pl.BlockSpec((1, MOD_ROWS, tn), lambda l, j: (l, 0, j)),
        out_shape=jax.ShapeDtypeStruct((DEPTH, MOD_ROWS, 3 * D_MODEL), F32),
        compiler_params=_params(("arbitrary", "arbitrary")),
        name="modulation",
    )(cond, w_ada, b_ada.reshape(DEPTH, 1, 3 * D_MODEL))


def _rope(x, c, s1, s2, shift):
    return x * c + pltpu.roll(x, LANES - shift, 1) * s1 + pltpu.roll(x, shift, 1) * s2


def _inproj_kernel(*refs, rope, ctx_out, ctx_alias):
    it = iter(refs)
    x_ref, sc_ref, sh_ref, w_ref, qg_ref, kg_ref, bdq_ref, bdk_ref = [next(it) for _ in range(8)]
    if rope:
        ca, s1a, s2a, cd, s1d, s2d = [next(it)[...] for _ in range(6)]
    if ctx_alias:
        for _ in range(4):
            next(it)
    qa_o, kdup_o, vaug_o, qd_o, kd_o, vaugd_o, gates_o, zr_o = [next(it) for _ in range(8)]
    if ctx_out:
        ctx_refs = [next(it) for _ in range(4)]
        if not ctx_alias:
            for r in ctx_refs:
                r[:, 1:] = jnp.zeros((r.shape[0], r.shape[1] - 1) + r.shape[2:], F32)

        def put_ctx(n, val):
            r = ctx_refs[n]
            r[:, 0] = val.reshape(r.shape[0], r.shape[2], r.shape[3])

    h = (x_ref[...] * (1.0 + sc_ref[0]) + sh_ref[0]).astype(BF16)

    def proj(lo, hi):
        return jnp.dot(h, w_ref[:, lo:hi], preferred_element_type=F32)

    tm = h.shape[0]
    low = lax.broadcasted_iota(jnp.int32, (tm, LANES), 1) < HEAD_DIM
    inv_d = 1.0 / HEAD_DIM

    aq = proj(_A_Q, _A_K)
    ssq = jnp.dot((aq * aq).astype(BF16), bdq_ref[...], preferred_element_type=F32)
    qn = aq * lax.rsqrt(ssq * inv_d + RMS_EPS) * qg_ref[...]
    for j in range(GQA_WIDTH // LANES):
        slab = qn[:, j * LANES:(j + 1) * LANES]
        if rope:
            slab = _rope(slab, ca, s1a, s2a, HEAD_DIM // 4)
        qa_o[:, j * LANES:(j + 1) * LANES] = slab.astype(BF16)

    ak = proj(_A_K, _A_V)
    ssqk = jnp.dot((ak * ak).astype(BF16), bdk_ref[...], preferred_element_type=F32)
    kn = ak * lax.rsqrt(ssqk * inv_d + RMS_EPS) * kg_ref[...]
    if ctx_out:
        put_ctx(0, kn)
    if rope:
        kn = _rope(kn, ca, s1a, s2a, HEAD_DIM // 4)
    kr = pltpu.roll(kn, HEAD_DIM, 1)
    kdup_o[0] = jnp.where(low, kn, kr).astype(BF16)
    kdup_o[1] = jnp.where(low, kr, kn).astype(BF16)

    av = proj(_A_V, _A_G)
    if ctx_out:
        put_ctx(1, av)
    vr = pltpu.roll(av, HEAD_DIM, 1)
    vaug_o[0] = jnp.where(low, av, 1.0).astype(BF16)
    vaug_o[1] = jnp.where(low, vr, 1.0).astype(BF16)

    gates_o[:, 0:GQA_WIDTH] = _silu(proj(_A_G, _D_Q)).astype(BF16)

    dq = proj(_D_Q, _D_K) * (DIFF_QK_DIM ** -0.5 * LOG2E)
    dk = proj(_D_K, _D_V)
    if ctx_out:
        put_ctx(2, dk)
    for j in range(DIFF_WIDTH // LANES):
        sq = dq[:, j * LANES:(j + 1) * LANES]
        sk = dk[:, j * LANES:(j + 1) * LANES]
        if rope:
            sq = _rope(sq, cd, s1d, s2d, DIFF_QK_DIM // 4)
            sk = _rope(sk, cd, s1d, s2d, DIFF_QK_DIM // 4)
        qd_o[:, j * LANES:(j + 1) * LANES] = sq.astype(BF16)
        kd_o[:, j * LANES:(j + 1) * LANES] = sk.astype(BF16)
    dv = proj(_D_V, _D_G)
    if ctx_out:
        put_ctx(3, dv)
    for j in range(DIFF_WIDTH // LANES):
        sv = dv[:, j * LANES:(j + 1) * LANES]
        vaugd_o[2 * j] = jnp.where(low, sv, 1.0).astype(BF16)
        vaugd_o[2 * j + 1] = jnp.where(low, 1.0, sv).astype(BF16)
    gates_o[:, GQA_WIDTH:GQA_WIDTH + DIFF_WIDTH] = _silu(proj(_D_G, _R_Z)).astype(BF16)

    zr = proj(_R_Z, _R_G)
    zr_o[:, 0:HGRN_WIDTH] = _silu(zr[:, 0:HGRN_WIDTH])
    zr_o[:, HGRN_WIDTH:] = zr[:, HGRN_WIDTH:]
    gates_o[:, GQA_WIDTH + DIFF_WIDTH:] = _silu(proj(_R_G, IN_WIDTH)).astype(BF16)


def _in_projection(x, scale, shift, w_in_b, qgain, kgain, bdq, bdk, rope_tabs, tokens_per_batch, ctx_out,
                   layer, ctx_prev):
    t_total = x.shape[0]
    tm = PROJ_ROWS
    nb_rows = tokens_per_batch // tm
    mod_rows = t_total // scale.shape[0] // tm
    rope = rope_tabs is not None
    assert not rope or nb_rows >= 1
    full = lambda shape: pl.BlockSpec(shape, lambda i: (0,) * len(shape))
    in_specs = [pl.BlockSpec((tm, D_MODEL), lambda i: (i, 0)),
                pl.BlockSpec((1, 1, D_MODEL), lambda i: (i // mod_rows, 0, 0)),
                pl.BlockSpec((1, 1, D_MODEL), lambda i: (i // mod_rows, 0, 0)),
                full((D_MODEL, IN_WIDTH)),
                full((1, GQA_WIDTH)), full((1, LANES)),
                full((GQA_WIDTH, GQA_WIDTH)), full((LANES, LANES))]
    args = [x, scale, shift, w_in_b, qgain, kgain, bdq, bdk]
    if rope:
        in_specs += [pl.BlockSpec((tm, LANES), lambda i: (i % nb_rows, 0))] * 6
        args += list(rope_tabs)
    row = lambda w: pl.BlockSpec((tm, w), lambda i: (i, 0))
    stk = lambda n: pl.BlockSpec((n, tm, LANES), lambda i: (0, i, 0))
    out_specs = [row(GQA_WIDTH), stk(2), stk(2), row(DIFF_WIDTH), row(DIFF_WIDTH), stk(4),
                 row(D_MODEL), row(4 * HGRN_WIDTH)]
    out_shape = [jax.ShapeDtypeStruct((t_total, GQA_WIDTH), BF16),
                 jax.ShapeDtypeStruct((2, t_total, LANES), BF16),
                 jax.ShapeDtypeStruct((2, t_total, LANES), BF16),
                 jax.ShapeDtypeStruct((t_total, DIFF_WIDTH), BF16),
                 jax.ShapeDtypeStruct((t_total, DIFF_WIDTH), BF16),
                 jax.ShapeDtypeStruct((4, t_total, LANES), BF16),
                 jax.ShapeDtypeStruct((t_total, D_MODEL), BF16),
                 jax.ShapeDtypeStruct((t_total, 4 * HGRN_WIDTH), F32)]
    aliases = {}
    if ctx_out:
        seq = tokens_per_batch
        nseq = tm // seq
        assert nseq * seq == tm
        widths = (LANES, LANES, DIFF_WIDTH, DIFF_WIDTH)
        if ctx_prev is not None:
            aliases = {len(args) + n: len(out_specs) + n for n in range(4)}
            in_specs += [pl.BlockSpec(memory_space=pl.ANY)] * 4
            args += list(ctx_prev)
        if ctx_prev is None:
            assert layer == 0
            out_specs += [pl.BlockSpec((nseq, DEPTH, seq, w), lambda i: (i, 0, 0, 0)) for w in widths]
        else:
            out_specs += [pl.BlockSpec((nseq, 1, seq, w), lambda i: (i, layer, 0, 0)) for w in widths]
        out_shape += [jax.ShapeDtypeStruct((t_total // seq, DEPTH, seq, w), F32) for w in widths]
    return pl.pallas_call(
        functools.partial(_inproj_kernel, rope=rope, ctx_out=ctx_out, ctx_alias=bool(aliases)),
        grid=(t_total // tm,),
        in_specs=in_specs, out_specs=out_specs, out_shape=out_shape,
        input_output_aliases=aliases,
        compiler_params=_params(("parallel",)),
        name="in_projection",
    )(*args)


def _chunk_rows(j, tk):
    return pl.ds(j * tk, tk) if isinstance(j, int) else pl.ds(pl.multiple_of(j * tk, tk), tk)


def _attention_scratch(n_items, rows, tk):
    rows = n_items * rows
    return ([pltpu.VMEM((rows, LANES), BF16)]
            + [pltpu.VMEM((rows, tk), F32)] * 2 + [pltpu.VMEM((rows, tk), BF16)] * 2
            + [pltpu.VMEM((rows, LANES), F32)] * 4)


def _item_scratch(refs, item, rows):
    return [r.at[pl.ds(item * rows, rows)] for r in refs]


def _softmax_stage(s_sc, p_sc, a_sc, m_sc, keys):
    for r in range(s_sc.shape[0] // SM_ROWS):
        rows = pl.ds(r * SM_ROWS, SM_ROWS)
        s = s_sc[rows, 0:keys]
        m_prev = m_sc[rows, :]
        m_new = jnp.maximum(m_prev, jnp.max(s, axis=1, keepdims=True))
        m_sc[rows, :] = m_new
        p_sc[rows, 0:keys] = jnp.exp2(s - jnp.concatenate([m_new] * (keys // LANES), axis=1)).astype(BF16)
        a_sc[rows, :] = jnp.exp2(m_prev - m_new)


def _item_timeline(n_new, cache, k_at, v_at, start, qk, sm, av, finish):
    o = 0 if cache is None else 1
    n = n_new + o
    k_of = lambda c: cache[0]() if (o and c == 0) else k_at(c - o)
    v_of = lambda c: cache[1]() if (o and c == 0) else v_at(c - o)
    steps = []
    for t in range(n + 2):
        step = [start] if t == 0 else []
        if t < n:
            step.append(lambda t=t: qk(t % 2, k_of(t)))
        if 0 <= t - 1 < n:
            step.append(lambda t=t: sm((t - 1) % 2))
        if 0 <= t - 2 < n:
            step.append(lambda t=t: av(t % 2, v_of(t - 2)))
        if t == n + 1:
            step.append(finish)
        steps.append(step)
    return steps


def _run_items(timelines):
    lag = max(1, len(timelines[0]) - 2)
    for g in range(lag * (len(timelines) - 1) + len(timelines[0])):
        for i, steps in enumerate(timelines):
            if 0 <= g - i * lag < len(steps):
                for stage in steps[g - i * lag]:
                    stage()


def _gqa_kernel(*refs, has_cache, tq, tk, nk, n_items, shared_kv):
    if has_cache:
        q_ref, kc_ref, vc_ref = refs[:3]
        refs = refs[3:]
    else:
        q_ref = refs[0]
        refs = refs[1:]
    k_ref, v_ref, o_ref = refs[:3]
    low = lax.broadcasted_iota(jnp.int32, (tq, LANES), 1) < HEAD_DIM
    rows = 2 * tq

    def item_stages(item):
        qs_sc, s0, s1, p0, p1, a0, a1, m_sc, acc_sc = _item_scratch(refs[3:], item, rows)
        s_b, p_b, a_b = (s0, s1), (p0, p1), (a0, a1)

        def start():
            q = q_ref[pl.ds(item * tq, tq), :]
            zero = jnp.zeros_like(q)
            qs_sc[0:tq, :] = jnp.where(low, q, zero)
            qs_sc[tq:, :] = jnp.where(low, zero, q)
            m_sc[...] = jnp.full(m_sc.shape, -jnp.inf, F32)
            acc_sc[...] = jnp.zeros(acc_sc.shape, F32)

        keys = {}

        def qk(slot, k):
            keys[slot] = k.shape[0]
            s_b[slot][:, 0:k.shape[0]] = lax.dot_general(qs_sc[...], k, _NT, preferred_element_type=F32)

        def sm(slot):
            _softmax_stage(s_b[slot], p_b[slot], a_b[slot], m_sc, keys[slot])

        def av(slot, v):
            acc_sc[...] = (a_b[slot][...] * acc_sc[...]
                           + jnp.dot(p_b[slot][:, 0:v.shape[0]], v, preferred_element_type=F32))

        def finish():
            acc = acc_sc[...]
            on = acc / pltpu.roll(acc, HEAD_DIM, 1)
            o_ref[pl.ds(item * tq, tq), :] = jnp.where(low, on[:tq], pltpu.roll(on[tq:], HEAD_DIM, 1)).astype(BF16)

        base = 0 if shared_kv else item * nk
        chunk = lambda ref: (lambda j: ref[0, _chunk_rows(base + j, tk), :])
        cache = (lambda: kc_ref[0, 0], lambda: vc_ref[0, 0]) if has_cache else None
        return _item_timeline(nk, cache, chunk(k_ref), chunk(v_ref), start, qk, sm, av, finish)

    _run_items([item_stages(item) for item in range(n_items)])


def _item_geometry(batch, tokens_per_batch, tq, n_items):
    nq = tokens_per_batch // tq
    if nq == 1:
        return batch // n_items, 1, n_items * tokens_per_batch, False
    assert nq % n_items == 0
    return batch, nq // n_items, tokens_per_batch, True


def _gqa_attention(qa, kdup, vaug, cache, batch, tokens_per_batch, tq, tk, n_items):
    t_total = qa.shape[0]
    nk = tokens_per_batch // tk
    has_cache = cache is not None
    bsteps, nq, kv_rows, shared_kv = _item_geometry(batch, tokens_per_batch, tq, n_items)
    assert shared_kv or not has_cache
    in_specs = [pl.BlockSpec((n_items * tq, LANES), lambda b, p, i: (b * nq + i, p))]
    args = [qa]
    if has_cache:
        past = cache[0].shape[2]
        in_specs += [pl.BlockSpec((1, 1, past, LANES), lambda b, p, i: (b, p // 2, 0, 0))] * 2
        args += list(cache)
    in_specs += [pl.BlockSpec((1, kv_rows, LANES), lambda b, p, i: (p // 2, b, 0))] * 2
    args += [kdup, vaug]
    return pl.pallas_call(
        functools.partial(_gqa_kernel, has_cache=has_cache, tq=tq, tk=tk, nk=nk, n_items=n_items,
                          shared_kv=shared_kv),
        grid=(bsteps, GQA_WIDTH // LANES, nq),
        in_specs=in_specs,
        out_specs=pl.BlockSpec((n_items * tq, LANES), lambda b, p, i: (b * nq + i, p)),
        out_shape=jax.ShapeDtypeStruct((t_total, GQA_WIDTH), BF16),
        scratch_shapes=_attention_scratch(n_items, 2 * tq, tk),
        compiler_params=_params(("parallel", "parallel", "parallel")),
        name="gqa_attention",
    )(*args)


def _diff_kernel(*refs, has_cache, tq, tk, nk, n_items, shared_kv, lam_init):
    q_ref, lp_ref = refs[:2]
    if has_cache:
        kc_ref, vc0_ref, vc1_ref = refs[2:5]
        refs = refs[5:]
    else:
        refs = refs[2:]
    k_ref, v0_ref, v1_ref, o_ref = refs[:4]
    lane = lax.broadcasted_iota(jnp.int32, (tq, LANES), 1)
    lp = lp_ref[...]
    lam = (jnp.exp(jnp.sum(lp[0:1] * lp[1:2], axis=1, keepdims=True))
           - jnp.exp(jnp.sum(lp[2:3] * lp[3:4], axis=1, keepdims=True)) + lam_init)
    h2 = 2 * tq
    rows = 4 * tq

    def item_stages(item):
        qs_sc, s0, s1, p0, p1, a0, a1, m_sc, acc_sc = _item_scratch(refs[4:], item, rows)
        s_b, p_b, a_b = (s0, s1), (p0, p1), (a0, a1)

        def start():
            q = q_ref[pl.ds(item * tq, tq), :]
            zero = jnp.zeros_like(q)
            for j in range(4):
                qs_sc[j * tq:(j + 1) * tq, :] = jnp.where((lane >> 5) == j, q, zero)
            m_sc[...] = jnp.full(m_sc.shape, -jnp.inf, F32)
            acc_sc[...] = jnp.zeros(acc_sc.shape, F32)

        keys = {}

        def qk(slot, k):
            keys[slot] = k.shape[0]
            s_b[slot][:, 0:k.shape[0]] = lax.dot_general(qs_sc[...], k, _NT, preferred_element_type=F32)

        def sm(slot):
            _softmax_stage(s_b[slot], p_b[slot], a_b[slot], m_sc, keys[slot])

        def av(slot, v):
            for half, vh in zip((pl.ds(0, h2), pl.ds(h2, h2)), v):
                acc_sc[half, :] = (a_b[slot][half, :] * acc_sc[half, :]
                                   + jnp.dot(p_b[slot][half, 0:vh.shape[0]], vh, preferred_element_type=F32))

        def finish():
            acc = acc_sc[...]
            on = acc / pltpu.roll(acc, HEAD_DIM, 1)
            o0 = on[0:tq] - lam * on[tq:h2]
            o1 = on[h2:h2 + tq] - lam * on[h2 + tq:]
            o_ref[pl.ds(item * tq, tq), :] = jnp.where(lane < HEAD_DIM, o0, o1).astype(BF16)

        base = 0 if shared_kv else item * nk
        k_at = lambda j: k_ref[_chunk_rows(base + j, tk), :]
        v_at = lambda j: (v0_ref[0, _chunk_rows(base + j, tk), :], v1_ref[0, _chunk_rows(base + j, tk), :])
        cache = (lambda: kc_ref[0], lambda: (vc0_ref[0, 0], vc1_ref[0, 0])) if has_cache else None
        return _item_timeline(nk, cache, k_at, v_at, start, qk, sm, av, finish)

    _run_items([item_stages(item) for item in range(n_items)])


def _diff_attention(qd, kd, vaugd, lam_p, cache, batch, tokens_per_batch, tq, tk, n_items, lam_init):
    t_total = qd.shape[0]
    nk = tokens_per_batch // tk
    has_cache = cache is not None
    bsteps, nq, kv_rows, shared_kv = _item_geometry(batch, tokens_per_batch, tq, n_items)
    assert shared_kv or not has_cache
    in_specs = [pl.BlockSpec((n_items * tq, LANES), lambda b, p, i: (b * nq + i, p)),
                pl.BlockSpec((4, DIFF_QK_DIM), lambda b, p, i: (0, 0))]
    args = [qd, lam_p]
    if has_cache:
        ckd, cvd = cache
        past = ckd.shape[1]
        in_specs += [pl.BlockSpec((1, past, LANES), lambda b, p, i: (b, 0, p)),
                     pl.BlockSpec((1, 1, past, LANES), lambda b, p, i: (b, 2 * p, 0, 0)),
                     pl.BlockSpec((1, 1, past, LANES), lambda b, p, i: (b, 2 * p + 1, 0, 0))]
        args += [ckd, cvd, cvd]
    in_specs += [pl.BlockSpec((kv_rows, LANES), lambda b, p, i: (b, p)),
                 pl.BlockSpec((1, kv_rows, LANES), lambda b, p, i: (2 * p, b, 0)),
                 pl.BlockSpec((1, kv_rows, LANES), lambda b, p, i: (2 * p + 1, b, 0))]
    args += [kd, vaugd, vaugd]
    return pl.pallas_call(
        functools.partial(_diff_kernel, has_cache=has_cache, tq=tq, tk=tk, nk=nk, n_items=n_items,
                          shared_kv=shared_kv, lam_init=lam_init),
        grid=(bsteps, DIFF_WIDTH // LANES, nq),
        in_specs=in_specs,
        out_specs=pl.BlockSpec((n_items * tq, LANES), lambda b, p, i: (b * nq + i, p)),
        out_shape=jax.ShapeDtypeStruct((t_total, DIFF_WIDTH), BF16),
        scratch_shapes=_attention_scratch(n_items, 4 * tq, tk),
        compiler_params=_params(("parallel", "parallel", "parallel")),
        name="diff_attention",
    )(*args)


def _hgrn_gates(qs, z, v, lb, g_sc, tri):
    sig = jax.nn.sigmoid(z)
    om = 1.0 - lb
    k = om * (1.0 - sig)
    g = jnp.log2(jnp.maximum(lb + om * sig, FORGET_MIN))
    g1 = g.astype(BF16)
    e1 = g - g1.astype(F32)
    g2 = e1.astype(BF16)
    g3 = (e1 - g2.astype(F32)).astype(BF16)
    G = (jnp.dot(tri, g1, preferred_element_type=F32) + jnp.dot(tri, g2, preferred_element_type=F32)
         + jnp.dot(tri, g3, preferred_element_type=F32))
    g_sc[...] = G
    return qs, k, G, v.astype(BF16)


def _hgrn_operands(gates, g_sc, hmask, rev):
    C, W = SCAN_CHUNK, HGRN_WIDTH
    qs, k, G, vb = gates
    g_last = g_sc[pl.ds(0 if rev else C - 1, 1), :]
    q_in = (qs * jnp.exp2(G)).astype(BF16)
    k_end = (k * jnp.exp2(g_last - G)).astype(BF16)

    def block_diag(a):
        return jnp.concatenate([a] * HGRN_HEADS, axis=0) * hmask

    def ref_rows(block, off):
        return jnp.concatenate([jnp.broadcast_to(g_sc[pl.ds(b0 + off, 1), :], (block, W))
                                for b0 in range(0, C, block)], axis=0)

    levels = []
    for level in range(HGRN_LEVELS):
        block = 8 << level
        half = block // 2
        if level == 0:
            r = ref_rows(block, half)
            qe, ke = G - r, r - G
        else:
            r = ref_rows(block, half if rev else half - 1)
            qe, ke = jnp.minimum(G - r, 0.0), jnp.minimum(r - G, 0.0)
        levels.append(((qs * jnp.exp2(qe)).astype(BF16), block_diag((k * jnp.exp2(ke)).astype(BF16))))
    return q_in, jnp.exp2(g_last), k_end, vb, block_diag(vb), levels


def _hgrn_intra(operands, lmask_ref, bdmask, rev):
    q_in, decay, k_end, vb, v_bd, levels = operands
    ut = lax.dot_general(vb, k_end, _TN, preferred_element_type=F32) * bdmask
    a = None
    for level, (ql, kl) in enumerate(levels):
        al = lax.dot_general(ql, kl, _NT, preferred_element_type=F32)
        m = lmask_ref[rev, level]
        a = jnp.where(m > 0.5, al, 0.0) if level == 0 else a + al * m
    intra = jnp.dot(a.astype(BF16), v_bd, preferred_element_type=F32)
    return q_in, decay, ut, intra


def _hgrn_state(parts, st):
    q_in, decay, ut, intra = parts
    out = lax.dot_general(q_in, st.astype(BF16), _NT, preferred_element_type=F32) + intra
    return out, st * decay + ut


def _hgrn_kernel(*refs, layer, n_chunks, n_seq, has_s0, state_out, state_alias):
    it = iter(refs)
    qf, zf, vf, qb, zb, vb, lbraw, tri_ref, lmask_ref, hmask_ref, bdmask_ref = [next(it) for _ in range(11)]
    s0 = next(it) if has_s0 else None
    if state_alias:
        next(it)
    of_ref, ob_ref = next(it), next(it)
    sto_ref = next(it) if state_out else None
    st_ref, g_sc = next(it), next(it)

    @pl.when(pl.program_id(1) == 0)
    def _():
        if has_s0:
            st_ref[...] = s0[0]
        else:
            st_ref[...] = jnp.zeros(st_ref.shape, F32)

    lbr = lbraw[...]
    ex = jnp.exp(lbr - jnp.max(lbr, axis=0, keepdims=True))
    sm = ex / jnp.sum(ex, axis=0, keepdims=True)
    lbs = jnp.zeros(sm.shape[1:], F32)
    for j in range(1, layer + 1):
        lbs = lbs + sm[j]
    lb_f, lb_b = lbs[0:1], lbs[1:2]

    hmask = hmask_ref[...]
    bdmask = bdmask_ref[...]

    chunks = []
    for cidx in range(n_chunks):
        for s in range(n_seq):
            base = s * n_chunks
            chunks.append((2 * s, 0, pl.ds((base + cidx) * SCAN_CHUNK, SCAN_CHUNK), qf, zf, vf, lb_f, of_ref))
            chunks.append((2 * s + 1, 1, pl.ds((base + n_chunks - 1 - cidx) * SCAN_CHUNK, SCAN_CHUNK),
                           qb, zb, vb, lb_b, ob_ref))
    n = len(chunks)
    state = [st_ref[i] for i in range(2 * n_seq)]
    gates, operands, parts = {}, {}, {}
    for t in range(n + 3):
        if t < n:
            _, d, rows, q_r, z_r, v_r, lb, _ = chunks[t]
            gates[t] = _hgrn_gates(q_r[rows, :], z_r[rows, :], v_r[rows, :], lb, g_sc.at[t], tri_ref[d])
        if 0 <= t - 1 < n:
            operands[t - 1] = _hgrn_operands(gates.pop(t - 1), g_sc.at[t - 1], hmask, chunks[t - 1][1])
        if 0 <= t - 2 < n:
            parts[t - 2] = _hgrn_intra(operands.pop(t - 2), lmask_ref, bdmask, chunks[t - 2][1])
        if 0 <= t - 3 < n:
            si, _, rows, _, _, _, _, o_r = chunks[t - 3]
            out, state[si] = _hgrn_state(parts.pop(t - 3), state[si])
            o_r[rows, :] = out.astype(BF16)
    for i, s in enumerate(state):
        st_ref[i] = s

    if state_out:
        @pl.when(pl.program_id(1) == pl.num_programs(1) - 1)
        def _():
            if not state_alias:
                sto_ref[:, 1:] = jnp.zeros((sto_ref.shape[0], sto_ref.shape[1] - 1) + sto_ref.shape[2:], F32)
            for i, s in enumerate(state):
                for hh in range(HGRN_HEADS):
                    slab = s[hh * HEAD_DIM:(hh + 1) * HEAD_DIM, (hh // 2) * LANES:(hh // 2 + 1) * LANES]
                    sto_ref[i // 2, 0, i % 2, hh] = pltpu.roll(slab, HEAD_DIM, 1) if hh % 2 else slab


def _hgrn_scan(zr, lb_raw, scan_consts, s0, state_prev, state_out, layer, batch, tokens_per_batch):
    t_total = zr.shape[0]
    tb = min(SCAN_BLOCK, tokens_per_batch)
    nblk = tokens_per_batch // tb
    n_seq = SCAN_BLOCK // tb if nblk == 1 else 1
    W = HGRN_WIDTH
    has_s0 = s0 is not None
    assert n_seq == 1 or not has_s0
    state_alias = state_prev is not None
    rows = n_seq * tb
    fblk = lambda col: pl.BlockSpec((rows, W), lambda b, j: (b * nblk + j, col))
    bblk = lambda col: pl.BlockSpec((rows, W), lambda b, j: (b * nblk + nblk - 1 - j, col))
    full = lambda a: pl.BlockSpec(a.shape, lambda b, j: (0,) * a.ndim)
    in_specs = [fblk(0), fblk(1), fblk(3), bblk(0), bblk(2), bblk(3),
                pl.BlockSpec((DEPTH, 2, W), lambda b, j: (0, 0, 0))] + [full(a) for a in scan_consts]
    args = [zr] * 6 + [lb_raw] + list(scan_consts)
    if has_s0:
        in_specs.append(pl.BlockSpec((1, 2, W, W), lambda b, j: (b, 0, 0, 0)))
        args.append(s0)
    aliases = {}
    if state_alias:
        aliases = {len(args): 2}
        in_specs.append(pl.BlockSpec(memory_space=pl.ANY))
        args.append(state_prev)
    out_specs = [pl.BlockSpec((rows, W), lambda b, j: (b * nblk + j, 0)),
                 pl.BlockSpec((rows, W), lambda b, j: (b * nblk + nblk - 1 - j, 0))]
    out_shape = [jax.ShapeDtypeStruct((t_total, W), BF16), jax.ShapeDtypeStruct((t_total, W), BF16)]
    if state_out and not state_alias:
        assert layer == 0
        out_specs.append(pl.BlockSpec((n_seq, DEPTH, 2, HGRN_HEADS, HEAD_DIM, LANES),
                                      lambda b, j: (b, 0, 0, 0, 0, 0)))
    elif state_out:
        out_specs.append(pl.BlockSpec((n_seq, 1, 2, HGRN_HEADS, HEAD_DIM, LANES),
                                      lambda b, j: (b, layer, 0, 0, 0, 0)))
    if state_out:
        out_shape.append(jax.ShapeDtypeStruct((batch, DEPTH, 2, HGRN_HEADS, HEAD_DIM, LANES), F32))
    n_units = 2 * rows // SCAN_CHUNK
    return pl.pallas_call(
        functools.partial(_hgrn_kernel, layer=layer, n_chunks=tb // SCAN_CHUNK, n_seq=n_seq, has_s0=has_s0,
                          state_out=state_out, state_alias=state_alias),
        grid=(batch // n_seq, nblk),
        in_specs=in_specs, out_specs=out_specs, out_shape=out_shape,
        input_output_aliases=aliases,
        scratch_shapes=[pltpu.VMEM((2 * n_seq, W, W), F32), pltpu.VMEM((n_units, SCAN_CHUNK, W), F32)],
        compiler_params=_params(("parallel", "arbitrary")),
        name="hgrn_scan",
    )(*args)


def _outproj_kernel(oa_ref, od_ref, of_ref, ob_ref, g_ref, x_ref, gate_ref, w_ref, sub_ref, hgn_ref,
                    bd_ref, lng_ref, lnb_ref, o_ref, *, lam_init):
    inv_d = 1.0 / HEAD_DIM
    g = g_ref[...].astype(F32)
    bd = bd_ref[...]

    def head_rms(t, gain):
        ssq = jnp.dot((t * t).astype(BF16), bd, preferred_element_type=F32)
        return t * lax.rsqrt(ssq * inv_d + RMS_EPS) * gain

    ma = (oa_ref[...].astype(F32) * g[:, 0:GQA_WIDTH]).astype(BF16)
    md = (head_rms(od_ref[...].astype(F32), sub_ref[...]) * (1.0 - lam_init)
          * g[:, GQA_WIDTH:GQA_WIDTH + DIFF_WIDTH]).astype(BF16)
    mr = (head_rms(of_ref[...].astype(F32) + ob_ref[...].astype(F32), hgn_ref[...])
          * g[:, GQA_WIDTH + DIFF_WIDTH:]).astype(BF16)
    y = (jnp.dot(ma, w_ref[0:GQA_WIDTH, :], preferred_element_type=F32)
         + jnp.dot(md, w_ref[GQA_WIDTH:GQA_WIDTH + DIFF_WIDTH, :], preferred_element_type=F32)
         + jnp.dot(mr, w_ref[GQA_WIDTH + DIFF_WIDTH:, :], preferred_element_type=F32))
    xn = DEEPNORM_ALPHA * x_ref[...] + gate_ref[0] * y
    mu = jnp.mean(xn, axis=-1, keepdims=True)
    xc = xn - mu
    var = jnp.mean(xc * xc, axis=-1, keepdims=True)
    o_ref[...] = xc * lax.rsqrt(var + LN_EPS) * lng_ref[...] + lnb_ref[...]


def _out_projection(out_a, out_d, o_f, o_b, gates, x, gate, w_out_b, subln, hgn, bd, ln_g, ln_b,
                    tokens_per_batch, lam_init):
    t_total = x.shape[0]
    tm = PROJ_ROWS
    mod_rows = t_total // gate.shape[0] // tm
    row = lambda w: pl.BlockSpec((tm, w), lambda i: (i, 0))
    full = lambda shape: pl.BlockSpec(shape, lambda i: (0,) * len(shape))
    return pl.pallas_call(
        functools.partial(_outproj_kernel, lam_init=lam_init),
        grid=(t_total // tm,),
        in_specs=[row(GQA_WIDTH), row(DIFF_WIDTH), row(HGRN_WIDTH), row(HGRN_WIDTH), row(D_MODEL), row(D_MODEL),
                  pl.BlockSpec((1, 1, D_MODEL), lambda i: (i // mod_rows, 0, 0)),
                  full((D_MODEL, D_MODEL)), full((1, DIFF_WIDTH)), full((1, HGRN_WIDTH)),
                  full((DIFF_WIDTH, DIFF_WIDTH)), full((1, D_MODEL)), full((1, D_MODEL))],
        out_specs=row(D_MODEL),
        out_shape=jax.ShapeDtypeStruct((t_total, D_MODEL), F32),
        compiler_params=_params(("parallel",)),
        name="out_projection",
    )(out_a, out_d, o_f, o_b, gates, x, gate, w_out_b, subln, hgn, bd, ln_g, ln_b)


def _block_diag_ones(n):
    idx = np.arange(n) // HEAD_DIM
    return jnp.asarray((idx[:, None] == idx[None, :]).astype(np.float32), dtype=BF16)


def _scan_constants():
    C, W, H = SCAN_CHUNK, HGRN_WIDTH, HGRN_HEADS
    t = np.arange(C)[:, None]
    u = np.arange(C)[None, :]
    tri = np.stack([u <= t, u >= t])
    s = (np.arange(W) % C)[None, :]
    lmask = np.zeros((2, HGRN_LEVELS, C, W), np.float32)
    for level in range(HGRN_LEVELS):
        block = 8 << level
        same = (t // block) == (s // block)
        if level == 0:
            lmask[0, level] = same & (s <= t)
            lmask[1, level] = same & (s >= t)
        else:
            t_hi, s_hi = (t % block) >= block // 2, (s % block) >= block // 2
            lmask[0, level] = same & t_hi & ~s_hi
            lmask[1, level] = same & ~t_hi & s_hi
    rows = np.arange(H * C)[:, None]
    hmask = (rows // C) == (np.arange(W)[None, :] // HEAD_DIM)
    bdmask = (np.arange(W)[:, None] // HEAD_DIM) == (np.arange(W)[None, :] // HEAD_DIM)
    return (jnp.asarray(tri.astype(np.float32), dtype=BF16), jnp.asarray(lmask),
            jnp.asarray(hmask.astype(np.float32), dtype=BF16), jnp.asarray(bdmask.astype(np.float32)))


def _rope_tables(n_tokens, dim):
    rows = n_tokens // GRID_W
    row = jnp.repeat(jnp.arange(rows, dtype=F32), GRID_W)
    col = jnp.tile(jnp.arange(GRID_W, dtype=F32), rows)
    quarter = dim // 4
    inv_freq = ROPE_THETA ** (-jnp.arange(quarter, dtype=F32) / quarter)
    ar = row[:, None] * inv_freq[None, :]
    ac = col[:, None] * inv_freq[None, :]
    ang = jnp.concatenate([ar, ar, ac, ac], axis=-1)
    cos = jnp.tile(jnp.cos(ang), (1, LANES // dim))
    sin = jnp.tile(jnp.sin(ang), (1, LANES // dim))
    first = (jnp.arange(LANES) % (dim // 2)) < quarter
    return cos, jnp.where(first, -sin, 0.0), jnp.where(first, 0.0, sin)


def _mixer_layer(x, scale, shift, gate, layer, batch, tokens_per_batch, rope_tabs, cache, consts, weights,
                 ctx_out, ctx_prev):
    (w_in_b, w_out_b, qgain, kgain, lam_p, subln, hgn, lb_raw, ln_g, ln_b) = weights
    bdq, bdk, bdd, scan_consts = consts
    lam_init = 0.8 - 0.6 * math.exp(-0.3 * layer)
    outs = _in_projection(x, scale, shift, w_in_b, qgain, kgain, bdq, bdk, rope_tabs, tokens_per_batch, ctx_out,
                          layer, None if ctx_prev is None else ctx_prev[0])
    qa, kdup, vaug, qd, kd, vaugd, gates, zr = outs[:8]
    if cache is None:
        gqa_cache = diff_cache = s0 = None
        tq_a = tq_d = tk = tokens_per_batch
        items_a, items_d = 2, 2
    else:
        gqa_cache, diff_cache, s0 = cache
        tq_a, tq_d, tk = 512, 256, 512
        items_a = items_d = 1
    out_a = _gqa_attention(qa, kdup, vaug, gqa_cache, batch, tokens_per_batch, tq_a, tk, items_a)
    out_d = _diff_attention(qd, kd, vaugd, lam_p, diff_cache, batch, tokens_per_batch, tq_d, tk, items_d, lam_init)
    scan = _hgrn_scan(zr, lb_raw, scan_consts, s0, None if ctx_prev is None else ctx_prev[1], ctx_out, layer,
                      batch, tokens_per_batch)
    x_new = _out_projection(out_a, out_d, scan[0], scan[1], gates, x, gate, w_out_b, subln, hgn, bdd, ln_g, ln_b,
                            tokens_per_batch, lam_init)
    return x_new, ((outs[8:], scan[2]) if ctx_out else None)


def kernel(x_prompt, x_sample, cache_gqa_k, cache_gqa_v, cache_diff_k, cache_diff_v, state_hgrn, c, c_ctx,
           w_ada, b_ada, w_in, gqa_q_norm, gqa_k_norm, diff_lambda, diff_subln, hgrn_lower_bounds, hgrn_norm,
           w_out, ln_g, ln_b):
    batch, seq, _ = x_prompt.shape
    dec_batch, dec_seq, _ = x_sample.shape
    past = cache_gqa_k.shape[2]
    H, HD = HGRN_HEADS, HEAD_DIM

    cond = jnp.zeros((MOD_ROWS, D_MODEL), F32).at[0].set(c_ctx).at[1:1 + dec_batch].set(c)
    mod = _modulation(cond, w_ada, b_ada)
    consts = (_block_diag_ones(GQA_WIDTH), _block_diag_ones(LANES), _block_diag_ones(DIFF_WIDTH),
              _scan_constants())
    rope_tabs = _rope_tables(dec_seq, HEAD_DIM) + _rope_tables(dec_seq, DIFF_QK_DIM)
    w_in_b = w_in.astype(BF16)
    w_out_b = w_out.astype(BF16)

    def layer_weights(l):
        return (w_in_b[l], w_out_b[l],
                jnp.tile(gqa_q_norm[l], GQA_WIDTH // HD)[None, :] * (HD ** -0.5 * LOG2E),
                jnp.tile(gqa_k_norm[l], LANES // HD)[None, :],
                diff_lambda[l],
                jnp.tile(diff_subln[l], DIFF_WIDTH // HD)[None, :],
                jnp.tile(hgrn_norm[l], HGRN_WIDTH // HD)[None, :],
                hgrn_lower_bounds, ln_g[l][None, :], ln_b[l][None, :])

    def mod_rows(l, lo, n):
        m = mod[l, lo:lo + n]
        return (m[:, None, 0:D_MODEL], m[:, None, D_MODEL:2 * D_MODEL], m[:, None, 2 * D_MODEL:])

    x = x_prompt.reshape(batch * seq, D_MODEL)
    ctx = None
    for l in range(DEPTH):
        shift, scale, gate = mod_rows(l, 0, 1)
        x, ctx = _mixer_layer(x, scale, shift, gate, l, batch, seq, None, None, consts, layer_weights(l), True, ctx)
    y_prompt = x.reshape(batch, seq, D_MODEL)
    (kn, av, dk, dv), st = ctx
    new_leaves = [kn.reshape(batch, DEPTH, seq, GQA_KV_HEADS, HD), av.reshape(batch, DEPTH, seq, GQA_KV_HEADS, HD),
                  dk.reshape(batch, DEPTH, seq, DIFF_HEADS, HD), dv.reshape(batch, DEPTH, seq, DIFF_HEADS, HD),
                  jnp.swapaxes(st[..., :HD], -1, -2)]

    ones = jnp.ones((dec_batch, past, HD), F32)
    eye = jnp.eye(H, dtype=F32)
    x = x_sample.reshape(dec_batch * dec_seq, D_MODEL)
    for l in range(DEPTH):
        shift, scale, gate = mod_rows(l, 1, dec_batch)
        ck, cv = cache_gqa_k[:, l], cache_gqa_v[:, l]
        ckd = jnp.stack([jnp.concatenate([ck[:, :, g], ck[:, :, g]], axis=-1) for g in range(GQA_KV_HEADS)], axis=1)
        cvd = jnp.stack([jnp.concatenate([cv[:, :, g], ones], axis=-1) for g in range(GQA_KV_HEADS)], axis=1)
        dkc = cache_diff_k[:, l].reshape(dec_batch, past, DIFF_WIDTH)
        dvc = cache_diff_v[:, l]
        dvd = jnp.stack([jnp.concatenate([dvc[:, :, h], ones] if h % 2 == 0 else [ones, dvc[:, :, h]], axis=-1)
                         for h in range(DIFF_HEADS)], axis=1)
        s0t = jnp.swapaxes(state_hgrn[:, l].astype(F32), -1, -2)
        s0 = (s0t[:, :, :, :, None, :] * eye[None, None, :, None, :, None]).reshape(dec_batch, 2, H * HD, H * HD)
        cache = ((ckd.astype(BF16), cvd.astype(BF16)), (dkc.astype(BF16), dvd.astype(BF16)), s0)
        x, _ = _mixer_layer(x, scale, shift, gate, l, dec_batch, dec_seq, rope_tabs, cache, consts,
                            layer_weights(l), False, None)
    y_sample = x.reshape(dec_batch, dec_seq, D_MODEL)
    return (y_prompt, y_sample) + tuple(new_leaves)
```

```python
import functools
import math

import numpy as np
import jax
import jax.numpy as jnp
from jax import lax
from jax.experimental import pallas as pl
from jax.experimental.pallas import tpu as pltpu

F32 = jnp.float32
BF16 = jnp.bfloat16

D_MODEL = 1024
DEPTH = 2
GRID_W = 64
HEAD_DIM = 64
GQA_WIDTH = 512
GQA_KV_HEADS = 2
DIFF_WIDTH = 256
DIFF_HEADS = 4
DIFF_QK_DIM = 32
HGRN_WIDTH = 256
HGRN_HEADS = 4
IN_WIDTH = 3584
SCAN_CHUNK = 64
HGRN_LEVELS = 4
ROPE_THETA = 10000.0
RMS_EPS = 1e-6
LN_EPS = 1e-5
FORGET_MIN = 1e-6
DEEPNORM_ALPHA = (2 * DEPTH) ** 0.25
LANES = 128
SUBLANES = 8
MOD_ROWS = SUBLANES
VMEM_LIMIT = 48 * 1024 * 1024
SM_ROWS = 32
PROJ_ROWS = 512
SCAN_BLOCK = 512
LOG2E = math.log2(math.e)

_A_Q, _A_K, _A_V, _A_G = 0, 512, 640, 768
_D_Q, _D_K, _D_V, _D_G = 1280, 1536, 1792, 2048
_R_Z, _R_G = 2304, 3328

_NT = (((1,), (1,)), ((), ()))
_TN = (((0,), (0,)), ((), ()))


def _silu(x):
    return x * jax.nn.sigmoid(x)


def _params(sem):
    return pltpu.CompilerParams(dimension_semantics=sem, vmem_limit_bytes=VMEM_LIMIT)


def _mod_kernel(c_ref, w_ref, b_ref, o_ref):
    c = c_ref[...]
    o_ref[0] = jnp.dot(_silu(c), w_ref[0], preferred_element_type=F32) + b_ref[0]


def _modulation(cond, w_ada, b_ada):
    tn = 1024
    return pl.pallas_call(
        _mod_kernel,
        grid=(DEPTH, 3 * D_MODEL // tn),
        in_specs=[pl.BlockSpec((MOD_ROWS, D_MODEL), lambda l, j: (0, 0)),
                  pl.BlockSpec((1, D_MODEL, tn), lambda l, j: (l, 0, j)),
                  pl.BlockSpec((1, 1, tn), lambda l, j: (l, 0, j))],
        out_specs=pl.BlockSpec((1, MOD_ROWS, tn), lambda l, j: (l, 0, j)),
        out_shape=jax.ShapeDtypeStruct((DEPTH, MOD_ROWS, 3 * D_MODEL), F32),
        compiler_params=_params(("arbitrary", "arbitrary")),
        name="modulation",
    )(cond, w_ada, b_ada.reshape(DEPTH, 1, 3 * D_MODEL))


def _rope(x, c, s1, s2, shift):
    return x * c + pltpu.roll(x, LANES - shift, 1) * s1 + pltpu.roll(x, shift, 1) * s2


def _inproj_kernel(*refs, rope, ctx_out, ctx_alias):
    it = iter(refs)
    x_ref, sc_ref, sh_ref, w_ref, qg_ref, kg_ref, bdq_ref, bdk_ref = [next(it) for _ in range(8)]
    if rope:
        ca, s1a, s2a, cd, s1d, s2d = [next(it)[...] for _ in range(6)]
    if ctx_alias:
        for _ in range(4):
            next(it)
    qa_o, kdup_o, vaug_o, qd_o, kd_o, vaugd_o, gates_o, zr_o = [next(it) for _ in range(8)]
    if ctx_out:
        ctx_refs = [next(it) for _ in range(4)]
        if not ctx_alias:
            for r in ctx_refs:
                r[:, 1:] = jnp.zeros((r.shape[0], r.shape[1] - 1) + r.shape[2:], F32)

        def put_ctx(n, val):
            r = ctx_refs[n]
            r[:, 0] = val.reshape(r.shape[0], r.shape[2], r.shape[3])

    h = (x_ref[...] * (1.0 + sc_ref[0]) + sh_ref[0]).astype(BF16)

    def proj(lo, hi):
        return jnp.dot(h, w_ref[:, lo:hi], preferred_element_type=F32)

    tm = h.shape[0]
    low = lax.broadcasted_iota(jnp.int32, (tm, LANES), 1) < HEAD_DIM
    inv_d = 1.0 / HEAD_DIM

    aq = proj(_A_Q, _A_K)
    ssq = jnp.dot((aq * aq).astype(BF16), bdq_ref[...], preferred_element_type=F32)
    qn = aq * lax.rsqrt(ssq * inv_d + RMS_EPS) * qg_ref[...]
    for j in range(GQA_WIDTH // LANES):
        slab = qn[:, j * LANES:(j + 1) * LANES]
        if rope:
            slab = _rope(slab, ca, s1a, s2a, HEAD_DIM // 4)
        qa_o[:, j * LANES:(j + 1) * LANES] = slab.astype(BF16)

    ak = proj(_A_K, _A_V)
    ssqk = jnp.dot((ak * ak).astype(BF16), bdk_ref[...], preferred_element_type=F32)
    kn = ak * lax.rsqrt(ssqk * inv_d + RMS_EPS) * kg_ref[...]
    if ctx_out:
        put_ctx(0, kn)
    if rope:
        kn = _rope(kn, ca, s1a, s2a, HEAD_DIM // 4)
    kr = pltpu.roll(kn, HEAD_DIM, 1)
    kdup_o[0] = jnp.where(low, kn, kr).astype(BF16)
    kdup_o[1] = jnp.where(low, kr, kn).astype(BF16)

    av = proj(_A_V, _A_G)
    if ctx_out:
        put_ctx(1, av)
    vr = pltpu.roll(av, HEAD_DIM, 1)
    vaug_o[0] = jnp.where(low, av, 1.0).astype(BF16)
    vaug_o[1] = jnp.where(low, vr, 1.0).astype(BF16)

    gates_o[:, 0:GQA_WIDTH] = _silu(proj(_A_G, _D_Q)).astype(BF16)

    dq = proj(_D_Q, _D_K) * (DIFF_QK_DIM ** -0.5 * LOG2E)
    dk = proj(_D_K, _D_V)
    if ctx_out:
        put_ctx(2, dk)
    for j in range(DIFF_WIDTH // LANES):
        sq = dq[:, j * LANES:(j + 1) * LANES]
        sk = dk[:, j * LANES:(j + 1) * LANES]
        if rope:
            sq = _rope(sq, cd, s1d, s2d, DIFF_QK_DIM // 4)
            sk = _rope(sk, cd, s1d, s2d, DIFF_QK_DIM // 4)
        qd_o[:, j * LANES:(j + 1) * LANES] = sq.astype(BF16)
        kd_o[:, j * LANES:(j + 1) * LANES] = sk.astype(BF16)
    dv = proj(_D_V, _D_G)
    if ctx_out:
        put_ctx(3, dv)
    for j in range(DIFF_WIDTH // LANES):
        sv = dv[:, j * LANES:(j + 1) * LANES]
        vaugd_o[2 * j] = jnp.where(low, sv, 1.0).astype(BF16)
        vaugd_o[2 * j + 1] = jnp.where(low, 1.0, sv).astype(BF16)
    gates_o[:, GQA_WIDTH:GQA_WIDTH + DIFF_WIDTH] = _silu(proj(_D_G, _R_Z)).astype(BF16)

    gates_o[:, GQA_WIDTH + DIFF_WIDTH:] = _silu(proj(_R_G, IN_WIDTH)).astype(BF16)
    zr = proj(_R_Z, _R_G)
    zr_o[:, 0:HGRN_WIDTH] = _silu(zr[:, 0:HGRN_WIDTH])
    zr_o[:, HGRN_WIDTH:] = zr[:, HGRN_WIDTH:]


def _in_projection(x, scale, shift, w_in_b, qgain, kgain, bdq, bdk, rope_tabs, tokens_per_batch, ctx_out,
                   layer, ctx_prev):
    t_total = x.shape[0]
    tm = PROJ_ROWS
    nb_rows = tokens_per_batch // tm
    mod_rows = t_total // scale.shape[0] // tm
    rope = rope_tabs is not None
    assert not rope or nb_rows >= 1
    full = lambda shape: pl.BlockSpec(shape, lambda i: (0,) * len(shape))
    in_specs = [pl.BlockSpec((tm, D_MODEL), lambda i: (i, 0)),
                pl.BlockSpec((1, 1, D_MODEL), lambda i: (i // mod_rows, 0, 0)),
                pl.BlockSpec((1, 1, D_MODEL), lambda i: (i // mod_rows, 0, 0)),
                pl.BlockSpec((None, D_MODEL, IN_WIDTH), lambda i: (layer, 0, 0)),
                full((1, GQA_WIDTH)), full((1, LANES)),
                full((GQA_WIDTH, GQA_WIDTH)), full((LANES, LANES))]
    args = [x, scale, shift, w_in_b, qgain, kgain, bdq, bdk]
    if rope:
        in_specs += [pl.BlockSpec((tm, LANES), lambda i: (i % nb_rows, 0))] * 6
        args += list(rope_tabs)
    row = lambda w: pl.BlockSpec((tm, w), lambda i: (i, 0))
    stk = lambda n: pl.BlockSpec((n, tm, LANES), lambda i: (0, i, 0))
    out_specs = [row(GQA_WIDTH), stk(2), stk(2), row(DIFF_WIDTH), row(DIFF_WIDTH), stk(4),
                 row(D_MODEL), row(4 * HGRN_WIDTH)]
    out_shape = [jax.ShapeDtypeStruct((t_total, GQA_WIDTH), BF16),
                 jax.ShapeDtypeStruct((2, t_total, LANES), BF16),
                 jax.ShapeDtypeStruct((2, t_total, LANES), BF16),
                 jax.ShapeDtypeStruct((t_total, DIFF_WIDTH), BF16),
                 jax.ShapeDtypeStruct((t_total, DIFF_WIDTH), BF16),
                 jax.ShapeDtypeStruct((4, t_total, LANES), BF16),
                 jax.ShapeDtypeStruct((t_total, D_MODEL), BF16),
                 jax.ShapeDtypeStruct((t_total, 4 * HGRN_WIDTH), F32)]
    aliases = {}
    if ctx_out:
        seq = tokens_per_batch
        nseq = tm // seq
        assert nseq * seq == tm
        widths = (LANES, LANES, DIFF_WIDTH, DIFF_WIDTH)
        if ctx_prev is not None:
            aliases = {len(args) + n: len(out_specs) + n for n in range(4)}
            in_specs += [pl.BlockSpec(memory_space=pl.ANY)] * 4
            args += list(ctx_prev)
        if ctx_prev is None:
            assert layer == 0
            out_specs += [pl.BlockSpec((nseq, DEPTH, seq, w), lambda i: (i, 0, 0, 0)) for w in widths]
        else:
            out_specs += [pl.BlockSpec((nseq, 1, seq, w), lambda i: (i, layer, 0, 0)) for w in widths]
        out_shape += [jax.ShapeDtypeStruct((t_total // seq, DEPTH, seq, w), F32) for w in widths]
    return pl.pallas_call(
        functools.partial(_inproj_kernel, rope=rope, ctx_out=ctx_out, ctx_alias=bool(aliases)),
        grid=(t_total // tm,),
        in_specs=in_specs, out_specs=out_specs, out_shape=out_shape,
        input_output_aliases=aliases,
        compiler_params=_params(("parallel",)),
        name="in_projection",
    )(*args)


def _chunk_rows(j, tk):
    return pl.ds(j * tk, tk) if isinstance(j, int) else pl.ds(pl.multiple_of(j * tk, tk), tk)


def _attention_scratch(n_items, rows, tk):
    rows = n_items * rows
    return ([pltpu.VMEM((rows, LANES), BF16)]
            + [pltpu.VMEM((rows, tk), F32)] * 2 + [pltpu.VMEM((rows, tk), BF16)] * 2
            + [pltpu.VMEM((rows, LANES), F32)] * 4)


def _item_scratch(refs, item, rows):
    return [r.at[pl.ds(item * rows, rows)] for r in refs]


def _softmax_stage(s_sc, p_sc, a_sc, m_sc, keys, first):
    for r in range(s_sc.shape[0] // SM_ROWS):
        rows = pl.ds(r * SM_ROWS, SM_ROWS)
        s = s_sc[rows, 0:keys]
        m_cur = jnp.max(s, axis=1, keepdims=True)
        if first:
            m_new = jnp.broadcast_to(m_cur, (SM_ROWS, LANES))
        else:
            m_prev = m_sc[rows, :]
            m_new = jnp.maximum(m_prev, m_cur)
            a_sc[rows, :] = jnp.exp2(m_prev - m_new)
        m_sc[rows, :] = m_new
        p_sc[rows, 0:keys] = jnp.exp2(s - jnp.concatenate([m_new] * (keys // LANES), axis=1)).astype(BF16)


def _item_timeline(n_new, cache, k_at, v_at, start, qk, sm, av, finish):
    o = 0 if cache is None else 1
    n = n_new + o
    k_of = lambda c: cache[0]() if (o and c == 0) else k_at(c - o)
    v_of = lambda c: cache[1]() if (o and c == 0) else v_at(c - o)
    steps = []
    for t in range(n + 2):
        step = [start] if t == 0 else []
        if t < n:
            step.append(lambda t=t: qk(t % 2, k_of(t)))
        if 0 <= t - 1 < n:
            step.append(lambda t=t: sm((t - 1) % 2, t == 1))
        if 0 <= t - 2 < n:
            step.append(lambda t=t: av(t % 2, v_of(t - 2), t == 2))
        if t == n + 1:
            step.append(finish)
        steps.append(step)
    return steps


def _run_items(timelines):
    lag = max(1, len(timelines[0]) - 2)
    for g in range(lag * (len(timelines) - 1) + len(timelines[0])):
        for i, steps in enumerate(timelines):
            if 0 <= g - i * lag < len(steps):
                for stage in steps[g - i * lag]:
                    stage()


def _gqa_kernel(*refs, has_cache, tq, tk, nk, n_items, shared_kv):
    if has_cache:
        q_ref, kc_ref, vc_ref = refs[:3]
        refs = refs[3:]
    else:
        q_ref = refs[0]
        refs = refs[1:]
    k_ref, v_ref, o_ref = refs[:3]
    low = lax.broadcasted_iota(jnp.int32, (tq, LANES), 1) < HEAD_DIM
    rows = 2 * tq

    def item_stages(item):
        qs_sc, s0, s1, p0, p1, a0, a1, m_sc, acc_sc = _item_scratch(refs[3:], item, rows)
        s_b, p_b, a_b = (s0, s1), (p0, p1), (a0, a1)

        def start():
            q = q_ref[pl.ds(item * tq, tq), :]
            zero = jnp.zeros_like(q)
            qs_sc[0:tq, :] = jnp.where(low, q, zero)
            qs_sc[tq:, :] = jnp.where(low, zero, q)

        keys = {}

        def qk(slot, k):
            keys[slot] = k.shape[0]
            s_b[slot][:, 0:k.shape[0]] = lax.dot_general(qs_sc[...], k, _NT, preferred_element_type=F32)

        def sm(slot, first):
            _softmax_stage(s_b[slot], p_b[slot], a_b[slot], m_sc, keys[slot], first)

        def av(slot, v, first):
            pv = jnp.dot(p_b[slot][:, 0:v.shape[0]], v, preferred_element_type=F32)
            acc_sc[...] = pv if first else a_b[slot][...] * acc_sc[...] + pv

        def finish():
            acc = acc_sc[...]
            on = acc / pltpu.roll(acc, HEAD_DIM, 1)
            o_ref[pl.ds(item * tq, tq), :] = jnp.where(low, on[:tq], pltpu.roll(on[tq:], HEAD_DIM, 1)).astype(BF16)

        base = 0 if shared_kv else item * nk
        chunk = lambda ref: (lambda j: ref[0, _chunk_rows(base + j, tk), :])
        cache = (lambda: kc_ref[0, 0], lambda: vc_ref[0, 0]) if has_cache else None
        return _item_timeline(nk, cache, chunk(k_ref), chunk(v_ref), start, qk, sm, av, finish)

    _run_items([item_stages(item) for item in range(n_items)])


def _item_geometry(batch, tokens_per_batch, tq, n_items):
    nq = tokens_per_batch // tq
    if nq == 1:
        return batch // n_items, 1, n_items * tokens_per_batch, False
    assert nq % n_items == 0
    return batch, nq // n_items, tokens_per_batch, True


def _gqa_attention(qa, kdup, vaug, cache, batch, tokens_per_batch, tq, tk, n_items):
    t_total = qa.shape[0]
    nk = tokens_per_batch // tk
    has_cache = cache is not None
    bsteps, nq, kv_rows, shared_kv = _item_geometry(batch, tokens_per_batch, tq, n_items)
    assert shared_kv or not has_cache
    in_specs = [pl.BlockSpec((n_items * tq, LANES), lambda b, p, i: (b * nq + i, p))]
    args = [qa]
    if has_cache:
        past = cache[0].shape[2]
        in_specs += [pl.BlockSpec((1, 1, past, LANES), lambda b, p, i: (b, p // 2, 0, 0))] * 2
        args += list(cache)
    in_specs += [pl.BlockSpec((1, kv_rows, LANES), lambda b, p, i: (p // 2, b, 0))] * 2
    args += [kdup, vaug]
    return pl.pallas_call(
        functools.partial(_gqa_kernel, has_cache=has_cache, tq=tq, tk=tk, nk=nk, n_items=n_items,
                          shared_kv=shared_kv),
        grid=(bsteps, GQA_WIDTH // LANES, nq),
        in_specs=in_specs,
        out_specs=pl.BlockSpec((n_items * tq, LANES), lambda b, p, i: (b * nq + i, p)),
        out_shape=jax.ShapeDtypeStruct((t_total, GQA_WIDTH), BF16),
        scratch_shapes=_attention_scratch(n_items, 2 * tq, tk),
        compiler_params=_params(("parallel", "parallel", "parallel")),
        name="gqa_attention",
    )(*args)


def _diff_kernel(*refs, has_cache, tq, tk, nk, n_items, shared_kv, lam_init):
    q_ref, lp_ref = refs[:2]
    if has_cache:
        kc_ref, vc0_ref, vc1_ref = refs[2:5]
        refs = refs[5:]
    else:
        refs = refs[2:]
    k_ref, v0_ref, v1_ref, o_ref = refs[:4]
    lane = lax.broadcasted_iota(jnp.int32, (tq, LANES), 1)
    lp = lp_ref[...]
    lam = (jnp.exp(jnp.sum(lp[0:1] * lp[1:2], axis=1, keepdims=True))
           - jnp.exp(jnp.sum(lp[2:3] * lp[3:4], axis=1, keepdims=True)) + lam_init)
    h2 = 2 * tq
    rows = 4 * tq

    def item_stages(item):
        qs_sc, s0, s1, p0, p1, a0, a1, m_sc, acc_sc = _item_scratch(refs[4:], item, rows)
        s_b, p_b, a_b = (s0, s1), (p0, p1), (a0, a1)

        def start():
            q = q_ref[pl.ds(item * tq, tq), :]
            zero = jnp.zeros_like(q)
            for j in range(4):
                qs_sc[j * tq:(j + 1) * tq, :] = jnp.where((lane >> 5) == j, q, zero)

        keys = {}

        def qk(slot, k):
            keys[slot] = k.shape[0]
            s_b[slot][:, 0:k.shape[0]] = lax.dot_general(qs_sc[...], k, _NT, preferred_element_type=F32)

        def sm(slot, first):
            _softmax_stage(s_b[slot], p_b[slot], a_b[slot], m_sc, keys[slot], first)

        def av(slot, v, first):
            for half, vh in zip((pl.ds(0, h2), pl.ds(h2, h2)), v):
                pv = jnp.dot(p_b[slot][half, 0:vh.shape[0]], vh, preferred_element_type=F32)
                acc_sc[half, :] = pv if first else a_b[slot][half, :] * acc_sc[half, :] + pv

        def finish():
            acc = acc_sc[...]
            on = acc / pltpu.roll(acc, HEAD_DIM, 1)
            o0 = on[0:tq] - lam * on[tq:h2]
            o1 = on[h2:h2 + tq] - lam * on[h2 + tq:]
            o_ref[pl.ds(item * tq, tq), :] = jnp.where(lane < HEAD_DIM, o0, o1).astype(BF16)

        base = 0 if shared_kv else item * nk
        k_at = lambda j: k_ref[_chunk_rows(base + j, tk), :]
        v_at = lambda j: (v0_ref[0, _chunk_rows(base + j, tk), :], v1_ref[0, _chunk_rows(base + j, tk), :])
        cache = (lambda: kc_ref[0], lambda: (vc0_ref[0, 0], vc1_ref[0, 0])) if has_cache else None
        return _item_timeline(nk, cache, k_at, v_at, start, qk, sm, av, finish)

    _run_items([item_stages(item) for item in range(n_items)])


def _diff_attention(qd, kd, vaugd, lam_p, cache, batch, tokens_per_batch, tq, tk, n_items, lam_init):
    t_total = qd.shape[0]
    nk = tokens_per_batch // tk
    has_cache = cache is not None
    bsteps, nq, kv_rows, shared_kv = _item_geometry(batch, tokens_per_batch, tq, n_items)
    assert shared_kv or not has_cache
    in_specs = [pl.BlockSpec((n_items * tq, LANES), lambda b, p, i: (b * nq + i, p)),
                pl.BlockSpec((4, DIFF_QK_DIM), lambda b, p, i: (0, 0))]
    args = [qd, lam_p]
    if has_cache:
        ckd, cvd = cache
        past = ckd.shape[1]
        in_specs += [pl.BlockSpec((1, past, LANES), lambda b, p, i: (b, 0, p)),
                     pl.BlockSpec((1, 1, past, LANES), lambda b, p, i: (b, 2 * p, 0, 0)),
                     pl.BlockSpec((1, 1, past, LANES), lambda b, p, i: (b, 2 * p + 1, 0, 0))]
        args += [ckd, cvd, cvd]
    in_specs += [pl.BlockSpec((kv_rows, LANES), lambda b, p, i: (b, p)),
                 pl.BlockSpec((1, kv_rows, LANES), lambda b, p, i: (2 * p, b, 0)),
                 pl.BlockSpec((1, kv_rows, LANES), lambda b, p, i: (2 * p + 1, b, 0))]
    args += [kd, vaugd, vaugd]
    return pl.pallas_call(
        functools.partial(_diff_kernel, has_cache=has_cache, tq=tq, tk=tk, nk=nk, n_items=n_items,
                          shared_kv=shared_kv, lam_init=lam_init),
        grid=(bsteps, DIFF_WIDTH // LANES, nq),
        in_specs=in_specs,
        out_specs=pl.BlockSpec((n_items * tq, LANES), lambda b, p, i: (b * nq + i, p)),
        out_shape=jax.ShapeDtypeStruct((t_total, DIFF_WIDTH), BF16),
        scratch_shapes=_attention_scratch(n_items, 4 * tq, tk),
        compiler_params=_params(("parallel", "parallel", "parallel")),
        name="diff_attention",
    )(*args)


def _hgrn_gates(qs, z, v, lb, g_sc, tri):
    sig = jax.nn.sigmoid(z)
    om = 1.0 - lb
    k = om * (1.0 - sig)
    g = jnp.log2(jnp.maximum(lb + om * sig, FORGET_MIN))
    g1 = g.astype(BF16)
    e1 = g - g1.astype(F32)
    g2 = e1.astype(BF16)
    g3 = (e1 - g2.astype(F32)).astype(BF16)
    G = (jnp.dot(tri, g1, preferred_element_type=F32) + jnp.dot(tri, g2, preferred_element_type=F32)
         + jnp.dot(tri, g3, preferred_element_type=F32))
    g_sc[...] = G
    return qs, k, G, v.astype(BF16)


def _hgrn_operands(gates, g_sc, hmask, rev):
    C, W = SCAN_CHUNK, HGRN_WIDTH
    qs, k, G, vb = gates
    g_last = g_sc[pl.ds(0 if rev else C - 1, 1), :]
    q_in = (qs * jnp.exp2(G)).astype(BF16)
    k_end = (k * jnp.exp2(g_last - G)).astype(BF16)

    def block_diag(a):
        return jnp.concatenate([a] * HGRN_HEADS, axis=0) * hmask

    def ref_rows(block, off):
        return jnp.concatenate([jnp.broadcast_to(g_sc[pl.ds(b0 + off, 1), :], (block, W))
                                for b0 in range(0, C, block)], axis=0)

    levels = []
    for level in range(HGRN_LEVELS):
        block = 8 << level
        half = block // 2
        if level == 0:
            r = ref_rows(block, half)
            qe, ke = G - r, r - G
        else:
            r = ref_rows(block, half if rev else half - 1)
            qe, ke = jnp.minimum(G - r, 0.0), jnp.minimum(r - G, 0.0)
        levels.append(((qs * jnp.exp2(qe)).astype(BF16), block_diag((k * jnp.exp2(ke)).astype(BF16))))
    return q_in, jnp.exp2(g_last), k_end, vb, block_diag(vb), levels


def _hgrn_intra(operands, lmask_ref, bdmask, rev):
    q_in, decay, k_end, vb, v_bd, levels = operands
    ut = lax.dot_general(vb, k_end, _TN, preferred_element_type=F32) * bdmask
    a = None
    for level, (ql, kl) in enumerate(levels):
        al = lax.dot_general(ql, kl, _NT, preferred_element_type=F32)
        m = lmask_ref[rev, level]
        a = jnp.where(m > 0.5, al, 0.0) if level == 0 else a + al * m
    intra = jnp.dot(a.astype(BF16), v_bd, preferred_element_type=F32)
    return q_in, decay, ut, intra


def _hgrn_state(parts, st):
    q_in, decay, ut, intra = parts
    out = lax.dot_general(q_in, st.astype(BF16), _NT, preferred_element_type=F32) + intra
    return out, st * decay + ut


def _hgrn_kernel(*refs, layer, n_chunks, n_seq, has_s0, state_out, state_alias):
    it = iter(refs)
    qf, zf, vf, qb, zb, vb, lbraw, tri_ref, lmask_ref, hmask_ref, bdmask_ref = [next(it) for _ in range(11)]
    s0 = next(it) if has_s0 else None
    if state_alias:
        next(it)
    of_ref, ob_ref = next(it), next(it)
    sto_ref = next(it) if state_out else None
    st_ref, g_sc = next(it), next(it)

    @pl.when(pl.program_id(1) == 0)
    def _():
        if has_s0:
            st_ref[...] = s0[0]
        else:
            st_ref[...] = jnp.zeros(st_ref.shape, F32)

    lbr = lbraw[...]
    ex = jnp.exp(lbr - jnp.max(lbr, axis=0, keepdims=True))
    sm = ex / jnp.sum(ex, axis=0, keepdims=True)
    lbs = jnp.zeros(sm.shape[1:], F32)
    for j in range(1, layer + 1):
        lbs = lbs + sm[j]
    lb_f, lb_b = lbs[0:1], lbs[1:2]

    hmask = hmask_ref[...]
    bdmask = bdmask_ref[...]

    chunks = []
    for cidx in range(n_chunks):
        for s in range(n_seq):
            base = s * n_chunks
            chunks.append((2 * s, 0, pl.ds((base + cidx) * SCAN_CHUNK, SCAN_CHUNK), qf, zf, vf, lb_f, of_ref))
            chunks.append((2 * s + 1, 1, pl.ds((base + n_chunks - 1 - cidx) * SCAN_CHUNK, SCAN_CHUNK),
                           qb, zb, vb, lb_b, ob_ref))
    n = len(chunks)
    state = [st_ref[i] for i in range(2 * n_seq)]
    gates, operands, parts = {}, {}, {}
    for t in range(n + 3):
        if t < n:
            _, d, rows, q_r, z_r, v_r, lb, _ = chunks[t]
            gates[t] = _hgrn_gates(q_r[rows, :], z_r[rows, :], v_r[rows, :], lb, g_sc.at[t], tri_ref[d])
        if 0 <= t - 1 < n:
            operands[t - 1] = _hgrn_operands(gates.pop(t - 1), g_sc.at[t - 1], hmask, chunks[t - 1][1])
        if 0 <= t - 2 < n:
            parts[t - 2] = _hgrn_intra(operands.pop(t - 2), lmask_ref, bdmask, chunks[t - 2][1])
        if 0 <= t - 3 < n:
            si, _, rows, _, _, _, _, o_r = chunks[t - 3]
            out, state[si] = _hgrn_state(parts.pop(t - 3), state[si])
            o_r[rows, :] = out.astype(BF16)
    for i, s in enumerate(state):
        st_ref[i] = s

    if state_out:
        @pl.when(pl.program_id(1) == pl.num_programs(1) - 1)
        def _():
            if not state_alias:
                sto_ref[:, 1:] = jnp.zeros((sto_ref.shape[0], sto_ref.shape[1] - 1) + sto_ref.shape[2:], F32)
            for i, s in enumerate(state):
                for hh in range(HGRN_HEADS):
                    slab = s[hh * HEAD_DIM:(hh + 1) * HEAD_DIM, (hh // 2) * LANES:(hh // 2 + 1) * LANES]
                    sto_ref[i // 2, 0, i % 2, hh] = pltpu.roll(slab, HEAD_DIM, 1) if hh % 2 else slab


def _hgrn_scan(zr, lb_raw, scan_consts, s0, state_prev, state_out, layer, batch, tokens_per_batch):
    t_total = zr.shape[0]
    tb = min(SCAN_BLOCK, tokens_per_batch)
    nblk = tokens_per_batch // tb
    n_seq = SCAN_BLOCK // tb if nblk == 1 else 1
    W = HGRN_WIDTH
    has_s0 = s0 is not None
    assert n_seq == 1 or not has_s0
    state_alias = state_prev is not None
    rows = n_seq * tb
    fblk = lambda col: pl.BlockSpec((rows, W), lambda b, j: (b * nblk + j, col))
    bblk = lambda col: pl.BlockSpec((rows, W), lambda b, j: (b * nblk + nblk - 1 - j, col))
    full = lambda a: pl.BlockSpec(a.shape, lambda b, j: (0,) * a.ndim)
    in_specs = [fblk(0), fblk(1), fblk(3), bblk(0), bblk(2), bblk(3),
                pl.BlockSpec((DEPTH, 2, W), lambda b, j: (0, 0, 0))] + [full(a) for a in scan_consts]
    args = [zr] * 6 + [lb_raw] + list(scan_consts)
    if has_s0:
        in_specs.append(pl.BlockSpec((1, 2, W, W), lambda b, j: (b, 0, 0, 0)))
        args.append(s0)
    aliases = {}
    if state_alias:
        aliases = {len(args): 2}
        in_specs.append(pl.BlockSpec(memory_space=pl.ANY))
        args.append(state_prev)
    out_specs = [pl.BlockSpec((rows, W), lambda b, j: (b * nblk + j, 0)),
                 pl.BlockSpec((rows, W), lambda b, j: (b * nblk + nblk - 1 - j, 0))]
    out_shape = [jax.ShapeDtypeStruct((t_total, W), BF16), jax.ShapeDtypeStruct((t_total, W), BF16)]
    if state_out and not state_alias:
        assert layer == 0
        out_specs.append(pl.BlockSpec((n_seq, DEPTH, 2, HGRN_HEADS, HEAD_DIM, LANES),
                                      lambda b, j: (b, 0, 0, 0, 0, 0)))
    elif state_out:
        out_specs.append(pl.BlockSpec((n_seq, 1, 2, HGRN_HEADS, HEAD_DIM, LANES),
                                      lambda b, j: (b, layer, 0, 0, 0, 0)))
    if state_out:
        out_shape.append(jax.ShapeDtypeStruct((batch, DEPTH, 2, HGRN_HEADS, HEAD_DIM, LANES), F32))
    n_units = 2 * rows // SCAN_CHUNK
    return pl.pallas_call(
        functools.partial(_hgrn_kernel, layer=layer, n_chunks=tb // SCAN_CHUNK, n_seq=n_seq, has_s0=has_s0,
                          state_out=state_out, state_alias=state_alias),
        grid=(batch // n_seq, nblk),
        in_specs=in_specs, out_specs=out_specs, out_shape=out_shape,
        input_output_aliases=aliases,
        scratch_shapes=[pltpu.VMEM((2 * n_seq, W, W), F32), pltpu.VMEM((n_units, SCAN_CHUNK, W), F32)],
        compiler_params=_params(("parallel", "arbitrary")),
        name="hgrn_scan",
    )(*args)


def _outproj_kernel(oa_ref, od_ref, of_ref, ob_ref, g_ref, x_ref, gate_ref, w_ref, sub_ref, hgn_ref,
                    bd_ref, lng_ref, lnb_ref, o_ref, *, lam_init):
    inv_d = 1.0 / HEAD_DIM
    g = g_ref[...].astype(F32)
    bd = bd_ref[...]

    def head_rms(t, gain):
        ssq = jnp.dot((t * t).astype(BF16), bd, preferred_element_type=F32)
        return t * lax.rsqrt(ssq * inv_d + RMS_EPS) * gain

    ma = (oa_ref[...].astype(F32) * g[:, 0:GQA_WIDTH]).astype(BF16)
    md = (head_rms(od_ref[...].astype(F32), sub_ref[...]) * (1.0 - lam_init)
          * g[:, GQA_WIDTH:GQA_WIDTH + DIFF_WIDTH]).astype(BF16)
    mr = (head_rms(of_ref[...].astype(F32) + ob_ref[...].astype(F32), hgn_ref[...])
          * g[:, GQA_WIDTH + DIFF_WIDTH:]).astype(BF16)
    y = (jnp.dot(ma, w_ref[0:GQA_WIDTH, :], preferred_element_type=F32)
         + jnp.dot(md, w_ref[GQA_WIDTH:GQA_WIDTH + DIFF_WIDTH, :], preferred_element_type=F32)
         + jnp.dot(mr, w_ref[GQA_WIDTH + DIFF_WIDTH:, :], preferred_element_type=F32))
    xn = DEEPNORM_ALPHA * x_ref[...] + gate_ref[0] * y
    mu = jnp.mean(xn, axis=-1, keepdims=True)
    xc = xn - mu
    var = jnp.mean(xc * xc, axis=-1, keepdims=True)
    o_ref[...] = xc * lax.rsqrt(var + LN_EPS) * lng_ref[...] + lnb_ref[...]


def _out_projection(out_a, out_d, o_f, o_b, gates, x, gate, w_out_b, subln, hgn, bd, ln_g, ln_b,
                    layer, lam_init):
    t_total = x.shape[0]
    tm = PROJ_ROWS
    mod_rows = t_total // gate.shape[0] // tm
    row = lambda w: pl.BlockSpec((tm, w), lambda i: (i, 0))
    full = lambda shape: pl.BlockSpec(shape, lambda i: (0,) * len(shape))
    return pl.pallas_call(
        functools.partial(_outproj_kernel, lam_init=lam_init),
        grid=(t_total // tm,),
        in_specs=[row(GQA_WIDTH), row(DIFF_WIDTH), row(HGRN_WIDTH), row(HGRN_WIDTH), row(D_MODEL), row(D_MODEL),
                  pl.BlockSpec((1, 1, D_MODEL), lambda i: (i // mod_rows, 0, 0)),
                  pl.BlockSpec((None, D_MODEL, D_MODEL), lambda i: (layer, 0, 0)),
                  full((1, DIFF_WIDTH)), full((1, HGRN_WIDTH)),
                  full((DIFF_WIDTH, DIFF_WIDTH)), full((1, D_MODEL)), full((1, D_MODEL))],
        out_specs=row(D_MODEL),
        out_shape=jax.ShapeDtypeStruct((t_total, D_MODEL), F32),
        compiler_params=_params(("parallel",)),
        name="out_projection",
    )(out_a, out_d, o_f, o_b, gates, x, gate, w_out_b, subln, hgn, bd, ln_g, ln_b)


def _block_diag_ones(n):
    idx = np.arange(n) // HEAD_DIM
    return jnp.asarray((idx[:, None] == idx[None, :]).astype(np.float32), dtype=BF16)


def _scan_constants():
    C, W, H = SCAN_CHUNK, HGRN_WIDTH, HGRN_HEADS
    t = np.arange(C)[:, None]
    u = np.arange(C)[None, :]
    tri = np.stack([u <= t, u >= t])
    s = (np.arange(W) % C)[None, :]
    lmask = np.zeros((2, HGRN_LEVELS, C, W), np.float32)
    for level in range(HGRN_LEVELS):
        block = 8 << level
        same = (t // block) == (s // block)
        if level == 0:
            lmask[0, level] = same & (s <= t)
            lmask[1, level] = same & (s >= t)
        else:
            t_hi, s_hi = (t % block) >= block // 2, (s % block) >= block // 2
            lmask[0, level] = same & t_hi & ~s_hi
            lmask[1, level] = same & ~t_hi & s_hi
    rows = np.arange(H * C)[:, None]
    hmask = (rows // C) == (np.arange(W)[None, :] // HEAD_DIM)
    bdmask = (np.arange(W)[:, None] // HEAD_DIM) == (np.arange(W)[None, :] // HEAD_DIM)
    return (jnp.asarray(tri.astype(np.float32), dtype=BF16), jnp.asarray(lmask),
            jnp.asarray(hmask.astype(np.float32), dtype=BF16), jnp.asarray(bdmask.astype(np.float32)))


def _rope_tables(n_tokens, dim):
    rows = n_tokens // GRID_W
    row = jnp.repeat(jnp.arange(rows, dtype=F32), GRID_W)
    col = jnp.tile(jnp.arange(GRID_W, dtype=F32), rows)
    quarter = dim // 4
    inv_freq = ROPE_THETA ** (-jnp.arange(quarter, dtype=F32) / quarter)
    ar = row[:, None] * inv_freq[None, :]
    ac = col[:, None] * inv_freq[None, :]
    ang = jnp.concatenate([ar, ar, ac, ac], axis=-1)
    cos = jnp.tile(jnp.cos(ang), (1, LANES // dim))
    sin = jnp.tile(jnp.sin(ang), (1, LANES // dim))
    first = (jnp.arange(LANES) % (dim // 2)) < quarter
    return cos, jnp.where(first, -sin, 0.0), jnp.where(first, 0.0, sin)


def _mixer_layer(x, scale, shift, gate, layer, batch, tokens_per_batch, rope_tabs, cache, consts, weights,
                 ctx_out, ctx_prev):
    (w_in_b, w_out_b, qgain, kgain, lam_p, subln, hgn, lb_raw, ln_g, ln_b) = weights
    bdq, bdk, bdd, scan_consts = consts
    lam_init = 0.8 - 0.6 * math.exp(-0.3 * layer)
    outs = _in_projection(x, scale, shift, w_in_b, qgain, kgain, bdq, bdk, rope_tabs, tokens_per_batch, ctx_out,
                          layer, None if ctx_prev is None else ctx_prev[0])
    qa, kdup, vaug, qd, kd, vaugd, gates, zr = outs[:8]
    if cache is None:
        gqa_cache = diff_cache = s0 = None
        tq_a = tq_d = tk = tokens_per_batch
        items_a, items_d = 2, 2
    else:
        gqa_cache, diff_cache, s0 = cache
        tq_a, tq_d, tk = 512, 256, 512
        items_a = items_d = 1
    out_a = _gqa_attention(qa, kdup, vaug, gqa_cache, batch, tokens_per_batch, tq_a, tk, items_a)
    out_d = _diff_attention(qd, kd, vaugd, lam_p, diff_cache, batch, tokens_per_batch, tq_d, tk, items_d, lam_init)
    scan = _hgrn_scan(zr, lb_raw, scan_consts, s0, None if ctx_prev is None else ctx_prev[1], ctx_out, layer,
                      batch, tokens_per_batch)
    x_new = _out_projection(out_a, out_d, scan[0], scan[1], gates, x, gate, w_out_b, subln, hgn, bdd, ln_g, ln_b,
                            layer, lam_init)
    return x_new, ((outs[8:], scan[2]) if ctx_out else None)


def kernel(x_prompt, x_sample, cache_gqa_k, cache_gqa_v, cache_diff_k, cache_diff_v, state_hgrn, c, c_ctx,
           w_ada, b_ada, w_in, gqa_q_norm, gqa_k_norm, diff_lambda, diff_subln, hgrn_lower_bounds, hgrn_norm,
           w_out, ln_g, ln_b):
    batch, seq, _ = x_prompt.shape
    dec_batch, dec_seq, _ = x_sample.shape
    past = cache_gqa_k.shape[2]
    H, HD = HGRN_HEADS, HEAD_DIM

    cond = jnp.zeros((MOD_ROWS, D_MODEL), F32).at[0].set(c_ctx).at[1:1 + dec_batch].set(c)
    mod = _modulation(cond, w_ada, b_ada)
    consts = (_block_diag_ones(GQA_WIDTH), _block_diag_ones(LANES), _block_diag_ones(DIFF_WIDTH),
              _scan_constants())
    rope_tabs = _rope_tables(dec_seq, HEAD_DIM) + _rope_tables(dec_seq, DIFF_QK_DIM)
    w_in_b = w_in.astype(BF16)
    w_out_b = w_out.astype(BF16)

    def layer_weights(l):
        return (w_in_b, w_out_b,
                jnp.tile(gqa_q_norm[l], GQA_WIDTH // HD)[None, :] * (HD ** -0.5 * LOG2E),
                jnp.tile(gqa_k_norm[l], LANES // HD)[None, :],
                diff_lambda[l],
                jnp.tile(diff_subln[l], DIFF_WIDTH // HD)[None, :],
                jnp.tile(hgrn_norm[l], HGRN_WIDTH // HD)[None, :],
                hgrn_lower_bounds, ln_g[l][None, :], ln_b[l][None, :])

    def mod_rows(l, lo, n):
        m = mod[l, lo:lo + n]
        return (m[:, None, 0:D_MODEL], m[:, None, D_MODEL:2 * D_MODEL], m[:, None, 2 * D_MODEL:])

    x = x_prompt.reshape(batch * seq, D_MODEL)
    ctx = None
    for l in range(DEPTH):
        shift, scale, gate = mod_rows(l, 0, 1)
        x, ctx = _mixer_layer(x, scale, shift, gate, l, batch, seq, None, None, consts, layer_weights(l), True, ctx)
    y_prompt = x.reshape(batch, seq, D_MODEL)
    (kn, av, dk, dv), st = ctx
    new_leaves = [kn.reshape(batch, DEPTH, seq, GQA_KV_HEADS, HD), av.reshape(batch, DEPTH, seq, GQA_KV_HEADS, HD),
                  dk.reshape(batch, DEPTH, seq, DIFF_HEADS, HD), dv.reshape(batch, DEPTH, seq, DIFF_HEADS, HD),
                  jnp.swapaxes(st[..., :HD], -1, -2)]

    ones = jnp.ones((dec_batch, past, HD), F32)
    eye = jnp.eye(H, dtype=F32)
    x = x_sample.reshape(dec_batch * dec_seq, D_MODEL)
    for l in range(DEPTH):
        shift, scale, gate = mod_rows(l, 1, dec_batch)
        ck, cv = cache_gqa_k[:, l], cache_gqa_v[:, l]
        ckd = jnp.stack([jnp.concatenate([ck[:, :, g], ck[:, :, g]], axis=-1) for g in range(GQA_KV_HEADS)], axis=1)
        cvd = jnp.stack([jnp.concatenate([cv[:, :, g], ones], axis=-1) for g in range(GQA_KV_HEADS)], axis=1)
        dkc = cache_diff_k[:, l].reshape(dec_batch, past, DIFF_WIDTH)
        dvc = cache_diff_v[:, l]
        dvd = jnp.stack([jnp.concatenate([dvc[:, :, h], ones] if h % 2 == 0 else [ones, dvc[:, :, h]], axis=-1)
                         for h in range(DIFF_HEADS)], axis=1)
        s0t = jnp.swapaxes(state_hgrn[:, l].astype(F32), -1, -2)
        s0 = (s0t[:, :, :, :, None, :] * eye[None, None, :, None, :, None]).reshape(dec_batch, 2, H * HD, H * HD)
        cache = ((ckd.astype(BF16), cvd.astype(BF16)), (dkc.astype(BF16), dvd.astype(BF16)), s0)
        x, _ = _mixer_layer(x, scale, shift, gate, l, dec_batch, dec_seq, rope_tabs, cache, consts,
                            layer_weights(l), False, None)
    y_sample = x.reshape(dec_batch, dec_seq, D_MODEL)
    return (y_prompt, y_sample) + tuple(new_leaves)
```

```python
import functools
import math

import numpy as np
import jax
import jax.numpy as jnp
from jax import lax
from jax.experimental import pallas as pl
from jax.experimental.pallas import tpu as pltpu

F32 = jnp.float32
BF16 = jnp.bfloat16

D_MODEL = 1024
DEPTH = 2
GRID_W = 64
HEAD_DIM = 64
GQA_WIDTH = 512
GQA_KV_HEADS = 2
DIFF_WIDTH = 256
DIFF_HEADS = 4
DIFF_QK_DIM = 32
HGRN_WIDTH = 256
HGRN_HEADS = 4
IN_WIDTH = 3584
SCAN_CHUNK = 64
HGRN_LEVELS = 4
ROPE_THETA = 10000.0
RMS_EPS = 1e-6
LN_EPS = 1e-5
FORGET_MIN = 1e-6
DEEPNORM_ALPHA = (2 * DEPTH) ** 0.25
LANES = 128
SUBLANES = 8
MOD_ROWS = SUBLANES
VMEM_LIMIT = 48 * 1024 * 1024
SM_ROWS = 32
PROJ_ROWS = 512
SCAN_BLOCK = 512
LOG2E = math.log2(math.e)

_A_Q, _A_K, _A_V, _A_G = 0, 512, 640, 768
_D_Q, _D_K, _D_V, _D_G = 1280, 1536, 1792, 2048
_R_Z, _R_G = 2304, 3328

_NT = (((1,), (1,)), ((), ()))
_TN = (((0,), (0,)), ((), ()))


def _silu(x):
    return x * jax.nn.sigmoid(x)


def _params(sem, **kw):
    return pltpu.CompilerParams(dimension_semantics=sem, vmem_limit_bytes=VMEM_LIMIT, **kw)


def _mod_kernel(c_ref, w_ref, b_ref, o_ref):
    c = c_ref[...]
    o_ref[0] = jnp.dot(_silu(c), w_ref[0], preferred_element_type=F32) + b_ref[0]


def _modulation(cond, w_ada, b_ada):
    tn = 1024
    return pl.pallas_call(
        _mod_kernel,
        grid=(DEPTH, 3 * D_MODEL // tn),
        in_specs=[pl.BlockSpec((MOD_ROWS, D_MODEL), lambda l, j: (0, 0)),
                  pl.BlockSpec((1, D_MODEL, tn), lambda l, j: (l, 0, j)),
                  pl.BlockSpec((1, 1, tn), lambda l, j: (l, 0, j))],
        out_specs=pl.BlockSpec((1, MOD_ROWS, tn), lambda l, j: (l, 0, j)),
        out_shape=jax.ShapeDtypeStruct((DEPTH, MOD_ROWS, 3 * D_MODEL), F32),
        compiler_params=_params(("arbitrary", "arbitrary")),
        name="modulation",
    )(cond, w_ada, b_ada.reshape(DEPTH, 1, 3 * D_MODEL))


def _rope(x, c, s1, s2, shift):
    return x * c + pltpu.roll(x, LANES - shift, 1) * s1 + pltpu.roll(x, shift, 1) * s2


def _inproj_kernel(*refs, rope, ctx_out, ctx_alias):
    it = iter(refs)
    x_ref, sc_ref, sh_ref, w_ref, qg_ref, kg_ref, bdq_ref, bdk_ref = [next(it) for _ in range(8)]
    if rope:
        ca, s1a, s2a, cd, s1d, s2d = [next(it)[...] for _ in range(6)]
    if ctx_alias:
        for _ in range(4):
            next(it)
    qa_o, kdup_o, vaug_o, qd_o, kd_o, vaugd_o, gates_o, zr_o = [next(it) for _ in range(8)]
    if ctx_out:
        ctx_refs = [next(it) for _ in range(4)]
        if not ctx_alias:
            for r in ctx_refs:
                r[:, 1:] = jnp.zeros((r.shape[0], r.shape[1] - 1) + r.shape[2:], F32)

        def put_ctx(n, val):
            r = ctx_refs[n]
            r[:, 0] = val.reshape(r.shape[0], r.shape[2], r.shape[3])

    h = (x_ref[...] * (1.0 + sc_ref[0]) + sh_ref[0]).astype(BF16)

    def proj(lo, hi):
        return jnp.dot(h, w_ref[:, lo:hi], preferred_element_type=F32)

    tm = h.shape[0]
    low = lax.broadcasted_iota(jnp.int32, (tm, LANES), 1) < HEAD_DIM
    inv_d = 1.0 / HEAD_DIM

    aq = proj(_A_Q, _A_K)
    ssq = jnp.dot((aq * aq).astype(BF16), bdq_ref[...], preferred_element_type=F32)
    qn = aq * lax.rsqrt(ssq * inv_d + RMS_EPS) * qg_ref[...]
    for j in range(GQA_WIDTH // LANES):
        slab = qn[:, j * LANES:(j + 1) * LANES]
        if rope:
            slab = _rope(slab, ca, s1a, s2a, HEAD_DIM // 4)
        qa_o[:, j * LANES:(j + 1) * LANES] = slab.astype(BF16)

    ak = proj(_A_K, _A_V)
    ssqk = jnp.dot((ak * ak).astype(BF16), bdk_ref[...], preferred_element_type=F32)
    kn = ak * lax.rsqrt(ssqk * inv_d + RMS_EPS) * kg_ref[...]
    if ctx_out:
        put_ctx(0, kn)
    if rope:
        kn = _rope(kn, ca, s1a, s2a, HEAD_DIM // 4)
    kr = pltpu.roll(kn, HEAD_DIM, 1)
    kdup_o[0] = jnp.where(low, kn, kr).astype(BF16)
    kdup_o[1] = jnp.where(low, kr, kn).astype(BF16)

    av = proj(_A_V, _A_G)
    if ctx_out:
        put_ctx(1, av)
    vr = pltpu.roll(av, HEAD_DIM, 1)
    vaug_o[0] = jnp.where(low, av, 1.0).astype(BF16)
    vaug_o[1] = jnp.where(low, vr, 1.0).astype(BF16)

    gates_o[:, 0:GQA_WIDTH] = _silu(proj(_A_G, _D_Q)).astype(BF16)

    dq = proj(_D_Q, _D_K) * (DIFF_QK_DIM ** -0.5 * LOG2E)
    dk = proj(_D_K, _D_V)
    if ctx_out:
        put_ctx(2, dk)
    for j in range(DIFF_WIDTH // LANES):
        sq = dq[:, j * LANES:(j + 1) * LANES]
        sk = dk[:, j * LANES:(j + 1) * LANES]
        if rope:
            sq = _rope(sq, cd, s1d, s2d, DIFF_QK_DIM // 4)
            sk = _rope(sk, cd, s1d, s2d, DIFF_QK_DIM // 4)
        qd_o[:, j * LANES:(j + 1) * LANES] = sq.astype(BF16)
        kd_o[:, j * LANES:(j + 1) * LANES] = sk.astype(BF16)
    dv = proj(_D_V, _D_G)
    if ctx_out:
        put_ctx(3, dv)
    for j in range(DIFF_WIDTH // LANES):
        sv = dv[:, j * LANES:(j + 1) * LANES]
        vaugd_o[2 * j] = jnp.where(low, sv, 1.0).astype(BF16)
        vaugd_o[2 * j + 1] = jnp.where(low, 1.0, sv).astype(BF16)
    gates_o[:, GQA_WIDTH:GQA_WIDTH + DIFF_WIDTH] = _silu(proj(_D_G, _R_Z)).astype(BF16)

    gates_o[:, GQA_WIDTH + DIFF_WIDTH:] = _silu(proj(_R_G, IN_WIDTH)).astype(BF16)
    zr = proj(_R_Z, _R_G)
    zr_o[:, 0:HGRN_WIDTH] = _silu(zr[:, 0:HGRN_WIDTH])
    zr_o[:, HGRN_WIDTH:] = zr[:, HGRN_WIDTH:]


def _in_projection(x, scale, shift, w_in_b, qgain, kgain, bdq, bdk, rope_tabs, tokens_per_batch, ctx_out,
                   layer, ctx_prev):
    t_total = x.shape[0]
    tm = PROJ_ROWS
    nb_rows = tokens_per_batch // tm
    mod_rows = t_total // scale.shape[0] // tm
    rope = rope_tabs is not None
    assert not rope or nb_rows >= 1
    full = lambda shape: pl.BlockSpec(shape, lambda i: (0,) * len(shape))
    in_specs = [pl.BlockSpec((tm, D_MODEL), lambda i: (i, 0)),
                pl.BlockSpec((1, 1, D_MODEL), lambda i: (i // mod_rows, 0, 0)),
                pl.BlockSpec((1, 1, D_MODEL), lambda i: (i // mod_rows, 0, 0)),
                pl.BlockSpec((None, D_MODEL, IN_WIDTH), lambda i: (layer, 0, 0)),
                full((1, GQA_WIDTH)), full((1, LANES)),
                full((GQA_WIDTH, GQA_WIDTH)), full((LANES, LANES))]
    args = [x, scale, shift, w_in_b, qgain, kgain, bdq, bdk]
    if rope:
        in_specs += [pl.BlockSpec((tm, LANES), lambda i: (i % nb_rows, 0))] * 6
        args += list(rope_tabs)
    row = lambda w: pl.BlockSpec((tm, w), lambda i: (i, 0))
    stk = lambda n: pl.BlockSpec((n, tm, LANES), lambda i: (0, i, 0))
    out_specs = [row(GQA_WIDTH), stk(2), stk(2), row(DIFF_WIDTH), row(DIFF_WIDTH), stk(4),
                 row(D_MODEL), row(4 * HGRN_WIDTH)]
    out_shape = [jax.ShapeDtypeStruct((t_total, GQA_WIDTH), BF16),
                 jax.ShapeDtypeStruct((2, t_total, LANES), BF16),
                 jax.ShapeDtypeStruct((2, t_total, LANES), BF16),
                 jax.ShapeDtypeStruct((t_total, DIFF_WIDTH), BF16),
                 jax.ShapeDtypeStruct((t_total, DIFF_WIDTH), BF16),
                 jax.ShapeDtypeStruct((4, t_total, LANES), BF16),
                 jax.ShapeDtypeStruct((t_total, D_MODEL), BF16),
                 jax.ShapeDtypeStruct((t_total, 4 * HGRN_WIDTH), F32)]
    aliases = {}
    if ctx_out:
        seq = tokens_per_batch
        nseq = tm // seq
        assert nseq * seq == tm
        widths = (LANES, LANES, DIFF_WIDTH, DIFF_WIDTH)
        if ctx_prev is not None:
            aliases = {len(args) + n: len(out_specs) + n for n in range(4)}
            in_specs += [pl.BlockSpec(memory_space=pl.ANY)] * 4
            args += list(ctx_prev)
        if ctx_prev is None:
            assert layer == 0
            out_specs += [pl.BlockSpec((nseq, DEPTH, seq, w), lambda i: (i, 0, 0, 0)) for w in widths]
        else:
            out_specs += [pl.BlockSpec((nseq, 1, seq, w), lambda i: (i, layer, 0, 0)) for w in widths]
        out_shape += [jax.ShapeDtypeStruct((t_total // seq, DEPTH, seq, w), F32) for w in widths]
    return pl.pallas_call(
        functools.partial(_inproj_kernel, rope=rope, ctx_out=ctx_out, ctx_alias=bool(aliases)),
        grid=(t_total // tm,),
        in_specs=in_specs, out_specs=out_specs, out_shape=out_shape,
        input_output_aliases=aliases,
        compiler_params=_params(("parallel",)),
        name="in_projection",
    )(*args)


def _chunk_rows(j, tk):
    return pl.ds(j * tk, tk) if isinstance(j, int) else pl.ds(pl.multiple_of(j * tk, tk), tk)


def _attention_scratch(n_items, rows, tk):
    cols = n_items * rows
    return ([pltpu.VMEM((LANES, cols), BF16)]
            + [pltpu.VMEM((tk, cols), F32)] * 2 + [pltpu.VMEM((tk, cols), BF16)] * 2
            + [pltpu.VMEM((SUBLANES, cols), F32)] * 3
            + [pltpu.VMEM((LANES, cols), F32)])


def _item_scratch(refs, item, rows):
    return [r.at[:, pl.ds(item * rows, rows)] for r in refs]


def _softmax_stage(s_sc, p_sc, a_sc, m_sc, keys, first):
    for c in range(s_sc.shape[1] // LANES):
        cols = pl.ds(c * LANES, LANES)
        s = s_sc[0:keys, cols]
        m_cur = jnp.max(s, axis=0, keepdims=True)
        if first:
            m_new = jnp.broadcast_to(m_cur, (SUBLANES, LANES))
        else:
            m_prev = m_sc[:, cols]
            m_new = jnp.maximum(m_prev, m_cur)
            a_sc[:, cols] = jnp.exp2(m_prev - m_new)
        m_sc[:, cols] = m_new
        p_sc[0:keys, cols] = jnp.exp2(s - m_new[0:1, :]).astype(BF16)


def _rescale(a, acc):
    return jnp.concatenate([a] * (acc.shape[0] // SUBLANES), axis=0) * acc


def _transposed_bf16(x):
    return x.astype(F32).T.astype(BF16)


def _item_timeline(n_new, cache, k_at, v_at, start, qk, sm, av, finish):
    o = 0 if cache is None else 1
    n = n_new + o
    k_of = lambda c: cache[0]() if (o and c == 0) else k_at(c - o)
    v_of = lambda c: cache[1]() if (o and c == 0) else v_at(c - o)
    steps = []
    for t in range(n + 2):
        step = [start] if t == 0 else []
        if t < n:
            step.append(lambda t=t: qk(t % 2, k_of(t)))
        if 0 <= t - 1 < n:
            step.append(lambda t=t: sm((t - 1) % 2, t == 1))
        if 0 <= t - 2 < n:
            step.append(lambda t=t: av(t % 2, v_of(t - 2), t == 2))
        if t == n + 1:
            step.append(finish)
        steps.append(step)
    return steps


def _run_items(timelines):
    lag = max(1, len(timelines[0]) - 2)
    for g in range(lag * (len(timelines) - 1) + len(timelines[0])):
        for i, steps in enumerate(timelines):
            if 0 <= g - i * lag < len(steps):
                for stage in steps[g - i * lag]:
                    stage()


def _gqa_kernel(*refs, has_cache, tq, tk, nk, n_items, shared_kv):
    if has_cache:
        q_ref, kc_ref, vc_ref = refs[:3]
        refs = refs[3:]
    else:
        q_ref = refs[0]
        refs = refs[1:]
    k_ref, v_ref, o_ref = refs[:3]
    low = lax.broadcasted_iota(jnp.int32, (tq, LANES), 1) < HEAD_DIM
    rows = 2 * tq

    def item_stages(item):
        qs_sc, s0, s1, p0, p1, a0, a1, m_sc, acc_sc = _item_scratch(refs[3:], item, rows)
        s_b, p_b, a_b = (s0, s1), (p0, p1), (a0, a1)

        def start():
            q = q_ref[pl.ds(item * tq, tq), :]
            zero = jnp.zeros_like(q)
            qs_sc[:, 0:tq] = _transposed_bf16(jnp.where(low, q, zero))
            qs_sc[:, tq:] = _transposed_bf16(jnp.where(low, zero, q))

        keys = {}

        def qk(slot, k):
            keys[slot] = k.shape[0]
            s_b[slot][0:k.shape[0], :] = jnp.dot(k, qs_sc[...], preferred_element_type=F32)

        def sm(slot, first):
            _softmax_stage(s_b[slot], p_b[slot], a_b[slot], m_sc, keys[slot], first)

        def av(slot, v, first):
            pv = lax.dot_general(v, p_b[slot][0:v.shape[0], :], _TN, preferred_element_type=F32)
            acc_sc[...] = pv if first else _rescale(a_b[slot][...], acc_sc[...]) + pv

        def finish():
            acc = acc_sc[...]
            o_t = acc[0:HEAD_DIM, :] / acc[HEAD_DIM:, :]
            o_ref[pl.ds(item * tq, tq), :] = jnp.concatenate([o_t[:, 0:tq], o_t[:, tq:]], axis=0).T.astype(BF16)

        base = 0 if shared_kv else item * nk
        chunk = lambda ref: (lambda j: ref[0, _chunk_rows(base + j, tk), :])
        cache = (lambda: kc_ref[0, 0], lambda: vc_ref[0, 0]) if has_cache else None
        return _item_timeline(nk, cache, chunk(k_ref), chunk(v_ref), start, qk, sm, av, finish)

    _run_items([item_stages(item) for item in range(n_items)])


def _item_geometry(batch, tokens_per_batch, tq, n_items):
    nq = tokens_per_batch // tq
    if nq == 1:
        return batch // n_items, 1, n_items * tokens_per_batch, False
    assert nq % n_items == 0
    return batch, nq // n_items, tokens_per_batch, True


def _gqa_attention(qa, kdup, vaug, cache, batch, tokens_per_batch, tq, tk, n_items):
    t_total = qa.shape[0]
    nk = tokens_per_batch // tk
    has_cache = cache is not None
    bsteps, nq, kv_rows, shared_kv = _item_geometry(batch, tokens_per_batch, tq, n_items)
    assert shared_kv or not has_cache
    in_specs = [pl.BlockSpec((n_items * tq, LANES), lambda b, p, i: (b * nq + i, p))]
    args = [qa]
    if has_cache:
        past = cache[0].shape[2]
        in_specs += [pl.BlockSpec((1, 1, past, LANES), lambda b, p, i: (b, p // 2, 0, 0))] * 2
        args += list(cache)
    in_specs += [pl.BlockSpec((1, kv_rows, LANES), lambda b, p, i: (p // 2, b, 0))] * 2
    args += [kdup, vaug]
    return pl.pallas_call(
        functools.partial(_gqa_kernel, has_cache=has_cache, tq=tq, tk=tk, nk=nk, n_items=n_items,
                          shared_kv=shared_kv),
        grid=(bsteps, GQA_WIDTH // LANES, nq),
        in_specs=in_specs,
        out_specs=pl.BlockSpec((n_items * tq, LANES), lambda b, p, i: (b * nq + i, p)),
        out_shape=jax.ShapeDtypeStruct((t_total, GQA_WIDTH), BF16),
        scratch_shapes=_attention_scratch(n_items, 2 * tq, tk),
        compiler_params=_params(("parallel", "parallel", "parallel")),
        name="gqa_attention",
    )(*args)


def _diff_kernel(*refs, has_cache, tq, tk, nk, n_items, shared_kv, lam_init):
    q_ref, lp_ref = refs[:2]
    if has_cache:
        kc_ref, vc0_ref, vc1_ref = refs[2:5]
        refs = refs[5:]
    else:
        refs = refs[2:]
    k_ref, v0_ref, v1_ref, o_ref = refs[:4]
    lane = lax.broadcasted_iota(jnp.int32, (tq, LANES), 1)
    lp = lp_ref[...]
    lam = (jnp.exp(jnp.sum(lp[0:1] * lp[1:2], axis=1, keepdims=True))
           - jnp.exp(jnp.sum(lp[2:3] * lp[3:4], axis=1, keepdims=True)) + lam_init)
    h2 = 2 * tq
    rows = 4 * tq

    def item_stages(item):
        qs_sc, s0, s1, p0, p1, a0, a1, m_sc, acc_sc = _item_scratch(refs[4:], item, rows)
        s_b, p_b, a_b = (s0, s1), (p0, p1), (a0, a1)

        def start():
            q = q_ref[pl.ds(item * tq, tq), :]
            zero = jnp.zeros_like(q)
            for j in range(4):
                qs_sc[:, j * tq:(j + 1) * tq] = _transposed_bf16(jnp.where((lane >> 5) == j, q, zero))

        keys = {}

        def qk(slot, k):
            keys[slot] = k.shape[0]
            s_b[slot][0:k.shape[0], :] = jnp.dot(k, qs_sc[...], preferred_element_type=F32)

        def sm(slot, first):
            _softmax_stage(s_b[slot], p_b[slot], a_b[slot], m_sc, keys[slot], first)

        def av(slot, v, first):
            for half, vh in zip((pl.ds(0, h2), pl.ds(h2, h2)), v):
                pv = lax.dot_general(vh, p_b[slot][0:vh.shape[0], half], _TN, preferred_element_type=F32)
                acc_sc[:, half] = pv if first else _rescale(a_b[slot][:, half], acc_sc[:, half]) + pv

        def finish():
            n0 = acc_sc[0:HEAD_DIM, 0:h2] / acc_sc[HEAD_DIM:, 0:h2]
            n1 = acc_sc[HEAD_DIM:, h2:] / acc_sc[0:HEAD_DIM, h2:]
            o_t = jnp.concatenate([n0[:, 0:tq] - lam * n0[:, tq:], n1[:, 0:tq] - lam * n1[:, tq:]], axis=0)
            o_ref[pl.ds(item * tq, tq), :] = o_t.T.astype(BF16)

        base = 0 if shared_kv else item * nk
        k_at = lambda j: k_ref[_chunk_rows(base + j, tk), :]
        v_at = lambda j: (v0_ref[0, _chunk_rows(base + j, tk), :], v1_ref[0, _chunk_rows(base + j, tk), :])
        cache = (lambda: kc_ref[0], lambda: (vc0_ref[0, 0], vc1_ref[0, 0])) if has_cache else None
        return _item_timeline(nk, cache, k_at, v_at, start, qk, sm, av, finish)

    _run_items([item_stages(item) for item in range(n_items)])


def _diff_attention(qd, kd, vaugd, lam_p, cache, batch, tokens_per_batch, tq, tk, n_items, lam_init):
    t_total = qd.shape[0]
    nk = tokens_per_batch // tk
    has_cache = cache is not None
    bsteps, nq, kv_rows, shared_kv = _item_geometry(batch, tokens_per_batch, tq, n_items)
    assert shared_kv or not has_cache
    in_specs = [pl.BlockSpec((n_items * tq, LANES), lambda b, p, i: (b * nq + i, p)),
                pl.BlockSpec((4, DIFF_QK_DIM), lambda b, p, i: (0, 0))]
    args = [qd, lam_p]
    if has_cache:
        ckd, cvd = cache
        past = ckd.shape[1]
        in_specs += [pl.BlockSpec((1, past, LANES), lambda b, p, i: (b, 0, p)),
                     pl.BlockSpec((1, 1, past, LANES), lambda b, p, i: (b, 2 * p, 0, 0)),
                     pl.BlockSpec((1, 1, past, LANES), lambda b, p, i: (b, 2 * p + 1, 0, 0))]
        args += [ckd, cvd, cvd]
    in_specs += [pl.BlockSpec((kv_rows, LANES), lambda b, p, i: (b, p)),
                 pl.BlockSpec((1, kv_rows, LANES), lambda b, p, i: (2 * p, b, 0)),
                 pl.BlockSpec((1, kv_rows, LANES), lambda b, p, i: (2 * p + 1, b, 0))]
    args += [kd, vaugd, vaugd]
    return pl.pallas_call(
        functools.partial(_diff_kernel, has_cache=has_cache, tq=tq, tk=tk, nk=nk, n_items=n_items,
                          shared_kv=shared_kv, lam_init=lam_init),
        grid=(bsteps, DIFF_WIDTH // LANES, nq),
        in_specs=in_specs,
        out_specs=pl.BlockSpec((n_items * tq, LANES), lambda b, p, i: (b * nq + i, p)),
        out_shape=jax.ShapeDtypeStruct((t_total, DIFF_WIDTH), BF16),
        scratch_shapes=_attention_scratch(n_items, 4 * tq, tk),
        compiler_params=_params(("parallel", "parallel", "parallel")),
        name="diff_attention",
    )(*args)


def _hgrn_gates(qs, z, v, lb, g_sc, tri):
    sig = jax.nn.sigmoid(z)
    om = 1.0 - lb
    k = om * (1.0 - sig)
    g = jnp.log2(jnp.maximum(lb + om * sig, FORGET_MIN))
    g1 = g.astype(BF16)
    e1 = g - g1.astype(F32)
    g2 = e1.astype(BF16)
    g3 = (e1 - g2.astype(F32)).astype(BF16)
    G = (jnp.dot(tri, g1, preferred_element_type=F32) + jnp.dot(tri, g2, preferred_element_type=F32)
         + jnp.dot(tri, g3, preferred_element_type=F32))
    g_sc[...] = G
    return qs, k, G, v.astype(BF16)


def _hgrn_operands(gates, g_sc, hmask, rev):
    C, W = SCAN_CHUNK, HGRN_WIDTH
    qs, k, G, vb = gates
    g_last = g_sc[pl.ds(0 if rev else C - 1, 1), :]
    q_in = (qs * jnp.exp2(G)).astype(BF16)
    k_end = (k * jnp.exp2(g_last - G)).astype(BF16)

    def block_diag(a):
        return jnp.concatenate([a] * HGRN_HEADS, axis=0) * hmask

    def ref_rows(block, off):
        return jnp.concatenate([jnp.broadcast_to(g_sc[pl.ds(b0 + off, 1), :], (block, W))
                                for b0 in range(0, C, block)], axis=0)

    levels = []
    for level in range(HGRN_LEVELS):
        block = 8 << level
        half = block // 2
        if level == 0:
            r = ref_rows(block, half)
            qe, ke = G - r, r - G
        else:
            r = ref_rows(block, half if rev else half - 1)
            qe, ke = jnp.minimum(G - r, 0.0), jnp.minimum(r - G, 0.0)
        levels.append(((qs * jnp.exp2(qe)).astype(BF16), block_diag((k * jnp.exp2(ke)).astype(BF16))))
    return q_in, jnp.exp2(g_last), k_end, vb, block_diag(vb), levels


def _hgrn_intra(operands, lmask_ref, bdmask, rev):
    q_in, decay, k_end, vb, v_bd, levels = operands
    ut = lax.dot_general(vb, k_end, _TN, preferred_element_type=F32) * bdmask
    a = None
    for level, (ql, kl) in enumerate(levels):
        al = lax.dot_general(ql, kl, _NT, preferred_element_type=F32)
        m = lmask_ref[rev, level]
        a = jnp.where(m > 0.5, al, 0.0) if level == 0 else a + al * m
    intra = jnp.dot(a.astype(BF16), v_bd, preferred_element_type=F32)
    return q_in, decay, ut, intra


def _hgrn_state(parts, st):
    q_in, decay, ut, intra = parts
    out = lax.dot_general(q_in, st.astype(BF16), _NT, preferred_element_type=F32) + intra
    return out, st * decay + ut


def _hgrn_kernel(*refs, layer, n_chunks, n_seq, has_s0, state_out, state_alias):
    it = iter(refs)
    qf, zf, vf, qb, zb, vb, lbraw, tri_ref, lmask_ref, hmask_ref, bdmask_ref = [next(it) for _ in range(11)]
    s0 = next(it) if has_s0 else None
    if state_alias:
        next(it)
    of_ref, ob_ref = next(it), next(it)
    sto_ref = next(it) if state_out else None
    st_ref, g_sc = next(it), next(it)

    @pl.when(pl.program_id(1) == 0)
    def _():
        if has_s0:
            st_ref[...] = s0[0]
        else:
            st_ref[...] = jnp.zeros(st_ref.shape, F32)

    lbr = lbraw[...]
    ex = jnp.exp(lbr - jnp.max(lbr, axis=0, keepdims=True))
    sm = ex / jnp.sum(ex, axis=0, keepdims=True)
    lbs = jnp.zeros(sm.shape[1:], F32)
    for j in range(1, layer + 1):
        lbs = lbs + sm[j]
    lb_f, lb_b = lbs[0:1], lbs[1:2]

    hmask = hmask_ref[...]
    bdmask = bdmask_ref[...]

    chunks = []
    for cidx in range(n_chunks):
        for s in range(n_seq):
            base = s * n_chunks
            chunks.append((2 * s, 0, pl.ds((base + cidx) * SCAN_CHUNK, SCAN_CHUNK), qf, zf, vf, lb_f, of_ref))
            chunks.append((2 * s + 1, 1, pl.ds((base + n_chunks - 1 - cidx) * SCAN_CHUNK, SCAN_CHUNK),
                           qb, zb, vb, lb_b, ob_ref))
    n = len(chunks)
    state = [st_ref[i] for i in range(2 * n_seq)]
    gates, operands, parts = {}, {}, {}
    for t in range(n + 3):
        if t < n:
            _, d, rows, q_r, z_r, v_r, lb, _ = chunks[t]
            gates[t] = _hgrn_gates(q_r[rows, :], z_r[rows, :], v_r[rows, :], lb, g_sc.at[t], tri_ref[d])
        if 0 <= t - 1 < n:
            operands[t - 1] = _hgrn_operands(gates.pop(t - 1), g_sc.at[t - 1], hmask, chunks[t - 1][1])
        if 0 <= t - 2 < n:
            parts[t - 2] = _hgrn_intra(operands.pop(t - 2), lmask_ref, bdmask, chunks[t - 2][1])
        if 0 <= t - 3 < n:
            si, _, rows, _, _, _, _, o_r = chunks[t - 3]
            out, state[si] = _hgrn_state(parts.pop(t - 3), state[si])
            o_r[rows, :] = out.astype(BF16)
    for i, s in enumerate(state):
        st_ref[i] = s

    if state_out:
        @pl.when(pl.program_id(1) == pl.num_programs(1) - 1)
        def _():
            if not state_alias:
                sto_ref[:, 1:] = jnp.zeros((sto_ref.shape[0], sto_ref.shape[1] - 1) + sto_ref.shape[2:], F32)
            for i, s in enumerate(state):
                for hh in range(HGRN_HEADS):
                    slab = s[hh * HEAD_DIM:(hh + 1) * HEAD_DIM, (hh // 2) * LANES:(hh // 2 + 1) * LANES]
                    sto_ref[i // 2, 0, i % 2, hh] = pltpu.roll(slab, HEAD_DIM, 1) if hh % 2 else slab


def _hgrn_scan(zr, lb_raw, scan_consts, s0, state_prev, state_out, layer, batch, tokens_per_batch):
    t_total = zr.shape[0]
    tb = min(SCAN_BLOCK, tokens_per_batch)
    nblk = tokens_per_batch // tb
    n_seq = SCAN_BLOCK // tb if nblk == 1 else 1
    W = HGRN_WIDTH
    has_s0 = s0 is not None
    assert n_seq == 1 or not has_s0
    state_alias = state_prev is not None
    rows = n_seq * tb
    fblk = lambda col: pl.BlockSpec((rows, W), lambda b, j: (b * nblk + j, col))
    bblk = lambda col: pl.BlockSpec((rows, W), lambda b, j: (b * nblk + nblk - 1 - j, col))
    full = lambda a: pl.BlockSpec(a.shape, lambda b, j: (0,) * a.ndim)
    in_specs = [fblk(0), fblk(1), fblk(3), bblk(0), bblk(2), bblk(3),
                pl.BlockSpec((DEPTH, 2, W), lambda b, j: (0, 0, 0))] + [full(a) for a in scan_consts]
    args = [zr] * 6 + [lb_raw] + list(scan_consts)
    if has_s0:
        in_specs.append(pl.BlockSpec((1, 2, W, W), lambda b, j: (b, 0, 0, 0)))
        args.append(s0)
    aliases = {}
    if state_alias:
        aliases = {len(args): 2}
        in_specs.append(pl.BlockSpec(memory_space=pl.ANY))
        args.append(state_prev)
    out_specs = [pl.BlockSpec((rows, W), lambda b, j: (b * nblk + j, 0)),
                 pl.BlockSpec((rows, W), lambda b, j: (b * nblk + nblk - 1 - j, 0))]
    out_shape = [jax.ShapeDtypeStruct((t_total, W), BF16), jax.ShapeDtypeStruct((t_total, W), BF16)]
    if state_out and not state_alias:
        assert layer == 0
        out_specs.append(pl.BlockSpec((n_seq, DEPTH, 2, HGRN_HEADS, HEAD_DIM, LANES),
                                      lambda b, j: (b, 0, 0, 0, 0, 0)))
    elif state_out:
        out_specs.append(pl.BlockSpec((n_seq, 1, 2, HGRN_HEADS, HEAD_DIM, LANES),
                                      lambda b, j: (b, layer, 0, 0, 0, 0)))
    if state_out:
        out_shape.append(jax.ShapeDtypeStruct((batch, DEPTH, 2, HGRN_HEADS, HEAD_DIM, LANES), F32))
    n_units = 2 * rows // SCAN_CHUNK
    return pl.pallas_call(
        functools.partial(_hgrn_kernel, layer=layer, n_chunks=tb // SCAN_CHUNK, n_seq=n_seq, has_s0=has_s0,
                          state_out=state_out, state_alias=state_alias),
        grid=(batch // n_seq, nblk),
        in_specs=in_specs, out_specs=out_specs, out_shape=out_shape,
        input_output_aliases=aliases,
        scratch_shapes=[pltpu.VMEM((2 * n_seq, W, W), F32), pltpu.VMEM((n_units, SCAN_CHUNK, W), F32)],
        compiler_params=_params(("parallel", "arbitrary")),
        name="hgrn_scan",
    )(*args)


def _outproj_kernel(oa_ref, od_ref, of_ref, ob_ref, g_ref, x_ref, gate_ref, w_ref, sub_ref, hgn_ref,
                    bd_ref, lng_ref, lnb_ref, o_ref, *, lam_init):
    inv_d = 1.0 / HEAD_DIM
    g = g_ref[...].astype(F32)
    bd = bd_ref[...]

    def head_rms(t, gain):
        ssq = jnp.dot((t * t).astype(BF16), bd, preferred_element_type=F32)
        return t * lax.rsqrt(ssq * inv_d + RMS_EPS) * gain

    ma = (oa_ref[...].astype(F32) * g[:, 0:GQA_WIDTH]).astype(BF16)
    md = (head_rms(od_ref[...].astype(F32), sub_ref[...]) * (1.0 - lam_init)
          * g[:, GQA_WIDTH:GQA_WIDTH + DIFF_WIDTH]).astype(BF16)
    mr = (head_rms(of_ref[...].astype(F32) + ob_ref[...].astype(F32), hgn_ref[...])
          * g[:, GQA_WIDTH + DIFF_WIDTH:]).astype(BF16)
    y = (jnp.dot(ma, w_ref[0:GQA_WIDTH, :], preferred_element_type=F32)
         + jnp.dot(md, w_ref[GQA_WIDTH:GQA_WIDTH + DIFF_WIDTH, :], preferred_element_type=F32)
         + jnp.dot(mr, w_ref[GQA_WIDTH + DIFF_WIDTH:, :], preferred_element_type=F32))
    xn = DEEPNORM_ALPHA * x_ref[...] + gate_ref[0] * y
    mu = jnp.mean(xn, axis=-1, keepdims=True)
    xc = xn - mu
    var = jnp.mean(xc * xc, axis=-1, keepdims=True)
    o_ref[...] = xc * lax.rsqrt(var + LN_EPS) * lng_ref[...] + lnb_ref[...]


def _out_projection(out_a, out_d, o_f, o_b, gates, x, gate, w_out_b, subln, hgn, bd, ln_g, ln_b,
                    layer, lam_init):
    t_total = x.shape[0]
    tm = PROJ_ROWS
    mod_rows = t_total // gate.shape[0] // tm
    row = lambda w: pl.BlockSpec((tm, w), lambda i: (i, 0))
    full = lambda shape: pl.BlockSpec(shape, lambda i: (0,) * len(shape))
    return pl.pallas_call(
        functools.partial(_outproj_kernel, lam_init=lam_init),
        grid=(t_total // tm,),
        in_specs=[row(GQA_WIDTH), row(DIFF_WIDTH), row(HGRN_WIDTH), row(HGRN_WIDTH), row(D_MODEL), row(D_MODEL),
                  pl.BlockSpec((1, 1, D_MODEL), lambda i: (i // mod_rows, 0, 0)),
                  pl.BlockSpec((None, D_MODEL, D_MODEL), lambda i: (layer, 0, 0)),
                  full((1, DIFF_WIDTH)), full((1, HGRN_WIDTH)),
                  full((DIFF_WIDTH, DIFF_WIDTH)), full((1, D_MODEL)), full((1, D_MODEL))],
        out_specs=row(D_MODEL),
        out_shape=jax.ShapeDtypeStruct((t_total, D_MODEL), F32),
        compiler_params=_params(("parallel",)),
        name="out_projection",
    )(out_a, out_d, o_f, o_b, gates, x, gate, w_out_b, subln, hgn, bd, ln_g, ln_b)


def _block_diag_ones(n):
    idx = np.arange(n) // HEAD_DIM
    return jnp.asarray((idx[:, None] == idx[None, :]).astype(np.float32), dtype=BF16)


def _scan_constants():
    C, W, H = SCAN_CHUNK, HGRN_WIDTH, HGRN_HEADS
    t = np.arange(C)[:, None]
    u = np.arange(C)[None, :]
    tri = np.stack([u <= t, u >= t])
    s = (np.arange(W) % C)[None, :]
    lmask = np.zeros((2, HGRN_LEVELS, C, W), np.float32)
    for level in range(HGRN_LEVELS):
        block = 8 << level
        same = (t // block) == (s // block)
        if level == 0:
            lmask[0, level] = same & (s <= t)
            lmask[1, level] = same & (s >= t)
        else:
            t_hi, s_hi = (t % block) >= block // 2, (s % block) >= block // 2
            lmask[0, level] = same & t_hi & ~s_hi
            lmask[1, level] = same & ~t_hi & s_hi
    rows = np.arange(H * C)[:, None]
    hmask = (rows // C) == (np.arange(W)[None, :] // HEAD_DIM)
    bdmask = (np.arange(W)[:, None] // HEAD_DIM) == (np.arange(W)[None, :] // HEAD_DIM)
    return (jnp.asarray(tri.astype(np.float32), dtype=BF16), jnp.asarray(lmask),
            jnp.asarray(hmask.astype(np.float32), dtype=BF16), jnp.asarray(bdmask.astype(np.float32)))


def _rope_tables(n_tokens, dim):
    rows = n_tokens // GRID_W
    row = jnp.repeat(jnp.arange(rows, dtype=F32), GRID_W)
    col = jnp.tile(jnp.arange(GRID_W, dtype=F32), rows)
    quarter = dim // 4
    inv_freq = ROPE_THETA ** (-jnp.arange(quarter, dtype=F32) / quarter)
    ar = row[:, None] * inv_freq[None, :]
    ac = col[:, None] * inv_freq[None, :]
    ang = jnp.concatenate([ar, ar, ac, ac], axis=-1)
    cos = jnp.tile(jnp.cos(ang), (1, LANES // dim))
    sin = jnp.tile(jnp.sin(ang), (1, LANES // dim))
    first = (jnp.arange(LANES) % (dim // 2)) < quarter
    return cos, jnp.where(first, -sin, 0.0), jnp.where(first, 0.0, sin)


def _mixer_layer(x, scale, shift, gate, layer, batch, tokens_per_batch, rope_tabs, cache, consts, weights,
                 ctx_out, ctx_prev):
    (w_in_b, w_out_b, qgain, kgain, lam_p, subln, hgn, lb_raw, ln_g, ln_b) = weights
    bdq, bdk, bdd, scan_consts = consts
    lam_init = 0.8 - 0.6 * math.exp(-0.3 * layer)
    outs = _in_projection(x, scale, shift, w_in_b, qgain, kgain, bdq, bdk, rope_tabs, tokens_per_batch, ctx_out,
                          layer, None if ctx_prev is None else ctx_prev[0])
    qa, kdup, vaug, qd, kd, vaugd, gates, zr = outs[:8]
    if cache is None:
        gqa_cache = diff_cache = s0 = None
        tq_a = tq_d = tk = tokens_per_batch
        items_a, items_d = 2, 2
    else:
        gqa_cache, diff_cache, s0 = cache
        tq_a, tq_d, tk = 512, 256, 512
        items_a = items_d = 1
    out_a = _gqa_attention(qa, kdup, vaug, gqa_cache, batch, tokens_per_batch, tq_a, tk, items_a)
    out_d = _diff_attention(qd, kd, vaugd, lam_p, diff_cache, batch, tokens_per_batch, tq_d, tk, items_d, lam_init)
    scan = _hgrn_scan(zr, lb_raw, scan_consts, s0, None if ctx_prev is None else ctx_prev[1], ctx_out, layer,
                      batch, tokens_per_batch)
    x_new = _out_projection(out_a, out_d, scan[0], scan[1], gates, x, gate, w_out_b, subln, hgn, bdd, ln_g, ln_b,
                            layer, lam_init)
    return x_new, ((outs[8:], scan[2]) if ctx_out else None)


def kernel(x_prompt, x_sample, cache_gqa_k, cache_gqa_v, cache_diff_k, cache_diff_v, state_hgrn, c, c_ctx,
           w_ada, b_ada, w_in, gqa_q_norm, gqa_k_norm, diff_lambda, diff_subln, hgrn_lower_bounds, hgrn_norm,
           w_out, ln_g, ln_b):
    batch, seq, _ = x_prompt.shape
    dec_batch, dec_seq, _ = x_sample.shape
    past = cache_gqa_k.shape[2]
    H, HD = HGRN_HEADS, HEAD_DIM

    cond = jnp.zeros((MOD_ROWS, D_MODEL), F32).at[0].set(c_ctx).at[1:1 + dec_batch].set(c)
    mod = _modulation(cond, w_ada, b_ada)
    consts = (_block_diag_ones(GQA_WIDTH), _block_diag_ones(LANES), _block_diag_ones(DIFF_WIDTH),
              _scan_constants())
    rope_tabs = _rope_tables(dec_seq, HEAD_DIM) + _rope_tables(dec_seq, DIFF_QK_DIM)
    w_in_b = w_in.astype(BF16)
    w_out_b = w_out.astype(BF16)

    def layer_weights(l):
        return (w_in_b, w_out_b,
                jnp.tile(gqa_q_norm[l], GQA_WIDTH // HD)[None, :] * (HD ** -0.5 * LOG2E),
                jnp.tile(gqa_k_norm[l], LANES // HD)[None, :],
                diff_lambda[l],
                jnp.tile(diff_subln[l], DIFF_WIDTH // HD)[None, :],
                jnp.tile(hgrn_norm[l], HGRN_WIDTH // HD)[None, :],
                hgrn_lower_bounds, ln_g[l][None, :], ln_b[l][None, :])

    def mod_rows(l, lo, n):
        m = mod[l, lo:lo + n]
        return (m[:, None, 0:D_MODEL], m[:, None, D_MODEL:2 * D_MODEL], m[:, None, 2 * D_MODEL:])

    x = x_prompt.reshape(batch * seq, D_MODEL)
    ctx = None
    for l in range(DEPTH):
        shift, scale, gate = mod_rows(l, 0, 1)
        x, ctx = _mixer_layer(x, scale, shift, gate, l, batch, seq, None, None, consts, layer_weights(l), True, ctx)
    y_prompt = x.reshape(batch, seq, D_MODEL)
    (kn, av, dk, dv), st = ctx
    new_leaves = [kn.reshape(batch, DEPTH, seq, GQA_KV_HEADS, HD), av.reshape(batch, DEPTH, seq, GQA_KV_HEADS, HD),
                  dk.reshape(batch, DEPTH, seq, DIFF_HEADS, HD), dv.reshape(batch, DEPTH, seq, DIFF_HEADS, HD),
                  jnp.swapaxes(st[..., :HD], -1, -2)]

    ones = jnp.ones((dec_batch, past, HD), F32)
    eye = jnp.eye(H, dtype=F32)
    x = x_sample.reshape(dec_batch * dec_seq, D_MODEL)
    for l in range(DEPTH):
        shift, scale, gate = mod_rows(l, 1, dec_batch)
        ck, cv = cache_gqa_k[:, l], cache_gqa_v[:, l]
        ckd = jnp.stack([jnp.concatenate([ck[:, :, g], ck[:, :, g]], axis=-1) for g in range(GQA_KV_HEADS)], axis=1)
        cvd = jnp.stack([jnp.concatenate([cv[:, :, g], ones], axis=-1) for g in range(GQA_KV_HEADS)], axis=1)
        dkc = cache_diff_k[:, l].reshape(dec_batch, past, DIFF_WIDTH)
        dvc = cache_diff_v[:, l]
        dvd = jnp.stack([jnp.concatenate([dvc[:, :, h], ones] if h % 2 == 0 else [ones, dvc[:, :, h]], axis=-1)
                         for h in range(DIFF_HEADS)], axis=1)
        s0t = jnp.swapaxes(state_hgrn[:, l].astype(F32), -1, -2)
        s0 = (s0t[:, :, :, :, None, :] * eye[None, None, :, None, :, None]).reshape(dec_batch, 2, H * HD, H * HD)
        cache = ((ckd.astype(BF16), cvd.astype(BF16)), (dkc.astype(BF16), dvd.astype(BF16)), s0)
        x, _ = _mixer_layer(x, scale, shift, gate, l, dec_batch, dec_seq, rope_tabs, cache, consts,
                            layer_weights(l), False, None)
    y_sample = x.reshape(dec_batch, dec_seq, D_MODEL)
    return (y_prompt, y_sample) + tuple(new_leaves)
```

```python
import functools
import math

import numpy as np
import jax
import jax.numpy as jnp
from jax import lax
from jax.experimental import pallas as pl
from jax.experimental.pallas import tpu as pltpu

F32 = jnp.float32
BF16 = jnp.bfloat16

D_MODEL = 1024
DEPTH = 2
GRID_W = 64
HEAD_DIM = 64
GQA_WIDTH = 512
GQA_KV_HEADS = 2
DIFF_WIDTH = 256
DIFF_HEADS = 4
DIFF_QK_DIM = 32
HGRN_WIDTH = 256
HGRN_HEADS = 4
IN_WIDTH = 3584
SCAN_CHUNK = 64
HGRN_LEVELS = 4
ROPE_THETA = 10000.0
RMS_EPS = 1e-6
LN_EPS = 1e-5
FORGET_MIN = 1e-6
DEEPNORM_ALPHA = (2 * DEPTH) ** 0.25
LANES = 128
SUBLANES = 8
MOD_ROWS = SUBLANES
VMEM_LIMIT = 48 * 1024 * 1024
SM_ROWS = 32
PROJ_ROWS = 512
SCAN_BLOCK = 512
LOG2E = math.log2(math.e)

_A_Q, _A_K, _A_V, _A_G = 0, 512, 640, 768
_D_Q, _D_K, _D_V, _D_G = 1280, 1536, 1792, 2048
_R_Z, _R_G = 2304, 3328

_NT = (((1,), (1,)), ((), ()))
_TN = (((0,), (0,)), ((), ()))


def _silu(x):
    return x * jax.nn.sigmoid(x)


def _params(sem):
    return pltpu.CompilerParams(dimension_semantics=sem, vmem_limit_bytes=VMEM_LIMIT)


def _mod_kernel(c_ref, w_ref, b_ref, o_ref):
    c = c_ref[...]
    o_ref[0] = jnp.dot(_silu(c), w_ref[0], preferred_element_type=F32) + b_ref[0]


def _modulation(cond, w_ada, b_ada):
    tn = 1024
    return pl.pallas_call(
        _mod_kernel,
        grid=(DEPTH, 3 * D_MODEL // tn),
        in_specs=[pl.BlockSpec((MOD_ROWS, D_MODEL), lambda l, j: (0, 0)),
                  pl.BlockSpec((1, D_MODEL, tn), lambda l, j: (l, 0, j)),
                  pl.BlockSpec((1, 1, tn), lambda l, j: (l, 0, j))],
        out_specs=pl.BlockSpec((1, MOD_ROWS, tn), lambda l, j: (l, 0, j)),
        out_shape=jax.ShapeDtypeStruct((DEPTH, MOD_ROWS, 3 * D_MODEL), F32),
        compiler_params=_params(("arbitrary", "arbitrary")),
        name="modulation",
    )(cond, w_ada, b_ada.reshape(DEPTH, 1, 3 * D_MODEL))


def _rope(x, c, s1, s2, shift):
    return x * c + pltpu.roll(x, LANES - shift, 1) * s1 + pltpu.roll(x, shift, 1) * s2


def _inproj_kernel(*refs, rope, ctx_out, ctx_alias):
    it = iter(refs)
    x_ref, sc_ref, sh_ref, w_ref, qg_ref, kg_ref, bdq_ref, bdk_ref = [next(it) for _ in range(8)]
    if rope:
        ca, s1a, s2a, cd, s1d, s2d = [next(it)[...] for _ in range(6)]
    if ctx_alias:
        for _ in range(4):
            next(it)
    qa_o, kdup_o, vaug_o, qd_o, kd_o, vaugd_o, gates_o, zr_o = [next(it) for _ in range(8)]
    if ctx_out:
        ctx_refs = [next(it) for _ in range(4)]
        if not ctx_alias:
            for r in ctx_refs:
                r[:, 1:] = jnp.zeros((r.shape[0], r.shape[1] - 1) + r.shape[2:], F32)

        def put_ctx(n, val):
            r = ctx_refs[n]
            r[:, 0] = val.reshape(r.shape[0], r.shape[2], r.shape[3])

    h = (x_ref[...] * (1.0 + sc_ref[0]) + sh_ref[0]).astype(BF16)

    def proj(lo, hi):
        return jnp.dot(h, w_ref[:, lo:hi], preferred_element_type=F32)

    tm = h.shape[0]
    low = lax.broadcasted_iota(jnp.int32, (tm, LANES), 1) < HEAD_DIM
    inv_d = 1.0 / HEAD_DIM

    aq = proj(_A_Q, _A_K)
    ssq = jnp.dot((aq * aq).astype(BF16), bdq_ref[...], preferred_element_type=F32)
    qn = aq * lax.rsqrt(ssq * inv_d + RMS_EPS) * qg_ref[...]
    for j in range(GQA_WIDTH // LANES):
        slab = qn[:, j * LANES:(j + 1) * LANES]
        if rope:
            slab = _rope(slab, ca, s1a, s2a, HEAD_DIM // 4)
        qa_o[:, j * LANES:(j + 1) * LANES] = slab.astype(BF16)

    ak = proj(_A_K, _A_V)
    ssqk = jnp.dot((ak * ak).astype(BF16), bdk_ref[...], preferred_element_type=F32)
    kn = ak * lax.rsqrt(ssqk * inv_d + RMS_EPS) * kg_ref[...]
    if ctx_out:
        put_ctx(0, kn)
    if rope:
        kn = _rope(kn, ca, s1a, s2a, HEAD_DIM // 4)
    kr = pltpu.roll(kn, HEAD_DIM, 1)
    kdup_o[0] = jnp.where(low, kn, kr).astype(BF16)
    kdup_o[1] = jnp.where(low, kr, kn).astype(BF16)

    av = proj(_A_V, _A_G)
    if ctx_out:
        put_ctx(1, av)
    vr = pltpu.roll(av, HEAD_DIM, 1)
    vaug_o[0] = jnp.where(low, av, 1.0).astype(BF16)
    vaug_o[1] = jnp.where(low, vr, 1.0).astype(BF16)

    gates_o[:, 0:GQA_WIDTH] = _silu(proj(_A_G, _D_Q)).astype(BF16)

    dq = proj(_D_Q, _D_K) * (DIFF_QK_DIM ** -0.5 * LOG2E)
    dk = proj(_D_K, _D_V)
    if ctx_out:
        put_ctx(2, dk)
    for j in range(DIFF_WIDTH // LANES):
        sq = dq[:, j * LANES:(j + 1) * LANES]
        sk = dk[:, j * LANES:(j + 1) * LANES]
        if rope:
            sq = _rope(sq, cd, s1d, s2d, DIFF_QK_DIM // 4)
            sk = _rope(sk, cd, s1d, s2d, DIFF_QK_DIM // 4)
        qd_o[:, j * LANES:(j + 1) * LANES] = sq.astype(BF16)
        kd_o[:, j * LANES:(j + 1) * LANES] = sk.astype(BF16)
    dv = proj(_D_V, _D_G)
    if ctx_out:
        put_ctx(3, dv)
    for j in range(DIFF_WIDTH // LANES):
        sv = dv[:, j * LANES:(j + 1) * LANES]
        vaugd_o[2 * j] = jnp.where(low, sv, 1.0).astype(BF16)
        vaugd_o[2 * j + 1] = jnp.where(low, 1.0, sv).astype(BF16)
    gates_o[:, GQA_WIDTH:GQA_WIDTH + DIFF_WIDTH] = _silu(proj(_D_G, _R_Z)).astype(BF16)

    gates_o[:, GQA_WIDTH + DIFF_WIDTH:] = _silu(proj(_R_G, IN_WIDTH)).astype(BF16)
    zr = proj(_R_Z, _R_G)
    zr_o[:, 0:HGRN_WIDTH] = _silu(zr[:, 0:HGRN_WIDTH])
    zr_o[:, HGRN_WIDTH:] = zr[:, HGRN_WIDTH:]


def _in_projection(x, scale, shift, w_in_b, qgain, kgain, bdq, bdk, rope_tabs, tokens_per_batch, ctx_out,
                   layer, ctx_prev):
    t_total = x.shape[0]
    tm = PROJ_ROWS
    nb_rows = tokens_per_batch // tm
    mod_rows = t_total // scale.shape[0] // tm
    rope = rope_tabs is not None
    assert not rope or nb_rows >= 1
    full = lambda shape: pl.BlockSpec(shape, lambda i: (0,) * len(shape))
    in_specs = [pl.BlockSpec((tm, D_MODEL), lambda i: (i, 0)),
                pl.BlockSpec((1, 1, D_MODEL), lambda i: (i // mod_rows, 0, 0)),
                pl.BlockSpec((1, 1, D_MODEL), lambda i: (i // mod_rows, 0, 0)),
                pl.BlockSpec((None, D_MODEL, IN_WIDTH), lambda i: (layer, 0, 0)),
                full((1, GQA_WIDTH)), full((1, LANES)),
                full((GQA_WIDTH, GQA_WIDTH)), full((LANES, LANES))]
    args = [x, scale, shift, w_in_b, qgain, kgain, bdq, bdk]
    if rope:
        in_specs += [pl.BlockSpec((tm, LANES), lambda i: (i % nb_rows, 0))] * 6
        args += list(rope_tabs)
    row = lambda w: pl.BlockSpec((tm, w), lambda i: (i, 0))
    stk = lambda n: pl.BlockSpec((n, tm, LANES), lambda i: (0, i, 0))
    out_specs = [row(GQA_WIDTH), stk(2), stk(2), row(DIFF_WIDTH), row(DIFF_WIDTH), stk(4),
                 row(D_MODEL), row(4 * HGRN_WIDTH)]
    out_shape = [jax.ShapeDtypeStruct((t_total, GQA_WIDTH), BF16),
                 jax.ShapeDtypeStruct((2, t_total, LANES), BF16),
                 jax.ShapeDtypeStruct((2, t_total, LANES), BF16),
                 jax.ShapeDtypeStruct((t_total, DIFF_WIDTH), BF16),
                 jax.ShapeDtypeStruct((t_total, DIFF_WIDTH), BF16),
                 jax.ShapeDtypeStruct((4, t_total, LANES), BF16),
                 jax.ShapeDtypeStruct((t_total, D_MODEL), BF16),
                 jax.ShapeDtypeStruct((t_total, 4 * HGRN_WIDTH), F32)]
    aliases = {}
    if ctx_out:
        seq = tokens_per_batch
        nseq = tm // seq
        assert nseq * seq == tm
        widths = (LANES, LANES, DIFF_WIDTH, DIFF_WIDTH)
        if ctx_prev is not None:
            aliases = {len(args) + n: len(out_specs) + n for n in range(4)}
            in_specs += [pl.BlockSpec(memory_space=pl.ANY)] * 4
            args += list(ctx_prev)
        if ctx_prev is None:
            assert layer == 0
            out_specs += [pl.BlockSpec((nseq, DEPTH, seq, w), lambda i: (i, 0, 0, 0)) for w in widths]
        else:
            out_specs += [pl.BlockSpec((nseq, 1, seq, w), lambda i: (i, layer, 0, 0)) for w in widths]
        out_shape += [jax.ShapeDtypeStruct((t_total // seq, DEPTH, seq, w), F32) for w in widths]
    return pl.pallas_call(
        functools.partial(_inproj_kernel, rope=rope, ctx_out=ctx_out, ctx_alias=bool(aliases)),
        grid=(t_total // tm,),
        in_specs=in_specs, out_specs=out_specs, out_shape=out_shape,
        input_output_aliases=aliases,
        compiler_params=_params(("parallel",)),
        name="in_projection",
    )(*args)


def _chunk_rows(j, tk):
    return pl.ds(j * tk, tk) if isinstance(j, int) else pl.ds(pl.multiple_of(j * tk, tk), tk)


def _attention_scratch(n_items, rows, tk):
    rows = n_items * rows
    return ([pltpu.VMEM((rows, LANES), BF16)]
            + [pltpu.VMEM((rows, tk), F32)] * 2 + [pltpu.VMEM((rows, tk), BF16)] * 2
            + [pltpu.VMEM((rows, LANES), F32)] * 4)


def _item_scratch(refs, item, rows):
    return [r.at[pl.ds(item * rows, rows)] for r in refs]


def _softmax_stage(s_sc, p_sc, a_sc, m_sc, keys, first):
    for r in range(s_sc.shape[0] // SM_ROWS):
        rows = pl.ds(r * SM_ROWS, SM_ROWS)
        s = s_sc[rows, 0:keys]
        m_cur = jnp.max(s, axis=1, keepdims=True)
        if first:
            m_new = jnp.broadcast_to(m_cur, (SM_ROWS, LANES))
        else:
            m_prev = m_sc[rows, :]
            m_new = jnp.maximum(m_prev, m_cur)
            a_sc[rows, :] = jnp.exp2(m_prev - m_new)
        m_sc[rows, :] = m_new
        p_sc[rows, 0:keys] = jnp.exp2((s - jnp.concatenate([m_new] * (keys // LANES), axis=1)).astype(BF16))


def _item_timeline(n_new, cache, k_at, v_at, start, qk, sm, av, finish):
    o = 0 if cache is None else 1
    n = n_new + o
    k_of = lambda c: cache[0]() if (o and c == 0) else k_at(c - o)
    v_of = lambda c: cache[1]() if (o and c == 0) else v_at(c - o)
    steps = []
    for t in range(n + 2):
        step = [start] if t == 0 else []
        if t < n:
            step.append(lambda t=t: qk(t % 2, k_of(t)))
        if 0 <= t - 1 < n:
            step.append(lambda t=t: sm((t - 1) % 2, t == 1))
        if 0 <= t - 2 < n:
            step.append(lambda t=t: av(t % 2, v_of(t - 2), t == 2))
        if t == n + 1:
            step.append(finish)
        steps.append(step)
    return steps


def _run_items(timelines):
    lag = max(1, len(timelines[0]) - 2)
    for g in range(lag * (len(timelines) - 1) + len(timelines[0])):
        for i, steps in enumerate(timelines):
            if 0 <= g - i * lag < len(steps):
                for stage in steps[g - i * lag]:
                    stage()


def _gqa_kernel(*refs, has_cache, tq, tk, nk, n_items, shared_kv):
    if has_cache:
        q_ref, kc_ref, vc_ref = refs[:3]
        refs = refs[3:]
    else:
        q_ref = refs[0]
        refs = refs[1:]
    k_ref, v_ref, o_ref = refs[:3]
    low = lax.broadcasted_iota(jnp.int32, (tq, LANES), 1) < HEAD_DIM
    rows = 2 * tq

    def item_stages(item):
        qs_sc, s0, s1, p0, p1, a0, a1, m_sc, acc_sc = _item_scratch(refs[3:], item, rows)
        s_b, p_b, a_b = (s0, s1), (p0, p1), (a0, a1)

        def start():
            q = q_ref[pl.ds(item * tq, tq), :]
            zero = jnp.zeros_like(q)
            qs_sc[0:tq, :] = jnp.where(low, q, zero)
            qs_sc[tq:, :] = jnp.where(low, zero, q)

        keys = {}

        def qk(slot, k):
            keys[slot] = k.shape[0]
            s_b[slot][:, 0:k.shape[0]] = lax.dot_general(qs_sc[...], k, _NT, preferred_element_type=F32)

        def sm(slot, first):
            _softmax_stage(s_b[slot], p_b[slot], a_b[slot], m_sc, keys[slot], first)

        def av(slot, v, first):
            pv = jnp.dot(p_b[slot][:, 0:v.shape[0]], v, preferred_element_type=F32)
            acc_sc[...] = pv if first else a_b[slot][...] * acc_sc[...] + pv

        def finish():
            acc = acc_sc[...]
            on = acc / pltpu.roll(acc, HEAD_DIM, 1)
            o_ref[pl.ds(item * tq, tq), :] = jnp.where(low, on[:tq], pltpu.roll(on[tq:], HEAD_DIM, 1)).astype(BF16)

        base = 0 if shared_kv else item * nk
        chunk = lambda ref: (lambda j: ref[0, _chunk_rows(base + j, tk), :])
        cache = (lambda: kc_ref[0, 0], lambda: vc_ref[0, 0]) if has_cache else None
        return _item_timeline(nk, cache, chunk(k_ref), chunk(v_ref), start, qk, sm, av, finish)

    _run_items([item_stages(item) for item in range(n_items)])


def _item_geometry(batch, tokens_per_batch, tq, n_items):
    nq = tokens_per_batch // tq
    if nq == 1:
        return batch // n_items, 1, n_items * tokens_per_batch, False
    assert nq % n_items == 0
    return batch, nq // n_items, tokens_per_batch, True


def _gqa_attention(qa, kdup, vaug, cache, batch, tokens_per_batch, tq, tk, n_items):
    t_total = qa.shape[0]
    nk = tokens_per_batch // tk
    has_cache = cache is not None
    bsteps, nq, kv_rows, shared_kv = _item_geometry(batch, tokens_per_batch, tq, n_items)
    assert shared_kv or not has_cache
    in_specs = [pl.BlockSpec((n_items * tq, LANES), lambda b, p, i: (b * nq + i, p))]
    args = [qa]
    if has_cache:
        past = cache[0].shape[2]
        in_specs += [pl.BlockSpec((1, 1, past, LANES), lambda b, p, i: (b, p // 2, 0, 0))] * 2
        args += list(cache)
    in_specs += [pl.BlockSpec((1, kv_rows, LANES), lambda b, p, i: (p // 2, b, 0))] * 2
    args += [kdup, vaug]
    return pl.pallas_call(
        functools.partial(_gqa_kernel, has_cache=has_cache, tq=tq, tk=tk, nk=nk, n_items=n_items,
                          shared_kv=shared_kv),
        grid=(bsteps, GQA_WIDTH // LANES, nq),
        in_specs=in_specs,
        out_specs=pl.BlockSpec((n_items * tq, LANES), lambda b, p, i: (b * nq + i, p)),
        out_shape=jax.ShapeDtypeStruct((t_total, GQA_WIDTH), BF16),
        scratch_shapes=_attention_scratch(n_items, 2 * tq, tk),
        compiler_params=_params(("parallel", "parallel", "parallel")),
        name="gqa_attention",
    )(*args)


def _diff_kernel(*refs, has_cache, tq, tk, nk, n_items, shared_kv, lam_init):
    q_ref, lp_ref = refs[:2]
    if has_cache:
        kc_ref, vc0_ref, vc1_ref = refs[2:5]
        refs = refs[5:]
    else:
        refs = refs[2:]
    k_ref, v0_ref, v1_ref, o_ref = refs[:4]
    lane = lax.broadcasted_iota(jnp.int32, (tq, LANES), 1)
    lp = lp_ref[...]
    lam = (jnp.exp(jnp.sum(lp[0:1] * lp[1:2], axis=1, keepdims=True))
           - jnp.exp(jnp.sum(lp[2:3] * lp[3:4], axis=1, keepdims=True)) + lam_init)
    h2 = 2 * tq
    rows = 4 * tq

    def item_stages(item):
        qs_sc, s0, s1, p0, p1, a0, a1, m_sc, acc_sc = _item_scratch(refs[4:], item, rows)
        s_b, p_b, a_b = (s0, s1), (p0, p1), (a0, a1)

        def start():
            q = q_ref[pl.ds(item * tq, tq), :]
            zero = jnp.zeros_like(q)
            for j in range(4):
                qs_sc[j * tq:(j + 1) * tq, :] = jnp.where((lane >> 5) == j, q, zero)

        keys = {}

        def qk(slot, k):
            keys[slot] = k.shape[0]
            s_b[slot][:, 0:k.shape[0]] = lax.dot_general(qs_sc[...], k, _NT, preferred_element_type=F32)

        def sm(slot, first):
            _softmax_stage(s_b[slot], p_b[slot], a_b[slot], m_sc, keys[slot], first)

        def av(slot, v, first):
            for half, vh in zip((pl.ds(0, h2), pl.ds(h2, h2)), v):
                pv = jnp.dot(p_b[slot][half, 0:vh.shape[0]], vh, preferred_element_type=F32)
                acc_sc[half, :] = pv if first else a_b[slot][half, :] * acc_sc[half, :] + pv

        def finish():
            acc = acc_sc[...]
            on = acc / pltpu.roll(acc, HEAD_DIM, 1)
            o0 = on[0:tq] - lam * on[tq:h2]
            o1 = on[h2:h2 + tq] - lam * on[h2 + tq:]
            o_ref[pl.ds(item * tq, tq), :] = jnp.where(lane < HEAD_DIM, o0, o1).astype(BF16)

        base = 0 if shared_kv else item * nk
        k_at = lambda j: k_ref[_chunk_rows(base + j, tk), :]
        v_at = lambda j: (v0_ref[0, _chunk_rows(base + j, tk), :], v1_ref[0, _chunk_rows(base + j, tk), :])
        cache = (lambda: kc_ref[0], lambda: (vc0_ref[0, 0], vc1_ref[0, 0])) if has_cache else None
        return _item_timeline(nk, cache, k_at, v_at, start, qk, sm, av, finish)

    _run_items([item_stages(item) for item in range(n_items)])


def _diff_attention(qd, kd, vaugd, lam_p, cache, batch, tokens_per_batch, tq, tk, n_items, lam_init):
    t_total = qd.shape[0]
    nk = tokens_per_batch // tk
    has_cache = cache is not None
    bsteps, nq, kv_rows, shared_kv = _item_geometry(batch, tokens_per_batch, tq, n_items)
    assert shared_kv or not has_cache
    in_specs = [pl.BlockSpec((n_items * tq, LANES), lambda b, p, i: (b * nq + i, p)),
                pl.BlockSpec((4, DIFF_QK_DIM), lambda b, p, i: (0, 0))]
    args = [qd, lam_p]
    if has_cache:
        ckd, cvd = cache
        past = ckd.shape[1]
        in_specs += [pl.BlockSpec((1, past, LANES), lambda b, p, i: (b, 0, p)),
                     pl.BlockSpec((1, 1, past, LANES), lambda b, p, i: (b, 2 * p, 0, 0)),
                     pl.BlockSpec((1, 1, past, LANES), lambda b, p, i: (b, 2 * p + 1, 0, 0))]
        args += [ckd, cvd, cvd]
    in_specs += [pl.BlockSpec((kv_rows, LANES), lambda b, p, i: (b, p)),
                 pl.BlockSpec((1, kv_rows, LANES), lambda b, p, i: (2 * p, b, 0)),
                 pl.BlockSpec((1, kv_rows, LANES), lambda b, p, i: (2 * p + 1, b, 0))]
    args += [kd, vaugd, vaugd]
    return pl.pallas_call(
        functools.partial(_diff_kernel, has_cache=has_cache, tq=tq, tk=tk, nk=nk, n_items=n_items,
                          shared_kv=shared_kv, lam_init=lam_init),
        grid=(bsteps, DIFF_WIDTH // LANES, nq),
        in_specs=in_specs,
        out_specs=pl.BlockSpec((n_items * tq, LANES), lambda b, p, i: (b * nq + i, p)),
        out_shape=jax.ShapeDtypeStruct((t_total, DIFF_WIDTH), BF16),
        scratch_shapes=_attention_scratch(n_items, 4 * tq, tk),
        compiler_params=_params(("parallel", "parallel", "parallel")),
        name="diff_attention",
    )(*args)


def _hgrn_gates(qs, z, v, lb, g_sc, tri):
    sig = jax.nn.sigmoid(z)
    om = 1.0 - lb
    k = om * (1.0 - sig)
    g = jnp.log2(jnp.maximum(lb + om * sig, FORGET_MIN))
    g1 = g.astype(BF16)
    e1 = g - g1.astype(F32)
    g2 = e1.astype(BF16)
    g3 = (e1 - g2.astype(F32)).astype(BF16)
    G = (jnp.dot(tri, g1, preferred_element_type=F32) + jnp.dot(tri, g2, preferred_element_type=F32)
         + jnp.dot(tri, g3, preferred_element_type=F32))
    g_sc[...] = G
    return qs, k, G, v.astype(BF16)


def _hgrn_operands(gates, g_sc, hmask, rev):
    C, W = SCAN_CHUNK, HGRN_WIDTH
    qs, k, G, vb = gates
    g_last = g_sc[pl.ds(0 if rev else C - 1, 1), :]
    q_in = (qs * jnp.exp2(G)).astype(BF16)
    k_end = (k * jnp.exp2(g_last - G)).astype(BF16)

    def block_diag(a):
        return jnp.concatenate([a] * HGRN_HEADS, axis=0) * hmask

    def ref_rows(block, off):
        return jnp.concatenate([jnp.broadcast_to(g_sc[pl.ds(b0 + off, 1), :], (block, W))
                                for b0 in range(0, C, block)], axis=0)

    levels = []
    for level in range(HGRN_LEVELS):
        block = 8 << level
        half = block // 2
        if level == 0:
            r = ref_rows(block, half)
            qe, ke = G - r, r - G
        else:
            r = ref_rows(block, half if rev else half - 1)
            qe, ke = jnp.minimum(G - r, 0.0), jnp.minimum(r - G, 0.0)
        levels.append(((qs * jnp.exp2(qe)).astype(BF16), block_diag((k * jnp.exp2(ke)).astype(BF16))))
    return q_in, jnp.exp2(g_last), k_end, vb, block_diag(vb), levels


def _hgrn_intra(operands, lmask_ref, bdmask, rev):
    q_in, decay, k_end, vb, v_bd, levels = operands
    ut = lax.dot_general(vb, k_end, _TN, preferred_element_type=F32) * bdmask
    a = None
    for level, (ql, kl) in enumerate(levels):
        al = lax.dot_general(ql, kl, _NT, preferred_element_type=F32)
        m = lmask_ref[rev, level]
        a = jnp.where(m > 0.5, al, 0.0) if level == 0 else a + al * m
    intra = jnp.dot(a.astype(BF16), v_bd, preferred_element_type=F32)
    return q_in, decay, ut, intra


def _hgrn_state(parts, st):
    q_in, decay, ut, intra = parts
    out = lax.dot_general(q_in, st.astype(BF16), _NT, preferred_element_type=F32) + intra
    return out, st * decay + ut


def _hgrn_kernel(*refs, layer, n_chunks, n_seq, has_s0, state_out, state_alias):
    it = iter(refs)
    qf, zf, vf, qb, zb, vb, lbraw, tri_ref, lmask_ref, hmask_ref, bdmask_ref = [next(it) for _ in range(11)]
    s0 = next(it) if has_s0 else None
    if state_alias:
        next(it)
    of_ref, ob_ref = next(it), next(it)
    sto_ref = next(it) if state_out else None
    st_ref, g_sc = next(it), next(it)

    @pl.when(pl.program_id(1) == 0)
    def _():
        if has_s0:
            st_ref[...] = s0[0]
        else:
            st_ref[...] = jnp.zeros(st_ref.shape, F32)

    lbr = lbraw[...]
    ex = jnp.exp(lbr - jnp.max(lbr, axis=0, keepdims=True))
    sm = ex / jnp.sum(ex, axis=0, keepdims=True)
    lbs = jnp.zeros(sm.shape[1:], F32)
    for j in range(1, layer + 1):
        lbs = lbs + sm[j]
    lb_f, lb_b = lbs[0:1], lbs[1:2]

    hmask = hmask_ref[...]
    bdmask = bdmask_ref[...]

    chunks = []
    for cidx in range(n_chunks):
        for s in range(n_seq):
            base = s * n_chunks
            chunks.append((2 * s, 0, pl.ds((base + cidx) * SCAN_CHUNK, SCAN_CHUNK), qf, zf, vf, lb_f, of_ref))
            chunks.append((2 * s + 1, 1, pl.ds((base + n_chunks - 1 - cidx) * SCAN_CHUNK, SCAN_CHUNK),
                           qb, zb, vb, lb_b, ob_ref))
    n = len(chunks)
    state = [st_ref[i] for i in range(2 * n_seq)]
    gates, operands, parts = {}, {}, {}
    for t in range(n + 3):
        if t < n:
            _, d, rows, q_r, z_r, v_r, lb, _ = chunks[t]
            gates[t] = _hgrn_gates(q_r[rows, :], z_r[rows, :], v_r[rows, :], lb, g_sc.at[t], tri_ref[d])
        if 0 <= t - 1 < n:
            operands[t - 1] = _hgrn_operands(gates.pop(t - 1), g_sc.at[t - 1], hmask, chunks[t - 1][1])
        if 0 <= t - 2 < n:
            parts[t - 2] = _hgrn_intra(operands.pop(t - 2), lmask_ref, bdmask, chunks[t - 2][1])
        if 0 <= t - 3 < n:
            si, _, rows, _, _, _, _, o_r = chunks[t - 3]
            out, state[si] = _hgrn_state(parts.pop(t - 3), state[si])
            o_r[rows, :] = out.astype(BF16)
    for i, s in enumerate(state):
        st_ref[i] = s

    if state_out:
        @pl.when(pl.program_id(1) == pl.num_programs(1) - 1)
        def _():
            if not state_alias:
                sto_ref[:, 1:] = jnp.zeros((sto_ref.shape[0], sto_ref.shape[1] - 1) + sto_ref.shape[2:], F32)
            for i, s in enumerate(state):
                for hh in range(HGRN_HEADS):
                    slab = s[hh * HEAD_DIM:(hh + 1) * HEAD_DIM, (hh // 2) * LANES:(hh // 2 + 1) * LANES]
                    sto_ref[i // 2, 0, i % 2, hh] = pltpu.roll(slab, HEAD_DIM, 1) if hh % 2 else slab


def _hgrn_scan(zr, lb_raw, scan_consts, s0, state_prev, state_out, layer, batch, tokens_per_batch):
    t_total = zr.shape[0]
    tb = min(SCAN_BLOCK, tokens_per_batch)
    nblk = tokens_per_batch // tb
    n_seq = SCAN_BLOCK // tb if nblk == 1 else 1
    W = HGRN_WIDTH
    has_s0 = s0 is not None
    assert n_seq == 1 or not has_s0
    state_alias = state_prev is not None
    rows = n_seq * tb
    fblk = lambda col: pl.BlockSpec((rows, W), lambda b, j: (b * nblk + j, col))
    bblk = lambda col: pl.BlockSpec((rows, W), lambda b, j: (b * nblk + nblk - 1 - j, col))
    full = lambda a: pl.BlockSpec(a.shape, lambda b, j: (0,) * a.ndim)
    in_specs = [fblk(0), fblk(1), fblk(3), bblk(0), bblk(2), bblk(3),
                pl.BlockSpec((DEPTH, 2, W), lambda b, j: (0, 0, 0))] + [full(a) for a in scan_consts]
    args = [zr] * 6 + [lb_raw] + list(scan_consts)
    if has_s0:
        in_specs.append(pl.BlockSpec((1, 2, W, W), lambda b, j: (b, 0, 0, 0)))
        args.append(s0)
    aliases = {}
    if state_alias:
        aliases = {len(args): 2}
        in_specs.append(pl.BlockSpec(memory_space=pl.ANY))
        args.append(state_prev)
    out_specs = [pl.BlockSpec((rows, W), lambda b, j: (b * nblk + j, 0)),
                 pl.BlockSpec((rows, W), lambda b, j: (b * nblk + nblk - 1 - j, 0))]
    out_shape = [jax.ShapeDtypeStruct((t_total, W), BF16), jax.ShapeDtypeStruct((t_total, W), BF16)]
    if state_out and not state_alias:
        assert layer == 0
        out_specs.append(pl.BlockSpec((n_seq, DEPTH, 2, HGRN_HEADS, HEAD_DIM, LANES),
                                      lambda b, j: (b, 0, 0, 0, 0, 0)))
    elif state_out:
        out_specs.append(pl.BlockSpec((n_seq, 1, 2, HGRN_HEADS, HEAD_DIM, LANES),
                                      lambda b, j: (b, layer, 0, 0, 0, 0)))
    if state_out:
        out_shape.append(jax.ShapeDtypeStruct((batch, DEPTH, 2, HGRN_HEADS, HEAD_DIM, LANES), F32))
    n_units = 2 * rows // SCAN_CHUNK
    return pl.pallas_call(
        functools.partial(_hgrn_kernel, layer=layer, n_chunks=tb // SCAN_CHUNK, n_seq=n_seq, has_s0=has_s0,
                          state_out=state_out, state_alias=state_alias),
        grid=(batch // n_seq, nblk),
        in_specs=in_specs, out_specs=out_specs, out_shape=out_shape,
        input_output_aliases=aliases,
        scratch_shapes=[pltpu.VMEM((2 * n_seq, W, W), F32), pltpu.VMEM((n_units, SCAN_CHUNK, W), F32)],
        compiler_params=_params(("parallel", "arbitrary")),
        name="hgrn_scan",
    )(*args)


def _outproj_kernel(oa_ref, od_ref, of_ref, ob_ref, g_ref, x_ref, gate_ref, w_ref, sub_ref, hgn_ref,
                    bd_ref, lng_ref, lnb_ref, o_ref, *, lam_init):
    inv_d = 1.0 / HEAD_DIM
    g = g_ref[...].astype(F32)
    bd = bd_ref[...]

    def head_rms(t, gain):
        ssq = jnp.dot((t * t).astype(BF16), bd, preferred_element_type=F32)
        return t * lax.rsqrt(ssq * inv_d + RMS_EPS) * gain

    ma = (oa_ref[...].astype(F32) * g[:, 0:GQA_WIDTH]).astype(BF16)
    md = (head_rms(od_ref[...].astype(F32), sub_ref[...]) * (1.0 - lam_init)
          * g[:, GQA_WIDTH:GQA_WIDTH + DIFF_WIDTH]).astype(BF16)
    mr = (head_rms(of_ref[...].astype(F32) + ob_ref[...].astype(F32), hgn_ref[...])
          * g[:, GQA_WIDTH + DIFF_WIDTH:]).astype(BF16)
    y = (jnp.dot(ma, w_ref[0:GQA_WIDTH, :], preferred_element_type=F32)
         + jnp.dot(md, w_ref[GQA_WIDTH:GQA_WIDTH + DIFF_WIDTH, :], preferred_element_type=F32)
         + jnp.dot(mr, w_ref[GQA_WIDTH + DIFF_WIDTH:, :], preferred_element_type=F32))
    xn = DEEPNORM_ALPHA * x_ref[...] + gate_ref[0] * y
    mu = jnp.mean(xn, axis=-1, keepdims=True)
    xc = xn - mu
    var = jnp.mean(xc * xc, axis=-1, keepdims=True)
    o_ref[...] = xc * lax.rsqrt(var + LN_EPS) * lng_ref[...] + lnb_ref[...]


def _out_projection(out_a, out_d, o_f, o_b, gates, x, gate, w_out_b, subln, hgn, bd, ln_g, ln_b,
                    layer, lam_init):
    t_total = x.shape[0]
    tm = PROJ_ROWS
    mod_rows = t_total // gate.shape[0] // tm
    row = lambda w: pl.BlockSpec((tm, w), lambda i: (i, 0))
    full = lambda shape: pl.BlockSpec(shape, lambda i: (0,) * len(shape))
    return pl.pallas_call(
        functools.partial(_outproj_kernel, lam_init=lam_init),
        grid=(t_total // tm,),
        in_specs=[row(GQA_WIDTH), row(DIFF_WIDTH), row(HGRN_WIDTH), row(HGRN_WIDTH), row(D_MODEL), row(D_MODEL),
                  pl.BlockSpec((1, 1, D_MODEL), lambda i: (i // mod_rows, 0, 0)),
                  pl.BlockSpec((None, D_MODEL, D_MODEL), lambda i: (layer, 0, 0)),
                  full((1, DIFF_WIDTH)), full((1, HGRN_WIDTH)),
                  full((DIFF_WIDTH, DIFF_WIDTH)), full((1, D_MODEL)), full((1, D_MODEL))],
        out_specs=row(D_MODEL),
        out_shape=jax.ShapeDtypeStruct((t_total, D_MODEL), F32),
        compiler_params=_params(("parallel",)),
        name="out_projection",
    )(out_a, out_d, o_f, o_b, gates, x, gate, w_out_b, subln, hgn, bd, ln_g, ln_b)


def _block_diag_ones(n):
    idx = np.arange(n) // HEAD_DIM
    return jnp.asarray((idx[:, None] == idx[None, :]).astype(np.float32), dtype=BF16)


def _scan_constants():
    C, W, H = SCAN_CHUNK, HGRN_WIDTH, HGRN_HEADS
    t = np.arange(C)[:, None]
    u = np.arange(C)[None, :]
    tri = np.stack([u <= t, u >= t])
    s = (np.arange(W) % C)[None, :]
    lmask = np.zeros((2, HGRN_LEVELS, C, W), np.float32)
    for level in range(HGRN_LEVELS):
        block = 8 << level
        same = (t // block) == (s // block)
        if level == 0:
            lmask[0, level] = same & (s <= t)
            lmask[1, level] = same & (s >= t)
        else:
            t_hi, s_hi = (t % block) >= block // 2, (s % block) >= block // 2
            lmask[0, level] = same & t_hi & ~s_hi
            lmask[1, level] = same & ~t_hi & s_hi
    rows = np.arange(H * C)[:, None]
    hmask = (rows // C) == (np.arange(W)[None, :] // HEAD_DIM)
    bdmask = (np.arange(W)[:, None] // HEAD_DIM) == (np.arange(W)[None, :] // HEAD_DIM)
    return (jnp.asarray(tri.astype(np.float32), dtype=BF16), jnp.asarray(lmask),
            jnp.asarray(hmask.astype(np.float32), dtype=BF16), jnp.asarray(bdmask.astype(np.float32)))


def _rope_tables(n_tokens, dim):
    rows = n_tokens // GRID_W
    row = jnp.repeat(jnp.arange(rows, dtype=F32), GRID_W)
    col = jnp.tile(jnp.arange(GRID_W, dtype=F32), rows)
    quarter = dim // 4
    inv_freq = ROPE_THETA ** (-jnp.arange(quarter, dtype=F32) / quarter)
    ar = row[:, None] * inv_freq[None, :]
    ac = col[:, None] * inv_freq[None, :]
    ang = jnp.concatenate([ar, ar, ac, ac], axis=-1)
    cos = jnp.tile(jnp.cos(ang), (1, LANES // dim))
    sin = jnp.tile(jnp.sin(ang), (1, LANES // dim))
    first = (jnp.arange(LANES) % (dim // 2)) < quarter
    return cos, jnp.where(first, -sin, 0.0), jnp.where(first, 0.0, sin)


def _mixer_layer(x, scale, shift, gate, layer, batch, tokens_per_batch, rope_tabs, cache, consts, weights,
                 ctx_out, ctx_prev):
    (w_in_b, w_out_b, qgain, kgain, lam_p, subln, hgn, lb_raw, ln_g, ln_b) = weights
    bdq, bdk, bdd, scan_consts = consts
    lam_init = 0.8 - 0.6 * math.exp(-0.3 * layer)
    outs = _in_projection(x, scale, shift, w_in_b, qgain, kgain, bdq, bdk, rope_tabs, tokens_per_batch, ctx_out,
                          layer, None if ctx_prev is None else ctx_prev[0])
    qa, kdup, vaug, qd, kd, vaugd, gates, zr = outs[:8]
    if cache is None:
        gqa_cache = diff_cache = s0 = None
        tq_a = tq_d = tk = tokens_per_batch
        items_a, items_d = 2, 2
    else:
        gqa_cache, diff_cache, s0 = cache
        tq_a, tq_d, tk = 512, 256, 512
        items_a = items_d = 1
    out_a = _gqa_attention(qa, kdup, vaug, gqa_cache, batch, tokens_per_batch, tq_a, tk, items_a)
    out_d = _diff_attention(qd, kd, vaugd, lam_p, diff_cache, batch, tokens_per_batch, tq_d, tk, items_d, lam_init)
    scan = _hgrn_scan(zr, lb_raw, scan_consts, s0, None if ctx_prev is None else ctx_prev[1], ctx_out, layer,
                      batch, tokens_per_batch)
    x_new = _out_projection(out_a, out_d, scan[0], scan[1], gates, x, gate, w_out_b, subln, hgn, bdd, ln_g, ln_b,
                            layer, lam_init)
    return x_new, ((outs[8:], scan[2]) if ctx_out else None)


def kernel(x_prompt, x_sample, cache_gqa_k, cache_gqa_v, cache_diff_k, cache_diff_v, state_hgrn, c, c_ctx,
           w_ada, b_ada, w_in, gqa_q_norm, gqa_k_norm, diff_lambda, diff_subln, hgrn_lower_bounds, hgrn_norm,
           w_out, ln_g, ln_b):
    batch, seq, _ = x_prompt.shape
    dec_batch, dec_seq, _ = x_sample.shape
    past = cache_gqa_k.shape[2]
    H, HD = HGRN_HEADS, HEAD_DIM

    cond = jnp.zeros((MOD_ROWS, D_MODEL), F32).at[0].set(c_ctx).at[1:1 + dec_batch].set(c)
    mod = _modulation(cond, w_ada, b_ada)
    consts = (_block_diag_ones(GQA_WIDTH), _block_diag_ones(LANES), _block_diag_ones(DIFF_WIDTH),
              _scan_constants())
    rope_tabs = _rope_tables(dec_seq, HEAD_DIM) + _rope_tables(dec_seq, DIFF_QK_DIM)
    w_in_b = w_in.astype(BF16)
    w_out_b = w_out.astype(BF16)

    def layer_weights(l):
        return (w_in_b, w_out_b,
                jnp.tile(gqa_q_norm[l], GQA_WIDTH // HD)[None, :] * (HD ** -0.5 * LOG2E),
                jnp.tile(gqa_k_norm[l], LANES // HD)[None, :],
                diff_lambda[l],
                jnp.tile(diff_subln[l], DIFF_WIDTH // HD)[None, :],
                jnp.tile(hgrn_norm[l], HGRN_WIDTH // HD)[None, :],
                hgrn_lower_bounds, ln_g[l][None, :], ln_b[l][None, :])

    def mod_rows(l, lo, n):
        m = mod[l, lo:lo + n]
        return (m[:, None, 0:D_MODEL], m[:, None, D_MODEL:2 * D_MODEL], m[:, None, 2 * D_MODEL:])

    x = x_prompt.reshape(batch * seq, D_MODEL)
    ctx = None
    for l in range(DEPTH):
        shift, scale, gate = mod_rows(l, 0, 1)
        x, ctx = _mixer_layer(x, scale, shift, gate, l, batch, seq, None, None, consts, layer_weights(l), True, ctx)
    y_prompt = x.reshape(batch, seq, D_MODEL)
    (kn, av, dk, dv), st = ctx
    new_leaves = [kn.reshape(batch, DEPTH, seq, GQA_KV_HEADS, HD), av.reshape(batch, DEPTH, seq, GQA_KV_HEADS, HD),
                  dk.reshape(batch, DEPTH, seq, DIFF_HEADS, HD), dv.reshape(batch, DEPTH, seq, DIFF_HEADS, HD),
                  jnp.swapaxes(st[..., :HD], -1, -2)]

    ones = jnp.ones((dec_batch, past, HD), F32)
    eye = jnp.eye(H, dtype=F32)
    x = x_sample.reshape(dec_batch * dec_seq, D_MODEL)
    for l in range(DEPTH):
        shift, scale, gate = mod_rows(l, 1, dec_batch)
        ck, cv = cache_gqa_k[:, l], cache_gqa_v[:, l]
        ckd = jnp.stack([jnp.concatenate([ck[:, :, g], ck[:, :, g]], axis=-1) for g in range(GQA_KV_HEADS)], axis=1)
        cvd = jnp.stack([jnp.concatenate([cv[:, :, g], ones], axis=-1) for g in range(GQA_KV_HEADS)], axis=1)
        dkc = cache_diff_k[:, l].reshape(dec_batch, past, DIFF_WIDTH)
        dvc = cache_diff_v[:, l]
        dvd = jnp.stack([jnp.concatenate([dvc[:, :, h], ones] if h % 2 == 0 else [ones, dvc[:, :, h]], axis=-1)
                         for h in range(DIFF_HEADS)], axis=1)
        s0t = jnp.swapaxes(state_hgrn[:, l].astype(F32), -1, -2)
        s0 = (s0t[:, :, :, :, None, :] * eye[None, None, :, None, :, None]).reshape(dec_batch, 2, H * HD, H * HD)
        cache = ((ckd.astype(BF16), cvd.astype(BF16)), (dkc.astype(BF16), dvd.astype(BF16)), s0)
        x, _ = _mixer_layer(x, scale, shift, gate, l, dec_batch, dec_seq, rope_tabs, cache, consts,
                            layer_weights(l), False, None)
    y_sample = x.reshape(dec_batch, dec_seq, D_MODEL)
    return (y_prompt, y_sample) + tuple(new_leaves)
```

```python
import functools
import math

import numpy as np
import jax
import jax.numpy as jnp
from jax import lax
from jax.experimental import pallas as pl
from jax.experimental.pallas import tpu as pltpu

F32 = jnp.float32
BF16 = jnp.bfloat16

D_MODEL = 1024
DEPTH = 2
GRID_W = 64
HEAD_DIM = 64
GQA_WIDTH = 512
GQA_KV_HEADS = 2
DIFF_WIDTH = 256
DIFF_HEADS = 4
DIFF_QK_DIM = 32
HGRN_WIDTH = 256
HGRN_HEADS = 4
IN_WIDTH = 3584
SCAN_CHUNK = 64
HGRN_LEVELS = 4
ROPE_THETA = 10000.0
RMS_EPS = 1e-6
LN_EPS = 1e-5
FORGET_MIN = 1e-6
DEEPNORM_ALPHA = (2 * DEPTH) ** 0.25
LANES = 128
SUBLANES = 8
MOD_ROWS = SUBLANES
VMEM_LIMIT = 48 * 1024 * 1024
SM_ROWS = 32
PROJ_ROWS = 512
SCAN_BLOCK = 512
LOG2E = math.log2(math.e)

_A_Q, _A_K, _A_V, _A_G = 0, 512, 640, 768
_D_Q, _D_K, _D_V, _D_G = 1280, 1536, 1792, 2048
_R_Z, _R_G = 2304, 3328

_NT = (((1,), (1,)), ((), ()))
_TN = (((0,), (0,)), ((), ()))


def _silu(x):
    return x * jax.nn.sigmoid(x)


def _params(sem):
    return pltpu.CompilerParams(dimension_semantics=sem, vmem_limit_bytes=VMEM_LIMIT)


def _mod_kernel(c_ref, w_ref, b_ref, o_ref):
    c = c_ref[...]
    o_ref[0] = jnp.dot(_silu(c), w_ref[0], preferred_element_type=F32) + b_ref[0]


def _modulation(cond, w_ada, b_ada):
    tn = 1024
    return pl.pallas_call(
        _mod_kernel,
        grid=(DEPTH, 3 * D_MODEL // tn),
        in_specs=[pl.BlockSpec((MOD_ROWS, D_MODEL), lambda l, j: (0, 0)),
                  pl.BlockSpec((1, D_MODEL, tn), lambda l, j: (l, 0, j)),
                  pl.BlockSpec((1, 1, tn), lambda l, j: (l, 0, j))],
        out_specs=pl.BlockSpec((1, MOD_ROWS, tn), lambda l, j: (l, 0, j)),
        out_shape=jax.ShapeDtypeStruct((DEPTH, MOD_ROWS, 3 * D_MODEL), F32),
        compiler_params=_params(("arbitrary", "arbitrary")),
        name="modulation",
    )(cond, w_ada, b_ada.reshape(DEPTH, 1, 3 * D_MODEL))


def _rope(x, c, s1, s2, shift):
    return x * c + pltpu.roll(x, LANES - shift, 1) * s1 + pltpu.roll(x, shift, 1) * s2


def _inproj_kernel(*refs, rope, ctx_out, ctx_alias):
    it = iter(refs)
    x_ref, sc_ref, sh_ref, w_ref, qg_ref, kg_ref, bdq_ref, bdk_ref = [next(it) for _ in range(8)]
    if rope:
        ca, s1a, s2a, cd, s1d, s2d = [next(it)[...] for _ in range(6)]
    if ctx_alias:
        for _ in range(4):
            next(it)
    qa_o, kdup_o, vaug_o, qd_o, kd_o, vaugd_o, gates_o, zr_o = [next(it) for _ in range(8)]
    if ctx_out:
        ctx_refs = [next(it) for _ in range(4)]
        if not ctx_alias:
            for r in ctx_refs:
                r[:, 1:] = jnp.zeros((r.shape[0], r.shape[1] - 1) + r.shape[2:], F32)

        def put_ctx(n, val):
            r = ctx_refs[n]
            r[:, 0] = val.reshape(r.shape[0], r.shape[2], r.shape[3])

    h = (x_ref[...] * (1.0 + sc_ref[0]) + sh_ref[0]).astype(BF16)

    def proj(lo, hi):
        return jnp.dot(h, w_ref[:, lo:hi], preferred_element_type=F32)

    tm = h.shape[0]
    low = lax.broadcasted_iota(jnp.int32, (tm, LANES), 1) < HEAD_DIM
    inv_d = 1.0 / HEAD_DIM

    aq = proj(_A_Q, _A_K)
    ssq = jnp.dot((aq * aq).astype(BF16), bdq_ref[...], preferred_element_type=F32)
    qn = aq * lax.rsqrt(ssq * inv_d + RMS_EPS) * qg_ref[...]
    for j in range(GQA_WIDTH // LANES):
        slab = qn[:, j * LANES:(j + 1) * LANES]
        if rope:
            slab = _rope(slab, ca, s1a, s2a, HEAD_DIM // 4)
        qa_o[:, j * LANES:(j + 1) * LANES] = slab.astype(BF16)

    ak = proj(_A_K, _A_V)
    ssqk = jnp.dot((ak * ak).astype(BF16), bdk_ref[...], preferred_element_type=F32)
    kn = ak * lax.rsqrt(ssqk * inv_d + RMS_EPS) * kg_ref[...]
    if ctx_out:
        put_ctx(0, kn)
    if rope:
        kn = _rope(kn, ca, s1a, s2a, HEAD_DIM // 4)
    kr = pltpu.roll(kn, HEAD_DIM, 1)
    kdup_o[0] = jnp.where(low, kn, kr).astype(BF16)
    kdup_o[1] = jnp.where(low, kr, kn).astype(BF16)

    av = proj(_A_V, _A_G)
    if ctx_out:
        put_ctx(1, av)
    vr = pltpu.roll(av, HEAD_DIM, 1)
    vaug_o[0] = jnp.where(low, av, 1.0).astype(BF16)
    vaug_o[1] = jnp.where(low, vr, 1.0).astype(BF16)

    gates_o[:, 0:GQA_WIDTH] = _silu(proj(_A_G, _D_Q)).astype(BF16)

    dq = proj(_D_Q, _D_K) * (DIFF_QK_DIM ** -0.5 * LOG2E)
    dk = proj(_D_K, _D_V)
    if ctx_out:
        put_ctx(2, dk)
    for j in range(DIFF_WIDTH // LANES):
        sq = dq[:, j * LANES:(j + 1) * LANES]
        sk = dk[:, j * LANES:(j + 1) * LANES]
        if rope:
            sq = _rope(sq, cd, s1d, s2d, DIFF_QK_DIM // 4)
            sk = _rope(sk, cd, s1d, s2d, DIFF_QK_DIM // 4)
        qd_o[:, j * LANES:(j + 1) * LANES] = sq.astype(BF16)
        kd_o[:, j * LANES:(j + 1) * LANES] = sk.astype(BF16)
    dv = proj(_D_V, _D_G)
    if ctx_out:
        put_ctx(3, dv)
    for j in range(DIFF_WIDTH // LANES):
        sv = dv[:, j * LANES:(j + 1) * LANES]
        vaugd_o[2 * j] = jnp.where(low, sv, 1.0).astype(BF16)
        vaugd_o[2 * j + 1] = jnp.where(low, 1.0, sv).astype(BF16)
    gates_o[:, GQA_WIDTH:GQA_WIDTH + DIFF_WIDTH] = _silu(proj(_D_G, _R_Z)).astype(BF16)

    gates_o[:, GQA_WIDTH + DIFF_WIDTH:] = _silu(proj(_R_G, IN_WIDTH)).astype(BF16)
    zr = proj(_R_Z, _R_G)
    zr_o[:, 0:HGRN_WIDTH] = _silu(zr[:, 0:HGRN_WIDTH])
    zr_o[:, HGRN_WIDTH:] = zr[:, HGRN_WIDTH:]


def _in_projection(x, scale, shift, w_in_b, qgain, kgain, bdq, bdk, rope_tabs, tokens_per_batch, ctx_out,
                   layer, ctx_prev):
    t_total = x.shape[0]
    tm = PROJ_ROWS
    nb_rows = tokens_per_batch // tm
    mod_rows = t_total // scale.shape[0] // tm
    rope = rope_tabs is not None
    assert not rope or nb_rows >= 1
    full = lambda shape: pl.BlockSpec(shape, lambda i: (0,) * len(shape))
    in_specs = [pl.BlockSpec((tm, D_MODEL), lambda i: (i, 0)),
                pl.BlockSpec((1, 1, D_MODEL), lambda i: (i // mod_rows, 0, 0)),
                pl.BlockSpec((1, 1, D_MODEL), lambda i: (i // mod_rows, 0, 0)),
                pl.BlockSpec((None, D_MODEL, IN_WIDTH), lambda i: (layer, 0, 0)),
                full((1, GQA_WIDTH)), full((1, LANES)),
                full((GQA_WIDTH, GQA_WIDTH)), full((LANES, LANES))]
    args = [x, scale, shift, w_in_b, qgain, kgain, bdq, bdk]
    if rope:
        in_specs += [pl.BlockSpec((tm, LANES), lambda i: (i % nb_rows, 0))] * 6
        args += list(rope_tabs)
    row = lambda w: pl.BlockSpec((tm, w), lambda i: (i, 0))
    stk = lambda n: pl.BlockSpec((n, tm, LANES), lambda i: (0, i, 0))
    out_specs = [row(GQA_WIDTH), stk(2), stk(2), row(DIFF_WIDTH), row(DIFF_WIDTH), stk(4),
                 row(D_MODEL), row(4 * HGRN_WIDTH)]
    out_shape = [jax.ShapeDtypeStruct((t_total, GQA_WIDTH), BF16),
                 jax.ShapeDtypeStruct((2, t_total, LANES), BF16),
                 jax.ShapeDtypeStruct((2, t_total, LANES), BF16),
                 jax.ShapeDtypeStruct((t_total, DIFF_WIDTH), BF16),
                 jax.ShapeDtypeStruct((t_total, DIFF_WIDTH), BF16),
                 jax.ShapeDtypeStruct((4, t_total, LANES), BF16),
                 jax.ShapeDtypeStruct((t_total, D_MODEL), BF16),
                 jax.ShapeDtypeStruct((t_total, 4 * HGRN_WIDTH), F32)]
    aliases = {}
    if ctx_out:
        seq = tokens_per_batch
        nseq = tm // seq
        assert nseq * seq == tm
        widths = (LANES, LANES, DIFF_WIDTH, DIFF_WIDTH)
        if ctx_prev is not None:
            aliases = {len(args) + n: len(out_specs) + n for n in range(4)}
            in_specs += [pl.BlockSpec(memory_space=pl.ANY)] * 4
            args += list(ctx_prev)
        if ctx_prev is None:
            assert layer == 0
            out_specs += [pl.BlockSpec((nseq, DEPTH, seq, w), lambda i: (i, 0, 0, 0)) for w in widths]
        else:
            out_specs += [pl.BlockSpec((nseq, 1, seq, w), lambda i: (i, layer, 0, 0)) for w in widths]
        out_shape += [jax.ShapeDtypeStruct((t_total // seq, DEPTH, seq, w), F32) for w in widths]
    return pl.pallas_call(
        functools.partial(_inproj_kernel, rope=rope, ctx_out=ctx_out, ctx_alias=bool(aliases)),
        grid=(t_total // tm,),
        in_specs=in_specs, out_specs=out_specs, out_shape=out_shape,
        input_output_aliases=aliases,
        compiler_params=_params(("parallel",)),
        name="in_projection",
    )(*args)


def _chunk_rows(j, tk):
    return pl.ds(j * tk, tk) if isinstance(j, int) else pl.ds(pl.multiple_of(j * tk, tk), tk)


def _attention_scratch(n_items, rows, tk):
    rows = n_items * rows
    return ([pltpu.VMEM((rows, LANES), BF16)]
            + [pltpu.VMEM((rows, tk), F32)] * 2 + [pltpu.VMEM((rows, tk), BF16)] * 2
            + [pltpu.VMEM((rows, LANES), F32)] * 4)


def _item_scratch(refs, item, rows):
    return [r.at[pl.ds(item * rows, rows)] for r in refs]


def _softmax_stage(s_sc, p_sc, a_sc, m_sc, keys, first):
    for r in range(s_sc.shape[0] // SM_ROWS):
        rows = pl.ds(r * SM_ROWS, SM_ROWS)
        s = s_sc[rows, 0:keys]
        m_cur = jnp.max(s, axis=1, keepdims=True)
        if first:
            m_new = jnp.broadcast_to(m_cur, (SM_ROWS, LANES))
        else:
            m_prev = m_sc[rows, :]
            m_new = jnp.maximum(m_prev, m_cur)
            a_sc[rows, :] = jnp.exp2(m_prev - m_new)
        m_sc[rows, :] = m_new
        p_sc[rows, 0:keys] = jnp.exp2(s - jnp.concatenate([m_new] * (keys // LANES), axis=1)).astype(BF16)


def _item_timeline(n_new, cache, k_at, v_at, start, qk, sm, av, finish):
    o = 0 if cache is None else 1
    n = n_new + o
    k_of = lambda c: cache[0]() if (o and c == 0) else k_at(c - o)
    v_of = lambda c: cache[1]() if (o and c == 0) else v_at(c - o)
    steps = []
    for t in range(n + 2):
        step = [start] if t == 0 else []
        if t < n:
            step.append(lambda t=t: qk(t % 2, k_of(t)))
        if 0 <= t - 1 < n:
            step.append(lambda t=t: sm((t - 1) % 2, t == 1))
        if 0 <= t - 2 < n:
            step.append(lambda t=t: av(t % 2, v_of(t - 2), t == 2))
        if t == n + 1:
            step.append(finish)
        steps.append(step)
    return steps


def _run_items(timelines):
    lag = max(1, len(timelines[0]) - 2)
    for g in range(lag * (len(timelines) - 1) + len(timelines[0])):
        for i, steps in enumerate(timelines):
            if 0 <= g - i * lag < len(steps):
                for stage in steps[g - i * lag]:
                    stage()


def _gqa_kernel(*refs, has_cache, tq, tk, nk, n_items, shared_kv, slabs):
    if has_cache:
        q_ref, kc_ref, vc_ref = refs[:3]
        refs = refs[3:]
    else:
        q_ref = refs[0]
        refs = refs[1:]
    k_ref, v_ref, o_ref = refs[:3]
    low = lax.broadcasted_iota(jnp.int32, (tq, LANES), 1) < HEAD_DIM
    rows = 2 * slabs * tq

    def item_stages(item):
        qs_sc, s0, s1, p0, p1, a0, a1, m_sc, acc_sc = _item_scratch(refs[3:], item, rows)
        s_b, p_b, a_b = (s0, s1), (p0, p1), (a0, a1)

        def start():
            for j in range(slabs):
                q = q_ref[pl.ds(item * tq, tq), j * LANES:(j + 1) * LANES]
                zero = jnp.zeros_like(q)
                qs_sc[2 * j * tq:(2 * j + 1) * tq, :] = jnp.where(low, q, zero)
                qs_sc[(2 * j + 1) * tq:(2 * j + 2) * tq, :] = jnp.where(low, zero, q)

        keys = {}

        def qk(slot, k):
            keys[slot] = k.shape[0]
            s_b[slot][:, 0:k.shape[0]] = lax.dot_general(qs_sc[...], k, _NT, preferred_element_type=F32)

        def sm(slot, first):
            _softmax_stage(s_b[slot], p_b[slot], a_b[slot], m_sc, keys[slot], first)

        def av(slot, v, first):
            pv = jnp.dot(p_b[slot][:, 0:v.shape[0]], v, preferred_element_type=F32)
            acc_sc[...] = pv if first else a_b[slot][...] * acc_sc[...] + pv

        def finish():
            acc = acc_sc[...]
            on = acc / pltpu.roll(acc, HEAD_DIM, 1)
            for j in range(slabs):
                first, second = on[2 * j * tq:(2 * j + 1) * tq], on[(2 * j + 1) * tq:(2 * j + 2) * tq]
                o_ref[pl.ds(item * tq, tq), j * LANES:(j + 1) * LANES] = jnp.where(
                    low, first, pltpu.roll(second, HEAD_DIM, 1)).astype(BF16)

        base = 0 if shared_kv else item * nk
        chunk = lambda ref: (lambda j: ref[0, _chunk_rows(base + j, tk), :])
        cache = (lambda: kc_ref[0, 0], lambda: vc_ref[0, 0]) if has_cache else None
        return _item_timeline(nk, cache, chunk(k_ref), chunk(v_ref), start, qk, sm, av, finish)

    _run_items([item_stages(item) for item in range(n_items)])


def _item_geometry(batch, tokens_per_batch, tq, n_items):
    nq = tokens_per_batch // tq
    if nq == 1:
        return batch // n_items, 1, n_items * tokens_per_batch, False
    assert nq % n_items == 0
    return batch, nq // n_items, tokens_per_batch, True


def _gqa_attention(qa, kdup, vaug, cache, batch, tokens_per_batch, tq, tk, n_items, slabs):
    t_total = qa.shape[0]
    nk = tokens_per_batch // tk
    has_cache = cache is not None
    bsteps, nq, kv_rows, shared_kv = _item_geometry(batch, tokens_per_batch, tq, n_items)
    assert shared_kv or not has_cache
    width = slabs * LANES
    group = lambda p: p * slabs // 2
    in_specs = [pl.BlockSpec((n_items * tq, width), lambda b, p, i: (b * nq + i, p))]
    args = [qa]
    if has_cache:
        past = cache[0].shape[2]
        in_specs += [pl.BlockSpec((1, 1, past, LANES), lambda b, p, i: (b, group(p), 0, 0))] * 2
        args += list(cache)
    in_specs += [pl.BlockSpec((1, kv_rows, LANES), lambda b, p, i: (group(p), b, 0))] * 2
    args += [kdup, vaug]
    return pl.pallas_call(
        functools.partial(_gqa_kernel, has_cache=has_cache, tq=tq, tk=tk, nk=nk, n_items=n_items,
                          shared_kv=shared_kv, slabs=slabs),
        grid=(bsteps, GQA_WIDTH // width, nq),
        in_specs=in_specs,
        out_specs=pl.BlockSpec((n_items * tq, width), lambda b, p, i: (b * nq + i, p)),
        out_shape=jax.ShapeDtypeStruct((t_total, GQA_WIDTH), BF16),
        scratch_shapes=_attention_scratch(n_items, 2 * slabs * tq, tk),
        compiler_params=_params(("parallel", "parallel", "parallel")),
        name="gqa_attention",
    )(*args)


def _diff_kernel(*refs, has_cache, tq, tk, nk, n_items, shared_kv, lam_init):
    q_ref, lp_ref = refs[:2]
    if has_cache:
        kc_ref, vc0_ref, vc1_ref = refs[2:5]
        refs = refs[5:]
    else:
        refs = refs[2:]
    k_ref, v0_ref, v1_ref, o_ref = refs[:4]
    lane = lax.broadcasted_iota(jnp.int32, (tq, LANES), 1)
    lp = lp_ref[...]
    lam = (jnp.exp(jnp.sum(lp[0:1] * lp[1:2], axis=1, keepdims=True))
           - jnp.exp(jnp.sum(lp[2:3] * lp[3:4], axis=1, keepdims=True)) + lam_init)
    h2 = 2 * tq
    rows = 4 * tq

    def item_stages(item):
        qs_sc, s0, s1, p0, p1, a0, a1, m_sc, acc_sc = _item_scratch(refs[4:], item, rows)
        s_b, p_b, a_b = (s0, s1), (p0, p1), (a0, a1)

        def start():
            q = q_ref[pl.ds(item * tq, tq), :]
            zero = jnp.zeros_like(q)
            for j in range(4):
                qs_sc[j * tq:(j + 1) * tq, :] = jnp.where((lane >> 5) == j, q, zero)

        keys = {}

        def qk(slot, k):
            keys[slot] = k.shape[0]
            s_b[slot][:, 0:k.shape[0]] = lax.dot_general(qs_sc[...], k, _NT, preferred_element_type=F32)

        def sm(slot, first):
            _softmax_stage(s_b[slot], p_b[slot], a_b[slot], m_sc, keys[slot], first)

        def av(slot, v, first):
            for half, vh in zip((pl.ds(0, h2), pl.ds(h2, h2)), v):
                pv = jnp.dot(p_b[slot][half, 0:vh.shape[0]], vh, preferred_element_type=F32)
                acc_sc[half, :] = pv if first else a_b[slot][half, :] * acc_sc[half, :] + pv

        def finish():
            acc = acc_sc[...]
            on = acc / pltpu.roll(acc, HEAD_DIM, 1)
            o0 = on[0:tq] - lam * on[tq:h2]
            o1 = on[h2:h2 + tq] - lam * on[h2 + tq:]
            o_ref[pl.ds(item * tq, tq), :] = jnp.where(lane < HEAD_DIM, o0, o1).astype(BF16)

        base = 0 if shared_kv else item * nk
        k_at = lambda j: k_ref[_chunk_rows(base + j, tk), :]
        v_at = lambda j: (v0_ref[0, _chunk_rows(base + j, tk), :], v1_ref[0, _chunk_rows(base + j, tk), :])
        cache = (lambda: kc_ref[0], lambda: (vc0_ref[0, 0], vc1_ref[0, 0])) if has_cache else None
        return _item_timeline(nk, cache, k_at, v_at, start, qk, sm, av, finish)

    _run_items([item_stages(item) for item in range(n_items)])


def _diff_attention(qd, kd, vaugd, lam_p, cache, batch, tokens_per_batch, tq, tk, n_items, lam_init):
    t_total = qd.shape[0]
    nk = tokens_per_batch // tk
    has_cache = cache is not None
    bsteps, nq, kv_rows, shared_kv = _item_geometry(batch, tokens_per_batch, tq, n_items)
    assert shared_kv or not has_cache
    in_specs = [pl.BlockSpec((n_items * tq, LANES), lambda b, p, i: (b * nq + i, p)),
                pl.BlockSpec((4, DIFF_QK_DIM), lambda b, p, i: (0, 0))]
    args = [qd, lam_p]
    if has_cache:
        ckd, cvd = cache
        past = ckd.shape[1]
        in_specs += [pl.BlockSpec((1, past, LANES), lambda b, p, i: (b, 0, p)),
                     pl.BlockSpec((1, 1, past, LANES), lambda b, p, i: (b, 2 * p, 0, 0)),
                     pl.BlockSpec((1, 1, past, LANES), lambda b, p, i: (b, 2 * p + 1, 0, 0))]
        args += [ckd, cvd, cvd]
    in_specs += [pl.BlockSpec((kv_rows, LANES), lambda b, p, i: (b, p)),
                 pl.BlockSpec((1, kv_rows, LANES), lambda b, p, i: (2 * p, b, 0)),
                 pl.BlockSpec((1, kv_rows, LANES), lambda b, p, i: (2 * p + 1, b, 0))]
    args += [kd, vaugd, vaugd]
    return pl.pallas_call(
        functools.partial(_diff_kernel, has_cache=has_cache, tq=tq, tk=tk, nk=nk, n_items=n_items,
                          shared_kv=shared_kv, lam_init=lam_init),
        grid=(bsteps, DIFF_WIDTH // LANES, nq),
        in_specs=in_specs,
        out_specs=pl.BlockSpec((n_items * tq, LANES), lambda b, p, i: (b * nq + i, p)),
        out_shape=jax.ShapeDtypeStruct((t_total, DIFF_WIDTH), BF16),
        scratch_shapes=_attention_scratch(n_items, 4 * tq, tk),
        compiler_params=_params(("parallel", "parallel", "parallel")),
        name="diff_attention",
    )(*args)


def _hgrn_gates(qs, z, v, lb, g_sc, tri):
    sig = jax.nn.sigmoid(z)
    om = 1.0 - lb
    k = om * (1.0 - sig)
    g = jnp.log2(jnp.maximum(lb + om * sig, FORGET_MIN))
    g1 = g.astype(BF16)
    e1 = g - g1.astype(F32)
    g2 = e1.astype(BF16)
    g3 = (e1 - g2.astype(F32)).astype(BF16)
    G = (jnp.dot(tri, g1, preferred_element_type=F32) + jnp.dot(tri, g2, preferred_element_type=F32)
         + jnp.dot(tri, g3, preferred_element_type=F32))
    g_sc[...] = G
    return qs, k, G, v.astype(BF16)


def _hgrn_operands(gates, g_sc, hmask, rev):
    C, W = SCAN_CHUNK, HGRN_WIDTH
    qs, k, G, vb = gates
    g_last = g_sc[pl.ds(0 if rev else C - 1, 1), :]
    q_in = (qs * jnp.exp2(G)).astype(BF16)
    k_end = (k * jnp.exp2(g_last - G)).astype(BF16)

    def block_diag(a):
        return jnp.concatenate([a] * HGRN_HEADS, axis=0) * hmask

    def ref_rows(block, off):
        return jnp.concatenate([jnp.broadcast_to(g_sc[pl.ds(b0 + off, 1), :], (block, W))
                                for b0 in range(0, C, block)], axis=0)

    levels = []
    for level in range(HGRN_LEVELS):
        block = 8 << level
        half = block // 2
        if level == 0:
            r = ref_rows(block, half)
            qe, ke = G - r, r - G
        else:
            r = ref_rows(block, half if rev else half - 1)
            qe, ke = jnp.minimum(G - r, 0.0), jnp.minimum(r - G, 0.0)
        levels.append(((qs * jnp.exp2(qe)).astype(BF16), block_diag((k * jnp.exp2(ke)).astype(BF16))))
    return q_in, jnp.exp2(g_last), k_end, vb, block_diag(vb), levels


def _hgrn_intra(operands, lmask_ref, bdmask, rev):
    q_in, decay, k_end, vb, v_bd, levels = operands
    ut = lax.dot_general(vb, k_end, _TN, preferred_element_type=F32) * bdmask
    a = None
    for level, (ql, kl) in enumerate(levels):
        al = lax.dot_general(ql, kl, _NT, preferred_element_type=F32)
        m = lmask_ref[rev, level]
        a = jnp.where(m > 0.5, al, 0.0) if level == 0 else a + al * m
    intra = jnp.dot(a.astype(BF16), v_bd, preferred_element_type=F32)
    return q_in, decay, ut, intra


def _hgrn_state(parts, st):
    q_in, decay, ut, intra = parts
    out = lax.dot_general(q_in, st.astype(BF16), _NT, preferred_element_type=F32) + intra
    return out, st * decay + ut


def _hgrn_kernel(*refs, layer, n_chunks, n_seq, has_s0, state_out, state_alias):
    it = iter(refs)
    qf, zf, vf, qb, zb, vb, lbraw, tri_ref, lmask_ref, hmask_ref, bdmask_ref = [next(it) for _ in range(11)]
    s0 = next(it) if has_s0 else None
    if state_alias:
        next(it)
    of_ref, ob_ref = next(it), next(it)
    sto_ref = next(it) if state_out else None
    st_ref, g_sc = next(it), next(it)

    @pl.when(pl.program_id(1) == 0)
    def _():
        if has_s0:
            st_ref[...] = s0[0]
        else:
            st_ref[...] = jnp.zeros(st_ref.shape, F32)

    lbr = lbraw[...]
    ex = jnp.exp(lbr - jnp.max(lbr, axis=0, keepdims=True))
    sm = ex / jnp.sum(ex, axis=0, keepdims=True)
    lbs = jnp.zeros(sm.shape[1:], F32)
    for j in range(1, layer + 1):
        lbs = lbs + sm[j]
    lb_f, lb_b = lbs[0:1], lbs[1:2]

    hmask = hmask_ref[...]
    bdmask = bdmask_ref[...]

    chunks = []
    for cidx in range(n_chunks):
        for s in range(n_seq):
            base = s * n_chunks
            chunks.append((2 * s, 0, pl.ds((base + cidx) * SCAN_CHUNK, SCAN_CHUNK), qf, zf, vf, lb_f, of_ref))
            chunks.append((2 * s + 1, 1, pl.ds((base + n_chunks - 1 - cidx) * SCAN_CHUNK, SCAN_CHUNK),
                           qb, zb, vb, lb_b, ob_ref))
    n = len(chunks)
    state = [st_ref[i] for i in range(2 * n_seq)]
    gates, operands, parts = {}, {}, {}
    for t in range(n + 3):
        if t < n:
            _, d, rows, q_r, z_r, v_r, lb, _ = chunks[t]
            gates[t] = _hgrn_gates(q_r[rows, :], z_r[rows, :], v_r[rows, :], lb, g_sc.at[t], tri_ref[d])
        if 0 <= t - 1 < n:
            operands[t - 1] = _hgrn_operands(gates.pop(t - 1), g_sc.at[t - 1], hmask, chunks[t - 1][1])
        if 0 <= t - 2 < n:
            parts[t - 2] = _hgrn_intra(operands.pop(t - 2), lmask_ref, bdmask, chunks[t - 2][1])
        if 0 <= t - 3 < n:
            si, _, rows, _, _, _, _, o_r = chunks[t - 3]
            out, state[si] = _hgrn_state(parts.pop(t - 3), state[si])
            o_r[rows, :] = out.astype(BF16)
    for i, s in enumerate(state):
        st_ref[i] = s

    if state_out:
        @pl.when(pl.program_id(1) == pl.num_programs(1) - 1)
        def _():
            if not state_alias:
                sto_ref[:, 1:] = jnp.zeros((sto_ref.shape[0], sto_ref.shape[1] - 1) + sto_ref.shape[2:], F32)
            for i, s in enumerate(state):
                for hh in range(HGRN_HEADS):
                    slab = s[hh * HEAD_DIM:(hh + 1) * HEAD_DIM, (hh // 2) * LANES:(hh // 2 + 1) * LANES]
                    sto_ref[i // 2, 0, i % 2, hh] = pltpu.roll(slab, HEAD_DIM, 1) if hh % 2 else slab


def _hgrn_scan(zr, lb_raw, scan_consts, s0, state_prev, state_out, layer, batch, tokens_per_batch):
    t_total = zr.shape[0]
    tb = min(SCAN_BLOCK, tokens_per_batch)
    nblk = tokens_per_batch // tb
    n_seq = SCAN_BLOCK // tb if nblk == 1 else 1
    W = HGRN_WIDTH
    has_s0 = s0 is not None
    assert n_seq == 1 or not has_s0
    state_alias = state_prev is not None
    rows = n_seq * tb
    fblk = lambda col: pl.BlockSpec((rows, W), lambda b, j: (b * nblk + j, col))
    bblk = lambda col: pl.BlockSpec((rows, W), lambda b, j: (b * nblk + nblk - 1 - j, col))
    full = lambda a: pl.BlockSpec(a.shape, lambda b, j: (0,) * a.ndim)
    in_specs = [fblk(0), fblk(1), fblk(3), bblk(0), bblk(2), bblk(3),
                pl.BlockSpec((DEPTH, 2, W), lambda b, j: (0, 0, 0))] + [full(a) for a in scan_consts]
    args = [zr] * 6 + [lb_raw] + list(scan_consts)
    if has_s0:
        in_specs.append(pl.BlockSpec((1, 2, W, W), lambda b, j: (b, 0, 0, 0)))
        args.append(s0)
    aliases = {}
    if state_alias:
        aliases = {len(args): 2}
        in_specs.append(pl.BlockSpec(memory_space=pl.ANY))
        args.append(state_prev)
    out_specs = [pl.BlockSpec((rows, W), lambda b, j: (b * nblk + j, 0)),
                 pl.BlockSpec((rows, W), lambda b, j: (b * nblk + nblk - 1 - j, 0))]
    out_shape = [jax.ShapeDtypeStruct((t_total, W), BF16), jax.ShapeDtypeStruct((t_total, W), BF16)]
    if state_out and not state_alias:
        assert layer == 0
        out_specs.append(pl.BlockSpec((n_seq, DEPTH, 2, HGRN_HEADS, HEAD_DIM, LANES),
                                      lambda b, j: (b, 0, 0, 0, 0, 0)))
    elif state_out:
        out_specs.append(pl.BlockSpec((n_seq, 1, 2, HGRN_HEADS, HEAD_DIM, LANES),
                                      lambda b, j: (b, layer, 0, 0, 0, 0)))
    if state_out:
        out_shape.append(jax.ShapeDtypeStruct((batch, DEPTH, 2, HGRN_HEADS, HEAD_DIM, LANES), F32))
    n_units = 2 * rows // SCAN_CHUNK
    return pl.pallas_call(
        functools.partial(_hgrn_kernel, layer=layer, n_chunks=tb // SCAN_CHUNK, n_seq=n_seq, has_s0=has_s0,
                          state_out=state_out, state_alias=state_alias),
        grid=(batch // n_seq, nblk),
        in_specs=in_specs, out_specs=out_specs, out_shape=out_shape,
        input_output_aliases=aliases,
        scratch_shapes=[pltpu.VMEM((2 * n_seq, W, W), F32), pltpu.VMEM((n_units, SCAN_CHUNK, W), F32)],
        compiler_params=_params(("parallel", "arbitrary")),
        name="hgrn_scan",
    )(*args)


def _outproj_kernel(oa_ref, od_ref, of_ref, ob_ref, g_ref, x_ref, gate_ref, w_ref, sub_ref, hgn_ref,
                    bd_ref, lng_ref, lnb_ref, o_ref, *, lam_init):
    inv_d = 1.0 / HEAD_DIM
    g = g_ref[...].astype(F32)
    bd = bd_ref[...]

    def head_rms(t, gain):
        ssq = jnp.dot((t * t).astype(BF16), bd, preferred_element_type=F32)
        return t * lax.rsqrt(ssq * inv_d + RMS_EPS) * gain

    ma = (oa_ref[...].astype(F32) * g[:, 0:GQA_WIDTH]).astype(BF16)
    md = (head_rms(od_ref[...].astype(F32), sub_ref[...]) * (1.0 - lam_init)
          * g[:, GQA_WIDTH:GQA_WIDTH + DIFF_WIDTH]).astype(BF16)
    mr = (head_rms(of_ref[...].astype(F32) + ob_ref[...].astype(F32), hgn_ref[...])
          * g[:, GQA_WIDTH + DIFF_WIDTH:]).astype(BF16)
    y = (jnp.dot(ma, w_ref[0:GQA_WIDTH, :], preferred_element_type=F32)
         + jnp.dot(md, w_ref[GQA_WIDTH:GQA_WIDTH + DIFF_WIDTH, :], preferred_element_type=F32)
         + jnp.dot(mr, w_ref[GQA_WIDTH + DIFF_WIDTH:, :], preferred_element_type=F32))
    xn = DEEPNORM_ALPHA * x_ref[...] + gate_ref[0] * y
    mu = jnp.mean(xn, axis=-1, keepdims=True)
    xc = xn - mu
    var = jnp.mean(xc * xc, axis=-1, keepdims=True)
    o_ref[...] = xc * lax.rsqrt(var + LN_EPS) * lng_ref[...] + lnb_ref[...]


def _out_projection(out_a, out_d, o_f, o_b, gates, x, gate, w_out_b, subln, hgn, bd, ln_g, ln_b,
                    layer, lam_init):
    t_total = x.shape[0]
    tm = PROJ_ROWS
    mod_rows = t_total // gate.shape[0] // tm
    row = lambda w: pl.BlockSpec((tm, w), lambda i: (i, 0))
    full = lambda shape: pl.BlockSpec(shape, lambda i: (0,) * len(shape))
    return pl.pallas_call(
        functools.partial(_outproj_kernel, lam_init=lam_init),
        grid=(t_total // tm,),
        in_specs=[row(GQA_WIDTH), row(DIFF_WIDTH), row(HGRN_WIDTH), row(HGRN_WIDTH), row(D_MODEL), row(D_MODEL),
                  pl.BlockSpec((1, 1, D_MODEL), lambda i: (i // mod_rows, 0, 0)),
                  pl.BlockSpec((None, D_MODEL, D_MODEL), lambda i: (layer, 0, 0)),
                  full((1, DIFF_WIDTH)), full((1, HGRN_WIDTH)),
                  full((DIFF_WIDTH, DIFF_WIDTH)), full((1, D_MODEL)), full((1, D_MODEL))],
        out_specs=row(D_MODEL),
        out_shape=jax.ShapeDtypeStruct((t_total, D_MODEL), F32),
        compiler_params=_params(("parallel",)),
        name="out_projection",
    )(out_a, out_d, o_f, o_b, gates, x, gate, w_out_b, subln, hgn, bd, ln_g, ln_b)


def _block_diag_ones(n):
    idx = np.arange(n) // HEAD_DIM
    return jnp.asarray((idx[:, None] == idx[None, :]).astype(np.float32), dtype=BF16)


def _scan_constants():
    C, W, H = SCAN_CHUNK, HGRN_WIDTH, HGRN_HEADS
    t = np.arange(C)[:, None]
    u = np.arange(C)[None, :]
    tri = np.stack([u <= t, u >= t])
    s = (np.arange(W) % C)[None, :]
    lmask = np.zeros((2, HGRN_LEVELS, C, W), np.float32)
    for level in range(HGRN_LEVELS):
        block = 8 << level
        same = (t // block) == (s // block)
        if level == 0:
            lmask[0, level] = same & (s <= t)
            lmask[1, level] = same & (s >= t)
        else:
            t_hi, s_hi = (t % block) >= block // 2, (s % block) >= block // 2
            lmask[0, level] = same & t_hi & ~s_hi
            lmask[1, level] = same & ~t_hi & s_hi
    rows = np.arange(H * C)[:, None]
    hmask = (rows // C) == (np.arange(W)[None, :] // HEAD_DIM)
    bdmask = (np.arange(W)[:, None] // HEAD_DIM) == (np.arange(W)[None, :] // HEAD_DIM)
    return (jnp.asarray(tri.astype(np.float32), dtype=BF16), jnp.asarray(lmask),
            jnp.asarray(hmask.astype(np.float32), dtype=BF16), jnp.asarray(bdmask.astype(np.float32)))


def _rope_tables(n_tokens, dim):
    rows = n_tokens // GRID_W
    row = jnp.repeat(jnp.arange(rows, dtype=F32), GRID_W)
    col = jnp.tile(jnp.arange(GRID_W, dtype=F32), rows)
    quarter = dim // 4
    inv_freq = ROPE_THETA ** (-jnp.arange(quarter, dtype=F32) / quarter)
    ar = row[:, None] * inv_freq[None, :]
    ac = col[:, None] * inv_freq[None, :]
    ang = jnp.concatenate([ar, ar, ac, ac], axis=-1)
    cos = jnp.tile(jnp.cos(ang), (1, LANES // dim))
    sin = jnp.tile(jnp.sin(ang), (1, LANES // dim))
    first = (jnp.arange(LANES) % (dim // 2)) < quarter
    return cos, jnp.where(first, -sin, 0.0), jnp.where(first, 0.0, sin)


def _mixer_layer(x, scale, shift, gate, layer, batch, tokens_per_batch, rope_tabs, cache, consts, weights,
                 ctx_out, ctx_prev):
    (w_in_b, w_out_b, qgain, kgain, lam_p, subln, hgn, lb_raw, ln_g, ln_b) = weights
    bdq, bdk, bdd, scan_consts = consts
    lam_init = 0.8 - 0.6 * math.exp(-0.3 * layer)
    outs = _in_projection(x, scale, shift, w_in_b, qgain, kgain, bdq, bdk, rope_tabs, tokens_per_batch, ctx_out,
                          layer, None if ctx_prev is None else ctx_prev[0])
    qa, kdup, vaug, qd, kd, vaugd, gates, zr = outs[:8]
    if cache is None:
        gqa_cache = diff_cache = s0 = None
        tq_a = tq_d = tk = tokens_per_batch
        items_a, items_d = 2, 2
        slabs_a = 2
    else:
        gqa_cache, diff_cache, s0 = cache
        tq_a, tq_d, tk = 512, 256, 512
        items_a = items_d = 1
        slabs_a = 1
    out_a = _gqa_attention(qa, kdup, vaug, gqa_cache, batch, tokens_per_batch, tq_a, tk, items_a, slabs_a)
    out_d = _diff_attention(qd, kd, vaugd, lam_p, diff_cache, batch, tokens_per_batch, tq_d, tk, items_d, lam_init)
    scan = _hgrn_scan(zr, lb_raw, scan_consts, s0, None if ctx_prev is None else ctx_prev[1], ctx_out, layer,
                      batch, tokens_per_batch)
    x_new = _out_projection(out_a, out_d, scan[0], scan[1], gates, x, gate, w_out_b, subln, hgn, bdd, ln_g, ln_b,
                            layer, lam_init)
    return x_new, ((outs[8:], scan[2]) if ctx_out else None)


def kernel(x_prompt, x_sample, cache_gqa_k, cache_gqa_v, cache_diff_k, cache_diff_v, state_hgrn, c, c_ctx,
           w_ada, b_ada, w_in, gqa_q_norm, gqa_k_norm, diff_lambda, diff_subln, hgrn_lower_bounds, hgrn_norm,
           w_out, ln_g, ln_b):
    batch, seq, _ = x_prompt.shape
    dec_batch, dec_seq, _ = x_sample.shape
    past = cache_gqa_k.shape[2]
    H, HD = HGRN_HEADS, HEAD_DIM

    cond = jnp.zeros((MOD_ROWS, D_MODEL), F32).at[0].set(c_ctx).at[1:1 + dec_batch].set(c)
    mod = _modulation(cond, w_ada, b_ada)
    consts = (_block_diag_ones(GQA_WIDTH), _block_diag_ones(LANES), _block_diag_ones(DIFF_WIDTH),
              _scan_constants())
    rope_tabs = _rope_tables(dec_seq, HEAD_DIM) + _rope_tables(dec_seq, DIFF_QK_DIM)
    w_in_b = w_in.astype(BF16)
    w_out_b = w_out.astype(BF16)

    def layer_weights(l):
        return (w_in_b, w_out_b,
                jnp.tile(gqa_q_norm[l], GQA_WIDTH // HD)[None, :] * (HD ** -0.5 * LOG2E),
                jnp.tile(gqa_k_norm[l], LANES // HD)[None, :],
                diff_lambda[l],
                jnp.tile(diff_subln[l], DIFF_WIDTH // HD)[None, :],
                jnp.tile(hgrn_norm[l], HGRN_WIDTH // HD)[None, :],
                hgrn_lower_bounds, ln_g[l][None, :], ln_b[l][None, :])

    def mod_rows(l, lo, n):
        m = mod[l, lo:lo + n]
        return (m[:, None, 0:D_MODEL], m[:, None, D_MODEL:2 * D_MODEL], m[:, None, 2 * D_MODEL:])

    x = x_prompt.reshape(batch * seq, D_MODEL)
    ctx = None
    for l in range(DEPTH):
        shift, scale, gate = mod_rows(l, 0, 1)
        x, ctx = _mixer_layer(x, scale, shift, gate, l, batch, seq, None, None, consts, layer_weights(l), True, ctx)
    y_prompt = x.reshape(batch, seq, D_MODEL)
    (kn, av, dk, dv), st = ctx
    new_leaves = [kn.reshape(batch, DEPTH, seq, GQA_KV_HEADS, HD), av.reshape(batch, DEPTH, seq, GQA_KV_HEADS, HD),
                  dk.reshape(batch, DEPTH, seq, DIFF_HEADS, HD), dv.reshape(batch, DEPTH, seq, DIFF_HEADS, HD),
                  jnp.swapaxes(st[..., :HD], -1, -2)]

    ones = jnp.ones((dec_batch, past, HD), F32)
    eye = jnp.eye(H, dtype=F32)
    x = x_sample.reshape(dec_batch * dec_seq, D_MODEL)
    for l in range(DEPTH):
        shift, scale, gate = mod_rows(l, 1, dec_batch)
        ck, cv = cache_gqa_k[:, l], cache_gqa_v[:, l]
        ckd = jnp.stack([jnp.concatenate([ck[:, :, g], ck[:, :, g]], axis=-1) for g in range(GQA_KV_HEADS)], axis=1)
        cvd = jnp.stack([jnp.concatenate([cv[:, :, g], ones], axis=-1) for g in range(GQA_KV_HEADS)], axis=1)
        dkc = cache_diff_k[:, l].reshape(dec_batch, past, DIFF_WIDTH)
        dvc = cache_diff_v[:, l]
        dvd = jnp.stack([jnp.concatenate([dvc[:, :, h], ones] if h % 2 == 0 else [ones, dvc[:, :, h]], axis=-1)
                         for h in range(DIFF_HEADS)], axis=1)
        s0t = jnp.swapaxes(state_hgrn[:, l].astype(F32), -1, -2)
        s0 = (s0t[:, :, :, :, None, :] * eye[None, None, :, None, :, None]).reshape(dec_batch, 2, H * HD, H * HD)
        cache = ((ckd.astype(BF16), cvd.astype(BF16)), (dkc.astype(BF16), dvd.astype(BF16)), s0)
        x, _ = _mixer_layer(x, scale, shift, gate, l, dec_batch, dec_seq, rope_tabs, cache, consts,
                            layer_weights(l), False, None)
    y_sample = x.reshape(dec_batch, dec_seq, D_MODEL)
    return (y_prompt, y_sample) + tuple(new_leaves)
```

```python
import functools
import math

import numpy as np
import jax
import jax.numpy as jnp
from jax import lax
from jax.experimental import pallas as pl
from jax.experimental.pallas import tpu as pltpu

F32 = jnp.float32
BF16 = jnp.bfloat16

D_MODEL = 1024
DEPTH = 2
GRID_W = 64
HEAD_DIM = 64
GQA_WIDTH = 512
GQA_KV_HEADS = 2
DIFF_WIDTH = 256
DIFF_HEADS = 4
DIFF_QK_DIM = 32
HGRN_WIDTH = 256
HGRN_HEADS = 4
IN_WIDTH = 3584
SCAN_CHUNK = 64
HGRN_LEVELS = 4
ROPE_THETA = 10000.0
RMS_EPS = 1e-6
LN_EPS = 1e-5
FORGET_MIN = 1e-6
DEEPNORM_ALPHA = (2 * DEPTH) ** 0.25
LANES = 128
SUBLANES = 8
MOD_ROWS = SUBLANES
VMEM_LIMIT = 48 * 1024 * 1024
SM_ROWS = 32
PROJ_ROWS = 512
SCAN_BLOCK = 512
LOG2E = math.log2(math.e)

_A_Q, _A_K, _A_V, _A_G = 0, 512, 640, 768
_D_Q, _D_K, _D_V, _D_G = 1280, 1536, 1792, 2048
_R_Z, _R_G = 2304, 3328

_NT = (((1,), (1,)), ((), ()))
_TN = (((0,), (0,)), ((), ()))


def _silu(x):
    return x * jax.nn.sigmoid(x)


def _params(sem):
    return pltpu.CompilerParams(dimension_semantics=sem, vmem_limit_bytes=VMEM_LIMIT)


def _mod_kernel(c_ref, w_ref, b_ref, o_ref):
    c = c_ref[...]
    o_ref[0] = jnp.dot(_silu(c), w_ref[0], preferred_element_type=F32) + b_ref[0]


def _modulation(cond, w_ada, b_ada):
    tn = 1024
    return pl.pallas_call(
        _mod_kernel,
        grid=(DEPTH, 3 * D_MODEL // tn),
        in_specs=[pl.BlockSpec((MOD_ROWS, D_MODEL), lambda l, j: (0, 0)),
                  pl.BlockSpec((1, D_MODEL, tn), lambda l, j: (l, 0, j)),
                  pl.BlockSpec((1, 1, tn), lambda l, j: (l, 0, j))],
        out_specs=pl.BlockSpec((1, MOD_ROWS, tn), lambda l, j: (l, 0, j)),
        out_shape=jax.ShapeDtypeStruct((DEPTH, MOD_ROWS, 3 * D_MODEL), F32),
        compiler_params=_params(("arbitrary", "arbitrary")),
        name="modulation",
    )(cond, w_ada, b_ada.reshape(DEPTH, 1, 3 * D_MODEL))


def _rope(x, c, s1, s2, shift):
    return x * c + pltpu.roll(x, LANES - shift, 1) * s1 + pltpu.roll(x, shift, 1) * s2


def _inproj_kernel(*refs, rope, ctx_out, ctx_alias):
    it = iter(refs)
    x_ref, sc_ref, sh_ref, w_ref, qg_ref, kg_ref, bdq_ref, bdk_ref = [next(it) for _ in range(8)]
    if rope:
        ca, s1a, s2a, cd, s1d, s2d = [next(it)[...] for _ in range(6)]
    if ctx_alias:
        for _ in range(4):
            next(it)
    qa_o, kdup_o, vaug_o, qd_o, kd_o, vaugd_o, gates_o, zr_o = [next(it) for _ in range(8)]
    if ctx_out:
        ctx_refs = [next(it) for _ in range(4)]
        if not ctx_alias:
            for r in ctx_refs:
                r[:, 1:] = jnp.zeros((r.shape[0], r.shape[1] - 1) + r.shape[2:], F32)

        def put_ctx(n, val):
            r = ctx_refs[n]
            r[:, 0] = val.reshape(r.shape[0], r.shape[2], r.shape[3])

    h = (x_ref[...] * (1.0 + sc_ref[...]) + sh_ref[...]).astype(BF16)

    def proj(lo, hi):
        return jnp.dot(h, w_ref[:, lo:hi], preferred_element_type=F32)

    tm = h.shape[0]
    low = lax.broadcasted_iota(jnp.int32, (tm, LANES), 1) < HEAD_DIM
    inv_d = 1.0 / HEAD_DIM

    aq = proj(_A_Q, _A_K)
    ssq = jnp.dot((aq * aq).astype(BF16), bdq_ref[...], preferred_element_type=F32)
    qn = aq * lax.rsqrt(ssq * inv_d + RMS_EPS) * qg_ref[...]
    for j in range(GQA_WIDTH // LANES):
        slab = qn[:, j * LANES:(j + 1) * LANES]
        if rope:
            slab = _rope(slab, ca, s1a, s2a, HEAD_DIM // 4)
        qa_o[:, j * LANES:(j + 1) * LANES] = slab.astype(BF16)

    ak = proj(_A_K, _A_V)
    ssqk = jnp.dot((ak * ak).astype(BF16), bdk_ref[...], preferred_element_type=F32)
    kn = ak * lax.rsqrt(ssqk * inv_d + RMS_EPS) * kg_ref[...]
    if ctx_out:
        put_ctx(0, kn)
    if rope:
        kn = _rope(kn, ca, s1a, s2a, HEAD_DIM // 4)
    kr = pltpu.roll(kn, HEAD_DIM, 1)
    kdup_o[0] = jnp.where(low, kn, kr).astype(BF16)
    kdup_o[1] = jnp.where(low, kr, kn).astype(BF16)

    av = proj(_A_V, _A_G)
    if ctx_out:
        put_ctx(1, av)
    vr = pltpu.roll(av, HEAD_DIM, 1)
    vaug_o[0] = jnp.where(low, av, 1.0).astype(BF16)
    vaug_o[1] = jnp.where(low, vr, 1.0).astype(BF16)

    gates_o[:, 0:GQA_WIDTH] = _silu(proj(_A_G, _D_Q)).astype(BF16)

    dq = proj(_D_Q, _D_K) * (DIFF_QK_DIM ** -0.5 * LOG2E)
    dk = proj(_D_K, _D_V)
    if ctx_out:
        put_ctx(2, dk)
    for j in range(DIFF_WIDTH // LANES):
        sq = dq[:, j * LANES:(j + 1) * LANES]
        sk = dk[:, j * LANES:(j + 1) * LANES]
        if rope:
            sq = _rope(sq, cd, s1d, s2d, DIFF_QK_DIM // 4)
            sk = _rope(sk, cd, s1d, s2d, DIFF_QK_DIM // 4)
        qd_o[:, j * LANES:(j + 1) * LANES] = sq.astype(BF16)
        kd_o[:, j * LANES:(j + 1) * LANES] = sk.astype(BF16)
    dv = proj(_D_V, _D_G)
    if ctx_out:
        put_ctx(3, dv)
    for j in range(DIFF_WIDTH // LANES):
        sv = dv[:, j * LANES:(j + 1) * LANES]
        vaugd_o[2 * j] = jnp.where(low, sv, 1.0).astype(BF16)
        vaugd_o[2 * j + 1] = jnp.where(low, 1.0, sv).astype(BF16)
    gates_o[:, GQA_WIDTH:GQA_WIDTH + DIFF_WIDTH] = _silu(proj(_D_G, _R_Z)).astype(BF16)

    gates_o[:, GQA_WIDTH + DIFF_WIDTH:] = _silu(proj(_R_G, IN_WIDTH)).astype(BF16)
    zr = proj(_R_Z, _R_G)
    zr_o[:, 0:HGRN_WIDTH] = _silu(zr[:, 0:HGRN_WIDTH])
    zr_o[:, HGRN_WIDTH:] = zr[:, HGRN_WIDTH:]


def _mod_spec(mod_sel, which, layer, t_total, tm):
    _, row0, n_rows = mod_sel
    blocks_per_row = t_total // n_rows // tm
    return pl.BlockSpec((None, None, None, 1, D_MODEL), lambda i: (layer, row0 + i // blocks_per_row, which, 0, 0))


def _in_projection(x, mod_sel, w_in_b, qgain, kgain, bdq, bdk, rope_tabs, tokens_per_batch, ctx_out,
                   layer, ctx_prev):
    t_total = x.shape[0]
    tm = PROJ_ROWS
    nb_rows = tokens_per_batch // tm
    rope = rope_tabs is not None
    assert not rope or nb_rows >= 1
    full = lambda shape: pl.BlockSpec(shape, lambda i: (0,) * len(shape))
    in_specs = [pl.BlockSpec((tm, D_MODEL), lambda i: (i, 0)),
                _mod_spec(mod_sel, 1, layer, t_total, tm), _mod_spec(mod_sel, 0, layer, t_total, tm),
                pl.BlockSpec((None, D_MODEL, IN_WIDTH), lambda i: (layer, 0, 0)),
                full((1, GQA_WIDTH)), full((1, LANES)),
                full((GQA_WIDTH, GQA_WIDTH)), full((LANES, LANES))]
    args = [x, mod_sel[0], mod_sel[0], w_in_b, qgain, kgain, bdq, bdk]
    if rope:
        in_specs += [pl.BlockSpec((tm, LANES), lambda i: (i % nb_rows, 0))] * 6
        args += list(rope_tabs)
    row = lambda w: pl.BlockSpec((tm, w), lambda i: (i, 0))
    stk = lambda n: pl.BlockSpec((n, tm, LANES), lambda i: (0, i, 0))
    out_specs = [row(GQA_WIDTH), stk(2), stk(2), row(DIFF_WIDTH), row(DIFF_WIDTH), stk(4),
                 row(D_MODEL), row(4 * HGRN_WIDTH)]
    out_shape = [jax.ShapeDtypeStruct((t_total, GQA_WIDTH), BF16),
                 jax.ShapeDtypeStruct((2, t_total, LANES), BF16),
                 jax.ShapeDtypeStruct((2, t_total, LANES), BF16),
                 jax.ShapeDtypeStruct((t_total, DIFF_WIDTH), BF16),
                 jax.ShapeDtypeStruct((t_total, DIFF_WIDTH), BF16),
                 jax.ShapeDtypeStruct((4, t_total, LANES), BF16),
                 jax.ShapeDtypeStruct((t_total, D_MODEL), BF16),
                 jax.ShapeDtypeStruct((t_total, 4 * HGRN_WIDTH), F32)]
    aliases = {}
    if ctx_out:
        seq = tokens_per_batch
        nseq = tm // seq
        assert nseq * seq == tm
        widths = (LANES, LANES, DIFF_WIDTH, DIFF_WIDTH)
        if ctx_prev is not None:
            aliases = {len(args) + n: len(out_specs) + n for n in range(4)}
            in_specs += [pl.BlockSpec(memory_space=pl.ANY)] * 4
            args += list(ctx_prev)
        if ctx_prev is None:
            assert layer == 0
            out_specs += [pl.BlockSpec((nseq, DEPTH, seq, w), lambda i: (i, 0, 0, 0)) for w in widths]
        else:
            out_specs += [pl.BlockSpec((nseq, 1, seq, w), lambda i: (i, layer, 0, 0)) for w in widths]
        out_shape += [jax.ShapeDtypeStruct((t_total // seq, DEPTH, seq, w), F32) for w in widths]
    return pl.pallas_call(
        functools.partial(_inproj_kernel, rope=rope, ctx_out=ctx_out, ctx_alias=bool(aliases)),
        grid=(t_total // tm,),
        in_specs=in_specs, out_specs=out_specs, out_shape=out_shape,
        input_output_aliases=aliases,
        compiler_params=_params(("parallel",)),
        name="in_projection",
    )(*args)


def _chunk_rows(j, tk):
    return pl.ds(j * tk, tk) if isinstance(j, int) else pl.ds(pl.multiple_of(j * tk, tk), tk)


def _attention_scratch(n_items, rows, tk):
    rows = n_items * rows
    return ([pltpu.VMEM((rows, LANES), BF16)]
            + [pltpu.VMEM((rows, tk), F32)] * 2 + [pltpu.VMEM((rows, tk), BF16)] * 2
            + [pltpu.VMEM((rows, LANES), F32)] * 4)


def _item_scratch(refs, item, rows):
    return [r.at[pl.ds(item * rows, rows)] for r in refs]


def _softmax_stage(s_sc, p_sc, a_sc, m_sc, keys, first):
    for r in range(s_sc.shape[0] // SM_ROWS):
        rows = pl.ds(r * SM_ROWS, SM_ROWS)
        s = s_sc[rows, 0:keys]
        m_cur = jnp.max(s, axis=1, keepdims=True)
        if first:
            m_new = jnp.broadcast_to(m_cur, (SM_ROWS, LANES))
        else:
            m_prev = m_sc[rows, :]
            m_new = jnp.maximum(m_prev, m_cur)
            a_sc[rows, :] = jnp.exp2(m_prev - m_new)
        m_sc[rows, :] = m_new
        p_sc[rows, 0:keys] = jnp.exp2(s - jnp.concatenate([m_new] * (keys // LANES), axis=1)).astype(BF16)


def _item_timeline(n_new, cache, k_at, v_at, start, qk, sm, av, finish):
    o = 0 if cache is None else 1
    n = n_new + o
    k_of = lambda c: cache[0]() if (o and c == 0) else k_at(c - o)
    v_of = lambda c: cache[1]() if (o and c == 0) else v_at(c - o)
    steps = []
    for t in range(n + 2):
        step = [start] if t == 0 else []
        if t < n:
            step.append(lambda t=t: qk(t % 2, k_of(t)))
        if 0 <= t - 1 < n:
            step.append(lambda t=t: sm((t - 1) % 2, t == 1))
        if 0 <= t - 2 < n:
            step.append(lambda t=t: av(t % 2, v_of(t - 2), t == 2))
        if t == n + 1:
            step.append(finish)
        steps.append(step)
    return steps


def _run_items(timelines):
    lag = max(1, len(timelines[0]) - 2)
    for g in range(lag * (len(timelines) - 1) + len(timelines[0])):
        for i, steps in enumerate(timelines):
            if 0 <= g - i * lag < len(steps):
                for stage in steps[g - i * lag]:
                    stage()


def _gqa_kernel(*refs, has_cache, tq, tk, nk, n_items, shared_kv, slabs):
    if has_cache:
        q_ref, kc_ref, vc_ref = refs[:3]
        refs = refs[3:]
    else:
        q_ref = refs[0]
        refs = refs[1:]
    k_ref, v_ref, o_ref = refs[:3]
    low = lax.broadcasted_iota(jnp.int32, (tq, LANES), 1) < HEAD_DIM
    rows = 2 * slabs * tq

    def item_stages(item):
        qs_sc, s0, s1, p0, p1, a0, a1, m_sc, acc_sc = _item_scratch(refs[3:], item, rows)
        s_b, p_b, a_b = (s0, s1), (p0, p1), (a0, a1)

        def start():
            for j in range(slabs):
                q = q_ref[pl.ds(item * tq, tq), j * LANES:(j + 1) * LANES]
                zero = jnp.zeros_like(q)
                qs_sc[2 * j * tq:(2 * j + 1) * tq, :] = jnp.where(low, q, zero)
                qs_sc[(2 * j + 1) * tq:(2 * j + 2) * tq, :] = jnp.where(low, zero, q)

        keys = {}

        def qk(slot, k):
            keys[slot] = k.shape[0]
            s_b[slot][:, 0:k.shape[0]] = lax.dot_general(qs_sc[...], k, _NT, preferred_element_type=F32)

        def sm(slot, first):
            _softmax_stage(s_b[slot], p_b[slot], a_b[slot], m_sc, keys[slot], first)

        def av(slot, v, first):
            pv = jnp.dot(p_b[slot][:, 0:v.shape[0]], v, preferred_element_type=F32)
            acc_sc[...] = pv if first else a_b[slot][...] * acc_sc[...] + pv

        def finish():
            acc = acc_sc[...]
            on = acc / pltpu.roll(acc, HEAD_DIM, 1)
            for j in range(slabs):
                first, second = on[2 * j * tq:(2 * j + 1) * tq], on[(2 * j + 1) * tq:(2 * j + 2) * tq]
                o_ref[pl.ds(item * tq, tq), j * LANES:(j + 1) * LANES] = jnp.where(
                    low, first, pltpu.roll(second, HEAD_DIM, 1)).astype(BF16)

        base = 0 if shared_kv else item * nk
        chunk = lambda ref: (lambda j: ref[0, _chunk_rows(base + j, tk), :])
        cache = (lambda: kc_ref[0, 0], lambda: vc_ref[0, 0]) if has_cache else None
        return _item_timeline(nk, cache, chunk(k_ref), chunk(v_ref), start, qk, sm, av, finish)

    _run_items([item_stages(item) for item in range(n_items)])


def _item_geometry(batch, tokens_per_batch, tq, n_items):
    nq = tokens_per_batch // tq
    if nq == 1:
        return batch // n_items, 1, n_items * tokens_per_batch, False
    assert nq % n_items == 0
    return batch, nq // n_items, tokens_per_batch, True


def _gqa_attention(qa, kdup, vaug, cache, batch, tokens_per_batch, tq, tk, n_items, slabs, layer):
    t_total = qa.shape[0]
    nk = tokens_per_batch // tk
    has_cache = cache is not None
    bsteps, nq, kv_rows, shared_kv = _item_geometry(batch, tokens_per_batch, tq, n_items)
    assert shared_kv or not has_cache
    width = slabs * LANES
    group = lambda p: p * slabs // 2
    in_specs = [pl.BlockSpec((n_items * tq, width), lambda b, p, i: (b * nq + i, p))]
    args = [qa]
    if has_cache:
        past = cache[0].shape[3]
        in_specs += [pl.BlockSpec((1, None, 1, past, LANES), lambda b, p, i: (b, layer, group(p), 0, 0))] * 2
        args += list(cache)
    in_specs += [pl.BlockSpec((1, kv_rows, LANES), lambda b, p, i: (group(p), b, 0))] * 2
    args += [kdup, vaug]
    return pl.pallas_call(
        functools.partial(_gqa_kernel, has_cache=has_cache, tq=tq, tk=tk, nk=nk, n_items=n_items,
                          shared_kv=shared_kv, slabs=slabs),
        grid=(bsteps, GQA_WIDTH // width, nq),
        in_specs=in_specs,
        out_specs=pl.BlockSpec((n_items * tq, width), lambda b, p, i: (b * nq + i, p)),
        out_shape=jax.ShapeDtypeStruct((t_total, GQA_WIDTH), BF16),
        scratch_shapes=_attention_scratch(n_items, 2 * slabs * tq, tk),
        compiler_params=_params(("parallel", "parallel", "parallel")),
        name="gqa_attention",
    )(*args)


def _diff_kernel(*refs, has_cache, tq, tk, nk, n_items, shared_kv, lam_init):
    q_ref, lp_ref = refs[:2]
    if has_cache:
        kc_ref, vc0_ref, vc1_ref = refs[2:5]
        refs = refs[5:]
    else:
        refs = refs[2:]
    k_ref, v0_ref, v1_ref, o_ref = refs[:4]
    lane = lax.broadcasted_iota(jnp.int32, (tq, LANES), 1)
    lp = lp_ref[...]
    lam = (jnp.exp(jnp.sum(lp[0:1] * lp[1:2], axis=1, keepdims=True))
           - jnp.exp(jnp.sum(lp[2:3] * lp[3:4], axis=1, keepdims=True)) + lam_init)
    h2 = 2 * tq
    rows = 4 * tq

    def item_stages(item):
        qs_sc, s0, s1, p0, p1, a0, a1, m_sc, acc_sc = _item_scratch(refs[4:], item, rows)
        s_b, p_b, a_b = (s0, s1), (p0, p1), (a0, a1)

        def start():
            q = q_ref[pl.ds(item * tq, tq), :]
            zero = jnp.zeros_like(q)
            for j in range(4):
                qs_sc[j * tq:(j + 1) * tq, :] = jnp.where((lane >> 5) == j, q, zero)

        keys = {}

        def qk(slot, k):
            keys[slot] = k.shape[0]
            s_b[slot][:, 0:k.shape[0]] = lax.dot_general(qs_sc[...], k, _NT, preferred_element_type=F32)

        def sm(slot, first):
            _softmax_stage(s_b[slot], p_b[slot], a_b[slot], m_sc, keys[slot], first)

        def av(slot, v, first):
            for half, vh in zip((pl.ds(0, h2), pl.ds(h2, h2)), v):
                pv = jnp.dot(p_b[slot][half, 0:vh.shape[0]], vh, preferred_element_type=F32)
                acc_sc[half, :] = pv if first else a_b[slot][half, :] * acc_sc[half, :] + pv

        def finish():
            acc = acc_sc[...]
            on = acc / pltpu.roll(acc, HEAD_DIM, 1)
            o0 = on[0:tq] - lam * on[tq:h2]
            o1 = on[h2:h2 + tq] - lam * on[h2 + tq:]
            o_ref[pl.ds(item * tq, tq), :] = jnp.where(lane < HEAD_DIM, o0, o1).astype(BF16)

        base = 0 if shared_kv else item * nk
        k_at = lambda j: k_ref[_chunk_rows(base + j, tk), :]
        v_at = lambda j: (v0_ref[0, _chunk_rows(base + j, tk), :], v1_ref[0, _chunk_rows(base + j, tk), :])
        cache = (lambda: kc_ref[0], lambda: (vc0_ref[0, 0], vc1_ref[0, 0])) if has_cache else None
        return _item_timeline(nk, cache, k_at, v_at, start, qk, sm, av, finish)

    _run_items([item_stages(item) for item in range(n_items)])


def _diff_attention(qd, kd, vaugd, lam_p, cache, batch, tokens_per_batch, tq, tk, n_items, layer, lam_init):
    t_total = qd.shape[0]
    nk = tokens_per_batch // tk
    has_cache = cache is not None
    bsteps, nq, kv_rows, shared_kv = _item_geometry(batch, tokens_per_batch, tq, n_items)
    assert shared_kv or not has_cache
    in_specs = [pl.BlockSpec((n_items * tq, LANES), lambda b, p, i: (b * nq + i, p)),
                pl.BlockSpec((4, DIFF_QK_DIM), lambda b, p, i: (0, 0))]
    args = [qd, lam_p]
    if has_cache:
        ckd, cvd = cache
        past = ckd.shape[2]
        in_specs += [pl.BlockSpec((1, None, past, LANES), lambda b, p, i: (b, layer, 0, p)),
                     pl.BlockSpec((1, None, 1, past, LANES), lambda b, p, i: (b, layer, 2 * p, 0, 0)),
                     pl.BlockSpec((1, None, 1, past, LANES), lambda b, p, i: (b, layer, 2 * p + 1, 0, 0))]
        args += [ckd, cvd, cvd]
    in_specs += [pl.BlockSpec((kv_rows, LANES), lambda b, p, i: (b, p)),
                 pl.BlockSpec((1, kv_rows, LANES), lambda b, p, i: (2 * p, b, 0)),
                 pl.BlockSpec((1, kv_rows, LANES), lambda b, p, i: (2 * p + 1, b, 0))]
    args += [kd, vaugd, vaugd]
    return pl.pallas_call(
        functools.partial(_diff_kernel, has_cache=has_cache, tq=tq, tk=tk, nk=nk, n_items=n_items,
                          shared_kv=shared_kv, lam_init=lam_init),
        grid=(bsteps, DIFF_WIDTH // LANES, nq),
        in_specs=in_specs,
        out_specs=pl.BlockSpec((n_items * tq, LANES), lambda b, p, i: (b * nq + i, p)),
        out_shape=jax.ShapeDtypeStruct((t_total, DIFF_WIDTH), BF16),
        scratch_shapes=_attention_scratch(n_items, 4 * tq, tk),
        compiler_params=_params(("parallel", "parallel", "parallel")),
        name="diff_attention",
    )(*args)


def _hgrn_gates(qs, z, v, lb, g_sc, tri):
    sig = jax.nn.sigmoid(z)
    om = 1.0 - lb
    k = om * (1.0 - sig)
    g = jnp.log2(jnp.maximum(lb + om * sig, FORGET_MIN))
    g1 = g.astype(BF16)
    e1 = g - g1.astype(F32)
    g2 = e1.astype(BF16)
    g3 = (e1 - g2.astype(F32)).astype(BF16)
    G = (jnp.dot(tri, g1, preferred_element_type=F32) + jnp.dot(tri, g2, preferred_element_type=F32)
         + jnp.dot(tri, g3, preferred_element_type=F32))
    g_sc[...] = G
    return qs, k, G, v.astype(BF16)


def _hgrn_operands(gates, g_sc, hmask, rev):
    C, W = SCAN_CHUNK, HGRN_WIDTH
    qs, k, G, vb = gates
    g_last = g_sc[pl.ds(0 if rev else C - 1, 1), :]
    q_in = (qs * jnp.exp2(G)).astype(BF16)
    k_end = (k * jnp.exp2(g_last - G)).astype(BF16)

    def block_diag(a):
        return jnp.concatenate([a] * HGRN_HEADS, axis=0) * hmask

    def ref_rows(block, off):
        return jnp.concatenate([jnp.broadcast_to(g_sc[pl.ds(b0 + off, 1), :], (block, W))
                                for b0 in range(0, C, block)], axis=0)

    levels = []
    for level in range(HGRN_LEVELS):
        block = 8 << level
        half = block // 2
        if level == 0:
            r = ref_rows(block, half)
            qe, ke = G - r, r - G
        else:
            r = ref_rows(block, half if rev else half - 1)
            qe, ke = jnp.minimum(G - r, 0.0), jnp.minimum(r - G, 0.0)
        levels.append(((qs * jnp.exp2(qe)).astype(BF16), block_diag((k * jnp.exp2(ke)).astype(BF16))))
    return q_in, jnp.exp2(g_last), k_end, vb, block_diag(vb), levels


def _hgrn_intra(operands, lmask_ref, bdmask, rev):
    q_in, decay, k_end, vb, v_bd, levels = operands
    ut = lax.dot_general(vb, k_end, _TN, preferred_element_type=F32) * bdmask
    a = None
    for level, (ql, kl) in enumerate(levels):
        al = lax.dot_general(ql, kl, _NT, preferred_element_type=F32)
        m = lmask_ref[rev, level]
        a = jnp.where(m > 0.5, al, 0.0) if level == 0 else a + al * m
    intra = jnp.dot(a.astype(BF16), v_bd, preferred_element_type=F32)
    return q_in, decay, ut, intra


def _hgrn_state(parts, st):
    q_in, decay, ut, intra = parts
    out = lax.dot_general(q_in, st.astype(BF16), _NT, preferred_element_type=F32) + intra
    return out, st * decay + ut


def _hgrn_kernel(*refs, layer, n_chunks, n_seq, has_s0, state_out, state_alias):
    it = iter(refs)
    qf, zf, vf, qb, zb, vb, lbraw, tri_ref, lmask_ref, hmask_ref, bdmask_ref = [next(it) for _ in range(11)]
    s0 = next(it) if has_s0 else None
    if state_alias:
        next(it)
    of_ref, ob_ref = next(it), next(it)
    sto_ref = next(it) if state_out else None
    st_ref, g_sc = next(it), next(it)

    @pl.when(pl.program_id(1) == 0)
    def _():
        if has_s0:
            st_ref[...] = s0[0]
        else:
            st_ref[...] = jnp.zeros(st_ref.shape, F32)

    lbr = lbraw[...]
    ex = jnp.exp(lbr - jnp.max(lbr, axis=0, keepdims=True))
    sm = ex / jnp.sum(ex, axis=0, keepdims=True)
    lbs = jnp.zeros(sm.shape[1:], F32)
    for j in range(1, layer + 1):
        lbs = lbs + sm[j]
    lb_f, lb_b = lbs[0:1], lbs[1:2]

    hmask = hmask_ref[...]
    bdmask = bdmask_ref[...]

    chunks = []
    for cidx in range(n_chunks):
        for s in range(n_seq):
            base = s * n_chunks
            chunks.append((2 * s, 0, pl.ds((base + cidx) * SCAN_CHUNK, SCAN_CHUNK), qf, zf, vf, lb_f, of_ref))
            chunks.append((2 * s + 1, 1, pl.ds((base + n_chunks - 1 - cidx) * SCAN_CHUNK, SCAN_CHUNK),
                           qb, zb, vb, lb_b, ob_ref))
    n = len(chunks)
    state = [st_ref[i] for i in range(2 * n_seq)]
    gates, operands, parts = {}, {}, {}
    for t in range(n + 3):
        if t < n:
            _, d, rows, q_r, z_r, v_r, lb, _ = chunks[t]
            gates[t] = _hgrn_gates(q_r[rows, :], z_r[rows, :], v_r[rows, :], lb, g_sc.at[t], tri_ref[d])
        if 0 <= t - 1 < n:
            operands[t - 1] = _hgrn_operands(gates.pop(t - 1), g_sc.at[t - 1], hmask, chunks[t - 1][1])
        if 0 <= t - 2 < n:
            parts[t - 2] = _hgrn_intra(operands.pop(t - 2), lmask_ref, bdmask, chunks[t - 2][1])
        if 0 <= t - 3 < n:
            si, _, rows, _, _, _, _, o_r = chunks[t - 3]
            out, state[si] = _hgrn_state(parts.pop(t - 3), state[si])
            o_r[rows, :] = out.astype(BF16)
    for i, s in enumerate(state):
        st_ref[i] = s

    if state_out:
        @pl.when(pl.program_id(1) == pl.num_programs(1) - 1)
        def _():
            if not state_alias:
                sto_ref[:, 1:] = jnp.zeros((sto_ref.shape[0], sto_ref.shape[1] - 1) + sto_ref.shape[2:], F32)
            for i, s in enumerate(state):
                for hh in range(HGRN_HEADS):
                    slab = s[hh * HEAD_DIM:(hh + 1) * HEAD_DIM, (hh // 2) * LANES:(hh // 2 + 1) * LANES]
                    sto_ref[i // 2, 0, i % 2, hh] = pltpu.roll(slab, HEAD_DIM, 1) if hh % 2 else slab


def _hgrn_scan(zr, lb_raw, scan_consts, s0, state_prev, state_out, layer, batch, tokens_per_batch):
    t_total = zr.shape[0]
    tb = min(SCAN_BLOCK, tokens_per_batch)
    nblk = tokens_per_batch // tb
    n_seq = SCAN_BLOCK // tb if nblk == 1 else 1
    W = HGRN_WIDTH
    has_s0 = s0 is not None
    assert n_seq == 1 or not has_s0
    state_alias = state_prev is not None
    rows = n_seq * tb
    fblk = lambda col: pl.BlockSpec((rows, W), lambda b, j: (b * nblk + j, col))
    bblk = lambda col: pl.BlockSpec((rows, W), lambda b, j: (b * nblk + nblk - 1 - j, col))
    full = lambda a: pl.BlockSpec(a.shape, lambda b, j: (0,) * a.ndim)
    in_specs = [fblk(0), fblk(1), fblk(3), bblk(0), bblk(2), bblk(3),
                pl.BlockSpec((DEPTH, 2, W), lambda b, j: (0, 0, 0))] + [full(a) for a in scan_consts]
    args = [zr] * 6 + [lb_raw] + list(scan_consts)
    if has_s0:
        in_specs.append(pl.BlockSpec((1, None, 2, W, W), lambda b, j: (b, layer, 0, 0, 0)))
        args.append(s0)
    aliases = {}
    if state_alias:
        aliases = {len(args): 2}
        in_specs.append(pl.BlockSpec(memory_space=pl.ANY))
        args.append(state_prev)
    out_specs = [pl.BlockSpec((rows, W), lambda b, j: (b * nblk + j, 0)),
                 pl.BlockSpec((rows, W), lambda b, j: (b * nblk + nblk - 1 - j, 0))]
    out_shape = [jax.ShapeDtypeStruct((t_total, W), BF16), jax.ShapeDtypeStruct((t_total, W), BF16)]
    if state_out and not state_alias:
        assert layer == 0
        out_specs.append(pl.BlockSpec((n_seq, DEPTH, 2, HGRN_HEADS, HEAD_DIM, LANES),
                                      lambda b, j: (b, 0, 0, 0, 0, 0)))
    elif state_out:
        out_specs.append(pl.BlockSpec((n_seq, 1, 2, HGRN_HEADS, HEAD_DIM, LANES),
                                      lambda b, j: (b, layer, 0, 0, 0, 0)))
    if state_out:
        out_shape.append(jax.ShapeDtypeStruct((batch, DEPTH, 2, HGRN_HEADS, HEAD_DIM, LANES), F32))
    n_units = 2 * rows // SCAN_CHUNK
    return pl.pallas_call(
        functools.partial(_hgrn_kernel, layer=layer, n_chunks=tb // SCAN_CHUNK, n_seq=n_seq, has_s0=has_s0,
                          state_out=state_out, state_alias=state_alias),
        grid=(batch // n_seq, nblk),
        in_specs=in_specs, out_specs=out_specs, out_shape=out_shape,
        input_output_aliases=aliases,
        scratch_shapes=[pltpu.VMEM((2 * n_seq, W, W), F32), pltpu.VMEM((n_units, SCAN_CHUNK, W), F32)],
        compiler_params=_params(("parallel", "arbitrary")),
        name="hgrn_scan",
    )(*args)


def _outproj_kernel(oa_ref, od_ref, of_ref, ob_ref, g_ref, x_ref, gate_ref, w_ref, sub_ref, hgn_ref,
                    bd_ref, lng_ref, lnb_ref, o_ref, *, lam_init):
    inv_d = 1.0 / HEAD_DIM
    g = g_ref[...].astype(F32)
    bd = bd_ref[...]

    def head_rms(t, gain):
        ssq = jnp.dot((t * t).astype(BF16), bd, preferred_element_type=F32)
        return t * lax.rsqrt(ssq * inv_d + RMS_EPS) * gain

    ma = (oa_ref[...].astype(F32) * g[:, 0:GQA_WIDTH]).astype(BF16)
    md = (head_rms(od_ref[...].astype(F32), sub_ref[...]) * (1.0 - lam_init)
          * g[:, GQA_WIDTH:GQA_WIDTH + DIFF_WIDTH]).astype(BF16)
    mr = (head_rms(of_ref[...].astype(F32) + ob_ref[...].astype(F32), hgn_ref[...])
          * g[:, GQA_WIDTH + DIFF_WIDTH:]).astype(BF16)
    y = (jnp.dot(ma, w_ref[0:GQA_WIDTH, :], preferred_element_type=F32)
         + jnp.dot(md, w_ref[GQA_WIDTH:GQA_WIDTH + DIFF_WIDTH, :], preferred_element_type=F32)
         + jnp.dot(mr, w_ref[GQA_WIDTH + DIFF_WIDTH:, :], preferred_element_type=F32))
    xn = DEEPNORM_ALPHA * x_ref[...] + gate_ref[...] * y
    mu = jnp.mean(xn, axis=-1, keepdims=True)
    xc = xn - mu
    var = jnp.mean(xc * xc, axis=-1, keepdims=True)
    o_ref[...] = xc * lax.rsqrt(var + LN_EPS) * lng_ref[...] + lnb_ref[...]


def _out_projection(out_a, out_d, o_f, o_b, gates, x, mod_sel, w_out_b, subln, hgn, bd, ln_g, ln_b,
                    layer, lam_init):
    t_total = x.shape[0]
    tm = PROJ_ROWS
    row = lambda w: pl.BlockSpec((tm, w), lambda i: (i, 0))
    full = lambda shape: pl.BlockSpec(shape, lambda i: (0,) * len(shape))
    return pl.pallas_call(
        functools.partial(_outproj_kernel, lam_init=lam_init),
        grid=(t_total // tm,),
        in_specs=[row(GQA_WIDTH), row(DIFF_WIDTH), row(HGRN_WIDTH), row(HGRN_WIDTH), row(D_MODEL), row(D_MODEL),
                  _mod_spec(mod_sel, 2, layer, t_total, tm),
                  pl.BlockSpec((None, D_MODEL, D_MODEL), lambda i: (layer, 0, 0)),
                  full((1, DIFF_WIDTH)), full((1, HGRN_WIDTH)),
                  full((DIFF_WIDTH, DIFF_WIDTH)), full((1, D_MODEL)), full((1, D_MODEL))],
        out_specs=row(D_MODEL),
        out_shape=jax.ShapeDtypeStruct((t_total, D_MODEL), F32),
        compiler_params=_params(("parallel",)),
        name="out_projection",
    )(out_a, out_d, o_f, o_b, gates, x, mod_sel[0], w_out_b, subln, hgn, bd, ln_g, ln_b)


def _block_diag_ones(n):
    idx = np.arange(n) // HEAD_DIM
    return jnp.asarray((idx[:, None] == idx[None, :]).astype(np.float32), dtype=BF16)


def _scan_constants():
    C, W, H = SCAN_CHUNK, HGRN_WIDTH, HGRN_HEADS
    t = np.arange(C)[:, None]
    u = np.arange(C)[None, :]
    tri = np.stack([u <= t, u >= t])
    s = (np.arange(W) % C)[None, :]
    lmask = np.zeros((2, HGRN_LEVELS, C, W), np.float32)
    for level in range(HGRN_LEVELS):
        block = 8 << level
        same = (t // block) == (s // block)
        if level == 0:
            lmask[0, level] = same & (s <= t)
            lmask[1, level] = same & (s >= t)
        else:
            t_hi, s_hi = (t % block) >= block // 2, (s % block) >= block // 2
            lmask[0, level] = same & t_hi & ~s_hi
            lmask[1, level] = same & ~t_hi & s_hi
    rows = np.arange(H * C)[:, None]
    hmask = (rows // C) == (np.arange(W)[None, :] // HEAD_DIM)
    bdmask = (np.arange(W)[:, None] // HEAD_DIM) == (np.arange(W)[None, :] // HEAD_DIM)
    return (jnp.asarray(tri.astype(np.float32), dtype=BF16), jnp.asarray(lmask),
            jnp.asarray(hmask.astype(np.float32), dtype=BF16), jnp.asarray(bdmask.astype(np.float32)))


def _rope_tables(n_tokens, dim):
    rows = n_tokens // GRID_W
    row = jnp.repeat(jnp.arange(rows, dtype=F32), GRID_W)
    col = jnp.tile(jnp.arange(GRID_W, dtype=F32), rows)
    quarter = dim // 4
    inv_freq = ROPE_THETA ** (-jnp.arange(quarter, dtype=F32) / quarter)
    ar = row[:, None] * inv_freq[None, :]
    ac = col[:, None] * inv_freq[None, :]
    ang = jnp.concatenate([ar, ar, ac, ac], axis=-1)
    cos = jnp.tile(jnp.cos(ang), (1, LANES // dim))
    sin = jnp.tile(jnp.sin(ang), (1, LANES // dim))
    first = (jnp.arange(LANES) % (dim // 2)) < quarter
    return cos, jnp.where(first, -sin, 0.0), jnp.where(first, 0.0, sin)


def _mixer_layer(x, mod_sel, layer, batch, tokens_per_batch, rope_tabs, cache, consts, weights,
                 ctx_out, ctx_prev):
    (w_in_b, w_out_b, qgain, kgain, lam_p, subln, hgn, lb_raw, ln_g, ln_b) = weights
    bdq, bdk, bdd, scan_consts = consts
    lam_init = 0.8 - 0.6 * math.exp(-0.3 * layer)
    outs = _in_projection(x, mod_sel, w_in_b, qgain, kgain, bdq, bdk, rope_tabs, tokens_per_batch, ctx_out,
                          layer, None if ctx_prev is None else ctx_prev[0])
    qa, kdup, vaug, qd, kd, vaugd, gates, zr = outs[:8]
    if cache is None:
        gqa_cache = diff_cache = s0 = None
        tq_a = tq_d = tk = tokens_per_batch
        items_a, items_d = 2, 2
        slabs_a = 2
    else:
        gqa_cache, diff_cache, s0 = cache
        tq_a, tq_d, tk = 512, 256, 512
        items_a = items_d = 1
        slabs_a = 1
    out_a = _gqa_attention(qa, kdup, vaug, gqa_cache, batch, tokens_per_batch, tq_a, tk, items_a, slabs_a, layer)
    out_d = _diff_attention(qd, kd, vaugd, lam_p, diff_cache, batch, tokens_per_batch, tq_d, tk, items_d, layer,
                            lam_init)
    scan = _hgrn_scan(zr, lb_raw, scan_consts, s0, None if ctx_prev is None else ctx_prev[1], ctx_out, layer,
                      batch, tokens_per_batch)
    x_new = _out_projection(out_a, out_d, scan[0], scan[1], gates, x, mod_sel, w_out_b, subln, hgn, bdd, ln_g, ln_b,
                            layer, lam_init)
    return x_new, ((outs[8:], scan[2]) if ctx_out else None)


def kernel(x_prompt, x_sample, cache_gqa_k, cache_gqa_v, cache_diff_k, cache_diff_v, state_hgrn, c, c_ctx,
           w_ada, b_ada, w_in, gqa_q_norm, gqa_k_norm, diff_lambda, diff_subln, hgrn_lower_bounds, hgrn_norm,
           w_out, ln_g, ln_b):
    batch, seq, _ = x_prompt.shape
    dec_batch, dec_seq, _ = x_sample.shape
    past = cache_gqa_k.shape[2]
    H, HD = HGRN_HEADS, HEAD_DIM

    cond = jnp.zeros((MOD_ROWS, D_MODEL), F32).at[0].set(c_ctx).at[1:1 + dec_batch].set(c)
    mod = _modulation(cond, w_ada, b_ada)
    consts = (_block_diag_ones(GQA_WIDTH), _block_diag_ones(LANES), _block_diag_ones(DIFF_WIDTH),
              _scan_constants())
    rope_tabs = _rope_tables(dec_seq, HEAD_DIM) + _rope_tables(dec_seq, DIFF_QK_DIM)
    w_in_b = w_in.astype(BF16)
    w_out_b = w_out.astype(BF16)

    def layer_weights(l):
        return (w_in_b, w_out_b,
                jnp.tile(gqa_q_norm[l], GQA_WIDTH // HD)[None, :] * (HD ** -0.5 * LOG2E),
                jnp.tile(gqa_k_norm[l], LANES // HD)[None, :],
                diff_lambda[l],
                jnp.tile(diff_subln[l], DIFF_WIDTH // HD)[None, :],
                jnp.tile(hgrn_norm[l], HGRN_WIDTH // HD)[None, :],
                hgrn_lower_bounds, ln_g[l][None, :], ln_b[l][None, :])

    mod5 = mod.reshape(DEPTH, MOD_ROWS, 3, 1, D_MODEL)

    x = x_prompt.reshape(batch * seq, D_MODEL)
    ctx = None
    for l in range(DEPTH):
        x, ctx = _mixer_layer(x, (mod5, 0, 1), l, batch, seq, None, None, consts, layer_weights(l), True, ctx)
    y_prompt = x.reshape(batch, seq, D_MODEL)
    (kn, av, dk, dv), st = ctx
    new_leaves = [kn.reshape(batch, DEPTH, seq, GQA_KV_HEADS, HD), av.reshape(batch, DEPTH, seq, GQA_KV_HEADS, HD),
                  dk.reshape(batch, DEPTH, seq, DIFF_HEADS, HD), dv.reshape(batch, DEPTH, seq, DIFF_HEADS, HD),
                  jnp.swapaxes(st[..., :HD], -1, -2)]

    ones = jnp.ones((dec_batch, DEPTH, past, HD), F32)
    ck, cv, dvc = cache_gqa_k, cache_gqa_v, cache_diff_v
    ckd = jnp.stack([jnp.concatenate([ck[..., g, :], ck[..., g, :]], axis=-1) for g in range(GQA_KV_HEADS)], axis=2)
    cvd = jnp.stack([jnp.concatenate([cv[..., g, :], ones], axis=-1) for g in range(GQA_KV_HEADS)], axis=2)
    dkc = cache_diff_k.reshape(dec_batch, DEPTH, past, DIFF_WIDTH)
    dvd = jnp.stack([jnp.concatenate([dvc[..., h, :], ones] if h % 2 == 0 else [ones, dvc[..., h, :]], axis=-1)
                     for h in range(DIFF_HEADS)], axis=2)
    s0t = jnp.swapaxes(state_hgrn.astype(F32), -1, -2)
    eye = jnp.eye(H, dtype=F32)
    s0 = (s0t[..., None, :] * eye[:, None, :, None]).reshape(dec_batch, DEPTH, 2, H * HD, H * HD)
    cache = ((ckd.astype(BF16), cvd.astype(BF16)), (dkc.astype(BF16), dvd.astype(BF16)), s0)
    x = x_sample.reshape(dec_batch * dec_seq, D_MODEL)
    for l in range(DEPTH):
        x, _ = _mixer_layer(x, (mod5, 1, dec_batch), l, dec_batch, dec_seq, rope_tabs, cache, consts,
                            layer_weights(l), False, None)
    y_sample = x.reshape(dec_batch, dec_seq, D_MODEL)
    return (y_prompt, y_sample) + tuple(new_leaves)
```

```python
import functools
import math

import numpy as np
import jax
import jax.numpy as jnp
from jax import lax
from jax.experimental import pallas as pl
from jax.experimental.pallas import tpu as pltpu

F32 = jnp.float32
BF16 = jnp.bfloat16

D_MODEL = 1024
DEPTH = 2
GRID_W = 64
HEAD_DIM = 64
GQA_WIDTH = 512
GQA_KV_HEADS = 2
DIFF_WIDTH = 256
DIFF_HEADS = 4
DIFF_QK_DIM = 32
HGRN_WIDTH = 256
HGRN_HEADS = 4
IN_WIDTH = 3584
SCAN_CHUNK = 64
HGRN_LEVELS = 4
ROPE_THETA = 10000.0
RMS_EPS = 1e-6
LN_EPS = 1e-5
FORGET_MIN = 1e-6
DEEPNORM_ALPHA = (2 * DEPTH) ** 0.25
LANES = 128
SUBLANES = 8
MOD_ROWS = SUBLANES
VMEM_LIMIT = 48 * 1024 * 1024
SM_ROWS = 32
PROJ_ROWS = 512
SCAN_BLOCK = 512
LOG2E = math.log2(math.e)

_A_Q, _A_K, _A_V, _A_G = 0, 512, 640, 768
_D_Q, _D_K, _D_V, _D_G = 1280, 1536, 1792, 2048
_R_Z, _R_G = 2304, 3328

_NT = (((1,), (1,)), ((), ()))
_TN = (((0,), (0,)), ((), ()))


def _silu(x):
    return x * jax.nn.sigmoid(x)


def _params(sem):
    return pltpu.CompilerParams(dimension_semantics=sem, vmem_limit_bytes=VMEM_LIMIT)


def _mod_kernel(c_ref, w_ref, b_ref, o_ref):
    c = c_ref[...]
    o_ref[0] = jnp.dot(_silu(c), w_ref[0], preferred_element_type=F32) + b_ref[0]


def _modulation(cond, w_ada, b_ada):
    tn = 1024
    return pl.pallas_call(
        _mod_kernel,
        grid=(DEPTH, 3 * D_MODEL // tn),
        in_specs=[pl.BlockSpec((MOD_ROWS, D_MODEL), lambda l, j: (0, 0)),
                  pl.BlockSpec((1, D_MODEL, tn), lambda l, j: (l, 0, j)),
                  pl.BlockSpec((1, 1, tn), lambda l, j: (l, 0, j))],
        out_specs=pl.BlockSpec((1, MOD_ROWS, tn), lambda l, j: (l, 0, j)),
        out_shape=jax.ShapeDtypeStruct((DEPTH, MOD_ROWS, 3 * D_MODEL), F32),
        compiler_params=_params(("arbitrary", "arbitrary")),
        name="modulation",
    )(cond, w_ada, b_ada.reshape(DEPTH, 1, 3 * D_MODEL))


def _rope(x, c, s1, s2, shift):
    return x * c + pltpu.roll(x, LANES - shift, 1) * s1 + pltpu.roll(x, shift, 1) * s2


def _inproj_kernel(*refs, rope, ctx_out, ctx_alias):
    it = iter(refs)
    x_ref, sc_ref, sh_ref, w_ref, qg_ref, kg_ref, bdq_ref, bdk_ref = [next(it) for _ in range(8)]
    if rope:
        ca, s1a, s2a, cd, s1d, s2d = [next(it)[...] for _ in range(6)]
    if ctx_alias:
        for _ in range(4):
            next(it)
    qa_o, kdup_o, vaug_o, qd_o, kd_o, vaugd_o, gates_o, zr_o = [next(it) for _ in range(8)]
    if ctx_out:
        ctx_refs = [next(it) for _ in range(4)]
        if not ctx_alias:
            for r in ctx_refs:
                r[:, 1:] = jnp.zeros((r.shape[0], r.shape[1] - 1) + r.shape[2:], F32)

        def put_ctx(n, val):
            r = ctx_refs[n]
            r[:, 0] = val.reshape(r.shape[0], r.shape[2], r.shape[3])

    h = (x_ref[...] * (1.0 + sc_ref[...]) + sh_ref[...]).astype(BF16)

    def proj(lo, hi):
        return jnp.dot(h, w_ref[:, lo:hi], preferred_element_type=F32)

    tm = h.shape[0]
    low = lax.broadcasted_iota(jnp.int32, (tm, LANES), 1) < HEAD_DIM
    inv_d = 1.0 / HEAD_DIM

    aq = proj(_A_Q, _A_K)
    ssq = jnp.dot((aq * aq).astype(BF16), bdq_ref[...], preferred_element_type=F32)
    qn = aq * lax.rsqrt(ssq * inv_d + RMS_EPS) * qg_ref[...]
    for j in range(GQA_WIDTH // LANES):
        slab = qn[:, j * LANES:(j + 1) * LANES]
        if rope:
            slab = _rope(slab, ca, s1a, s2a, HEAD_DIM // 4)
        qa_o[:, j * LANES:(j + 1) * LANES] = slab.astype(BF16)

    ak = proj(_A_K, _A_V)
    ssqk = jnp.dot((ak * ak).astype(BF16), bdk_ref[...], preferred_element_type=F32)
    kn = ak * lax.rsqrt(ssqk * inv_d + RMS_EPS) * kg_ref[...]
    if ctx_out:
        put_ctx(0, kn)
    if rope:
        kn = _rope(kn, ca, s1a, s2a, HEAD_DIM // 4)
    kr = pltpu.roll(kn, HEAD_DIM, 1)
    kdup_o[0] = jnp.where(low, kn, kr).astype(BF16)
    kdup_o[1] = jnp.where(low, kr, kn).astype(BF16)

    av = proj(_A_V, _A_G)
    if ctx_out:
        put_ctx(1, av)
    vr = pltpu.roll(av, HEAD_DIM, 1)
    vaug_o[0] = jnp.where(low, av, 1.0).astype(BF16)
    vaug_o[1] = jnp.where(low, vr, 1.0).astype(BF16)

    gates_o[:, 0:GQA_WIDTH] = _silu(proj(_A_G, _D_Q)).astype(BF16)

    dq = proj(_D_Q, _D_K) * (DIFF_QK_DIM ** -0.5 * LOG2E)
    dk = proj(_D_K, _D_V)
    if ctx_out:
        put_ctx(2, dk)
    for j in range(DIFF_WIDTH // LANES):
        sq = dq[:, j * LANES:(j + 1) * LANES]
        sk = dk[:, j * LANES:(j + 1) * LANES]
        if rope:
            sq = _rope(sq, cd, s1d, s2d, DIFF_QK_DIM // 4)
            sk = _rope(sk, cd, s1d, s2d, DIFF_QK_DIM // 4)
        qd_o[:, j * LANES:(j + 1) * LANES] = sq.astype(BF16)
        kd_o[:, j * LANES:(j + 1) * LANES] = sk.astype(BF16)
    dv = proj(_D_V, _D_G)
    if ctx_out:
        put_ctx(3, dv)
    for j in range(DIFF_WIDTH // LANES):
        sv = dv[:, j * LANES:(j + 1) * LANES]
        vaugd_o[2 * j] = jnp.where(low, sv, 1.0).astype(BF16)
        vaugd_o[2 * j + 1] = jnp.where(low, 1.0, sv).astype(BF16)
    gates_o[:, GQA_WIDTH:GQA_WIDTH + DIFF_WIDTH] = _silu(proj(_D_G, _R_Z)).astype(BF16)

    gates_o[:, GQA_WIDTH + DIFF_WIDTH:] = _silu(proj(_R_G, IN_WIDTH)).astype(BF16)
    zr = proj(_R_Z, _R_G)
    zr_o[:, 0:HGRN_WIDTH] = _silu(zr[:, 0:HGRN_WIDTH])
    zr_o[:, HGRN_WIDTH:] = zr[:, HGRN_WIDTH:]


def _mod_spec(mod_sel, which, layer, t_total, tm):
    _, row0, n_rows = mod_sel
    blocks_per_row = t_total // n_rows // tm
    return pl.BlockSpec((None, None, None, 1, D_MODEL), lambda i: (layer, row0 + i // blocks_per_row, which, 0, 0))


def _in_projection(x, mod_sel, w_in_b, qgain, kgain, bdq, bdk, rope_tabs, tokens_per_batch, ctx_out,
                   layer, ctx_prev):
    t_total = x.shape[0]
    tm = PROJ_ROWS
    nb_rows = tokens_per_batch // tm
    rope = rope_tabs is not None
    assert not rope or nb_rows >= 1
    full = lambda shape: pl.BlockSpec(shape, lambda i: (0,) * len(shape))
    in_specs = [pl.BlockSpec((tm, D_MODEL), lambda i: (i, 0)),
                _mod_spec(mod_sel, 1, layer, t_total, tm), _mod_spec(mod_sel, 0, layer, t_total, tm),
                pl.BlockSpec((None, D_MODEL, IN_WIDTH), lambda i: (layer, 0, 0)),
                full((1, GQA_WIDTH)), full((1, LANES)),
                full((GQA_WIDTH, GQA_WIDTH)), full((LANES, LANES))]
    args = [x, mod_sel[0], mod_sel[0], w_in_b, qgain, kgain, bdq, bdk]
    if rope:
        in_specs += [pl.BlockSpec((tm, LANES), lambda i: (i % nb_rows, 0))] * 6
        args += list(rope_tabs)
    row = lambda w: pl.BlockSpec((tm, w), lambda i: (i, 0))
    stk = lambda n: pl.BlockSpec((n, tm, LANES), lambda i: (0, i, 0))
    out_specs = [row(GQA_WIDTH), stk(2), stk(2), row(DIFF_WIDTH), row(DIFF_WIDTH), stk(4),
                 row(D_MODEL), row(4 * HGRN_WIDTH)]
    out_shape = [jax.ShapeDtypeStruct((t_total, GQA_WIDTH), BF16),
                 jax.ShapeDtypeStruct((2, t_total, LANES), BF16),
                 jax.ShapeDtypeStruct((2, t_total, LANES), BF16),
                 jax.ShapeDtypeStruct((t_total, DIFF_WIDTH), BF16),
                 jax.ShapeDtypeStruct((t_total, DIFF_WIDTH), BF16),
                 jax.ShapeDtypeStruct((4, t_total, LANES), BF16),
                 jax.ShapeDtypeStruct((t_total, D_MODEL), BF16),
                 jax.ShapeDtypeStruct((t_total, 4 * HGRN_WIDTH), F32)]
    aliases = {}
    if ctx_out:
        seq = tokens_per_batch
        nseq = tm // seq
        assert nseq * seq == tm
        widths = (LANES, LANES, DIFF_WIDTH, DIFF_WIDTH)
        if ctx_prev is not None:
            aliases = {len(args) + n: len(out_specs) + n for n in range(4)}
            in_specs += [pl.BlockSpec(memory_space=pl.ANY)] * 4
            args += list(ctx_prev)
        if ctx_prev is None:
            assert layer == 0
            out_specs += [pl.BlockSpec((nseq, DEPTH, seq, w), lambda i: (i, 0, 0, 0)) for w in widths]
        else:
            out_specs += [pl.BlockSpec((nseq, 1, seq, w), lambda i: (i, layer, 0, 0)) for w in widths]
        out_shape += [jax.ShapeDtypeStruct((t_total // seq, DEPTH, seq, w), F32) for w in widths]
    return pl.pallas_call(
        functools.partial(_inproj_kernel, rope=rope, ctx_out=ctx_out, ctx_alias=bool(aliases)),
        grid=(t_total // tm,),
        in_specs=in_specs, out_specs=out_specs, out_shape=out_shape,
        input_output_aliases=aliases,
        compiler_params=_params(("parallel",)),
        name="in_projection",
    )(*args)


def _chunk_rows(j, tk):
    return pl.ds(j * tk, tk) if isinstance(j, int) else pl.ds(pl.multiple_of(j * tk, tk), tk)


def _attention_scratch(n_items, rows, tk):
    rows = n_items * rows
    return ([pltpu.VMEM((rows, LANES), BF16)]
            + [pltpu.VMEM((rows, tk), F32)] * 2 + [pltpu.VMEM((rows, tk), BF16)] * 2
            + [pltpu.VMEM((rows, LANES), F32)] * 4)


def _item_scratch(refs, item, rows):
    return [r.at[pl.ds(item * rows, rows)] for r in refs]


def _softmax_stage(s_sc, p_sc, a_sc, m_sc, keys, first):
    for r in range(s_sc.shape[0] // SM_ROWS):
        rows = pl.ds(r * SM_ROWS, SM_ROWS)
        s = s_sc[rows, 0:keys]
        m_cur = jnp.max(s, axis=1, keepdims=True)
        if first:
            m_new = jnp.broadcast_to(m_cur, (SM_ROWS, LANES))
        else:
            m_prev = m_sc[rows, :]
            m_new = jnp.maximum(m_prev, m_cur)
            a_sc[rows, :] = jnp.exp2(m_prev - m_new)
        m_sc[rows, :] = m_new
        p_sc[rows, 0:keys] = jnp.exp2(s - jnp.concatenate([m_new] * (keys // LANES), axis=1)).astype(BF16)


def _item_timeline(n_new, cache, k_at, v_at, start, qk, sm, av, finish):
    o = 0 if cache is None else 1
    n = n_new + o
    k_of = lambda c: cache[0]() if (o and c == 0) else k_at(c - o)
    v_of = lambda c: cache[1]() if (o and c == 0) else v_at(c - o)
    steps = []
    for t in range(n + 2):
        step = [start] if t == 0 else []
        if t < n:
            step.append(lambda t=t: qk(t % 2, k_of(t)))
        if 0 <= t - 1 < n:
            step.append(lambda t=t: sm((t - 1) % 2, t == 1))
        if 0 <= t - 2 < n:
            step.append(lambda t=t: av(t % 2, v_of(t - 2), t == 2))
        if t == n + 1:
            step.append(finish)
        steps.append(step)
    return steps


def _run_items(timelines):
    lag = max(1, len(timelines[0]) - 2)
    for g in range(lag * (len(timelines) - 1) + len(timelines[0])):
        for i, steps in enumerate(timelines):
            if 0 <= g - i * lag < len(steps):
                for stage in steps[g - i * lag]:
                    stage()


def _gqa_kernel(*refs, has_cache, tq, tk, nk, n_items, shared_kv, slabs):
    if has_cache:
        q_ref, kc_ref, vc_ref = refs[:3]
        refs = refs[3:]
    else:
        q_ref = refs[0]
        refs = refs[1:]
    k_ref, v_ref, o_ref = refs[:3]
    low = lax.broadcasted_iota(jnp.int32, (tq, LANES), 1) < HEAD_DIM
    rows = 2 * slabs * tq

    def item_stages(item):
        qs_sc, s0, s1, p0, p1, a0, a1, m_sc, acc_sc = _item_scratch(refs[3:], item, rows)
        s_b, p_b, a_b = (s0, s1), (p0, p1), (a0, a1)

        def start():
            for j in range(slabs):
                q = q_ref[pl.ds(item * tq, tq), j * LANES:(j + 1) * LANES]
                zero = jnp.zeros_like(q)
                qs_sc[2 * j * tq:(2 * j + 1) * tq, :] = jnp.where(low, q, zero)
                qs_sc[(2 * j + 1) * tq:(2 * j + 2) * tq, :] = jnp.where(low, zero, q)

        keys = {}

        def qk(slot, k):
            keys[slot] = k.shape[0]
            s_b[slot][:, 0:k.shape[0]] = lax.dot_general(qs_sc[...], k, _NT, preferred_element_type=F32)

        def sm(slot, first):
            _softmax_stage(s_b[slot], p_b[slot], a_b[slot], m_sc, keys[slot], first)

        def av(slot, v, first):
            pv = jnp.dot(p_b[slot][:, 0:v.shape[0]], v, preferred_element_type=F32)
            acc_sc[...] = pv if first else a_b[slot][...] * acc_sc[...] + pv

        def finish():
            acc = acc_sc[...]
            on = acc / pltpu.roll(acc, HEAD_DIM, 1)
            for j in range(slabs):
                first, second = on[2 * j * tq:(2 * j + 1) * tq], on[(2 * j + 1) * tq:(2 * j + 2) * tq]
                o_ref[pl.ds(item * tq, tq), j * LANES:(j + 1) * LANES] = jnp.where(
                    low, first, pltpu.roll(second, HEAD_DIM, 1)).astype(BF16)

        base = 0 if shared_kv else item * nk
        chunk = lambda ref: (lambda j: ref[0, _chunk_rows(base + j, tk), :])
        cache = (lambda: kc_ref[0, 0], lambda: vc_ref[0, 0]) if has_cache else None
        return _item_timeline(nk, cache, chunk(k_ref), chunk(v_ref), start, qk, sm, av, finish)

    _run_items([item_stages(item) for item in range(n_items)])


def _item_geometry(batch, tokens_per_batch, tq, n_items):
    nq = tokens_per_batch // tq
    if nq == 1:
        return batch // n_items, 1, n_items * tokens_per_batch, False
    assert nq % n_items == 0
    return batch, nq // n_items, tokens_per_batch, True


def _gqa_attention(qa, kdup, vaug, cache, batch, tokens_per_batch, tq, tk, n_items, slabs, layer):
    t_total = qa.shape[0]
    nk = tokens_per_batch // tk
    has_cache = cache is not None
    bsteps, nq, kv_rows, shared_kv = _item_geometry(batch, tokens_per_batch, tq, n_items)
    assert shared_kv or not has_cache
    width = slabs * LANES
    group = lambda p: p * slabs // 2
    in_specs = [pl.BlockSpec((n_items * tq, width), lambda b, p, i: (b * nq + i, p))]
    args = [qa]
    if has_cache:
        past = cache[0].shape[3]
        in_specs += [pl.BlockSpec((1, None, 1, past, LANES), lambda b, p, i: (b, layer, group(p), 0, 0))] * 2
        args += list(cache)
    in_specs += [pl.BlockSpec((1, kv_rows, LANES), lambda b, p, i: (group(p), b, 0))] * 2
    args += [kdup, vaug]
    return pl.pallas_call(
        functools.partial(_gqa_kernel, has_cache=has_cache, tq=tq, tk=tk, nk=nk, n_items=n_items,
                          shared_kv=shared_kv, slabs=slabs),
        grid=(bsteps, GQA_WIDTH // width, nq),
        in_specs=in_specs,
        out_specs=pl.BlockSpec((n_items * tq, width), lambda b, p, i: (b * nq + i, p)),
        out_shape=jax.ShapeDtypeStruct((t_total, GQA_WIDTH), BF16),
        scratch_shapes=_attention_scratch(n_items, 2 * slabs * tq, tk),
        compiler_params=_params(("parallel", "parallel", "parallel")),
        name="gqa_attention",
    )(*args)


def _diff_kernel(*refs, has_cache, tq, tk, nk, n_items, shared_kv, lam_init):
    q_ref, lp_ref = refs[:2]
    if has_cache:
        kc_ref, vc0_ref, vc1_ref = refs[2:5]
        refs = refs[5:]
    else:
        refs = refs[2:]
    k_ref, v0_ref, v1_ref, o_ref = refs[:4]
    lane = lax.broadcasted_iota(jnp.int32, (tq, LANES), 1)
    lp = lp_ref[...]
    lam = (jnp.exp(jnp.sum(lp[0:1] * lp[1:2], axis=1, keepdims=True))
           - jnp.exp(jnp.sum(lp[2:3] * lp[3:4], axis=1, keepdims=True)) + lam_init)
    h2 = 2 * tq
    rows = 4 * tq

    def item_stages(item):
        qs_sc, s0, s1, p0, p1, a0, a1, m_sc, acc_sc = _item_scratch(refs[4:], item, rows)
        s_b, p_b, a_b = (s0, s1), (p0, p1), (a0, a1)

        def start():
            q = q_ref[pl.ds(item * tq, tq), :]
            zero = jnp.zeros_like(q)
            for j in range(4):
                qs_sc[j * tq:(j + 1) * tq, :] = jnp.where((lane >> 5) == j, q, zero)

        keys = {}

        def qk(slot, k):
            keys[slot] = k.shape[0]
            s_b[slot][:, 0:k.shape[0]] = lax.dot_general(qs_sc[...], k, _NT, preferred_element_type=F32)

        def sm(slot, first):
            _softmax_stage(s_b[slot], p_b[slot], a_b[slot], m_sc, keys[slot], first)

        def av(slot, v, first):
            for half, vh in zip((pl.ds(0, h2), pl.ds(h2, h2)), v):
                pv = jnp.dot(p_b[slot][half, 0:vh.shape[0]], vh, preferred_element_type=F32)
                acc_sc[half, :] = pv if first else a_b[slot][half, :] * acc_sc[half, :] + pv

        def finish():
            acc = acc_sc[...]
            on = acc / pltpu.roll(acc, HEAD_DIM, 1)
            o0 = on[0:tq] - lam * on[tq:h2]
            o1 = on[h2:h2 + tq] - lam * on[h2 + tq:]
            o_ref[pl.ds(item * tq, tq), :] = jnp.where(lane < HEAD_DIM, o0, o1).astype(BF16)

        base = 0 if shared_kv else item * nk
        k_at = lambda j: k_ref[_chunk_rows(base + j, tk), :]
        v_at = lambda j: (v0_ref[0, _chunk_rows(base + j, tk), :], v1_ref[0, _chunk_rows(base + j, tk), :])
        cache = (lambda: kc_ref[0], lambda: (vc0_ref[0, 0], vc1_ref[0, 0])) if has_cache else None
        return _item_timeline(nk, cache, k_at, v_at, start, qk, sm, av, finish)

    _run_items([item_stages(item) for item in range(n_items)])


def _diff_attention(qd, kd, vaugd, lam_p, cache, batch, tokens_per_batch, tq, tk, n_items, layer, lam_init):
    t_total = qd.shape[0]
    nk = tokens_per_batch // tk
    has_cache = cache is not None
    bsteps, nq, kv_rows, shared_kv = _item_geometry(batch, tokens_per_batch, tq, n_items)
    assert shared_kv or not has_cache
    in_specs = [pl.BlockSpec((n_items * tq, LANES), lambda b, p, i: (b * nq + i, p)),
                pl.BlockSpec((4, DIFF_QK_DIM), lambda b, p, i: (0, 0))]
    args = [qd, lam_p]
    if has_cache:
        ckd, cvd = cache
        past = ckd.shape[2]
        in_specs += [pl.BlockSpec((1, None, past, LANES), lambda b, p, i: (b, layer, 0, p)),
                     pl.BlockSpec((1, None, 1, past, LANES), lambda b, p, i: (b, layer, 2 * p, 0, 0)),
                     pl.BlockSpec((1, None, 1, past, LANES), lambda b, p, i: (b, layer, 2 * p + 1, 0, 0))]
        args += [ckd, cvd, cvd]
    in_specs += [pl.BlockSpec((kv_rows, LANES), lambda b, p, i: (b, p)),
                 pl.BlockSpec((1, kv_rows, LANES), lambda b, p, i: (2 * p, b, 0)),
                 pl.BlockSpec((1, kv_rows, LANES), lambda b, p, i: (2 * p + 1, b, 0))]
    args += [kd, vaugd, vaugd]
    return pl.pallas_call(
        functools.partial(_diff_kernel, has_cache=has_cache, tq=tq, tk=tk, nk=nk, n_items=n_items,
                          shared_kv=shared_kv, lam_init=lam_init),
        grid=(bsteps, DIFF_WIDTH // LANES, nq),
        in_specs=in_specs,
        out_specs=pl.BlockSpec((n_items * tq, LANES), lambda b, p, i: (b * nq + i, p)),
        out_shape=jax.ShapeDtypeStruct((t_total, DIFF_WIDTH), BF16),
        scratch_shapes=_attention_scratch(n_items, 4 * tq, tk),
        compiler_params=_params(("parallel", "parallel", "parallel")),
        name="diff_attention",
    )(*args)


def _hgrn_gates(qs, z, v, lb, g_sc, tri):
    sig = jax.nn.sigmoid(z)
    om = 1.0 - lb
    k = om * (1.0 - sig)
    g = jnp.log2(jnp.maximum(lb + om * sig, FORGET_MIN))
    g1 = g.astype(BF16)
    e1 = g - g1.astype(F32)
    g2 = e1.astype(BF16)
    g3 = (e1 - g2.astype(F32)).astype(BF16)
    G = (jnp.dot(tri, g1, preferred_element_type=F32) + jnp.dot(tri, g2, preferred_element_type=F32)
         + jnp.dot(tri, g3, preferred_element_type=F32))
    g_sc[...] = G
    return qs, k, G, v.astype(BF16)


def _hgrn_operands(gates, g_sc, hmask, rev):
    C, W = SCAN_CHUNK, HGRN_WIDTH
    qs, k, G, vb = gates
    g_last = g_sc[pl.ds(0 if rev else C - 1, 1), :]
    q_in = (qs * jnp.exp2(G)).astype(BF16)
    k_end = (k * jnp.exp2(g_last - G)).astype(BF16)

    def block_diag(a):
        return jnp.concatenate([a] * HGRN_HEADS, axis=0) * hmask

    def ref_rows(block, off):
        return jnp.concatenate([jnp.broadcast_to(g_sc[pl.ds(b0 + off, 1), :], (block, W))
                                for b0 in range(0, C, block)], axis=0)

    levels = []
    for level in range(HGRN_LEVELS):
        block = 8 << level
        half = block // 2
        if level == 0:
            r = ref_rows(block, half)
            qe, ke = G - r, r - G
        else:
            r = ref_rows(block, half if rev else half - 1)
            qe, ke = jnp.minimum(G - r, 0.0), jnp.minimum(r - G, 0.0)
        levels.append(((qs * jnp.exp2(qe)).astype(BF16), block_diag((k * jnp.exp2(ke)).astype(BF16))))
    return q_in, jnp.exp2(g_last), k_end, vb, block_diag(vb), levels


def _hgrn_intra(operands, lmask_ref, bdmask, rev):
    q_in, decay, k_end, vb, v_bd, levels = operands
    ut = lax.dot_general(vb, k_end, _TN, preferred_element_type=F32) * bdmask
    a = None
    for level, (ql, kl) in enumerate(levels):
        al = lax.dot_general(ql, kl, _NT, preferred_element_type=F32)
        m = lmask_ref[rev, level]
        a = jnp.where(m > 0.5, al, 0.0) if level == 0 else a + al * m
    intra = jnp.dot(a.astype(BF16), v_bd, preferred_element_type=F32)
    return q_in, decay, ut, intra


def _hgrn_state(parts, st):
    q_in, decay, ut, intra = parts
    out = lax.dot_general(q_in, st.astype(BF16), _NT, preferred_element_type=F32) + intra
    return out, st * decay + ut


def _hgrn_kernel(*refs, layer, n_chunks, n_seq, has_s0, state_out, state_alias):
    it = iter(refs)
    qf, zf, vf, qb, zb, vb, lbraw, tri_ref, lmask_ref, hmask_ref, bdmask_ref = [next(it) for _ in range(11)]
    s0 = next(it) if has_s0 else None
    if state_alias:
        next(it)
    of_ref, ob_ref = next(it), next(it)
    sto_ref = next(it) if state_out else None
    st_ref, g_sc = next(it), next(it)

    @pl.when(pl.program_id(1) == 0)
    def _():
        if has_s0:
            st_ref[...] = s0[0]
        else:
            st_ref[...] = jnp.zeros(st_ref.shape, F32)

    lbr = lbraw[...]
    ex = jnp.exp(lbr - jnp.max(lbr, axis=0, keepdims=True))
    sm = ex / jnp.sum(ex, axis=0, keepdims=True)
    lbs = jnp.zeros(sm.shape[1:], F32)
    for j in range(1, layer + 1):
        lbs = lbs + sm[j]
    lb_f, lb_b = lbs[0:1], lbs[1:2]

    hmask = hmask_ref[...]
    bdmask = bdmask_ref[...]

    chunks = []
    for cidx in range(n_chunks):
        for s in range(n_seq):
            base = s * n_chunks
            chunks.append((2 * s, 0, pl.ds((base + cidx) * SCAN_CHUNK, SCAN_CHUNK), qf, zf, vf, lb_f, of_ref))
            chunks.append((2 * s + 1, 1, pl.ds((base + n_chunks - 1 - cidx) * SCAN_CHUNK, SCAN_CHUNK),
                           qb, zb, vb, lb_b, ob_ref))
    n = len(chunks)
    state = [st_ref[i] for i in range(2 * n_seq)]
    gates, operands, parts = {}, {}, {}
    for t in range(n + 3):
        if t < n:
            _, d, rows, q_r, z_r, v_r, lb, _ = chunks[t]
            gates[t] = _hgrn_gates(q_r[rows, :], z_r[rows, :], v_r[rows, :], lb, g_sc.at[t], tri_ref[d])
        if 0 <= t - 1 < n:
            operands[t - 1] = _hgrn_operands(gates.pop(t - 1), g_sc.at[t - 1], hmask, chunks[t - 1][1])
        if 0 <= t - 2 < n:
            parts[t - 2] = _hgrn_intra(operands.pop(t - 2), lmask_ref, bdmask, chunks[t - 2][1])
        if 0 <= t - 3 < n:
            si, _, rows, _, _, _, _, o_r = chunks[t - 3]
            out, state[si] = _hgrn_state(parts.pop(t - 3), state[si])
            o_r[rows, :] = out.astype(BF16)
    for i, s in enumerate(state):
        st_ref[i] = s

    if state_out:
        @pl.when(pl.program_id(1) == pl.num_programs(1) - 1)
        def _():
            if not state_alias:
                sto_ref[:, 1:] = jnp.zeros((sto_ref.shape[0], sto_ref.shape[1] - 1) + sto_ref.shape[2:], F32)
            for i, s in enumerate(state):
                for hh in range(HGRN_HEADS):
                    slab = s[hh * HEAD_DIM:(hh + 1) * HEAD_DIM, (hh // 2) * LANES:(hh // 2 + 1) * LANES]
                    sto_ref[i // 2, 0, i % 2, hh] = pltpu.roll(slab, HEAD_DIM, 1) if hh % 2 else slab


def _hgrn_scan(zr, lb_raw, scan_consts, s0, state_prev, state_out, layer, batch, tokens_per_batch):
    t_total = zr.shape[0]
    tb = min(SCAN_BLOCK, tokens_per_batch)
    nblk = tokens_per_batch // tb
    n_seq = SCAN_BLOCK // tb if nblk == 1 else 1
    W = HGRN_WIDTH
    has_s0 = s0 is not None
    assert n_seq == 1 or not has_s0
    state_alias = state_prev is not None
    rows = n_seq * tb
    fblk = lambda col: pl.BlockSpec((rows, W), lambda b, j: (b * nblk + j, col))
    bblk = lambda col: pl.BlockSpec((rows, W), lambda b, j: (b * nblk + nblk - 1 - j, col))
    full = lambda a: pl.BlockSpec(a.shape, lambda b, j: (0,) * a.ndim)
    in_specs = [fblk(0), fblk(1), fblk(3), bblk(0), bblk(2), bblk(3),
                pl.BlockSpec((DEPTH, 2, W), lambda b, j: (0, 0, 0))] + [full(a) for a in scan_consts]
    args = [zr] * 6 + [lb_raw] + list(scan_consts)
    if has_s0:
        in_specs.append(pl.BlockSpec((1, None, 2, W, W), lambda b, j: (b, layer, 0, 0, 0)))
        args.append(s0)
    aliases = {}
    if state_alias:
        aliases = {len(args): 2}
        in_specs.append(pl.BlockSpec(memory_space=pl.ANY))
        args.append(state_prev)
    out_specs = [pl.BlockSpec((rows, W), lambda b, j: (b * nblk + j, 0)),
                 pl.BlockSpec((rows, W), lambda b, j: (b * nblk + nblk - 1 - j, 0))]
    out_shape = [jax.ShapeDtypeStruct((t_total, W), BF16), jax.ShapeDtypeStruct((t_total, W), BF16)]
    if state_out and not state_alias:
        assert layer == 0
        out_specs.append(pl.BlockSpec((n_seq, DEPTH, 2, HGRN_HEADS, HEAD_DIM, LANES),
                                      lambda b, j: (b, 0, 0, 0, 0, 0)))
    elif state_out:
        out_specs.append(pl.BlockSpec((n_seq, 1, 2, HGRN_HEADS, HEAD_DIM, LANES),
                                      lambda b, j: (b, layer, 0, 0, 0, 0)))
    if state_out:
        out_shape.append(jax.ShapeDtypeStruct((batch, DEPTH, 2, HGRN_HEADS, HEAD_DIM, LANES), F32))
    n_units = 2 * rows // SCAN_CHUNK
    return pl.pallas_call(
        functools.partial(_hgrn_kernel, layer=layer, n_chunks=tb // SCAN_CHUNK, n_seq=n_seq, has_s0=has_s0,
                          state_out=state_out, state_alias=state_alias),
        grid=(batch // n_seq, nblk),
        in_specs=in_specs, out_specs=out_specs, out_shape=out_shape,
        input_output_aliases=aliases,
        scratch_shapes=[pltpu.VMEM((2 * n_seq, W, W), F32), pltpu.VMEM((n_units, SCAN_CHUNK, W), F32)],
        compiler_params=_params(("parallel", "arbitrary")),
        name="hgrn_scan",
    )(*args)


def _outproj_kernel(oa_ref, od_ref, of_ref, ob_ref, g_ref, x_ref, gate_ref, w_ref, sub_ref, hgn_ref,
                    bd_ref, lng_ref, lnb_ref, o_ref, *, lam_init):
    inv_d = 1.0 / HEAD_DIM
    g = g_ref[...].astype(F32)
    bd = bd_ref[...]

    def head_rms(t, gain):
        ssq = jnp.dot((t * t).astype(BF16), bd, preferred_element_type=F32)
        return t * lax.rsqrt(ssq * inv_d + RMS_EPS) * gain

    ma = (oa_ref[...].astype(F32) * g[:, 0:GQA_WIDTH]).astype(BF16)
    md = (head_rms(od_ref[...].astype(F32), sub_ref[...]) * (1.0 - lam_init)
          * g[:, GQA_WIDTH:GQA_WIDTH + DIFF_WIDTH]).astype(BF16)
    mr = (head_rms(of_ref[...].astype(F32) + ob_ref[...].astype(F32), hgn_ref[...])
          * g[:, GQA_WIDTH + DIFF_WIDTH:]).astype(BF16)
    y = (jnp.dot(ma, w_ref[0:GQA_WIDTH, :], preferred_element_type=F32)
         + jnp.dot(md, w_ref[GQA_WIDTH:GQA_WIDTH + DIFF_WIDTH, :], preferred_element_type=F32)
         + jnp.dot(mr, w_ref[GQA_WIDTH + DIFF_WIDTH:, :], preferred_element_type=F32))
    xn = DEEPNORM_ALPHA * x_ref[...] + gate_ref[...] * y
    mu = jnp.mean(xn, axis=-1, keepdims=True)
    xc = xn - mu
    var = jnp.mean(xc * xc, axis=-1, keepdims=True)
    o_ref[...] = xc * lax.rsqrt(var + LN_EPS) * lng_ref[...] + lnb_ref[...]


def _out_projection(out_a, out_d, o_f, o_b, gates, x, mod_sel, w_out_b, subln, hgn, bd, ln_g, ln_b,
                    layer, lam_init):
    t_total = x.shape[0]
    tm = PROJ_ROWS
    row = lambda w: pl.BlockSpec((tm, w), lambda i: (i, 0))
    full = lambda shape: pl.BlockSpec(shape, lambda i: (0,) * len(shape))
    return pl.pallas_call(
        functools.partial(_outproj_kernel, lam_init=lam_init),
        grid=(t_total // tm,),
        in_specs=[row(GQA_WIDTH), row(DIFF_WIDTH), row(HGRN_WIDTH), row(HGRN_WIDTH), row(D_MODEL), row(D_MODEL),
                  _mod_spec(mod_sel, 2, layer, t_total, tm),
                  pl.BlockSpec((None, D_MODEL, D_MODEL), lambda i: (layer, 0, 0)),
                  full((1, DIFF_WIDTH)), full((1, HGRN_WIDTH)),
                  full((DIFF_WIDTH, DIFF_WIDTH)), full((1, D_MODEL)), full((1, D_MODEL))],
        out_specs=row(D_MODEL),
        out_shape=jax.ShapeDtypeStruct((t_total, D_MODEL), F32),
        compiler_params=_params(("parallel",)),
        name="out_projection",
    )(out_a, out_d, o_f, o_b, gates, x, mod_sel[0], w_out_b, subln, hgn, bd, ln_g, ln_b)


def _block_diag_ones(n):
    idx = np.arange(n) // HEAD_DIM
    return jnp.asarray((idx[:, None] == idx[None, :]).astype(np.float32), dtype=BF16)


def _scan_constants():
    C, W, H = SCAN_CHUNK, HGRN_WIDTH, HGRN_HEADS
    t = np.arange(C)[:, None]
    u = np.arange(C)[None, :]
    tri = np.stack([u <= t, u >= t])
    s = (np.arange(W) % C)[None, :]
    lmask = np.zeros((2, HGRN_LEVELS, C, W), np.float32)
    for level in range(HGRN_LEVELS):
        block = 8 << level
        same = (t // block) == (s // block)
        if level == 0:
            lmask[0, level] = same & (s <= t)
            lmask[1, level] = same & (s >= t)
        else:
            t_hi, s_hi = (t % block) >= block // 2, (s % block) >= block // 2
            lmask[0, level] = same & t_hi & ~s_hi
            lmask[1, level] = same & ~t_hi & s_hi
    rows = np.arange(H * C)[:, None]
    hmask = (rows // C) == (np.arange(W)[None, :] // HEAD_DIM)
    bdmask = (np.arange(W)[:, None] // HEAD_DIM) == (np.arange(W)[None, :] // HEAD_DIM)
    return (jnp.asarray(tri.astype(np.float32), dtype=BF16), jnp.asarray(lmask),
            jnp.asarray(hmask.astype(np.float32), dtype=BF16), jnp.asarray(bdmask.astype(np.float32)))


def _rope_tables(n_tokens, dim):
    rows = n_tokens // GRID_W
    row = np.repeat(np.arange(rows, dtype=np.float64), GRID_W)
    col = np.tile(np.arange(GRID_W, dtype=np.float64), rows)
    quarter = dim // 4
    inv_freq = ROPE_THETA ** (-np.arange(quarter, dtype=np.float64) / quarter)
    ar = row[:, None] * inv_freq[None, :]
    ac = col[:, None] * inv_freq[None, :]
    ang = np.concatenate([ar, ar, ac, ac], axis=-1)
    cos = np.tile(np.cos(ang), (1, LANES // dim))
    sin = np.tile(np.sin(ang), (1, LANES // dim))
    first = (np.arange(LANES) % (dim // 2)) < quarter
    return tuple(jnp.asarray(t, dtype=F32) for t in (cos, np.where(first, -sin, 0.0), np.where(first, 0.0, sin)))


def _mixer_layer(x, mod_sel, layer, batch, tokens_per_batch, rope_tabs, cache, consts, weights,
                 ctx_out, ctx_prev):
    (w_in_b, w_out_b, qgain, kgain, lam_p, subln, hgn, lb_raw, ln_g, ln_b) = weights
    bdq, bdk, bdd, scan_consts = consts
    lam_init = 0.8 - 0.6 * math.exp(-0.3 * layer)
    outs = _in_projection(x, mod_sel, w_in_b, qgain, kgain, bdq, bdk, rope_tabs, tokens_per_batch, ctx_out,
                          layer, None if ctx_prev is None else ctx_prev[0])
    qa, kdup, vaug, qd, kd, vaugd, gates, zr = outs[:8]
    if cache is None:
        gqa_cache = diff_cache = s0 = None
        tq_a = tq_d = tk = tokens_per_batch
        items_a, items_d = 2, 2
        slabs_a = 2
    else:
        gqa_cache, diff_cache, s0 = cache
        tq_a, tq_d, tk = 512, 256, 512
        items_a = items_d = 1
        slabs_a = 1
    out_a = _gqa_attention(qa, kdup, vaug, gqa_cache, batch, tokens_per_batch, tq_a, tk, items_a, slabs_a, layer)
    out_d = _diff_attention(qd, kd, vaugd, lam_p, diff_cache, batch, tokens_per_batch, tq_d, tk, items_d, layer,
                            lam_init)
    scan = _hgrn_scan(zr, lb_raw, scan_consts, s0, None if ctx_prev is None else ctx_prev[1], ctx_out, layer,
                      batch, tokens_per_batch)
    x_new = _out_projection(out_a, out_d, scan[0], scan[1], gates, x, mod_sel, w_out_b, subln, hgn, bdd, ln_g, ln_b,
                            layer, lam_init)
    return x_new, ((outs[8:], scan[2]) if ctx_out else None)


def kernel(x_prompt, x_sample, cache_gqa_k, cache_gqa_v, cache_diff_k, cache_diff_v, state_hgrn, c, c_ctx,
           w_ada, b_ada, w_in, gqa_q_norm, gqa_k_norm, diff_lambda, diff_subln, hgrn_lower_bounds, hgrn_norm,
           w_out, ln_g, ln_b):
    batch, seq, _ = x_prompt.shape
    dec_batch, dec_seq, _ = x_sample.shape
    past = cache_gqa_k.shape[2]
    H, HD = HGRN_HEADS, HEAD_DIM

    cond = jnp.zeros((MOD_ROWS, D_MODEL), F32).at[0].set(c_ctx).at[1:1 + dec_batch].set(c)
    mod = _modulation(cond, w_ada, b_ada)
    consts = (_block_diag_ones(GQA_WIDTH), _block_diag_ones(LANES), _block_diag_ones(DIFF_WIDTH),
              _scan_constants())
    rope_tabs = _rope_tables(dec_seq, HEAD_DIM) + _rope_tables(dec_seq, DIFF_QK_DIM)
    w_in_b = w_in.astype(BF16)
    w_out_b = w_out.astype(BF16)

    def layer_weights(l):
        return (w_in_b, w_out_b,
                jnp.tile(gqa_q_norm[l], GQA_WIDTH // HD)[None, :] * (HD ** -0.5 * LOG2E),
                jnp.tile(gqa_k_norm[l], LANES // HD)[None, :],
                diff_lambda[l],
                jnp.tile(diff_subln[l], DIFF_WIDTH // HD)[None, :],
                jnp.tile(hgrn_norm[l], HGRN_WIDTH // HD)[None, :],
                hgrn_lower_bounds, ln_g[l][None, :], ln_b[l][None, :])

    mod5 = mod.reshape(DEPTH, MOD_ROWS, 3, 1, D_MODEL)

    x = x_prompt.reshape(batch * seq, D_MODEL)
    ctx = None
    for l in range(DEPTH):
        x, ctx = _mixer_layer(x, (mod5, 0, 1), l, batch, seq, None, None, consts, layer_weights(l), True, ctx)
    y_prompt = x.reshape(batch, seq, D_MODEL)
    (kn, av, dk, dv), st = ctx
    new_leaves = [kn.reshape(batch, DEPTH, seq, GQA_KV_HEADS, HD), av.reshape(batch, DEPTH, seq, GQA_KV_HEADS, HD),
                  dk.reshape(batch, DEPTH, seq, DIFF_HEADS, HD), dv.reshape(batch, DEPTH, seq, DIFF_HEADS, HD),
                  jnp.swapaxes(st[..., :HD], -1, -2)]

    ones = jnp.ones((dec_batch, DEPTH, past, HD), F32)
    ck, cv, dvc = cache_gqa_k, cache_gqa_v, cache_diff_v
    ckd = jnp.stack([jnp.concatenate([ck[..., g, :], ck[..., g, :]], axis=-1) for g in range(GQA_KV_HEADS)], axis=2)
    cvd = jnp.stack([jnp.concatenate([cv[..., g, :], ones], axis=-1) for g in range(GQA_KV_HEADS)], axis=2)
    dkc = cache_diff_k.reshape(dec_batch, DEPTH, past, DIFF_WIDTH)
    dvd = jnp.stack([jnp.concatenate([dvc[..., h, :], ones] if h % 2 == 0 else [ones, dvc[..., h, :]], axis=-1)
                     for h in range(DIFF_HEADS)], axis=2)
    s0t = jnp.swapaxes(state_hgrn.astype(F32), -1, -2)
    eye = jnp.eye(H, dtype=F32)
    s0 = (s0t[..., None, :] * eye[:, None, :, None]).reshape(dec_batch, DEPTH, 2, H * HD, H * HD)
    cache = ((ckd.astype(BF16), cvd.astype(BF16)), (dkc.astype(BF16), dvd.astype(BF16)), s0)
    x = x_sample.reshape(dec_batch * dec_seq, D_MODEL)
    for l in range(DEPTH):
        x, _ = _mixer_layer(x, (mod5, 1, dec_batch), l, dec_batch, dec_seq, rope_tabs, cache, consts,
                            layer_weights(l), False, None)
    y_sample = x.reshape(dec_batch, dec_seq, D_MODEL)
    return (y_prompt, y_sample) + tuple(new_leaves)
```

```python
import functools
import math

import numpy as np
import jax
import jax.numpy as jnp
from jax import lax
from jax.experimental import pallas as pl
from jax.experimental.pallas import tpu as pltpu

F32 = jnp.float32
BF16 = jnp.bfloat16

D_MODEL = 1024
DEPTH = 2
GRID_W = 64
HEAD_DIM = 64
GQA_WIDTH = 512
GQA_KV_HEADS = 2
DIFF_WIDTH = 256
DIFF_HEADS = 4
DIFF_QK_DIM = 32
HGRN_WIDTH = 256
HGRN_HEADS = 4
IN_WIDTH = 3584
SCAN_CHUNK = 64
HGRN_LEVELS = 4
ROPE_THETA = 10000.0
RMS_EPS = 1e-6
LN_EPS = 1e-5
FORGET_MIN = 1e-6
DEEPNORM_ALPHA = (2 * DEPTH) ** 0.25
LANES = 128
SUBLANES = 8
MOD_ROWS = SUBLANES
VMEM_LIMIT = 48 * 1024 * 1024
SM_ROWS = 32
PROJ_ROWS = 512
SCAN_BLOCK = 512
LOG2E = math.log2(math.e)

_A_Q, _A_K, _A_V, _A_G = 0, 512, 640, 768
_D_Q, _D_K, _D_V, _D_G = 1280, 1536, 1792, 2048
_R_Z, _R_G = 2304, 3328

_NT = (((1,), (1,)), ((), ()))
_TN = (((0,), (0,)), ((), ()))


def _silu(x):
    return x * jax.nn.sigmoid(x)


def _params(sem):
    return pltpu.CompilerParams(dimension_semantics=sem, vmem_limit_bytes=VMEM_LIMIT)


def _mod_kernel(c_ref, w_ref, b_ref, o_ref):
    c = c_ref[...]
    o_ref[0] = jnp.dot(_silu(c), w_ref[0], preferred_element_type=F32) + b_ref[0]


def _modulation(cond, w_ada, b_ada):
    tn = 1024
    return pl.pallas_call(
        _mod_kernel,
        grid=(DEPTH, 3 * D_MODEL // tn),
        in_specs=[pl.BlockSpec((MOD_ROWS, D_MODEL), lambda l, j: (0, 0)),
                  pl.BlockSpec((1, D_MODEL, tn), lambda l, j: (l, 0, j)),
                  pl.BlockSpec((1, 1, tn), lambda l, j: (l, 0, j))],
        out_specs=pl.BlockSpec((1, MOD_ROWS, tn), lambda l, j: (l, 0, j)),
        out_shape=jax.ShapeDtypeStruct((DEPTH, MOD_ROWS, 3 * D_MODEL), F32),
        compiler_params=_params(("arbitrary", "arbitrary")),
        name="modulation",
    )(cond, w_ada, b_ada.reshape(DEPTH, 1, 3 * D_MODEL))


def _rope(x, c, s1, s2, shift):
    return x * c + pltpu.roll(x, LANES - shift, 1) * s1 + pltpu.roll(x, shift, 1) * s2


def _inproj_kernel(*refs, rope, ctx_out, ctx_alias):
    it = iter(refs)
    x_ref, sc_ref, sh_ref, w_ref, qg_ref, kg_ref, bdq_ref, bdk_ref = [next(it) for _ in range(8)]
    if rope:
        ca, s1a, s2a, cd, s1d, s2d = [next(it)[...] for _ in range(6)]
    if ctx_alias:
        for _ in range(4):
            next(it)
    qa_o, kdup_o, vaug_o, qd_o, kd_o, vaugd_o, gates_o, zr_o = [next(it) for _ in range(8)]
    if ctx_out:
        ctx_refs = [next(it) for _ in range(4)]
        if not ctx_alias:
            for r in ctx_refs:
                r[:, 1:] = jnp.zeros((r.shape[0], r.shape[1] - 1) + r.shape[2:], F32)

        def put_ctx(n, val):
            r = ctx_refs[n]
            r[:, 0] = val.reshape(r.shape[0], r.shape[2], r.shape[3])

    h = (x_ref[...] * (1.0 + sc_ref[...]) + sh_ref[...]).astype(BF16)

    def proj(lo, hi):
        return jnp.dot(h, w_ref[:, lo:hi], preferred_element_type=F32)

    tm = h.shape[0]
    low = lax.broadcasted_iota(jnp.int32, (tm, LANES), 1) < HEAD_DIM
    inv_d = 1.0 / HEAD_DIM

    aq = proj(_A_Q, _A_K)
    ssq = jnp.dot((aq * aq).astype(BF16), bdq_ref[...], preferred_element_type=F32)
    qn = aq * lax.rsqrt(ssq * inv_d + RMS_EPS) * qg_ref[...]
    for j in range(GQA_WIDTH // LANES):
        slab = qn[:, j * LANES:(j + 1) * LANES]
        if rope:
            slab = _rope(slab, ca, s1a, s2a, HEAD_DIM // 4)
        qa_o[:, j * LANES:(j + 1) * LANES] = slab.astype(BF16)

    ak = proj(_A_K, _A_V)
    ssqk = jnp.dot((ak * ak).astype(BF16), bdk_ref[...], preferred_element_type=F32)
    kn = ak * lax.rsqrt(ssqk * inv_d + RMS_EPS) * kg_ref[...]
    if ctx_out:
        put_ctx(0, kn)
    if rope:
        kn = _rope(kn, ca, s1a, s2a, HEAD_DIM // 4)
    kr = pltpu.roll(kn, HEAD_DIM, 1)
    kdup_o[0] = jnp.where(low, kn, kr).astype(BF16)
    kdup_o[1] = jnp.where(low, kr, kn).astype(BF16)

    av = proj(_A_V, _A_G)
    if ctx_out:
        put_ctx(1, av)
    vr = pltpu.roll(av, HEAD_DIM, 1)
    vaug_o[0] = jnp.where(low, av, 1.0).astype(BF16)
    vaug_o[1] = jnp.where(low, vr, 1.0).astype(BF16)

    gates_o[:, 0:GQA_WIDTH] = _silu(proj(_A_G, _D_Q)).astype(BF16)

    dq = proj(_D_Q, _D_K) * (DIFF_QK_DIM ** -0.5 * LOG2E)
    dk = proj(_D_K, _D_V)
    if ctx_out:
        put_ctx(2, dk)
    for j in range(DIFF_WIDTH // LANES):
        sq = dq[:, j * LANES:(j + 1) * LANES]
        sk = dk[:, j * LANES:(j + 1) * LANES]
        if rope:
            sq = _rope(sq, cd, s1d, s2d, DIFF_QK_DIM // 4)
            sk = _rope(sk, cd, s1d, s2d, DIFF_QK_DIM // 4)
        qd_o[:, j * LANES:(j + 1) * LANES] = sq.astype(BF16)
        kd_o[:, j * LANES:(j + 1) * LANES] = sk.astype(BF16)
    dv = proj(_D_V, _D_G)
    if ctx_out:
        put_ctx(3, dv)
    for j in range(DIFF_WIDTH // LANES):
        sv = dv[:, j * LANES:(j + 1) * LANES]
        vaugd_o[2 * j] = jnp.where(low, sv, 1.0).astype(BF16)
        vaugd_o[2 * j + 1] = jnp.where(low, 1.0, sv).astype(BF16)
    gates_o[:, GQA_WIDTH:GQA_WIDTH + DIFF_WIDTH] = _silu(proj(_D_G, _R_Z)).astype(BF16)

    gates_o[:, GQA_WIDTH + DIFF_WIDTH:] = _silu(proj(_R_G, IN_WIDTH)).astype(BF16)
    zr = proj(_R_Z, _R_G)
    zr_o[:, 0:HGRN_WIDTH] = _silu(zr[:, 0:HGRN_WIDTH])
    zr_o[:, HGRN_WIDTH:] = zr[:, HGRN_WIDTH:]


def _mod_spec(mod_sel, which, layer, t_total, tm):
    _, row0, n_rows = mod_sel
    blocks_per_row = t_total // n_rows // tm
    return pl.BlockSpec((None, None, None, 1, D_MODEL), lambda i: (layer, row0 + i // blocks_per_row, which, 0, 0))


def _in_projection(x, mod_sel, w_in_b, qgain, kgain, bdq, bdk, rope_tabs, tokens_per_batch, ctx_out,
                   layer, ctx_prev):
    t_total = x.shape[0]
    tm = PROJ_ROWS
    nb_rows = tokens_per_batch // tm
    rope = rope_tabs is not None
    assert not rope or nb_rows >= 1
    full = lambda shape: pl.BlockSpec(shape, lambda i: (0,) * len(shape))
    in_specs = [pl.BlockSpec((tm, D_MODEL), lambda i: (i, 0)),
                _mod_spec(mod_sel, 1, layer, t_total, tm), _mod_spec(mod_sel, 0, layer, t_total, tm),
                pl.BlockSpec((None, D_MODEL, IN_WIDTH), lambda i: (layer, 0, 0)),
                full((1, GQA_WIDTH)), full((1, LANES)),
                full((GQA_WIDTH, GQA_WIDTH)), full((LANES, LANES))]
    args = [x, mod_sel[0], mod_sel[0], w_in_b, qgain, kgain, bdq, bdk]
    if rope:
        in_specs += [pl.BlockSpec((tm, LANES), lambda i: (i % nb_rows, 0))] * 6
        args += list(rope_tabs)
    row = lambda w: pl.BlockSpec((tm, w), lambda i: (i, 0))
    stk = lambda n: pl.BlockSpec((n, tm, LANES), lambda i: (0, i, 0))
    out_specs = [row(GQA_WIDTH), stk(2), stk(2), row(DIFF_WIDTH), row(DIFF_WIDTH), stk(4),
                 row(D_MODEL), row(4 * HGRN_WIDTH)]
    out_shape = [jax.ShapeDtypeStruct((t_total, GQA_WIDTH), BF16),
                 jax.ShapeDtypeStruct((2, t_total, LANES), BF16),
                 jax.ShapeDtypeStruct((2, t_total, LANES), BF16),
                 jax.ShapeDtypeStruct((t_total, DIFF_WIDTH), BF16),
                 jax.ShapeDtypeStruct((t_total, DIFF_WIDTH), BF16),
                 jax.ShapeDtypeStruct((4, t_total, LANES), BF16),
                 jax.ShapeDtypeStruct((t_total, D_MODEL), BF16),
                 jax.ShapeDtypeStruct((t_total, 4 * HGRN_WIDTH), F32)]
    aliases = {}
    if ctx_out:
        seq = tokens_per_batch
        nseq = tm // seq
        assert nseq * seq == tm
        widths = (LANES, LANES, DIFF_WIDTH, DIFF_WIDTH)
        if ctx_prev is not None:
            aliases = {len(args) + n: len(out_specs) + n for n in range(4)}
            in_specs += [pl.BlockSpec(memory_space=pl.ANY)] * 4
            args += list(ctx_prev)
        if ctx_prev is None:
            assert layer == 0
            out_specs += [pl.BlockSpec((nseq, DEPTH, seq, w), lambda i: (i, 0, 0, 0)) for w in widths]
        else:
            out_specs += [pl.BlockSpec((nseq, 1, seq, w), lambda i: (i, layer, 0, 0)) for w in widths]
        out_shape += [jax.ShapeDtypeStruct((t_total // seq, DEPTH, seq, w), F32) for w in widths]
    return pl.pallas_call(
        functools.partial(_inproj_kernel, rope=rope, ctx_out=ctx_out, ctx_alias=bool(aliases)),
        grid=(t_total // tm,),
        in_specs=in_specs, out_specs=out_specs, out_shape=out_shape,
        input_output_aliases=aliases,
        compiler_params=_params(("parallel",)),
        name="in_projection",
    )(*args)


def _chunk_rows(j, tk):
    return pl.ds(j * tk, tk) if isinstance(j, int) else pl.ds(pl.multiple_of(j * tk, tk), tk)


def _attention_scratch(n_items, rows, tk):
    rows = n_items * rows
    return ([pltpu.VMEM((rows, LANES), BF16)]
            + [pltpu.VMEM((rows, tk), F32)] * 2 + [pltpu.VMEM((rows, tk), BF16)] * 2
            + [pltpu.VMEM((rows, LANES), F32)] * 4)


def _item_scratch(refs, item, rows):
    return [r.at[pl.ds(item * rows, rows)] for r in refs]


def _softmax_stage(s_sc, p_sc, a_sc, m_sc, keys, first):
    for r in range(s_sc.shape[0] // SM_ROWS):
        rows = pl.ds(r * SM_ROWS, SM_ROWS)
        s = s_sc[rows, 0:keys]
        m_cur = jnp.max(s, axis=1, keepdims=True)
        if first:
            m_new = jnp.broadcast_to(m_cur, (SM_ROWS, LANES))
        else:
            m_prev = m_sc[rows, :]
            m_new = jnp.maximum(m_prev, m_cur)
            a_sc[rows, :] = jnp.exp2(m_prev - m_new)
        m_sc[rows, :] = m_new
        p_sc[rows, 0:keys] = jnp.exp2(s - jnp.concatenate([m_new] * (keys // LANES), axis=1)).astype(BF16)


def _item_timeline(n_new, cache, k_at, v_at, start, qk, sm, av, finish):
    o = 0 if cache is None else 1
    n = n_new + o
    k_of = lambda c: cache[0]() if (o and c == 0) else k_at(c - o)
    v_of = lambda c: cache[1]() if (o and c == 0) else v_at(c - o)
    steps = []
    for t in range(n + 2):
        step = [start] if t == 0 else []
        if t < n:
            step.append(lambda t=t: qk(t % 2, k_of(t)))
        if 0 <= t - 1 < n:
            step.append(lambda t=t: sm((t - 1) % 2, t == 1))
        if 0 <= t - 2 < n:
            step.append(lambda t=t: av(t % 2, v_of(t - 2), t == 2))
        if t == n + 1:
            step.append(finish)
        steps.append(step)
    return steps


def _run_items(timelines):
    lag = max(1, len(timelines[0]) - 2)
    for g in range(lag * (len(timelines) - 1) + len(timelines[0])):
        for i, steps in enumerate(timelines):
            if 0 <= g - i * lag < len(steps):
                for stage in steps[g - i * lag]:
                    stage()


def _gqa_kernel(*refs, has_cache, tq, tk, nk, n_items, shared_kv, slabs):
    if has_cache:
        q_ref, kc_ref, vc_ref = refs[:3]
        refs = refs[3:]
    else:
        q_ref = refs[0]
        refs = refs[1:]
    k_ref, v_ref, o_ref = refs[:3]
    low = lax.broadcasted_iota(jnp.int32, (tq, LANES), 1) < HEAD_DIM
    rows = 2 * slabs * tq

    def item_stages(item):
        qs_sc, s0, s1, p0, p1, a0, a1, m_sc, acc_sc = _item_scratch(refs[3:], item, rows)
        s_b, p_b, a_b = (s0, s1), (p0, p1), (a0, a1)

        def start():
            for j in range(slabs):
                q = q_ref[pl.ds(item * tq, tq), j * LANES:(j + 1) * LANES]
                zero = jnp.zeros_like(q)
                qs_sc[2 * j * tq:(2 * j + 1) * tq, :] = jnp.where(low, q, zero)
                qs_sc[(2 * j + 1) * tq:(2 * j + 2) * tq, :] = jnp.where(low, zero, q)

        keys = {}

        def qk(slot, k):
            keys[slot] = k.shape[0]
            s_b[slot][:, 0:k.shape[0]] = lax.dot_general(qs_sc[...], k, _NT, preferred_element_type=F32)

        def sm(slot, first):
            _softmax_stage(s_b[slot], p_b[slot], a_b[slot], m_sc, keys[slot], first)

        def av(slot, v, first):
            pv = jnp.dot(p_b[slot][:, 0:v.shape[0]], v, preferred_element_type=F32)
            acc_sc[...] = pv if first else a_b[slot][...] * acc_sc[...] + pv

        def finish():
            acc = acc_sc[...]
            on = acc / pltpu.roll(acc, HEAD_DIM, 1)
            for j in range(slabs):
                first, second = on[2 * j * tq:(2 * j + 1) * tq], on[(2 * j + 1) * tq:(2 * j + 2) * tq]
                o_ref[pl.ds(item * tq, tq), j * LANES:(j + 1) * LANES] = jnp.where(
                    low, first, pltpu.roll(second, HEAD_DIM, 1)).astype(BF16)

        base = 0 if shared_kv else item * nk
        chunk = lambda ref: (lambda j: ref[0, _chunk_rows(base + j, tk), :])
        cache = (lambda: kc_ref[0, 0], lambda: vc_ref[0, 0]) if has_cache else None
        return _item_timeline(nk, cache, chunk(k_ref), chunk(v_ref), start, qk, sm, av, finish)

    _run_items([item_stages(item) for item in range(n_items)])


def _item_geometry(batch, tokens_per_batch, tq, n_items):
    nq = tokens_per_batch // tq
    if nq == 1:
        return batch // n_items, 1, n_items * tokens_per_batch, False
    assert nq % n_items == 0
    return batch, nq // n_items, tokens_per_batch, True


def _gqa_attention(qa, kdup, vaug, cache, batch, tokens_per_batch, tq, tk, n_items, slabs, layer):
    t_total = qa.shape[0]
    nk = tokens_per_batch // tk
    has_cache = cache is not None
    bsteps, nq, kv_rows, shared_kv = _item_geometry(batch, tokens_per_batch, tq, n_items)
    assert shared_kv or not has_cache
    width = slabs * LANES
    group = lambda p: p * slabs // 2
    in_specs = [pl.BlockSpec((n_items * tq, width), lambda b, p, i: (b * nq + i, p))]
    args = [qa]
    if has_cache:
        past = cache[0].shape[3]
        in_specs += [pl.BlockSpec((1, None, 1, past, LANES), lambda b, p, i: (b, layer, group(p), 0, 0))] * 2
        args += list(cache)
    in_specs += [pl.BlockSpec((1, kv_rows, LANES), lambda b, p, i: (group(p), b, 0))] * 2
    args += [kdup, vaug]
    return pl.pallas_call(
        functools.partial(_gqa_kernel, has_cache=has_cache, tq=tq, tk=tk, nk=nk, n_items=n_items,
                          shared_kv=shared_kv, slabs=slabs),
        grid=(bsteps, GQA_WIDTH // width, nq),
        in_specs=in_specs,
        out_specs=pl.BlockSpec((n_items * tq, width), lambda b, p, i: (b * nq + i, p)),
        out_shape=jax.ShapeDtypeStruct((t_total, GQA_WIDTH), BF16),
        scratch_shapes=_attention_scratch(n_items, 2 * slabs * tq, tk),
        compiler_params=_params(("parallel", "parallel", "parallel")),
        name="gqa_attention",
    )(*args)


def _diff_kernel(*refs, has_cache, tq, tk, nk, n_items, shared_kv, lam_init):
    q_ref, lp_ref = refs[:2]
    if has_cache:
        kc_ref, vc0_ref, vc1_ref = refs[2:5]
        refs = refs[5:]
    else:
        refs = refs[2:]
    k_ref, v0_ref, v1_ref, o_ref = refs[:4]
    lane = lax.broadcasted_iota(jnp.int32, (tq, LANES), 1)
    lp = lp_ref[...]
    lam = (jnp.exp(jnp.sum(lp[0:1] * lp[1:2], axis=1, keepdims=True))
           - jnp.exp(jnp.sum(lp[2:3] * lp[3:4], axis=1, keepdims=True)) + lam_init)
    h2 = 2 * tq
    rows = 4 * tq

    def item_stages(item):
        qs_sc, s0, s1, p0, p1, a0, a1, m_sc, acc_sc = _item_scratch(refs[4:], item, rows)
        s_b, p_b, a_b = (s0, s1), (p0, p1), (a0, a1)

        def start():
            q = q_ref[pl.ds(item * tq, tq), :]
            zero = jnp.zeros_like(q)
            for j in range(4):
                qs_sc[j * tq:(j + 1) * tq, :] = jnp.where((lane >> 5) == j, q, zero)

        keys = {}

        def qk(slot, k):
            keys[slot] = k.shape[0]
            s_b[slot][:, 0:k.shape[0]] = lax.dot_general(qs_sc[...], k, _NT, preferred_element_type=F32)

        def sm(slot, first):
            _softmax_stage(s_b[slot], p_b[slot], a_b[slot], m_sc, keys[slot], first)

        def av(slot, v, first):
            for half, vh in zip((pl.ds(0, h2), pl.ds(h2, h2)), v):
                pv = jnp.dot(p_b[slot][half, 0:vh.shape[0]], vh, preferred_element_type=F32)
                acc_sc[half, :] = pv if first else a_b[slot][half, :] * acc_sc[half, :] + pv

        def finish():
            acc = acc_sc[...]
            on = acc / pltpu.roll(acc, HEAD_DIM, 1)
            o0 = on[0:tq] - lam * on[tq:h2]
            o1 = on[h2:h2 + tq] - lam * on[h2 + tq:]
            o_ref[pl.ds(item * tq, tq), :] = jnp.where(lane < HEAD_DIM, o0, o1).astype(BF16)

        base = 0 if shared_kv else item * nk
        k_at = lambda j: k_ref[_chunk_rows(base + j, tk), :]
        v_at = lambda j: (v0_ref[0, _chunk_rows(base + j, tk), :], v1_ref[0, _chunk_rows(base + j, tk), :])
        cache = (lambda: kc_ref[0], lambda: (vc0_ref[0, 0], vc1_ref[0, 0])) if has_cache else None
        return _item_timeline(nk, cache, k_at, v_at, start, qk, sm, av, finish)

    _run_items([item_stages(item) for item in range(n_items)])


def _diff_attention(qd, kd, vaugd, lam_p, cache, batch, tokens_per_batch, tq, tk, n_items, layer, lam_init):
    t_total = qd.shape[0]
    nk = tokens_per_batch // tk
    has_cache = cache is not None
    bsteps, nq, kv_rows, shared_kv = _item_geometry(batch, tokens_per_batch, tq, n_items)
    assert shared_kv or not has_cache
    in_specs = [pl.BlockSpec((n_items * tq, LANES), lambda b, p, i: (b * nq + i, p)),
                pl.BlockSpec((4, DIFF_QK_DIM), lambda b, p, i: (0, 0))]
    args = [qd, lam_p]
    if has_cache:
        ckd, cvd = cache
        past = ckd.shape[2]
        in_specs += [pl.BlockSpec((1, None, past, LANES), lambda b, p, i: (b, layer, 0, p)),
                     pl.BlockSpec((1, None, 1, past, LANES), lambda b, p, i: (b, layer, 2 * p, 0, 0)),
                     pl.BlockSpec((1, None, 1, past, LANES), lambda b, p, i: (b, layer, 2 * p + 1, 0, 0))]
        args += [ckd, cvd, cvd]
    in_specs += [pl.BlockSpec((kv_rows, LANES), lambda b, p, i: (b, p)),
                 pl.BlockSpec((1, kv_rows, LANES), lambda b, p, i: (2 * p, b, 0)),
                 pl.BlockSpec((1, kv_rows, LANES), lambda b, p, i: (2 * p + 1, b, 0))]
    args += [kd, vaugd, vaugd]
    return pl.pallas_call(
        functools.partial(_diff_kernel, has_cache=has_cache, tq=tq, tk=tk, nk=nk, n_items=n_items,
                          shared_kv=shared_kv, lam_init=lam_init),
        grid=(bsteps, DIFF_WIDTH // LANES, nq),
        in_specs=in_specs,
        out_specs=pl.BlockSpec((n_items * tq, LANES), lambda b, p, i: (b * nq + i, p)),
        out_shape=jax.ShapeDtypeStruct((t_total, DIFF_WIDTH), BF16),
        scratch_shapes=_attention_scratch(n_items, 4 * tq, tk),
        compiler_params=_params(("parallel", "parallel", "parallel")),
        name="diff_attention",
    )(*args)


def _hgrn_gates(qs, z, v, lb, g_sc, tri):
    sig = jax.nn.sigmoid(z)
    om = 1.0 - lb
    k = om * (1.0 - sig)
    g = jnp.log2(jnp.maximum(lb + om * sig, FORGET_MIN))
    g1 = g.astype(BF16)
    e1 = g - g1.astype(F32)
    g2 = e1.astype(BF16)
    g3 = (e1 - g2.astype(F32)).astype(BF16)
    G = (jnp.dot(tri, g1, preferred_element_type=F32) + jnp.dot(tri, g2, preferred_element_type=F32)
         + jnp.dot(tri, g3, preferred_element_type=F32))
    g_sc[...] = G
    return qs, k, G, v.astype(BF16)


def _hgrn_operands(gates, g_sc, hmask, rev):
    C, W = SCAN_CHUNK, HGRN_WIDTH
    qs, k, G, vb = gates
    g_last = g_sc[pl.ds(0 if rev else C - 1, 1), :]
    q_in = (qs * jnp.exp2(G)).astype(BF16)
    k_end = (k * jnp.exp2(g_last - G)).astype(BF16)

    def block_diag(a):
        return jnp.concatenate([a] * HGRN_HEADS, axis=0) * hmask

    def ref_rows(block, off):
        return jnp.concatenate([jnp.broadcast_to(g_sc[pl.ds(b0 + off, 1), :], (block, W))
                                for b0 in range(0, C, block)], axis=0)

    levels = []
    for level in range(HGRN_LEVELS):
        block = 8 << level
        half = block // 2
        if level == 0:
            r = ref_rows(block, half)
            qe, ke = G - r, r - G
        else:
            r = ref_rows(block, half if rev else half - 1)
            qe, ke = jnp.minimum(G - r, 0.0), jnp.minimum(r - G, 0.0)
        levels.append(((qs * jnp.exp2(qe)).astype(BF16), block_diag((k * jnp.exp2(ke)).astype(BF16))))
    return q_in, jnp.exp2(g_last), k_end, vb, block_diag(vb), levels


def _hgrn_intra(operands, lmask_ref, bdmask, rev):
    q_in, decay, k_end, vb, v_bd, levels = operands
    ut = lax.dot_general(vb, k_end, _TN, preferred_element_type=F32) * bdmask
    a = None
    for level, (ql, kl) in enumerate(levels):
        al = lax.dot_general(ql, kl, _NT, preferred_element_type=F32)
        m = lmask_ref[rev, level]
        a = jnp.where(m > 0.5, al, 0.0) if level == 0 else a + al * m
    intra = jnp.dot(a.astype(BF16), v_bd, preferred_element_type=F32)
    return q_in, decay, ut, intra


def _hgrn_state(parts, st):
    q_in, decay, ut, intra = parts
    out = lax.dot_general(q_in, st.astype(BF16), _NT, preferred_element_type=F32) + intra
    return out, st * decay + ut


def _hgrn_kernel(*refs, layer, n_chunks, n_seq, has_s0, state_out, state_alias):
    it = iter(refs)
    qf, zf, vf, qb, zb, vb, lbraw, tri_ref, lmask_ref, hmask_ref, bdmask_ref = [next(it) for _ in range(11)]
    s0 = next(it) if has_s0 else None
    if state_alias:
        next(it)
    of_ref, ob_ref = next(it), next(it)
    sto_ref = next(it) if state_out else None
    st_ref, g_sc = next(it), next(it)

    @pl.when(pl.program_id(1) == 0)
    def _():
        if has_s0:
            st_ref[...] = s0[0]
        else:
            st_ref[...] = jnp.zeros(st_ref.shape, F32)

    lbr = lbraw[...]
    ex = jnp.exp(lbr - jnp.max(lbr, axis=0, keepdims=True))
    sm = ex / jnp.sum(ex, axis=0, keepdims=True)
    lbs = jnp.zeros(sm.shape[1:], F32)
    for j in range(1, layer + 1):
        lbs = lbs + sm[j]
    lb_f, lb_b = lbs[0:1], lbs[1:2]

    hmask = hmask_ref[...]
    bdmask = bdmask_ref[...]

    chunks = []
    for cidx in range(n_chunks):
        for s in range(n_seq):
            base = s * n_chunks
            chunks.append((2 * s, 0, pl.ds((base + cidx) * SCAN_CHUNK, SCAN_CHUNK), qf, zf, vf, lb_f, of_ref))
            chunks.append((2 * s + 1, 1, pl.ds((base + n_chunks - 1 - cidx) * SCAN_CHUNK, SCAN_CHUNK),
                           qb, zb, vb, lb_b, ob_ref))
    n = len(chunks)
    state = [st_ref[i] for i in range(2 * n_seq)]
    gates, operands, parts = {}, {}, {}
    for t in range(n + 3):
        if t < n:
            _, d, rows, q_r, z_r, v_r, lb, _ = chunks[t]
            gates[t] = _hgrn_gates(q_r[rows, :], z_r[rows, :], v_r[rows, :], lb, g_sc.at[t], tri_ref[d])
        if 0 <= t - 1 < n:
            operands[t - 1] = _hgrn_operands(gates.pop(t - 1), g_sc.at[t - 1], hmask, chunks[t - 1][1])
        if 0 <= t - 2 < n:
            parts[t - 2] = _hgrn_intra(operands.pop(t - 2), lmask_ref, bdmask, chunks[t - 2][1])
        if 0 <= t - 3 < n:
            si, _, rows, _, _, _, _, o_r = chunks[t - 3]
            out, state[si] = _hgrn_state(parts.pop(t - 3), state[si])
            o_r[rows, :] = out.astype(BF16)
    for i, s in enumerate(state):
        st_ref[i] = s

    if state_out:
        @pl.when(pl.program_id(1) == pl.num_programs(1) - 1)
        def _():
            if not state_alias:
                sto_ref[:, 1:] = jnp.zeros((sto_ref.shape[0], sto_ref.shape[1] - 1) + sto_ref.shape[2:], F32)
            for i, s in enumerate(state):
                for hh in range(HGRN_HEADS):
                    slab = s[hh * HEAD_DIM:(hh + 1) * HEAD_DIM, (hh // 2) * LANES:(hh // 2 + 1) * LANES]
                    if hh % 2:
                        slab = pltpu.roll(slab, HEAD_DIM, 1)
                    square = jnp.concatenate([slab, jnp.zeros_like(slab)], axis=0).T
                    sto_ref[i // 2, 0, i % 2, hh] = square[0:HEAD_DIM, 0:HEAD_DIM]


def _hgrn_scan(zr, lb_raw, scan_consts, s0, state_prev, state_out, layer, batch, tokens_per_batch):
    t_total = zr.shape[0]
    tb = min(SCAN_BLOCK, tokens_per_batch)
    nblk = tokens_per_batch // tb
    n_seq = SCAN_BLOCK // tb if nblk == 1 else 1
    W = HGRN_WIDTH
    has_s0 = s0 is not None
    assert n_seq == 1 or not has_s0
    state_alias = state_prev is not None
    rows = n_seq * tb
    fblk = lambda col: pl.BlockSpec((rows, W), lambda b, j: (b * nblk + j, col))
    bblk = lambda col: pl.BlockSpec((rows, W), lambda b, j: (b * nblk + nblk - 1 - j, col))
    full = lambda a: pl.BlockSpec(a.shape, lambda b, j: (0,) * a.ndim)
    in_specs = [fblk(0), fblk(1), fblk(3), bblk(0), bblk(2), bblk(3),
                pl.BlockSpec((DEPTH, 2, W), lambda b, j: (0, 0, 0))] + [full(a) for a in scan_consts]
    args = [zr] * 6 + [lb_raw] + list(scan_consts)
    if has_s0:
        in_specs.append(pl.BlockSpec((1, None, 2, W, W), lambda b, j: (b, layer, 0, 0, 0)))
        args.append(s0)
    aliases = {}
    if state_alias:
        aliases = {len(args): 2}
        in_specs.append(pl.BlockSpec(memory_space=pl.ANY))
        args.append(state_prev)
    out_specs = [pl.BlockSpec((rows, W), lambda b, j: (b * nblk + j, 0)),
                 pl.BlockSpec((rows, W), lambda b, j: (b * nblk + nblk - 1 - j, 0))]
    out_shape = [jax.ShapeDtypeStruct((t_total, W), BF16), jax.ShapeDtypeStruct((t_total, W), BF16)]
    if state_out and not state_alias:
        assert layer == 0
        out_specs.append(pl.BlockSpec((n_seq, DEPTH, 2, HGRN_HEADS, HEAD_DIM, HEAD_DIM),
                                      lambda b, j: (b, 0, 0, 0, 0, 0)))
    elif state_out:
        out_specs.append(pl.BlockSpec((n_seq, 1, 2, HGRN_HEADS, HEAD_DIM, HEAD_DIM),
                                      lambda b, j: (b, layer, 0, 0, 0, 0)))
    if state_out:
        out_shape.append(jax.ShapeDtypeStruct((batch, DEPTH, 2, HGRN_HEADS, HEAD_DIM, HEAD_DIM), F32))
    n_units = 2 * rows // SCAN_CHUNK
    return pl.pallas_call(
        functools.partial(_hgrn_kernel, layer=layer, n_chunks=tb // SCAN_CHUNK, n_seq=n_seq, has_s0=has_s0,
                          state_out=state_out, state_alias=state_alias),
        grid=(batch // n_seq, nblk),
        in_specs=in_specs, out_specs=out_specs, out_shape=out_shape,
        input_output_aliases=aliases,
        scratch_shapes=[pltpu.VMEM((2 * n_seq, W, W), F32), pltpu.VMEM((n_units, SCAN_CHUNK, W), F32)],
        compiler_params=_params(("parallel", "arbitrary")),
        name="hgrn_scan",
    )(*args)


def _outproj_kernel(oa_ref, od_ref, of_ref, ob_ref, g_ref, x_ref, gate_ref, w_ref, sub_ref, hgn_ref,
                    bd_ref, lng_ref, lnb_ref, o_ref, *, lam_init):
    inv_d = 1.0 / HEAD_DIM
    g = g_ref[...].astype(F32)
    bd = bd_ref[...]

    def head_rms(t, gain):
        ssq = jnp.dot((t * t).astype(BF16), bd, preferred_element_type=F32)
        return t * lax.rsqrt(ssq * inv_d + RMS_EPS) * gain

    ma = (oa_ref[...].astype(F32) * g[:, 0:GQA_WIDTH]).astype(BF16)
    md = (head_rms(od_ref[...].astype(F32), sub_ref[...]) * (1.0 - lam_init)
          * g[:, GQA_WIDTH:GQA_WIDTH + DIFF_WIDTH]).astype(BF16)
    mr = (head_rms(of_ref[...].astype(F32) + ob_ref[...].astype(F32), hgn_ref[...])
          * g[:, GQA_WIDTH + DIFF_WIDTH:]).astype(BF16)
    y = (jnp.dot(ma, w_ref[0:GQA_WIDTH, :], preferred_element_type=F32)
         + jnp.dot(md, w_ref[GQA_WIDTH:GQA_WIDTH + DIFF_WIDTH, :], preferred_element_type=F32)
         + jnp.dot(mr, w_ref[GQA_WIDTH + DIFF_WIDTH:, :], preferred_element_type=F32))
    xn = DEEPNORM_ALPHA * x_ref[...] + gate_ref[...] * y
    mu = jnp.mean(xn, axis=-1, keepdims=True)
    xc = xn - mu
    var = jnp.mean(xc * xc, axis=-1, keepdims=True)
    o_ref[...] = xc * lax.rsqrt(var + LN_EPS) * lng_ref[...] + lnb_ref[...]


def _out_projection(out_a, out_d, o_f, o_b, gates, x, mod_sel, w_out_b, subln, hgn, bd, ln_g, ln_b,
                    layer, lam_init):
    t_total = x.shape[0]
    tm = PROJ_ROWS
    row = lambda w: pl.BlockSpec((tm, w), lambda i: (i, 0))
    full = lambda shape: pl.BlockSpec(shape, lambda i: (0,) * len(shape))
    return pl.pallas_call(
        functools.partial(_outproj_kernel, lam_init=lam_init),
        grid=(t_total // tm,),
        in_specs=[row(GQA_WIDTH), row(DIFF_WIDTH), row(HGRN_WIDTH), row(HGRN_WIDTH), row(D_MODEL), row(D_MODEL),
                  _mod_spec(mod_sel, 2, layer, t_total, tm),
                  pl.BlockSpec((None, D_MODEL, D_MODEL), lambda i: (layer, 0, 0)),
                  full((1, DIFF_WIDTH)), full((1, HGRN_WIDTH)),
                  full((DIFF_WIDTH, DIFF_WIDTH)), full((1, D_MODEL)), full((1, D_MODEL))],
        out_specs=row(D_MODEL),
        out_shape=jax.ShapeDtypeStruct((t_total, D_MODEL), F32),
        compiler_params=_params(("parallel",)),
        name="out_projection",
    )(out_a, out_d, o_f, o_b, gates, x, mod_sel[0], w_out_b, subln, hgn, bd, ln_g, ln_b)


def _block_diag_ones(n):
    idx = np.arange(n) // HEAD_DIM
    return jnp.asarray((idx[:, None] == idx[None, :]).astype(np.float32), dtype=BF16)


def _scan_constants():
    C, W, H = SCAN_CHUNK, HGRN_WIDTH, HGRN_HEADS
    t = np.arange(C)[:, None]
    u = np.arange(C)[None, :]
    tri = np.stack([u <= t, u >= t])
    s = (np.arange(W) % C)[None, :]
    lmask = np.zeros((2, HGRN_LEVELS, C, W), np.float32)
    for level in range(HGRN_LEVELS):
        block = 8 << level
        same = (t // block) == (s // block)
        if level == 0:
            lmask[0, level] = same & (s <= t)
            lmask[1, level] = same & (s >= t)
        else:
            t_hi, s_hi = (t % block) >= block // 2, (s % block) >= block // 2
            lmask[0, level] = same & t_hi & ~s_hi
            lmask[1, level] = same & ~t_hi & s_hi
    rows = np.arange(H * C)[:, None]
    hmask = (rows // C) == (np.arange(W)[None, :] // HEAD_DIM)
    bdmask = (np.arange(W)[:, None] // HEAD_DIM) == (np.arange(W)[None, :] // HEAD_DIM)
    return (jnp.asarray(tri.astype(np.float32), dtype=BF16), jnp.asarray(lmask),
            jnp.asarray(hmask.astype(np.float32), dtype=BF16), jnp.asarray(bdmask.astype(np.float32)))


def _rope_tables(n_tokens, dim):
    rows = n_tokens // GRID_W
    row = np.repeat(np.arange(rows, dtype=np.float64), GRID_W)
    col = np.tile(np.arange(GRID_W, dtype=np.float64), rows)
    quarter = dim // 4
    inv_freq = ROPE_THETA ** (-np.arange(quarter, dtype=np.float64) / quarter)
    ar = row[:, None] * inv_freq[None, :]
    ac = col[:, None] * inv_freq[None, :]
    ang = np.concatenate([ar, ar, ac, ac], axis=-1)
    cos = np.tile(np.cos(ang), (1, LANES // dim))
    sin = np.tile(np.sin(ang), (1, LANES // dim))
    first = (np.arange(LANES) % (dim // 2)) < quarter
    return tuple(jnp.asarray(t, dtype=F32) for t in (cos, np.where(first, -sin, 0.0), np.where(first, 0.0, sin)))


def _mixer_layer(x, mod_sel, layer, batch, tokens_per_batch, rope_tabs, cache, consts, weights,
                 ctx_out, ctx_prev):
    (w_in_b, w_out_b, qgain, kgain, lam_p, subln, hgn, lb_raw, ln_g, ln_b) = weights
    bdq, bdk, bdd, scan_consts = consts
    lam_init = 0.8 - 0.6 * math.exp(-0.3 * layer)
    outs = _in_projection(x, mod_sel, w_in_b, qgain, kgain, bdq, bdk, rope_tabs, tokens_per_batch, ctx_out,
                          layer, None if ctx_prev is None else ctx_prev[0])
    qa, kdup, vaug, qd, kd, vaugd, gates, zr = outs[:8]
    if cache is None:
        gqa_cache = diff_cache = s0 = None
        tq_a = tq_d = tk = tokens_per_batch
        items_a, items_d = 2, 2
        slabs_a = 2
    else:
        gqa_cache, diff_cache, s0 = cache
        tq_a, tq_d, tk = 512, 256, 512
        items_a = items_d = 1
        slabs_a = 1
    out_a = _gqa_attention(qa, kdup, vaug, gqa_cache, batch, tokens_per_batch, tq_a, tk, items_a, slabs_a, layer)
    out_d = _diff_attention(qd, kd, vaugd, lam_p, diff_cache, batch, tokens_per_batch, tq_d, tk, items_d, layer,
                            lam_init)
    scan = _hgrn_scan(zr, lb_raw, scan_consts, s0, None if ctx_prev is None else ctx_prev[1], ctx_out, layer,
                      batch, tokens_per_batch)
    x_new = _out_projection(out_a, out_d, scan[0], scan[1], gates, x, mod_sel, w_out_b, subln, hgn, bdd, ln_g, ln_b,
                            layer, lam_init)
    return x_new, ((outs[8:], scan[2]) if ctx_out else None)


def kernel(x_prompt, x_sample, cache_gqa_k, cache_gqa_v, cache_diff_k, cache_diff_v, state_hgrn, c, c_ctx,
           w_ada, b_ada, w_in, gqa_q_norm, gqa_k_norm, diff_lambda, diff_subln, hgrn_lower_bounds, hgrn_norm,
           w_out, ln_g, ln_b):
    batch, seq, _ = x_prompt.shape
    dec_batch, dec_seq, _ = x_sample.shape
    past = cache_gqa_k.shape[2]
    H, HD = HGRN_HEADS, HEAD_DIM

    cond = jnp.zeros((MOD_ROWS, D_MODEL), F32).at[0].set(c_ctx).at[1:1 + dec_batch].set(c)
    mod = _modulation(cond, w_ada, b_ada)
    consts = (_block_diag_ones(GQA_WIDTH), _block_diag_ones(LANES), _block_diag_ones(DIFF_WIDTH),
              _scan_constants())
    rope_tabs = _rope_tables(dec_seq, HEAD_DIM) + _rope_tables(dec_seq, DIFF_QK_DIM)
    w_in_b = w_in.astype(BF16)
    w_out_b = w_out.astype(BF16)

    def layer_weights(l):
        return (w_in_b, w_out_b,
                jnp.tile(gqa_q_norm[l], GQA_WIDTH // HD)[None, :] * (HD ** -0.5 * LOG2E),
                jnp.tile(gqa_k_norm[l], LANES // HD)[None, :],
                diff_lambda[l],
                jnp.tile(diff_subln[l], DIFF_WIDTH // HD)[None, :],
                jnp.tile(hgrn_norm[l], HGRN_WIDTH // HD)[None, :],
                hgrn_lower_bounds, ln_g[l][None, :], ln_b[l][None, :])

    mod5 = mod.reshape(DEPTH, MOD_ROWS, 3, 1, D_MODEL)

    x = x_prompt.reshape(batch * seq, D_MODEL)
    ctx = None
    for l in range(DEPTH):
        x, ctx = _mixer_layer(x, (mod5, 0, 1), l, batch, seq, None, None, consts, layer_weights(l), True, ctx)
    y_prompt = x.reshape(batch, seq, D_MODEL)
    (kn, av, dk, dv), st = ctx
    new_leaves = [kn.reshape(batch, DEPTH, seq, GQA_KV_HEADS, HD), av.reshape(batch, DEPTH, seq, GQA_KV_HEADS, HD),
                  dk.reshape(batch, DEPTH, seq, DIFF_HEADS, HD), dv.reshape(batch, DEPTH, seq, DIFF_HEADS, HD),
                  st]

    ones = jnp.ones((dec_batch, DEPTH, past, HD), F32)
    ck, cv, dvc = cache_gqa_k, cache_gqa_v, cache_diff_v
    ckd = jnp.stack([jnp.concatenate([ck[..., g, :], ck[..., g, :]], axis=-1) for g in range(GQA_KV_HEADS)], axis=2)
    cvd = jnp.stack([jnp.concatenate([cv[..., g, :], ones], axis=-1) for g in range(GQA_KV_HEADS)], axis=2)
    dkc = cache_diff_k.reshape(dec_batch, DEPTH, past, DIFF_WIDTH)
    dvd = jnp.stack([jnp.concatenate([dvc[..., h, :], ones] if h % 2 == 0 else [ones, dvc[..., h, :]], axis=-1)
                     for h in range(DIFF_HEADS)], axis=2)
    s0t = jnp.swapaxes(state_hgrn.astype(F32), -1, -2)
    eye = jnp.eye(H, dtype=F32)
    s0 = (s0t[..., None, :] * eye[:, None, :, None]).reshape(dec_batch, DEPTH, 2, H * HD, H * HD)
    cache = ((ckd.astype(BF16), cvd.astype(BF16)), (dkc.astype(BF16), dvd.astype(BF16)), s0)
    x = x_sample.reshape(dec_batch * dec_seq, D_MODEL)
    for l in range(DEPTH):
        x, _ = _mixer_layer(x, (mod5, 1, dec_batch), l, dec_batch, dec_seq, rope_tabs, cache, consts,
                            layer_weights(l), False, None)
    y_sample = x.reshape(dec_batch, dec_seq, D_MODEL)
    return (y_prompt, y_sample) + tuple(new_leaves)
```
